```python
import jax, jax.numpy as jnp
from jax import lax
import numpy as np

D_MODEL = 1024
BATCH = 8
SEQ = 2048
DEPTH = 1
DEC_BATCH = 128
DEC_SEQ = 1
PAST_LEN = 16384
PAGE_SIZE = 128

LRU_WIDTH = D_MODEL
LRU_BLOCKS = 8
LRU_BLOCK = LRU_WIDTH // LRU_BLOCKS
CONV_WIDTH = 4
LRU_C = 8.0
RET_HEADS = 4
RET_DK = D_MODEL // RET_HEADS
RET_DV = 2 * RET_DK
RET_CHUNK = 128
ROPE_BASE = 10000.0
MEM_TOKENS = 256
MEM_HEADS = 4
MEM_DH = D_MODEL // MEM_HEADS
N_GROUPS = 4
EXP_PER_GROUP = 8
N_EXPERTS = N_GROUPS * EXP_PER_GROUP
EXP_FF = 512
TOP_K_IN_GROUP = 2

N_BRANCHES = 3
EPS = 1e-6
SPLITS = (LRU_WIDTH, LRU_WIDTH, RET_HEADS * RET_DK, RET_HEADS * RET_DK,
          RET_HEADS * RET_DV, RET_HEADS * RET_DV, MEM_HEADS * MEM_DH, N_BRANCHES * D_MODEL)
IN_WIDTH = sum(SPLITS)

kernel_name = 'hybrid_rglru_retention_memxattn_hmoe_step'


def split_points():
    pts = []
    acc = 0
    for s in SPLITS[:-1]:
        acc += s
        pts.append(acc)
    return pts


def rmsnorm(x, g):
    xf = x.astype(jnp.float32)
    y = xf * lax.rsqrt(jnp.mean(xf * xf, axis=-1, keepdims=True) + EPS)
    return (y * g.astype(jnp.float32)).astype(x.dtype)


def rotary(x, pos):
    half = x.shape[-1] // 2
    inv = ROPE_BASE ** (-jnp.linspace(0.0, 1.0, half, dtype=jnp.float32))
    ang = pos.astype(jnp.float32)[:, None] * inv[None, :]
    cos = jnp.cos(ang)[None, :, None, :]
    sin = jnp.sin(ang)[None, :, None, :]
    x1, x2 = x[..., :half], x[..., half:]
    return jnp.concatenate([x1 * cos - x2 * sin, x2 * cos + x1 * sin], axis=-1)


def causal_conv(x, prev, w, b):
    length = x.shape[1]
    xp = jnp.concatenate([prev.astype(x.dtype), x], axis=1)
    y = b
    for j in range(CONV_WIDTH):
        y = y + w[j] * xp[:, j:j + length]
    return y, xp[:, -(CONV_WIDTH - 1):]


def linear_scan(a, b, h0):
    b = b.at[:, 0].add(a[:, 0] * h0)

    def combine(left, right):
        return left[0] * right[0], right[0] * left[1] + right[1]

    _, h = lax.associative_scan(combine, (a, b), axis=1)
    return h


def rg_lru(x, pos, h0, w_r, b_r, w_i, b_i, lam):
    f32 = jnp.float32
    bsz, length, width = x.shape
    xb = x.reshape(bsz, length, LRU_BLOCKS, LRU_BLOCK)
    r = jax.nn.sigmoid(jnp.einsum('blnc,ncd->blnd', xb, w_r).reshape(bsz, length, width).astype(f32) + b_r.astype(f32))
    i = jax.nn.sigmoid(jnp.einsum('blnc,ncd->blnd', xb, w_i).reshape(bsz, length, width).astype(f32) + b_i.astype(f32))
    log_a = -LRU_C * r * jax.nn.softplus(-lam.astype(f32))
    a = jnp.exp(log_a)
    mult = jnp.sqrt(-jnp.expm1(2.0 * log_a))
    mult = jnp.where((pos == 0)[None, :, None], 1.0, mult)
    h = linear_scan(a, mult * i * x.astype(f32), h0.astype(f32))
    return h, h[:, -1]


def retention(q, k, v, s0):
    bsz, length = q.shape[0], q.shape[1]
    c = RET_CHUNK if length % RET_CHUNK == 0 else length
    nc = length // c
    log_g = jnp.log1p(-(2.0 ** (-5.0 - jnp.arange(RET_HEADS, dtype=jnp.float32))))
    n = jnp.arange(c, dtype=jnp.float32)
    diff = n[:, None] - n[None, :]
    dmask = jnp.where(diff[None] >= 0, jnp.exp(jnp.maximum(diff, 0.0)[None] * log_g[:, None, None]), 0.0)
    q_decay = jnp.exp((n[None, :] + 1.0) * log_g[:, None])
    k_decay = jnp.exp((c - 1.0 - n[None, :]) * log_g[:, None])
    s_decay = jnp.exp(c * log_g)

    def to_chunks(t):
        return t.reshape(bsz, nc, c, RET_HEADS, t.shape[-1]).transpose(1, 0, 3, 2, 4)

    def step(s, inp):
        qc, kc, vc = inp
        att = jnp.einsum('bhnk,bhmk->bhnm', qc, kc) * dmask
        o = (jnp.einsum('bhnm,bhmv->bhnv', att, vc)
             + jnp.einsum('bhnk,bhkv->bhnv', qc * q_decay[..., None], s))
        s = s * s_decay[:, None, None] + jnp.einsum('bhmk,bhmv->bhkv', kc * k_decay[..., None], vc)
        return s, o

    s_last, o = lax.scan(step, s0, (to_chunks(q), to_chunks(k), to_chunks(v)))
    o = o.transpose(1, 0, 3, 2, 4).reshape(bsz, length, RET_HEADS, RET_DV)
    return o, s_last


def hier_moe(x, w_grp, b_grp, w_exp_router, b_exp_router, w_gate, w_up, w_down):
    f32 = jnp.float32
    t = x.shape[0]
    grp_logits = (x @ w_grp).astype(f32) + b_grp.astype(f32)
    grp_prob = jax.nn.softmax(grp_logits, axis=-1)
    g_idx = jnp.argmax(grp_logits, axis=-1)
    g_val = jnp.take_along_axis(grp_prob, g_idx[:, None], axis=-1)
    exp_logits = jnp.einsum('td,gde->tge', x, w_exp_router).astype(f32) + b_exp_router.astype(f32)
    exp_logits = jnp.take_along_axis(exp_logits, g_idx[:, None, None], axis=1)[:, 0]
    exp_prob = jax.nn.softmax(exp_logits, axis=-1)
    top_v, top_i = lax.top_k(exp_prob, TOP_K_IN_GROUP)
    top_w = g_val * top_v / jnp.sum(top_v, axis=-1, keepdims=True)
    flat = g_idx[:, None].astype(jnp.int32) * EXP_PER_GROUP + top_i
    dense_w = jnp.sum(jax.nn.one_hot(flat, N_EXPERTS, dtype=f32) * top_w[..., None], axis=1)
    dense_w = dense_w.astype(x.dtype).reshape(t, N_GROUPS, EXP_PER_GROUP)
    y = jnp.zeros_like(x)
    for gi in range(N_GROUPS):
        hid = jax.nn.silu(jnp.einsum('td,edf->tef', x, w_gate[gi])) * jnp.einsum('td,edf->tef', x, w_up[gi])
        y = y + jnp.einsum('tef,efd->td', hid * dense_w[:, gi, :, None], w_down[gi])
    return y


def layer_step(x, pos, conv_prev, h_prev, s_prev, mem_k, mem_v, lp):
    f32 = jnp.float32
    bsz, length, _ = x.shape
    dt = x.dtype
    xn = rmsnorm(x, lp['norm_mix'])
    z = xn @ lp['w_in']
    u_x, u_y, q, k, v, g, mq, gates = jnp.split(z, split_points(), axis=-1)
    u_c, conv_new = causal_conv(u_x, conv_prev, lp['conv_w'], lp['conv_b'])
    h, h_last = rg_lru(u_c, pos, h_prev, lp['w_r'], lp['b_r'], lp['w_i'], lp['b_i'], lp['lru_lambda'])
    a_out = (h.astype(dt) * jax.nn.gelu(u_y)) @ lp['w_branch_lru']
    qh = rotary(q.reshape(bsz, length, RET_HEADS, RET_DK).astype(f32), pos)
    kh = rotary(k.reshape(bsz, length, RET_HEADS, RET_DK).astype(f32), pos) * (RET_DK ** -0.5)
    vh = v.reshape(bsz, length, RET_HEADS, RET_DV).astype(f32)
    o, s_new = retention(qh, kh, vh, s_prev.astype(f32))
    o = o * lax.rsqrt(jnp.mean(o * o, axis=-1, keepdims=True) + EPS)
    b_out = (jax.nn.silu(g) * o.reshape(bsz, length, RET_HEADS * RET_DV).astype(dt)) @ lp['w_branch_ret']
    mqh = mq.reshape(bsz, length, MEM_HEADS, MEM_DH)
    logits = jnp.einsum('blhd,bmhd->bhlm', mqh, mem_k).astype(f32) * (MEM_DH ** -0.5)
    p = jax.nn.softmax(logits, axis=-1).astype(dt)
    c = jnp.einsum('bhlm,bmhd->blhd', p, mem_v).reshape(bsz, length, MEM_HEADS * MEM_DH)
    c_out = c @ lp['w_branch_mem']
    g_a, g_b, g_c = jnp.split(gates, N_BRANCHES, axis=-1)
    merged = jax.nn.sigmoid(g_a) * a_out + jax.nn.sigmoid(g_b) * b_out + jax.nn.sigmoid(g_c) * c_out
    x = x + merged @ lp['w_out']
    xn2 = rmsnorm(x, lp['norm_ffn']).reshape(bsz * length, D_MODEL)
    ffn = hier_moe(xn2, lp['w_grp'], lp['b_grp'], lp['w_exp_router'], lp['b_exp_router'],
                   lp['w_gate'], lp['w_up'], lp['w_down'])
    x = x + ffn.reshape(bsz, length, D_MODEL)
    return x, conv_new, h_last, s_new


def setup_inputs(seed: int = 0) -> dict:
    key = jax.random.key(seed)
    ks = jax.random.split(key, 40)
    f32 = jnp.float32

    def nrm(k, shape, scale):
        return jax.random.normal(k, shape, f32) * scale

    u = jax.random.uniform(ks[39], (DEPTH, LRU_WIDTH), f32, 0.9, 0.999)
    sig = u ** (1.0 / LRU_C)
    lam = jnp.log(sig) - jnp.log1p(-sig)
    return {
        'x_prompt': nrm(ks[0], (BATCH, SEQ, D_MODEL), 1.0),
        'x_sample': nrm(ks[1], (DEC_BATCH, DEC_SEQ, D_MODEL), 1.0),
        'mem_prompt': nrm(ks[2], (BATCH, MEM_TOKENS, D_MODEL), 1.0),
        'state_conv': nrm(ks[3], (DEPTH, DEC_BATCH, CONV_WIDTH - 1, LRU_WIDTH), 1.0),
        'state_lru': nrm(ks[4], (DEPTH, DEC_BATCH, LRU_WIDTH), 0.5),
        'state_ret': nrm(ks[5], (DEPTH, DEC_BATCH, RET_HEADS, RET_DK, RET_DV), 0.5),
        'cache_mem_k': nrm(ks[6], (DEPTH, DEC_BATCH, MEM_TOKENS, MEM_HEADS, MEM_DH), 1.0),
        'cache_mem_v': nrm(ks[7], (DEPTH, DEC_BATCH, MEM_TOKENS, MEM_HEADS, MEM_DH), 1.0),
        'norm_mix': 1.0 + nrm(ks[8], (DEPTH, D_MODEL), 0.05),
        'norm_ffn': 1.0 + nrm(ks[9], (DEPTH, D_MODEL), 0.05),
        'norm_mem': 1.0 + nrm(ks[10], (DEPTH, D_MODEL), 0.05),
        'norm_final': 1.0 + nrm(ks[11], (D_MODEL,), 0.05),
        'w_in': nrm(ks[12], (DEPTH, D_MODEL, IN_WIDTH), D_MODEL ** -0.5),
        'w_mem_kv': nrm(ks[13], (DEPTH, D_MODEL, 2 * MEM_HEADS * MEM_DH), D_MODEL ** -0.5),
        'conv_w': nrm(ks[14], (DEPTH, CONV_WIDTH, LRU_WIDTH), CONV_WIDTH ** -0.5),
        'conv_b': nrm(ks[15], (DEPTH, LRU_WIDTH), 0.01),
        'w_r': nrm(ks[16], (DEPTH, LRU_BLOCKS, LRU_BLOCK, LRU_BLOCK), LRU_BLOCK ** -0.5),
        'b_r': nrm(ks[17], (DEPTH, LRU_WIDTH), 0.01),
        'w_i': nrm(ks[18], (DEPTH, LRU_BLOCKS, LRU_BLOCK, LRU_BLOCK), LRU_BLOCK ** -0.5),
        'b_i': nrm(ks[19], (DEPTH, LRU_WIDTH), 0.01),
        'lru_lambda': lam,
        'w_branch_lru': nrm(ks[20], (DEPTH, LRU_WIDTH, D_MODEL), LRU_WIDTH ** -0.5),
        'w_branch_ret': nrm(ks[21], (DEPTH, RET_HEADS * RET_DV, D_MODEL), (RET_HEADS * RET_DV) ** -0.5),
        'w_branch_mem': nrm(ks[22], (DEPTH, MEM_HEADS * MEM_DH, D_MODEL), (MEM_HEADS * MEM_DH) ** -0.5),
        'w_out': nrm(ks[23], (DEPTH, D_MODEL, D_MODEL), D_MODEL ** -0.5),
        'w_grp': nrm(ks[24], (DEPTH, D_MODEL, N_GROUPS), D_MODEL ** -0.5),
        'b_grp': nrm(ks[25], (DEPTH, N_GROUPS), 0.01),
        'w_exp_router': nrm(ks[26], (DEPTH, N_GROUPS, D_MODEL, EXP_PER_GROUP), D_MODEL ** -0.5),
        'b_exp_router': nrm(ks[27], (DEPTH, N_GROUPS, EXP_PER_GROUP), 0.01),
        'w_gate': nrm(ks[28], (DEPTH, N_GROUPS, EXP_PER_GROUP, D_MODEL, EXP_FF), D_MODEL ** -0.5),
        'w_up': nrm(ks[29], (DEPTH, N_GROUPS, EXP_PER_GROUP, D_MODEL, EXP_FF), D_MODEL ** -0.5),
        'w_down': nrm(ks[30], (DEPTH, N_GROUPS, EXP_PER_GROUP, EXP_FF, D_MODEL), EXP_FF ** -0.5),
    }


def reference(x_prompt, x_sample, mem_prompt, state_conv, state_lru, state_ret, cache_mem_k, cache_mem_v,
              norm_mix, norm_ffn, norm_mem, norm_final, w_in, w_mem_kv, conv_w, conv_b, w_r, b_r, w_i, b_i,
              lru_lambda, w_branch_lru, w_branch_ret, w_branch_mem, w_out, w_grp, b_grp, w_exp_router,
              b_exp_router, w_gate, w_up, w_down):
    f32 = jnp.float32
    bp, lp_len = x_prompt.shape[0], x_prompt.shape[1]
    pos_p = jnp.arange(lp_len, dtype=jnp.int32)
    pos_s = PAST_LEN + jnp.arange(x_sample.shape[1], dtype=jnp.int32)
    yp, ys = x_prompt, x_sample
    conv_p, lru_p, ret_p, mk_p, mv_p = [], [], [], [], []
    conv_s, lru_s, ret_s = [], [], []
    for layer in range(DEPTH):
        lp = {
            'norm_mix': norm_mix[layer], 'norm_ffn': norm_ffn[layer], 'w_in': w_in[layer],
            'conv_w': conv_w[layer], 'conv_b': conv_b[layer], 'w_r': w_r[layer], 'b_r': b_r[layer],
            'w_i': w_i[layer], 'b_i': b_i[layer], 'lru_lambda': lru_lambda[layer],
            'w_branch_lru': w_branch_lru[layer], 'w_branch_ret': w_branch_ret[layer],
            'w_branch_mem': w_branch_mem[layer], 'w_out': w_out[layer], 'w_grp': w_grp[layer],
            'b_grp': b_grp[layer], 'w_exp_router': w_exp_router[layer], 'b_exp_router': b_exp_router[layer],
            'w_gate': w_gate[layer], 'w_up': w_up[layer], 'w_down': w_down[layer],
        }
        mkv = rmsnorm(mem_prompt, norm_mem[layer]) @ w_mem_kv[layer]
        mk, mv = jnp.split(mkv, 2, axis=-1)
        mk = mk.reshape(bp, MEM_TOKENS, MEM_HEADS, MEM_DH)
        mv = mv.reshape(bp, MEM_TOKENS, MEM_HEADS, MEM_DH)
        conv0 = jnp.zeros((bp, CONV_WIDTH - 1, LRU_WIDTH), x_prompt.dtype)
        h0 = jnp.zeros((bp, LRU_WIDTH), f32)
        s0 = jnp.zeros((bp, RET_HEADS, RET_DK, RET_DV), f32)
        yp, c_new, h_new, s_new = layer_step(yp, pos_p, conv0, h0, s0, mk, mv, lp)
        conv_p.append(c_new); lru_p.append(h_new); ret_p.append(s_new); mk_p.append(mk); mv_p.append(mv)
        ys, c_new, h_new, s_new = layer_step(ys, pos_s, state_conv[layer], state_lru[layer], state_ret[layer],
                                             cache_mem_k[layer], cache_mem_v[layer], lp)
        conv_s.append(c_new); lru_s.append(h_new); ret_s.append(s_new)
    yp = rmsnorm(yp, norm_final)
    ys = rmsnorm(ys, norm_final)
    dp, ds = x_prompt.dtype, state_ret.dtype
    return (yp, ys,
            jnp.stack(conv_p).astype(dp), jnp.stack(lru_p).astype(dp), jnp.stack(ret_p).astype(dp),
            jnp.stack(mk_p).astype(dp), jnp.stack(mv_p).astype(dp),
            jnp.stack(conv_s).astype(state_conv.dtype), jnp.stack(lru_s).astype(state_lru.dtype),
            jnp.stack(ret_s).astype(ds))
```

```python
import functools
import math

import jax
import jax.numpy as jnp
from jax import lax
from jax.experimental import pallas as pl
from jax.experimental.pallas import tpu as pltpu

F32 = jnp.float32
BF16 = jnp.bfloat16

EPS = 1e-6
LRU_C = 8.0
ROPE_BASE = 10000.0
PAST_LEN = 16384
TOP_K = 2
LANES = 128
SUBLANES = 8
VMEM_LIMIT = 56 * 1024 * 1024


def _dot(a, b):
    return jnp.dot(a, b, preferred_element_type=F32)


def _dot_nt(a, b):
    return lax.dot_general(a, b, (((1,), (1,)), ((), ())), preferred_element_type=F32)


def _dot_tn(a, b):
    return lax.dot_general(a, b, (((0,), (0,)), ((), ())), preferred_element_type=F32)


def _rms(x, g):
    return x * lax.rsqrt(jnp.mean(x * x, axis=-1, keepdims=True) + EPS) * g


def _sigmoid(x):
    return 1.0 / (1.0 + jnp.exp(-x))


def _silu(x):
    return x * _sigmoid(x)


def _gelu_tanh(x):
    return 0.5 * x * (1.0 + jnp.tanh(math.sqrt(2.0 / math.pi) * (x + 0.044715 * (x * x * x))))


def _softplus(x):
    return jnp.maximum(x, 0.0) + jnp.log1p(jnp.exp(-jnp.abs(x)))


def _const_spec(shape):
    nd = len(shape)
    return pl.BlockSpec(shape, lambda *_: (0,) * nd, pipeline_mode=pl.Buffered(1))


def _params(sem):
    return pltpu.CompilerParams(dimension_semantics=sem, vmem_limit_bytes=VMEM_LIMIT)


def _lru_coeffs(uc, wr_ref, br, wi_ref, bi, lam):
    nb, bs = wr_ref.shape[0], wr_ref.shape[1]
    ucb = uc.astype(BF16)
    r_lin = jnp.concatenate([_dot(ucb[:, n * bs:(n + 1) * bs], wr_ref[n]) for n in range(nb)], axis=-1)
    i_lin = jnp.concatenate([_dot(ucb[:, n * bs:(n + 1) * bs], wi_ref[n]) for n in range(nb)], axis=-1)
    r = _sigmoid(r_lin + br)
    i = _sigmoid(i_lin + bi)
    log_a = (-LRU_C) * r * _softplus(-lam)
    a = jnp.exp(log_a)
    mult = jnp.sqrt(1.0 - a * a)
    return a, mult, i


def _scan_rows(a, b):
    rows = a.shape[0]
    row = lax.broadcasted_iota(jnp.int32, a.shape, 0)
    d = 1
    while d < rows:
        a_sh = pltpu.roll(a, d, 0)
        b_sh = pltpu.roll(b, d, 0)
        keep = row >= d
        b = jnp.where(keep, b + a * b_sh, b)
        a = jnp.where(keep, a * a_sh, a)
        d *= 2
    return a, b


def _rotary(t, cos, sin):
    half = t.shape[-1] // 2
    t1, t2 = t[:, :half], t[:, half:]
    return jnp.concatenate([t1 * cos - t2 * sin, t2 * cos + t1 * sin], axis=-1)


def _route(logits, n_groups, per_group):
    col = lax.broadcasted_iota(jnp.int32, logits.shape, 1)
    big = jnp.int32(1 << 20)
    neg = jnp.float32(-jnp.inf)
    gl = jnp.where(col < n_groups, logits, neg)
    gmax = jnp.max(gl, axis=-1, keepdims=True)
    g_idx = jnp.min(jnp.where(gl == gmax, col, big), axis=-1, keepdims=True)
    g_val = 1.0 / jnp.sum(jnp.exp(gl - gmax), axis=-1, keepdims=True)
    lo = n_groups + per_group * g_idx
    el = jnp.where((col >= lo) & (col < lo + per_group), logits, neg)
    m1 = jnp.max(el, axis=-1, keepdims=True)
    i1 = jnp.min(jnp.where(el == m1, col, big), axis=-1, keepdims=True)
    el2 = jnp.where(col == i1, neg, el)
    m2 = jnp.max(el2, axis=-1, keepdims=True)
    i2 = jnp.min(jnp.where(el2 == m2, col, big), axis=-1, keepdims=True)
    e2 = jnp.exp(m2 - m1)
    w1 = g_val / (1.0 + e2)
    w2 = g_val * e2 / (1.0 + e2)
    f1 = (i1 - n_groups).astype(F32)
    f2 = (i2 - n_groups).astype(F32)
    return jnp.where(col == 0, f1, jnp.where(col == 1, f2, jnp.where(col == 2, w1, jnp.where(col == 3, w2, 0.0))))


def _merge_tail(x, merged, wout_ref, gffn, wrt_ref, brt, n_groups, per_group):
    x1 = x + _dot(merged.astype(BF16), wout_ref[...])
    xn2 = _rms(x1, gffn).astype(BF16)
    logits = _dot(xn2, wrt_ref[...]) + brt
    return x1, xn2, _route(logits, n_groups, per_group)


def _lru_prompt_kernel(x_ref, g_ref, wxy_ref, wga_ref, cw_ref, cb_ref, wr_ref, br_ref, wi_ref, bi_ref,
                       lam_ref, wbr_ref, out_ref, conv_ref, h_ref, ubuf, hcar):
    c = pl.program_id(1)
    tl = x_ref.shape[1]
    width = ubuf.shape[1]
    kw = cw_ref.shape[0]
    base = SUBLANES

    @pl.when(c == 0)
    def _():
        ubuf[0:base, :] = jnp.zeros((base, width), F32)
        hcar[...] = jnp.zeros_like(hcar)

    xn = _rms(x_ref[0], g_ref[...]).astype(BF16)
    z = _dot(xn, wxy_ref[...])
    ux, uy = z[:, :width], z[:, width:]
    ubuf[base:base + tl, :] = ux
    uc = cb_ref[...]
    for j in range(kw):
        uc = uc + cw_ref[j:j + 1, :] * ubuf[base - (kw - 1) + j: base - (kw - 1) + j + tl, :]
    a, mult, gate = _lru_coeffs(uc, wr_ref, br_ref[...], wi_ref, bi_ref[...], lam_ref[...])
    row = lax.broadcasted_iota(jnp.int32, a.shape, 0)
    mult = jnp.where(row + c * tl == 0, 1.0, mult)
    a_cum, h = _scan_rows(a, mult * gate * uc)
    h = h + a_cum * hcar[...]
    hcar[...] = h[tl - 1:tl, :]
    tail = ubuf[base + tl - (kw - 1): base + tl, :]
    ubuf[base - (kw - 1): base, :] = tail
    a_out = _dot((h * _gelu_tanh(uy)).astype(BF16), wbr_ref[...])
    ga = _dot(xn, wga_ref[...])
    out_ref[0] = (_sigmoid(ga) * a_out).astype(out_ref.dtype)

    @pl.when(c == pl.num_programs(1) - 1)
    def _():
        conv_ref[0] = tail
        h_ref[0] = h[tl - 1:tl, :]


def _ret_prompt_kernel(x_ref, g_ref, wq_ref, wk_ref, wv_ref, wg_ref, wgb_ref, cos_ref, sin_ref, wbr_ref,
                       out_ref, s_out_ref, s_scr, y_scr, *, heads, chunk):
    c = pl.program_id(1)
    tl = x_ref.shape[1]
    dk = wq_ref.shape[1] // heads
    dv = wv_ref.shape[1] // heads

    @pl.when(c == 0)
    def _():
        s_scr[...] = jnp.zeros_like(s_scr)

    xn = _rms(x_ref[0], g_ref[...]).astype(BF16)
    q = _dot(xn, wq_ref[...])
    k = _dot(xn, wk_ref[...])
    v = _dot(xn, wv_ref[...])
    g = _dot(xn, wg_ref[...])
    cos, sin = cos_ref[...], sin_ref[...]
    n_i = lax.broadcasted_iota(jnp.int32, (chunk, chunk), 0)
    m_i = lax.broadcasted_iota(jnp.int32, (chunk, chunk), 1)
    diff = (n_i - m_i).astype(F32)
    rowk = lax.broadcasted_iota(jnp.int32, (chunk, dk), 0).astype(F32)
    for h in range(heads):
        log_g = math.log1p(-(2.0 ** (-5.0 - h)))
        dmask = jnp.where(diff >= 0, jnp.exp(jnp.maximum(diff, 0.0) * log_g), 0.0)
        q_decay = jnp.exp((rowk + 1.0) * log_g)
        k_decay = jnp.exp((chunk - 1.0 - rowk) * log_g)
        s_decay = math.exp(chunk * log_g)
        for sub in range(tl // chunk):
            r0 = sub * chunk
            cs, sn = cos[r0:r0 + chunk], sin[r0:r0 + chunk]
            qr = _rotary(q[r0:r0 + chunk, h * dk:(h + 1) * dk], cs, sn)
            kr = _rotary(k[r0:r0 + chunk, h * dk:(h + 1) * dk], cs, sn) * (dk ** -0.5)
            vh = v[r0:r0 + chunk, h * dv:(h + 1) * dv].astype(BF16)
            s = s_scr[h]
            att = _dot_nt(qr.astype(BF16), kr.astype(BF16)) * dmask
            o = _dot(att.astype(BF16), vh) + _dot((qr * q_decay).astype(BF16), s.astype(BF16))
            s_scr[h] = s * s_decay + _dot_tn((kr * k_decay).astype(BF16), vh)
            o = o * lax.rsqrt(jnp.mean(o * o, axis=-1, keepdims=True) + EPS)
            gh = g[r0:r0 + chunk, h * dv:(h + 1) * dv]
            y_scr[r0:r0 + chunk, h * dv:(h + 1) * dv] = (_silu(gh) * o).astype(BF16)
    b_out = _dot(y_scr[...], wbr_ref[...])
    gb = _dot(xn, wgb_ref[...])
    out_ref[0] = (_sigmoid(gb) * b_out).astype(out_ref.dtype)

    @pl.when(c == pl.num_programs(1) - 1)
    def _():
        s_out_ref[0] = s_scr[...]


def _memkv_kernel(m_ref, g_ref, w_ref, k_ref, v_ref):
    width = k_ref.shape[-1]
    mn = _rms(m_ref[0], g_ref[...]).astype(BF16)
    kv = _dot(mn, w_ref[...])
    k_ref[0] = kv[:, :width]
    v_ref[0] = kv[:, width:]


def _attn_merge_prompt_kernel(x_ref, ag_ref, bg_ref, mk_ref, mv_ref, g_ref, wmq_ref, wgc_ref, wbm_ref,
                              wout_ref, gffn_ref, wrt_ref, brt_ref, x1_ref, xn2_ref, route_ref,
                              *, heads, n_groups, per_group):
    x = x_ref[0]
    xn = _rms(x, g_ref[...]).astype(BF16)
    mq = _dot(xn, wmq_ref[...]).astype(BF16)
    dh = mq.shape[1] // heads
    mk = mk_ref[0].astype(BF16)
    mv = mv_ref[0].astype(BF16)
    cs = []
    for h in range(heads):
        hs = slice(h * dh, (h + 1) * dh)
        lg = _dot_nt(mq[:, hs], mk[:, hs]) * (dh ** -0.5)
        p = jnp.exp(lg - jnp.max(lg, axis=-1, keepdims=True))
        den = jnp.sum(p, axis=-1, keepdims=True)
        cs.append(_dot(p.astype(BF16), mv[:, hs]) / den)
    c_out = _dot(jnp.concatenate(cs, axis=-1).astype(BF16), wbm_ref[...])
    gc = _dot(xn, wgc_ref[...])
    merged = _sigmoid(gc) * c_out + ag_ref[0].astype(F32) + bg_ref[0].astype(F32)
    x1, xn2, route = _merge_tail(x, merged, wout_ref, gffn_ref[...], wrt_ref, brt_ref[...], n_groups, per_group)
    x1_ref[0] = x1
    xn2_ref[0] = xn2
    route_ref[0] = route


def _proj_kernel(x_ref, g_ref, w_ref, z_ref):
    xn = _rms(x_ref[...], g_ref[...]).astype(BF16)
    z_ref[...] = _dot(xn, w_ref[...])


def _lru_sample_kernel(ux_ref, uy_ref, ga_ref, cprev_ref, hprev_ref, cw_ref, cb_ref, wr_ref, br_ref, wi_ref,
                       bi_ref, lam_ref, wbr_ref, out_ref, conv_ref, h_ref, *, at_start):
    kw = cw_ref.shape[0]
    ux = ux_ref[...]
    uc = cb_ref[...] + cw_ref[kw - 1:kw, :] * ux
    for j in range(kw - 1):
        uc = uc + cw_ref[j:j + 1, :] * cprev_ref[:, j, :]
    a, mult, gate = _lru_coeffs(uc, wr_ref, br_ref[...], wi_ref, bi_ref[...], lam_ref[...])
    if at_start:
        mult = jnp.ones_like(mult)
    h = a * hprev_ref[...] + mult * gate * uc
    h_ref[...] = h
    for j in range(kw - 2):
        conv_ref[:, j, :] = cprev_ref[:, j + 1, :]
    conv_ref[:, kw - 2, :] = ux
    a_out = _dot((h * _gelu_tanh(uy_ref[...])).astype(BF16), wbr_ref[...])
    out_ref[...] = _sigmoid(ga_ref[...]) * a_out


def _rot_sample_kernel(q_ref, k_ref, cos_ref, sin_ref, qo_ref, ko_ref, *, heads):
    dk = q_ref.shape[1] // heads
    cos, sin = cos_ref[...], sin_ref[...]
    for h in range(heads):
        hs = slice(h * dk, (h + 1) * dk)
        qo_ref[:, hs] = _rotary(q_ref[:, hs], cos, sin)
        ko_ref[:, hs] = _rotary(k_ref[:, hs], cos, sin) * (dk ** -0.5)


def _ret_sample_kernel(qt_ref, kt_ref, q_ref, k_ref, v_ref, s_ref, o_ref, s_out_ref, *, heads):
    bb = q_ref.shape[1]
    dk = s_ref.shape[2]
    dv = s_ref.shape[3]
    for j in range(bb):
        for h in range(heads):
            decay = 1.0 - 2.0 ** (-5.0 - h)
            qcol = qt_ref[0, h, :, j:j + 1]
            kcol = kt_ref[0, h, :, j:j + 1]
            qrow = q_ref[0, j:j + 1, h * dk:(h + 1) * dk]
            krow = k_ref[0, j:j + 1, h * dk:(h + 1) * dk]
            vrow = v_ref[0, j:j + 1, h * dv:(h + 1) * dv]
            s = s_ref[j, h]
            att = jnp.sum(qrow * krow, axis=-1, keepdims=True)
            o = att * vrow + jnp.sum((qcol * decay) * s, axis=0, keepdims=True)
            o_ref[0, j:j + 1, h * dv:(h + 1) * dv] = o
            s_out_ref[j, h] = s * decay + kcol * vrow


def _attn_sample_kernel(mq_ref, k_ref, v_ref, c_ref, *, heads):
    bb = mq_ref.shape[0]
    dh = mq_ref.shape[1] // heads
    mq = mq_ref[...].astype(BF16)
    for j in range(bb):
        kj = k_ref[j].astype(BF16)
        vj = v_ref[j].astype(BF16)
        for h in range(heads):
            hs = slice(h * dh, (h + 1) * dh)
            lg = _dot_nt(mq[:, hs], kj[:, hs])[j:j + 1, :] * (dh ** -0.5)
            p = jnp.exp(lg - jnp.max(lg, axis=-1, keepdims=True))
            den = jnp.sum(p, axis=-1, keepdims=True)
            p8 = jnp.broadcast_to(p, (SUBLANES, p.shape[1])).astype(BF16)
            c_ref[j:j + 1, hs] = _dot(p8, vj[:, hs])[0:1, :] / den


def _merge_sample_kernel(x_ref, ag_ref, o_ref, g_ref, c_ref, gb_ref, gc_ref, wbr_ref, wbm_ref, wout_ref,
                         gffn_ref, wrt_ref, brt_ref, x1_ref, xn2_ref, route_ref, *, heads, n_groups, per_group):
    dv = o_ref.shape[1] // heads
    ys = []
    for h in range(heads):
        o = o_ref[:, h * dv:(h + 1) * dv]
        o = o * lax.rsqrt(jnp.mean(o * o, axis=-1, keepdims=True) + EPS)
        ys.append((_silu(g_ref[:, h * dv:(h + 1) * dv]) * o).astype(BF16))
    b_out = _dot(jnp.concatenate(ys, axis=-1), wbr_ref[...])
    c_out = _dot(c_ref[...].astype(BF16), wbm_ref[...])
    merged = ag_ref[...] + _sigmoid(gb_ref[...]) * b_out + _sigmoid(gc_ref[...]) * c_out
    x1, xn2, route = _merge_tail(x_ref[...], merged, wout_ref, gffn_ref[...], wrt_ref, brt_ref[...],
                                 n_groups, per_group)
    x1_ref[...] = x1
    xn2_ref[...] = xn2
    route_ref[...] = route


def _expert_kernel(te_ref, nu_ref, xs_ref, wg_ref, wu_ref, wd_ref, ys_ref):
    i = pl.program_id(0)

    @pl.when(i < nu_ref[0])
    def _():
        x = xs_ref[...]
        hg = _dot(x, wg_ref[0].astype(BF16))
        hu = _dot(x, wu_ref[0].astype(BF16))
        hid = (_silu(hg) * hu).astype(BF16)
        ys_ref[...] = _dot(hid, wd_ref[0].astype(BF16))

    @pl.when(i >= nu_ref[0])
    def _():
        ys_ref[...] = jnp.zeros_like(ys_ref)


def _final_kernel(x1_ref, yg_ref, route_ref, g_ref, out_ref):
    d = x1_ref.shape[1]
    w1 = route_ref[:, 2:3]
    w2 = route_ref[:, 3:4]
    x2 = x1_ref[...] + (w1 * yg_ref[:, :d] + w2 * yg_ref[:, d:])
    out_ref[...] = _rms(x2, g_ref[...])


def _tile(n, target):
    t = min(n, target)
    while n % t:
        t //= 2
    return t


def _row(v):
    return v.reshape(1, -1).astype(F32)


def _prep_layer(p, dims):
    d, w, hk, hv, hm = dims['d'], dims['w'], dims['hk'], dims['hv'], dims['hm']
    w_in = p['w_in']
    cuts = [0, w, 2 * w, 2 * w + hk, 2 * w + 2 * hk, 2 * w + 2 * hk + hv, 2 * w + 2 * hk + 2 * hv,
            2 * w + 2 * hk + 2 * hv + hm, w_in.shape[1]]
    seg = [w_in[:, cuts[i]:cuts[i + 1]].astype(BF16) for i in range(8)]
    gates = seg[7]
    g, e = p['w_exp_router'].shape[0], p['w_exp_router'].shape[2]
    w_rt = jnp.concatenate([p['w_grp'], jnp.moveaxis(p['w_exp_router'], 0, 1).reshape(d, g * e)], axis=1)
    w_rt = jnp.pad(w_rt, ((0, 0), (0, LANES - w_rt.shape[1]))).astype(BF16)
    b_rt = jnp.pad(jnp.concatenate([p['b_grp'], p['b_exp_router'].reshape(-1)]), (0, LANES - g - g * e))
    return dict(
        w_in=w_in.astype(BF16),
        w_xy=jnp.concatenate([seg[0], seg[1]], axis=1), w_q=seg[2], w_k=seg[3], w_v=seg[4], w_g=seg[5],
        w_mq=seg[6], w_ga=gates[:, :d], w_gb=gates[:, d:2 * d], w_gc=gates[:, 2 * d:],
        norm_mix=_row(p['norm_mix']), norm_ffn=_row(p['norm_ffn']), norm_mem=_row(p['norm_mem']),
        w_mem_kv=p['w_mem_kv'].astype(BF16),
        conv_w=p['conv_w'].astype(F32), conv_b=_row(p['conv_b']),
        w_r=p['w_r'].astype(BF16), b_r=_row(p['b_r']), w_i=p['w_i'].astype(BF16), b_i=_row(p['b_i']),
        lam=_row(p['lru_lambda']),
        w_br_lru=p['w_branch_lru'].astype(BF16), w_br_ret=p['w_branch_ret'].astype(BF16),
        w_br_mem=p['w_branch_mem'].astype(BF16), w_out=p['w_out'].astype(BF16),
        w_rt=w_rt, b_rt=_row(b_rt),
        w_gate=p['w_gate'].reshape((g * e,) + p['w_gate'].shape[2:]),
        w_up=p['w_up'].reshape((g * e,) + p['w_up'].shape[2:]),
        w_down=p['w_down'].reshape((g * e,) + p['w_down'].shape[2:]),
    )


def _rope_tables(pos, dk):
    half = dk // 2
    inv = ROPE_BASE ** (-jnp.linspace(0.0, 1.0, half, dtype=F32))
    ang = pos.astype(F32)[:, None] * inv[None, :]
    return jnp.cos(ang), jnp.sin(ang)


def _prompt_mixer(x, mem, lw, dims):
    b, l, d = x.shape
    w, heads, dk, dv = dims['w'], dims['heads'], dims['dk'], dims['dv']
    kw = lw['conv_w'].shape[0]
    tl = _tile(l, 256)
    grid = (b, l // tl)
    xspec = pl.BlockSpec((1, tl, d), lambda i, c: (i, c, 0))
    sem = ("parallel", "arbitrary")

    a_g, conv_new, h_last = pl.pallas_call(
        _lru_prompt_kernel,
        grid=grid,
        in_specs=[xspec, _const_spec((1, d)), _const_spec((d, 2 * w)), _const_spec((d, d)),
                  _const_spec((kw, w)), _const_spec((1, w)), _const_spec(lw['w_r'].shape), _const_spec((1, w)),
                  _const_spec(lw['w_i'].shape), _const_spec((1, w)), _const_spec((1, w)), _const_spec((w, d))],
        out_specs=[pl.BlockSpec((1, tl, d), lambda i, c: (i, c, 0)),
                   pl.BlockSpec((1, kw - 1, w), lambda i, c: (i, 0, 0)),
                   pl.BlockSpec((1, 1, w), lambda i, c: (i, 0, 0))],
        out_shape=[jax.ShapeDtypeStruct((b, l, d), BF16), jax.ShapeDtypeStruct((b, kw - 1, w), F32),
                   jax.ShapeDtypeStruct((b, 1, w), F32)],
        scratch_shapes=[pltpu.VMEM((tl + SUBLANES, w), F32), pltpu.VMEM((1, w), F32)],
        compiler_params=_params(sem),
    )(x, lw['norm_mix'], lw['w_xy'], lw['w_ga'], lw['conv_w'], lw['conv_b'], lw['w_r'], lw['b_r'],
      lw['w_i'], lw['b_i'], lw['lam'], lw['w_br_lru'])

    chunk = _tile(tl, 128)
    cos, sin = _rope_tables(jnp.arange(l, dtype=jnp.int32), dk)
    half = dk // 2
    b_g, s_new = pl.pallas_call(
        functools.partial(_ret_prompt_kernel, heads=heads, chunk=chunk),
        grid=grid,
        in_specs=[xspec, _const_spec((1, d)), _const_spec((d, heads * dk)), _const_spec((d, heads * dk)),
                  _const_spec((d, heads * dv)), _const_spec((d, heads * dv)), _const_spec((d, d)),
                  pl.BlockSpec((tl, half), lambda i, c: (c, 0)), pl.BlockSpec((tl, half), lambda i, c: (c, 0)),
                  _const_spec((heads * dv, d))],
        out_specs=[pl.BlockSpec((1, tl, d), lambda i, c: (i, c, 0)),
                   pl.BlockSpec((1, heads, dk, dv), lambda i, c: (i, 0, 0, 0))],
        out_shape=[jax.ShapeDtypeStruct((b, l, d), BF16), jax.ShapeDtypeStruct((b, heads, dk, dv), F32)],
        scratch_shapes=[pltpu.VMEM((heads, dk, dv), F32), pltpu.VMEM((tl, heads * dv), BF16)],
        compiler_params=_params(sem),
    )(x, lw['norm_mix'], lw['w_q'], lw['w_k'], lw['w_v'], lw['w_g'], lw['w_gb'], cos, sin, lw['w_br_ret'])

    m = mem.shape[1]
    hm = dims['hm']
    mk, mv = pl.pallas_call(
        _memkv_kernel,
        grid=(b,),
        in_specs=[pl.BlockSpec((1, m, d), lambda i: (i, 0, 0)), _const_spec((1, d)), _const_spec((d, 2 * hm))],
        out_specs=[pl.BlockSpec((1, m, hm), lambda i: (i, 0, 0)), pl.BlockSpec((1, m, hm), lambda i: (i, 0, 0))],
        out_shape=[jax.ShapeDtypeStruct((b, m, hm), F32), jax.ShapeDtypeStruct((b, m, hm), F32)],
        compiler_params=_params(("parallel",)),
    )(mem, lw['norm_mem'], lw['w_mem_kv'])

    x1, xn2, route = pl.pallas_call(
        functools.partial(_attn_merge_prompt_kernel, heads=dims['mheads'], n_groups=dims['g'],
                          per_group=dims['e']),
        grid=grid,
        in_specs=[xspec, xspec, xspec,
                  pl.BlockSpec((1, m, hm), lambda i, c: (i, 0, 0)), pl.BlockSpec((1, m, hm), lambda i, c: (i, 0, 0)),
                  _const_spec((1, d)), _const_spec((d, hm)), _const_spec((d, d)), _const_spec((hm, d)),
                  _const_spec((d, d)), _const_spec((1, d)), _const_spec((d, LANES)), _const_spec((1, LANES))],
        out_specs=[xspec, xspec, pl.BlockSpec((1, tl, LANES), lambda i, c: (i, c, 0))],
        out_shape=[jax.ShapeDtypeStruct((b, l, d), F32), jax.ShapeDtypeStruct((b, l, d), BF16),
                   jax.ShapeDtypeStruct((b, l, LANES), F32)],
        compiler_params=_params(("parallel", "parallel")),
    )(x, a_g, b_g, mk, mv, lw['norm_mix'], lw['w_mq'], lw['w_gc'], lw['w_br_mem'], lw['w_out'],
      lw['norm_ffn'], lw['w_rt'], lw['b_rt'])
    return x1, xn2, route, conv_new, h_last[:, 0], s_new, mk, mv


def _sample_mixer(x, conv_prev, h_prev, s_prev, mem_k, mem_v, lw, dims):
    n, d = x.shape
    w, heads, dk, dv, hm = dims['w'], dims['heads'], dims['dk'], dims['dv'], dims['hm']
    hk, hv = heads * dk, heads * dv
    kw = lw['conv_w'].shape[0]
    n_in = lw['w_in'].shape[1]
    tn = _tile(n_in, 1024)
    z = pl.pallas_call(
        _proj_kernel,
        grid=(n_in // tn,),
        in_specs=[_const_spec((n, d)), _const_spec((1, d)), pl.BlockSpec((d, tn), lambda j: (0, j))],
        out_specs=pl.BlockSpec((n, tn), lambda j: (0, j)),
        out_shape=jax.ShapeDtypeStruct((n, n_in), F32),
        compiler_params=_params(("parallel",)),
    )(x, lw['norm_mix'], lw['w_in'])
    o0 = 0
    parts = []
    for sz in (w, w, hk, hk, hv, hv, hm, d, d, d):
        parts.append(z[:, o0:o0 + sz])
        o0 += sz
    ux, uy, q, k, v, g, mq, ga, gb, gc = parts

    a_g, conv_new, h_new = pl.pallas_call(
        functools.partial(_lru_sample_kernel, at_start=(PAST_LEN == 0)),
        out_shape=[jax.ShapeDtypeStruct((n, d), F32), jax.ShapeDtypeStruct((n, kw - 1, w), F32),
                   jax.ShapeDtypeStruct((n, w), F32)],
        compiler_params=pltpu.CompilerParams(vmem_limit_bytes=VMEM_LIMIT),
    )(ux, uy, ga, conv_prev, h_prev, lw['conv_w'], lw['conv_b'], lw['w_r'], lw['b_r'], lw['w_i'], lw['b_i'],
      lw['lam'], lw['w_br_lru'])

    cos, sin = _rope_tables(PAST_LEN + jnp.arange(1, dtype=jnp.int32), dk)
    qr, kr = pl.pallas_call(
        functools.partial(_rot_sample_kernel, heads=heads),
        out_shape=[jax.ShapeDtypeStruct((n, hk), F32), jax.ShapeDtypeStruct((n, hk), F32)],
    )(q, k, cos, sin)

    bb = _tile(n, 4)

    def cols(t):
        return t.reshape(n // bb, bb, heads, dk).transpose(0, 2, 3, 1)

    def rows(t):
        return t.reshape(n // bb, bb, t.shape[1])

    o, s_new = pl.pallas_call(
        functools.partial(_ret_sample_kernel, heads=heads),
        grid=(n // bb,),
        in_specs=[pl.BlockSpec((1, heads, dk, bb), lambda i: (i, 0, 0, 0)),
                  pl.BlockSpec((1, heads, dk, bb), lambda i: (i, 0, 0, 0)),
                  pl.BlockSpec((1, bb, hk), lambda i: (i, 0, 0)), pl.BlockSpec((1, bb, hk), lambda i: (i, 0, 0)),
                  pl.BlockSpec((1, bb, hv), lambda i: (i, 0, 0)),
                  pl.BlockSpec((bb, heads, dk, dv), lambda i: (i, 0, 0, 0))],
        out_specs=[pl.BlockSpec((1, bb, hv), lambda i: (i, 0, 0)),
                   pl.BlockSpec((bb, heads, dk, dv), lambda i: (i, 0, 0, 0))],
        out_shape=[jax.ShapeDtypeStruct((n // bb, bb, hv), F32), jax.ShapeDtypeStruct(s_prev.shape, F32)],
        compiler_params=_params(("parallel",)),
    )(cols(qr), cols(kr), rows(qr), rows(kr), rows(v), s_prev)
    o = o.reshape(n, hv)
    bb = _tile(n, SUBLANES)

    m = mem_k.shape[1]
    c = pl.pallas_call(
        functools.partial(_attn_sample_kernel, heads=dims['mheads']),
        grid=(n // bb,),
        in_specs=[pl.BlockSpec((bb, hm), lambda i: (i, 0)),
                  pl.BlockSpec((bb, m, hm), lambda i: (i, 0, 0)), pl.BlockSpec((bb, m, hm), lambda i: (i, 0, 0))],
        out_specs=pl.BlockSpec((bb, hm), lambda i: (i, 0)),
        out_shape=jax.ShapeDtypeStruct((n, hm), F32),
        compiler_params=_params(("parallel",)),
    )(mq, mem_k.reshape(n, m, hm), mem_v.reshape(n, m, hm))

    x1, xn2, route = pl.pallas_call(
        functools.partial(_merge_sample_kernel, heads=heads, n_groups=dims['g'], per_group=dims['e']),
        out_shape=[jax.ShapeDtypeStruct((n, d), F32), jax.ShapeDtypeStruct((n, d), BF16),
                   jax.ShapeDtypeStruct((n, LANES), F32)],
        compiler_params=pltpu.CompilerParams(vmem_limit_bytes=VMEM_LIMIT),
    )(x, a_g, o, g, c, gb, gc, lw['w_br_ret'], lw['w_br_mem'], lw['w_out'], lw['norm_ffn'], lw['w_rt'],
      lw['b_rt'])
    return x1, xn2, route, conv_new, h_new, s_new


def _moe(xn2_parts, route_parts, lw, dims, tm=256):
    xn2 = jnp.concatenate(xn2_parts, axis=0)
    t, d = xn2.shape
    n_exp = dims['g'] * dims['e']
    ff = lw['w_gate'].shape[2]
    flat_e = jnp.concatenate([r[:, :TOP_K] for r in route_parts], axis=0).astype(jnp.int32).reshape(-1)
    onehot = (flat_e[:, None] == jnp.arange(n_exp, dtype=jnp.int32)[None, :]).astype(jnp.int32)
    csum = jnp.cumsum(onehot, axis=0)
    rank = jnp.sum(csum * onehot, axis=1) - 1
    counts = csum[-1]
    padded = ((counts + tm - 1) // tm) * tm
    ends = jnp.cumsum(padded)
    dest = (ends - padded)[flat_e] + rank
    n_slots = TOP_K * t + n_exp * tm
    n_slots = ((n_slots + tm - 1) // tm) * tm
    n_tiles = n_slots // tm
    src = jnp.zeros((n_slots,), jnp.int32).at[dest].set(jnp.arange(TOP_K * t, dtype=jnp.int32) // TOP_K)
    n_used = (ends[-1] // tm).astype(jnp.int32)
    tile_e = jnp.searchsorted(ends, jnp.arange(n_tiles, dtype=jnp.int32) * tm, side='right').astype(jnp.int32)
    last_e = jnp.take(tile_e, n_used - 1)
    tile_e = jnp.where(jnp.arange(n_tiles) < n_used, tile_e, last_e)
    xs = jnp.take(xn2, src, axis=0)

    def row_map(i, te, nu):
        return (jnp.minimum(i, nu[0] - 1), 0)

    ys = pl.pallas_call(
        _expert_kernel,
        grid_spec=pltpu.PrefetchScalarGridSpec(
            num_scalar_prefetch=2,
            grid=(n_tiles,),
            in_specs=[pl.BlockSpec((tm, d), row_map),
                      pl.BlockSpec((1, d, ff), lambda i, te, nu: (te[i], 0, 0)),
                      pl.BlockSpec((1, d, ff), lambda i, te, nu: (te[i], 0, 0)),
                      pl.BlockSpec((1, ff, d), lambda i, te, nu: (te[i], 0, 0))],
            out_specs=pl.BlockSpec((tm, d), lambda i, te, nu: (i, 0)),
        ),
        out_shape=jax.ShapeDtypeStruct((n_slots, d), F32),
        compiler_params=_params(("arbitrary",)),
    )(tile_e, n_used.reshape(1), xs, lw['w_gate'], lw['w_up'], lw['w_down'])
    out, o0 = [], 0
    for part in xn2_parts:
        n_rows = part.shape[0]
        out.append(jnp.take(ys, dest[o0:o0 + TOP_K * n_rows], axis=0).reshape(n_rows, TOP_K * d))
        o0 += TOP_K * n_rows
    return out


def _final(x1, yg, route, g):
    t, d = x1.shape
    tl = _tile(t, 512)
    return pl.pallas_call(
        _final_kernel,
        grid=(t // tl,),
        in_specs=[pl.BlockSpec((tl, d), lambda i: (i, 0)), pl.BlockSpec((tl, TOP_K * d), lambda i: (i, 0)),
                  pl.BlockSpec((tl, LANES), lambda i: (i, 0)), _const_spec((1, d))],
        out_specs=pl.BlockSpec((tl, d), lambda i: (i, 0)),
        out_shape=jax.ShapeDtypeStruct((t, d), F32),
        compiler_params=_params(("parallel",)),
    )(x1, yg, route, g)


def kernel(x_prompt, x_sample, mem_prompt, state_conv, state_lru, state_ret, cache_mem_k, cache_mem_v, norm_mix, norm_ffn, norm_mem, norm_final, w_in, w_mem_kv, conv_w, conv_b, w_r, b_r, w_i, b_i, lru_lambda, w_branch_lru, w_branch_ret, w_branch_mem, w_out, w_grp, b_grp, w_exp_router, b_exp_router, w_gate, w_up, w_down):
    depth = w_in.shape[0]
    assert depth == 1, "the two request groups are chained per layer only for a single-layer trunk"
    assert x_sample.shape[1] == 1
    bp, lp, d = x_prompt.shape
    ns = x_sample.shape[0]
    heads, dk, dv = state_ret.shape[2], state_ret.shape[3], state_ret.shape[4]
    mheads, mdh = cache_mem_k.shape[3], cache_mem_k.shape[4]
    dims = dict(d=d, w=state_lru.shape[2], heads=heads, dk=dk, dv=dv, hk=heads * dk, hv=heads * dv,
                mheads=mheads, hm=mheads * mdh, g=w_exp_router.shape[1], e=w_exp_router.shape[3])
    layer = 0
    p = dict(norm_mix=norm_mix[layer], norm_ffn=norm_ffn[layer], norm_mem=norm_mem[layer], w_in=w_in[layer],
             w_mem_kv=w_mem_kv[layer], conv_w=conv_w[layer], conv_b=conv_b[layer], w_r=w_r[layer], b_r=b_r[layer],
             w_i=w_i[layer], b_i=b_i[layer], lru_lambda=lru_lambda[layer], w_branch_lru=w_branch_lru[layer],
             w_branch_ret=w_branch_ret[layer], w_branch_mem=w_branch_mem[layer], w_out=w_out[layer],
             w_grp=w_grp[layer], b_grp=b_grp[layer], w_exp_router=w_exp_router[layer],
             b_exp_router=b_exp_router[layer], w_gate=w_gate[layer], w_up=w_up[layer], w_down=w_down[layer])
    lw = _prep_layer(p, dims)

    x1p, xn2p, route_p, conv_p, lru_p, ret_p, mk, mv = _prompt_mixer(x_prompt, mem_prompt, lw, dims)
    x1s, xn2s, route_s, conv_s, lru_s, ret_s = _sample_mixer(
        x_sample[:, 0], state_conv[layer], state_lru[layer], state_ret[layer], cache_mem_k[layer],
        cache_mem_v[layer], lw, dims)

    tp = bp * lp
    route_p = route_p.reshape(tp, LANES)
    yg_p, yg_s = _moe([xn2p.reshape(tp, d), xn2s], [route_p, route_s], lw, dims)
    yp = _final(x1p.reshape(tp, d), yg_p, route_p, _row(norm_final)).reshape(bp, lp, d)
    ys = _final(x1s, yg_s, route_s, _row(norm_final)).reshape(ns, 1, d)
    mshape = (1, bp, mem_prompt.shape[1], mheads, mdh)
    return (yp, ys, conv_p[None], lru_p[None], ret_p[None], mk.reshape(mshape), mv.reshape(mshape),
            conv_s[None], lru_s[None], ret_s[None])
```

```python
import functools
import math

import jax
import jax.numpy as jnp
from jax import lax
from jax.experimental import pallas as pl
from jax.experimental.pallas import tpu as pltpu

F32 = jnp.float32
BF16 = jnp.bfloat16

EPS = 1e-6
LRU_C = 8.0
ROPE_BASE = 10000.0
PAST_LEN = 16384
TOP_K = 2
LANES = 128
SUBLANES = 8
VMEM_LIMIT = 56 * 1024 * 1024


def _dot(a, b):
    return jnp.dot(a, b, preferred_element_type=F32)


def _dot_nt(a, b):
    return lax.dot_general(a, b, (((1,), (1,)), ((), ())), preferred_element_type=F32)


def _dot_tn(a, b):
    return lax.dot_general(a, b, (((0,), (0,)), ((), ())), preferred_element_type=F32)


def _rms(x, g):
    return x * lax.rsqrt(jnp.mean(x * x, axis=-1, keepdims=True) + EPS) * g


def _sigmoid(x):
    return 1.0 / (1.0 + jnp.exp(-x))


def _silu(x):
    return x * _sigmoid(x)


def _gelu_tanh(x):
    return 0.5 * x * (1.0 + jnp.tanh(math.sqrt(2.0 / math.pi) * (x + 0.044715 * (x * x * x))))


def _softplus(x):
    return jnp.maximum(x, 0.0) + jnp.log1p(jnp.exp(-jnp.abs(x)))


def _const_spec(shape):
    nd = len(shape)
    return pl.BlockSpec(shape, lambda *_: (0,) * nd, pipeline_mode=pl.Buffered(1))


def _params(sem):
    return pltpu.CompilerParams(dimension_semantics=sem, vmem_limit_bytes=VMEM_LIMIT)


def _lru_coeffs(uc, wr_ref, br, wi_ref, bi, lam):
    nb, bs = wr_ref.shape[0], wr_ref.shape[1]
    ucb = uc.astype(BF16)
    r_lin = jnp.concatenate([_dot(ucb[:, n * bs:(n + 1) * bs], wr_ref[n]) for n in range(nb)], axis=-1)
    i_lin = jnp.concatenate([_dot(ucb[:, n * bs:(n + 1) * bs], wi_ref[n]) for n in range(nb)], axis=-1)
    r = _sigmoid(r_lin + br)
    i = _sigmoid(i_lin + bi)
    log_a = (-LRU_C) * r * _softplus(-lam)
    a = jnp.exp(log_a)
    mult = jnp.sqrt(1.0 - a * a)
    return a, mult, i


def _scan_rows(a, b):
    rows = a.shape[0]
    row = lax.broadcasted_iota(jnp.int32, a.shape, 0)
    d = 1
    while d < rows:
        a_sh = pltpu.roll(a, d, 0)
        b_sh = pltpu.roll(b, d, 0)
        keep = row >= d
        b = jnp.where(keep, b + a * b_sh, b)
        a = jnp.where(keep, a * a_sh, a)
        d *= 2
    return a, b


def _rotary(t, cos, sin):
    half = t.shape[-1] // 2
    t1, t2 = t[:, :half], t[:, half:]
    return jnp.concatenate([t1 * cos - t2 * sin, t2 * cos + t1 * sin], axis=-1)


def _route(logits, run, n_groups, per_group):
    col = lax.broadcasted_iota(jnp.int32, logits.shape, 1)
    big = jnp.int32(1 << 20)
    neg = jnp.float32(-jnp.inf)
    gl = jnp.where(col < n_groups, logits, neg)
    gmax = jnp.max(gl, axis=-1, keepdims=True)
    g_idx = jnp.min(jnp.where(gl == gmax, col, big), axis=-1, keepdims=True)
    g_val = 1.0 / jnp.sum(jnp.exp(gl - gmax), axis=-1, keepdims=True)
    lo = n_groups + per_group * g_idx
    el = jnp.where((col >= lo) & (col < lo + per_group), logits, neg)
    m1 = jnp.max(el, axis=-1, keepdims=True)
    i1 = jnp.min(jnp.where(el == m1, col, big), axis=-1, keepdims=True)
    el2 = jnp.where(col == i1, neg, el)
    m2 = jnp.max(el2, axis=-1, keepdims=True)
    i2 = jnp.min(jnp.where(el2 == m2, col, big), axis=-1, keepdims=True)
    e2 = jnp.exp(m2 - m1)
    w1 = g_val / (1.0 + e2)
    w2 = g_val * e2 / (1.0 + e2)
    f1 = (i1 - n_groups).astype(F32)
    f2 = (i2 - n_groups).astype(F32)
    hit1, hit2 = col == i1, col == i2
    hits = jnp.where(hit1 | hit2, 1.0, 0.0)
    rows = logits.shape[0]
    earlier = lax.broadcasted_iota(jnp.int32, (rows, rows), 1) < lax.broadcasted_iota(jnp.int32, (rows, rows), 0)
    before = run + _dot(jnp.where(earlier, 1.0, 0.0).astype(BF16), hits.astype(BF16))
    r1 = jnp.sum(jnp.where(hit1, before, 0.0), axis=-1, keepdims=True)
    r2 = jnp.sum(jnp.where(hit2, before, 0.0), axis=-1, keepdims=True)
    route = f1
    for j, val in enumerate((f2, w1, w2, r1, r2), start=1):
        route = jnp.where(col == j, val, route)
    route = jnp.where(col > 5, 0.0, route)
    return route, run + jnp.sum(hits, axis=0, keepdims=True)


def _merge_tail(x, merged, wout_ref, gffn, wrt_ref, brt, run, n_groups, per_group):
    x1 = x + _dot(merged.astype(BF16), wout_ref[...])
    xn2 = _rms(x1, gffn).astype(BF16)
    logits = _dot(xn2, wrt_ref[...]) + brt
    route, run = _route(logits, run, n_groups, per_group)
    return x1, xn2, route, run


def _lru_prompt_kernel(x_ref, g_ref, wxy_ref, wga_ref, cw_ref, cb_ref, wr_ref, br_ref, wi_ref, bi_ref,
                       lam_ref, wbr_ref, out_ref, conv_ref, h_ref, ubuf, hcar):
    c = pl.program_id(1)
    tl = x_ref.shape[1]
    width = ubuf.shape[1]
    kw = cw_ref.shape[0]
    base = SUBLANES

    @pl.when(c == 0)
    def _():
        ubuf[0:base, :] = jnp.zeros((base, width), F32)
        hcar[...] = jnp.zeros_like(hcar)

    xn = _rms(x_ref[0], g_ref[...]).astype(BF16)
    z = _dot(xn, wxy_ref[...])
    ux, uy = z[:, :width], z[:, width:]
    ubuf[base:base + tl, :] = ux
    uc = cb_ref[...]
    for j in range(kw):
        uc = uc + cw_ref[j:j + 1, :] * ubuf[base - (kw - 1) + j: base - (kw - 1) + j + tl, :]
    a, mult, gate = _lru_coeffs(uc, wr_ref, br_ref[...], wi_ref, bi_ref[...], lam_ref[...])
    row = lax.broadcasted_iota(jnp.int32, a.shape, 0)
    mult = jnp.where(row + c * tl == 0, 1.0, mult)
    a_cum, h = _scan_rows(a, mult * gate * uc)
    h = h + a_cum * hcar[...]
    hcar[...] = h[tl - 1:tl, :]
    tail = ubuf[base + tl - (kw - 1): base + tl, :]
    ubuf[base - (kw - 1): base, :] = tail
    a_out = _dot((h * _gelu_tanh(uy)).astype(BF16), wbr_ref[...])
    ga = _dot(xn, wga_ref[...])
    out_ref[0] = (_sigmoid(ga) * a_out).astype(out_ref.dtype)

    @pl.when(c == pl.num_programs(1) - 1)
    def _():
        conv_ref[0] = tail
        h_ref[0] = h[tl - 1:tl, :]


def _ret_prompt_kernel(x_ref, g_ref, wq_ref, wk_ref, wv_ref, wg_ref, wgb_ref, cos_ref, sin_ref, wbr_ref,
                       out_ref, s_out_ref, s_scr, y_scr, *, heads, chunk):
    c = pl.program_id(1)
    tl = x_ref.shape[1]
    dk = wq_ref.shape[1] // heads
    dv = wv_ref.shape[1] // heads

    @pl.when(c == 0)
    def _():
        s_scr[...] = jnp.zeros_like(s_scr)

    xn = _rms(x_ref[0], g_ref[...]).astype(BF16)
    q = _dot(xn, wq_ref[...])
    k = _dot(xn, wk_ref[...])
    v = _dot(xn, wv_ref[...])
    g = _dot(xn, wg_ref[...])
    cos, sin = cos_ref[...], sin_ref[...]
    n_i = lax.broadcasted_iota(jnp.int32, (chunk, chunk), 0)
    m_i = lax.broadcasted_iota(jnp.int32, (chunk, chunk), 1)
    diff = (n_i - m_i).astype(F32)
    rowk = lax.broadcasted_iota(jnp.int32, (chunk, dk), 0).astype(F32)
    for h in range(heads):
        log_g = math.log1p(-(2.0 ** (-5.0 - h)))
        dmask = jnp.where(diff >= 0, jnp.exp(jnp.maximum(diff, 0.0) * log_g), 0.0)
        q_decay = jnp.exp((rowk + 1.0) * log_g)
        k_decay = jnp.exp((chunk - 1.0 - rowk) * log_g)
        s_decay = math.exp(chunk * log_g)
        for sub in range(tl // chunk):
            r0 = sub * chunk
            cs, sn = cos[r0:r0 + chunk], sin[r0:r0 + chunk]
            qr = _rotary(q[r0:r0 + chunk, h * dk:(h + 1) * dk], cs, sn)
            kr = _rotary(k[r0:r0 + chunk, h * dk:(h + 1) * dk], cs, sn) * (dk ** -0.5)
            vh = v[r0:r0 + chunk, h * dv:(h + 1) * dv].astype(BF16)
            s = s_scr[h]
            att = _dot_nt(qr.astype(BF16), kr.astype(BF16)) * dmask
            o = _dot(att.astype(BF16), vh) + _dot((qr * q_decay).astype(BF16), s.astype(BF16))
            s_scr[h] = s * s_decay + _dot_tn((kr * k_decay).astype(BF16), vh)
            o = o * lax.rsqrt(jnp.mean(o * o, axis=-1, keepdims=True) + EPS)
            gh = g[r0:r0 + chunk, h * dv:(h + 1) * dv]
            y_scr[r0:r0 + chunk, h * dv:(h + 1) * dv] = (_silu(gh) * o).astype(BF16)
    b_out = _dot(y_scr[...], wbr_ref[...])
    gb = _dot(xn, wgb_ref[...])
    out_ref[0] = (_sigmoid(gb) * b_out).astype(out_ref.dtype)

    @pl.when(c == pl.num_programs(1) - 1)
    def _():
        s_out_ref[0] = s_scr[...]


def _memkv_kernel(m_ref, g_ref, w_ref, k_ref, v_ref):
    width = k_ref.shape[-1]
    mn = _rms(m_ref[0], g_ref[...]).astype(BF16)
    kv = _dot(mn, w_ref[...])
    k_ref[0] = kv[:, :width]
    v_ref[0] = kv[:, width:]


def _attn_merge_prompt_kernel(x_ref, ag_ref, bg_ref, mk_ref, mv_ref, g_ref, wmq_ref, wgc_ref, wbm_ref,
                              wout_ref, gffn_ref, wrt_ref, brt_ref, x1_ref, xn2_ref, route_ref, cnt_ref,
                              *, heads, n_groups, per_group):
    @pl.when((pl.program_id(0) == 0) & (pl.program_id(1) == 0))
    def _():
        cnt_ref[...] = jnp.zeros_like(cnt_ref)

    x = x_ref[0]
    xn = _rms(x, g_ref[...]).astype(BF16)
    mq = _dot(xn, wmq_ref[...]).astype(BF16)
    dh = mq.shape[1] // heads
    mk = mk_ref[0].astype(BF16)
    mv = mv_ref[0].astype(BF16)
    cs = []
    for h in range(heads):
        hs = slice(h * dh, (h + 1) * dh)
        lg = _dot_nt(mq[:, hs], mk[:, hs]) * (dh ** -0.5)
        p = jnp.exp(lg - jnp.max(lg, axis=-1, keepdims=True))
        den = jnp.sum(p, axis=-1, keepdims=True)
        cs.append(_dot(p.astype(BF16), mv[:, hs]) / den)
    c_out = _dot(jnp.concatenate(cs, axis=-1).astype(BF16), wbm_ref[...])
    gc = _dot(xn, wgc_ref[...])
    merged = _sigmoid(gc) * c_out + ag_ref[0].astype(F32) + bg_ref[0].astype(F32)
    x1, xn2, route, run = _merge_tail(x, merged, wout_ref, gffn_ref[...], wrt_ref, brt_ref[...], cnt_ref[...],
                                      n_groups, per_group)
    x1_ref[0] = x1
    xn2_ref[0] = xn2
    route_ref[0] = route
    cnt_ref[...] = run


def _proj_kernel(x_ref, g_ref, w_ref, z_ref):
    xn = _rms(x_ref[...], g_ref[...]).astype(BF16)
    z_ref[...] = _dot(xn, w_ref[...])


def _lru_sample_kernel(ux_ref, uy_ref, ga_ref, cprev_ref, hprev_ref, cw_ref, cb_ref, wr_ref, br_ref, wi_ref,
                       bi_ref, lam_ref, wbr_ref, out_ref, conv_ref, h_ref, *, at_start):
    kw = cw_ref.shape[0]
    ux = ux_ref[...]
    uc = cb_ref[...] + cw_ref[kw - 1:kw, :] * ux
    for j in range(kw - 1):
        uc = uc + cw_ref[j:j + 1, :] * cprev_ref[:, j, :]
    a, mult, gate = _lru_coeffs(uc, wr_ref, br_ref[...], wi_ref, bi_ref[...], lam_ref[...])
    if at_start:
        mult = jnp.ones_like(mult)
    h = a * hprev_ref[...] + mult * gate * uc
    h_ref[...] = h
    for j in range(kw - 2):
        conv_ref[:, j, :] = cprev_ref[:, j + 1, :]
    conv_ref[:, kw - 2, :] = ux
    a_out = _dot((h * _gelu_tanh(uy_ref[...])).astype(BF16), wbr_ref[...])
    out_ref[...] = _sigmoid(ga_ref[...]) * a_out


def _rot_sample_kernel(q_ref, k_ref, cos_ref, sin_ref, qo_ref, ko_ref, *, heads):
    dk = q_ref.shape[1] // heads
    cos, sin = cos_ref[...], sin_ref[...]
    for h in range(heads):
        hs = slice(h * dk, (h + 1) * dk)
        qo_ref[:, hs] = _rotary(q_ref[:, hs], cos, sin)
        ko_ref[:, hs] = _rotary(k_ref[:, hs], cos, sin) * (dk ** -0.5)


def _ret_sample_kernel(qt_ref, kt_ref, q_ref, k_ref, v_ref, s_ref, o_ref, s_out_ref, *, heads):
    bb = q_ref.shape[1]
    dk = s_ref.shape[2]
    dv = s_ref.shape[3]
    for j in range(bb):
        for h in range(heads):
            decay = 1.0 - 2.0 ** (-5.0 - h)
            qcol = qt_ref[0, h, :, j:j + 1]
            kcol = kt_ref[0, h, :, j:j + 1]
            qrow = q_ref[0, j:j + 1, h * dk:(h + 1) * dk]
            krow = k_ref[0, j:j + 1, h * dk:(h + 1) * dk]
            vrow = v_ref[0, j:j + 1, h * dv:(h + 1) * dv]
            s = s_ref[j, h]
            att = jnp.sum(qrow * krow, axis=-1, keepdims=True)
            o = att * vrow + jnp.sum((qcol * decay) * s, axis=0, keepdims=True)
            o_ref[0, j:j + 1, h * dv:(h + 1) * dv] = o
            s_out_ref[j, h] = s * decay + kcol * vrow


def _attn_sample_kernel(mq_ref, k_ref, v_ref, c_ref, *, heads):
    bb, _, dh = mq_ref.shape
    for j in range(bb):
        q = mq_ref[j]
        lg = jnp.sum(k_ref[j] * q[None], axis=-1, keepdims=True) * (dh ** -0.5)
        p = jnp.exp(lg - jnp.max(lg, axis=0, keepdims=True))
        den = jnp.sum(p, axis=0)
        c_ref[j] = jnp.sum(v_ref[j] * p, axis=0) / den


def _merge_sample_kernel(x_ref, ag_ref, o_ref, g_ref, c_ref, gb_ref, gc_ref, wbr_ref, wbm_ref, wout_ref,
                         gffn_ref, wrt_ref, brt_ref, run_ref, x1_ref, xn2_ref, route_ref, cnt_ref,
                         *, heads, n_groups, per_group):
    dv = o_ref.shape[1] // heads
    ys = []
    for h in range(heads):
        o = o_ref[:, h * dv:(h + 1) * dv]
        o = o * lax.rsqrt(jnp.mean(o * o, axis=-1, keepdims=True) + EPS)
        ys.append((_silu(g_ref[:, h * dv:(h + 1) * dv]) * o).astype(BF16))
    b_out = _dot(jnp.concatenate(ys, axis=-1), wbr_ref[...])
    c_out = _dot(c_ref[...].astype(BF16), wbm_ref[...])
    merged = ag_ref[...] + _sigmoid(gb_ref[...]) * b_out + _sigmoid(gc_ref[...]) * c_out
    x1, xn2, route, run = _merge_tail(x_ref[...], merged, wout_ref, gffn_ref[...], wrt_ref, brt_ref[...],
                                      run_ref[...], n_groups, per_group)
    x1_ref[...] = x1
    xn2_ref[...] = xn2
    route_ref[...] = route
    cnt_ref[...] = run


def _expert_kernel(te_ref, nu_ref, xs_ref, wg_ref, wu_ref, wd_ref, ys_ref):
    i = pl.program_id(0)

    @pl.when(i < nu_ref[0])
    def _():
        x = xs_ref[...]
        hg = _dot(x, wg_ref[0].astype(BF16))
        hu = _dot(x, wu_ref[0].astype(BF16))
        hid = (_silu(hg) * hu).astype(BF16)
        ys_ref[...] = _dot(hid, wd_ref[0].astype(BF16))

    @pl.when(i >= nu_ref[0])
    def _():
        ys_ref[...] = jnp.zeros_like(ys_ref)


def _final_kernel(x1_ref, y1_ref, y2_ref, route_ref, g_ref, out_ref):
    w1 = route_ref[:, 2:3]
    w2 = route_ref[:, 3:4]
    x2 = x1_ref[...] + (w1 * y1_ref[...] + w2 * y2_ref[...])
    out_ref[...] = _rms(x2, g_ref[...])


def _tile(n, target):
    t = min(n, target)
    while n % t:
        t //= 2
    return t


def _row(v):
    return v.reshape(1, -1).astype(F32)


def _prep_layer(p, dims):
    d, w, hk, hv, hm = dims['d'], dims['w'], dims['hk'], dims['hv'], dims['hm']
    w_in = p['w_in']
    cuts = [0, w, 2 * w, 2 * w + hk, 2 * w + 2 * hk, 2 * w + 2 * hk + hv, 2 * w + 2 * hk + 2 * hv,
            2 * w + 2 * hk + 2 * hv + hm, w_in.shape[1]]
    seg = [w_in[:, cuts[i]:cuts[i + 1]].astype(BF16) for i in range(8)]
    gates = seg[7]
    g, e = p['w_exp_router'].shape[0], p['w_exp_router'].shape[2]
    w_rt = jnp.concatenate([p['w_grp'], jnp.moveaxis(p['w_exp_router'], 0, 1).reshape(d, g * e)], axis=1)
    w_rt = jnp.pad(w_rt, ((0, 0), (0, LANES - w_rt.shape[1]))).astype(BF16)
    b_rt = jnp.pad(jnp.concatenate([p['b_grp'], p['b_exp_router'].reshape(-1)]), (0, LANES - g - g * e))
    return dict(
        w_in=w_in.astype(BF16),
        w_xy=jnp.concatenate([seg[0], seg[1]], axis=1), w_q=seg[2], w_k=seg[3], w_v=seg[4], w_g=seg[5],
        w_mq=seg[6], w_ga=gates[:, :d], w_gb=gates[:, d:2 * d], w_gc=gates[:, 2 * d:],
        norm_mix=_row(p['norm_mix']), norm_ffn=_row(p['norm_ffn']), norm_mem=_row(p['norm_mem']),
        w_mem_kv=p['w_mem_kv'].astype(BF16),
        conv_w=p['conv_w'].astype(F32), conv_b=_row(p['conv_b']),
        w_r=p['w_r'].astype(BF16), b_r=_row(p['b_r']), w_i=p['w_i'].astype(BF16), b_i=_row(p['b_i']),
        lam=_row(p['lru_lambda']),
        w_br_lru=p['w_branch_lru'].astype(BF16), w_br_ret=p['w_branch_ret'].astype(BF16),
        w_br_mem=p['w_branch_mem'].astype(BF16), w_out=p['w_out'].astype(BF16),
        w_rt=w_rt, b_rt=_row(b_rt),
        w_gate=p['w_gate'].reshape((g * e,) + p['w_gate'].shape[2:]),
        w_up=p['w_up'].reshape((g * e,) + p['w_up'].shape[2:]),
        w_down=p['w_down'].reshape((g * e,) + p['w_down'].shape[2:]),
    )


def _rope_tables(pos, dk):
    half = dk // 2
    inv = ROPE_BASE ** (-jnp.linspace(0.0, 1.0, half, dtype=F32))
    ang = pos.astype(F32)[:, None] * inv[None, :]
    return jnp.cos(ang), jnp.sin(ang)


def _prompt_mixer(x, mem, lw, dims):
    b, l, d = x.shape
    w, heads, dk, dv = dims['w'], dims['heads'], dims['dk'], dims['dv']
    kw = lw['conv_w'].shape[0]
    tl = _tile(l, 256)
    grid = (b, l // tl)
    xspec = pl.BlockSpec((1, tl, d), lambda i, c: (i, c, 0))
    sem = ("parallel", "arbitrary")

    a_g, conv_new, h_last = pl.pallas_call(
        _lru_prompt_kernel,
        grid=grid,
        in_specs=[xspec, _const_spec((1, d)), _const_spec((d, 2 * w)), _const_spec((d, d)),
                  _const_spec((kw, w)), _const_spec((1, w)), _const_spec(lw['w_r'].shape), _const_spec((1, w)),
                  _const_spec(lw['w_i'].shape), _const_spec((1, w)), _const_spec((1, w)), _const_spec((w, d))],
        out_specs=[pl.BlockSpec((1, tl, d), lambda i, c: (i, c, 0)),
                   pl.BlockSpec((1, kw - 1, w), lambda i, c: (i, 0, 0)),
                   pl.BlockSpec((1, 1, w), lambda i, c: (i, 0, 0))],
        out_shape=[jax.ShapeDtypeStruct((b, l, d), BF16), jax.ShapeDtypeStruct((b, kw - 1, w), F32),
                   jax.ShapeDtypeStruct((b, 1, w), F32)],
        scratch_shapes=[pltpu.VMEM((tl + SUBLANES, w), F32), pltpu.VMEM((1, w), F32)],
        compiler_params=_params(sem),
    )(x, lw['norm_mix'], lw['w_xy'], lw['w_ga'], lw['conv_w'], lw['conv_b'], lw['w_r'], lw['b_r'],
      lw['w_i'], lw['b_i'], lw['lam'], lw['w_br_lru'])

    chunk = _tile(tl, 128)
    cos, sin = _rope_tables(jnp.arange(l, dtype=jnp.int32), dk)
    half = dk // 2
    b_g, s_new = pl.pallas_call(
        functools.partial(_ret_prompt_kernel, heads=heads, chunk=chunk),
        grid=grid,
        in_specs=[xspec, _const_spec((1, d)), _const_spec((d, heads * dk)), _const_spec((d, heads * dk)),
                  _const_spec((d, heads * dv)), _const_spec((d, heads * dv)), _const_spec((d, d)),
                  pl.BlockSpec((tl, half), lambda i, c: (c, 0)), pl.BlockSpec((tl, half), lambda i, c: (c, 0)),
                  _const_spec((heads * dv, d))],
        out_specs=[pl.BlockSpec((1, tl, d), lambda i, c: (i, c, 0)),
                   pl.BlockSpec((1, heads, dk, dv), lambda i, c: (i, 0, 0, 0))],
        out_shape=[jax.ShapeDtypeStruct((b, l, d), BF16), jax.ShapeDtypeStruct((b, heads, dk, dv), F32)],
        scratch_shapes=[pltpu.VMEM((heads, dk, dv), F32), pltpu.VMEM((tl, heads * dv), BF16)],
        compiler_params=_params(sem),
    )(x, lw['norm_mix'], lw['w_q'], lw['w_k'], lw['w_v'], lw['w_g'], lw['w_gb'], cos, sin, lw['w_br_ret'])

    m = mem.shape[1]
    hm = dims['hm']
    mk, mv = pl.pallas_call(
        _memkv_kernel,
        grid=(b,),
        in_specs=[pl.BlockSpec((1, m, d), lambda i: (i, 0, 0)), _const_spec((1, d)), _const_spec((d, 2 * hm))],
        out_specs=[pl.BlockSpec((1, m, hm), lambda i: (i, 0, 0)), pl.BlockSpec((1, m, hm), lambda i: (i, 0, 0))],
        out_shape=[jax.ShapeDtypeStruct((b, m, hm), F32), jax.ShapeDtypeStruct((b, m, hm), F32)],
        compiler_params=_params(("parallel",)),
    )(mem, lw['norm_mem'], lw['w_mem_kv'])

    x1, xn2, route, counts = pl.pallas_call(
        functools.partial(_attn_merge_prompt_kernel, heads=dims['mheads'], n_groups=dims['g'],
                          per_group=dims['e']),
        grid=grid,
        in_specs=[xspec, xspec, xspec,
                  pl.BlockSpec((1, m, hm), lambda i, c: (i, 0, 0)), pl.BlockSpec((1, m, hm), lambda i, c: (i, 0, 0)),
                  _const_spec((1, d)), _const_spec((d, hm)), _const_spec((d, d)), _const_spec((hm, d)),
                  _const_spec((d, d)), _const_spec((1, d)), _const_spec((d, LANES)), _const_spec((1, LANES))],
        out_specs=[xspec, xspec, pl.BlockSpec((1, tl, LANES), lambda i, c: (i, c, 0)),
                   pl.BlockSpec((1, LANES), lambda i, c: (0, 0))],
        out_shape=[jax.ShapeDtypeStruct((b, l, d), F32), jax.ShapeDtypeStruct((b, l, d), BF16),
                   jax.ShapeDtypeStruct((b, l, LANES), F32), jax.ShapeDtypeStruct((1, LANES), F32)],
        compiler_params=_params(("arbitrary", "arbitrary")),
    )(x, a_g, b_g, mk, mv, lw['norm_mix'], lw['w_mq'], lw['w_gc'], lw['w_br_mem'], lw['w_out'],
      lw['norm_ffn'], lw['w_rt'], lw['b_rt'])
    return x1, xn2, route, counts, conv_new, h_last[:, 0], s_new, mk, mv


def _sample_mixer(x, conv_prev, h_prev, s_prev, mem_k, mem_v, counts0, lw, dims):
    n, d = x.shape
    w, heads, dk, dv, hm = dims['w'], dims['heads'], dims['dk'], dims['dv'], dims['hm']
    hk, hv = heads * dk, heads * dv
    kw = lw['conv_w'].shape[0]
    n_in = lw['w_in'].shape[1]
    tn = _tile(n_in, 1024)
    z = pl.pallas_call(
        _proj_kernel,
        grid=(n_in // tn,),
        in_specs=[_const_spec((n, d)), _const_spec((1, d)), pl.BlockSpec((d, tn), lambda j: (0, j))],
        out_specs=pl.BlockSpec((n, tn), lambda j: (0, j)),
        out_shape=jax.ShapeDtypeStruct((n, n_in), F32),
        compiler_params=_params(("parallel",)),
    )(x, lw['norm_mix'], lw['w_in'])
    o0 = 0
    parts = []
    for sz in (w, w, hk, hk, hv, hv, hm, d, d, d):
        parts.append(z[:, o0:o0 + sz])
        o0 += sz
    ux, uy, q, k, v, g, mq, ga, gb, gc = parts

    a_g, conv_new, h_new = pl.pallas_call(
        functools.partial(_lru_sample_kernel, at_start=(PAST_LEN == 0)),
        out_shape=[jax.ShapeDtypeStruct((n, d), F32), jax.ShapeDtypeStruct((n, kw - 1, w), F32),
                   jax.ShapeDtypeStruct((n, w), F32)],
        compiler_params=pltpu.CompilerParams(vmem_limit_bytes=VMEM_LIMIT),
    )(ux, uy, ga, conv_prev, h_prev, lw['conv_w'], lw['conv_b'], lw['w_r'], lw['b_r'], lw['w_i'], lw['b_i'],
      lw['lam'], lw['w_br_lru'])

    cos, sin = _rope_tables(PAST_LEN + jnp.arange(1, dtype=jnp.int32), dk)
    qr, kr = pl.pallas_call(
        functools.partial(_rot_sample_kernel, heads=heads),
        out_shape=[jax.ShapeDtypeStruct((n, hk), F32), jax.ShapeDtypeStruct((n, hk), F32)],
    )(q, k, cos, sin)

    bb = _tile(n, 4)

    def cols(t):
        return t.reshape(n // bb, bb, heads, dk).transpose(0, 2, 3, 1)

    def rows(t):
        return t.reshape(n // bb, bb, t.shape[1])

    o, s_new = pl.pallas_call(
        functools.partial(_ret_sample_kernel, heads=heads),
        grid=(n // bb,),
        in_specs=[pl.BlockSpec((1, heads, dk, bb), lambda i: (i, 0, 0, 0)),
                  pl.BlockSpec((1, heads, dk, bb), lambda i: (i, 0, 0, 0)),
                  pl.BlockSpec((1, bb, hk), lambda i: (i, 0, 0)), pl.BlockSpec((1, bb, hk), lambda i: (i, 0, 0)),
                  pl.BlockSpec((1, bb, hv), lambda i: (i, 0, 0)),
                  pl.BlockSpec((bb, heads, dk, dv), lambda i: (i, 0, 0, 0))],
        out_specs=[pl.BlockSpec((1, bb, hv), lambda i: (i, 0, 0)),
                   pl.BlockSpec((bb, heads, dk, dv), lambda i: (i, 0, 0, 0))],
        out_shape=[jax.ShapeDtypeStruct((n // bb, bb, hv), F32), jax.ShapeDtypeStruct(s_prev.shape, F32)],
        compiler_params=_params(("parallel",)),
    )(cols(qr), cols(kr), rows(qr), rows(kr), rows(v), s_prev)
    o = o.reshape(n, hv)
    bb = _tile(n, SUBLANES)

    m = mem_k.shape[1]
    c = pl.pallas_call(
        functools.partial(_attn_sample_kernel, heads=dims['mheads']),
        grid=(n // bb,),
        in_specs=[pl.BlockSpec((bb,) + mem_k.shape[2:], lambda i: (i, 0, 0)),
                  pl.BlockSpec((bb,) + mem_k.shape[1:], lambda i: (i, 0, 0, 0)),
                  pl.BlockSpec((bb,) + mem_v.shape[1:], lambda i: (i, 0, 0, 0))],
        out_specs=pl.BlockSpec((bb,) + mem_k.shape[2:], lambda i: (i, 0, 0)),
        out_shape=jax.ShapeDtypeStruct((n,) + mem_k.shape[2:], F32),
        compiler_params=_params(("parallel",)),
    )(mq.reshape((n,) + mem_k.shape[2:]), mem_k, mem_v).reshape(n, hm)

    x1, xn2, route, counts = pl.pallas_call(
        functools.partial(_merge_sample_kernel, heads=heads, n_groups=dims['g'], per_group=dims['e']),
        out_shape=[jax.ShapeDtypeStruct((n, d), F32), jax.ShapeDtypeStruct((n, d), BF16),
                   jax.ShapeDtypeStruct((n, LANES), F32), jax.ShapeDtypeStruct((1, LANES), F32)],
        compiler_params=pltpu.CompilerParams(vmem_limit_bytes=VMEM_LIMIT),
    )(x, a_g, o, g, c, gb, gc, lw['w_br_ret'], lw['w_br_mem'], lw['w_out'], lw['norm_ffn'], lw['w_rt'],
      lw['b_rt'], counts0)
    return x1, xn2, route, counts, conv_new, h_new, s_new


def _moe(xn2_parts, route_parts, counts, lw, dims, tm=256):
    xn2 = jnp.concatenate(xn2_parts, axis=0)
    t, d = xn2.shape
    g, n_exp = dims['g'], dims['g'] * dims['e']
    ff = lw['w_gate'].shape[2]
    counts = counts[0, g:g + n_exp].astype(jnp.int32)
    padded = ((counts + tm - 1) // tm) * tm
    ends = jnp.cumsum(padded)
    offs = ends - padded
    n_slots = ((TOP_K * t + n_exp * tm + tm - 1) // tm) * tm
    n_tiles = n_slots // tm
    dests = [offs.at[r[:, 0:TOP_K].astype(jnp.int32)].get(mode='promise_in_bounds')
             + r[:, 4:4 + TOP_K].astype(jnp.int32) for r in route_parts]
    dest = jnp.concatenate(dests, axis=0).reshape(-1)
    src = jnp.zeros((n_slots,), jnp.int32).at[dest].set(
        jnp.arange(TOP_K * t, dtype=jnp.int32) // TOP_K, mode='promise_in_bounds', unique_indices=True)
    n_used = (ends[-1] // tm).astype(jnp.int32)
    tile_row = jnp.arange(n_tiles, dtype=jnp.int32) * tm
    tile_e = jnp.sum((ends[None, :] <= tile_row[:, None]).astype(jnp.int32), axis=1)
    last_e = jnp.sum((ends <= (n_used - 1) * tm).astype(jnp.int32))
    tile_e = jnp.where(tile_row < ends[-1], tile_e, last_e)
    xs = xn2.at[src].get(mode='promise_in_bounds')

    def row_map(i, te, nu):
        return (jnp.minimum(i, nu[0] - 1), 0)

    ys = pl.pallas_call(
        _expert_kernel,
        grid_spec=pltpu.PrefetchScalarGridSpec(
            num_scalar_prefetch=2,
            grid=(n_tiles,),
            in_specs=[pl.BlockSpec((tm, d), row_map),
                      pl.BlockSpec((1, d, ff), lambda i, te, nu: (te[i], 0, 0)),
                      pl.BlockSpec((1, d, ff), lambda i, te, nu: (te[i], 0, 0)),
                      pl.BlockSpec((1, ff, d), lambda i, te, nu: (te[i], 0, 0))],
            out_specs=pl.BlockSpec((tm, d), lambda i, te, nu: (i, 0)),
        ),
        out_shape=jax.ShapeDtypeStruct((n_slots, d), F32),
        compiler_params=_params(("arbitrary",)),
    )(tile_e, n_used.reshape(1), xs, lw['w_gate'], lw['w_up'], lw['w_down'])
    return [tuple(ys.at[dp[:, j]].get(mode='promise_in_bounds') for j in range(TOP_K)) for dp in dests]


def _final(x1, y_pair, route, g):
    t, d = x1.shape
    tl = _tile(t, 512)
    row = pl.BlockSpec((tl, d), lambda i: (i, 0))
    return pl.pallas_call(
        _final_kernel,
        grid=(t // tl,),
        in_specs=[row, row, row, pl.BlockSpec((tl, LANES), lambda i: (i, 0)), _const_spec((1, d))],
        out_specs=row,
        out_shape=jax.ShapeDtypeStruct((t, d), F32),
        compiler_params=_params(("parallel",)),
    )(x1, y_pair[0], y_pair[1], route, g)


def kernel(x_prompt, x_sample, mem_prompt, state_conv, state_lru, state_ret, cache_mem_k, cache_mem_v, norm_mix, norm_ffn, norm_mem, norm_final, w_in, w_mem_kv, conv_w, conv_b, w_r, b_r, w_i, b_i, lru_lambda, w_branch_lru, w_branch_ret, w_branch_mem, w_out, w_grp, b_grp, w_exp_router, b_exp_router, w_gate, w_up, w_down):
    depth = w_in.shape[0]
    assert depth == 1, "the two request groups are chained per layer only for a single-layer trunk"
    assert x_sample.shape[1] == 1
    bp, lp, d = x_prompt.shape
    ns = x_sample.shape[0]
    heads, dk, dv = state_ret.shape[2], state_ret.shape[3], state_ret.shape[4]
    mheads, mdh = cache_mem_k.shape[3], cache_mem_k.shape[4]
    dims = dict(d=d, w=state_lru.shape[2], heads=heads, dk=dk, dv=dv, hk=heads * dk, hv=heads * dv,
                mheads=mheads, hm=mheads * mdh, g=w_exp_router.shape[1], e=w_exp_router.shape[3])
    layer = 0
    p = dict(norm_mix=norm_mix[layer], norm_ffn=norm_ffn[layer], norm_mem=norm_mem[layer], w_in=w_in[layer],
             w_mem_kv=w_mem_kv[layer], conv_w=conv_w[layer], conv_b=conv_b[layer], w_r=w_r[layer], b_r=b_r[layer],
             w_i=w_i[layer], b_i=b_i[layer], lru_lambda=lru_lambda[layer], w_branch_lru=w_branch_lru[layer],
             w_branch_ret=w_branch_ret[layer], w_branch_mem=w_branch_mem[layer], w_out=w_out[layer],
             w_grp=w_grp[layer], b_grp=b_grp[layer], w_exp_router=w_exp_router[layer],
             b_exp_router=b_exp_router[layer], w_gate=w_gate[layer], w_up=w_up[layer], w_down=w_down[layer])
    lw = _prep_layer(p, dims)

    x1p, xn2p, route_p, counts, conv_p, lru_p, ret_p, mk, mv = _prompt_mixer(x_prompt, mem_prompt, lw, dims)
    x1s, xn2s, route_s, counts, conv_s, lru_s, ret_s = _sample_mixer(
        x_sample[:, 0], state_conv[layer], state_lru[layer], state_ret[layer], cache_mem_k[layer],
        cache_mem_v[layer], counts, lw, dims)

    tp = bp * lp
    route_p = route_p.reshape(tp, LANES)
    yg_p, yg_s = _moe([xn2p.reshape(tp, d), xn2s], [route_p, route_s], counts, lw, dims)
    yp = _final(x1p.reshape(tp, d), yg_p, route_p, _row(norm_final)).reshape(bp, lp, d)
    ys = _final(x1s, yg_s, route_s, _row(norm_final)).reshape(ns, 1, d)
    mshape = (1, bp, mem_prompt.shape[1], mheads, mdh)
    return (yp, ys, conv_p[None], lru_p[None], ret_p[None], mk.reshape(mshape), mv.reshape(mshape),
            conv_s[None], lru_s[None], ret_s[None])
```

```python
import functools
import math

import jax
import jax.numpy as jnp
from jax import lax
from jax.experimental import pallas as pl
from jax.experimental.pallas import tpu as pltpu

F32 = jnp.float32
BF16 = jnp.bfloat16

EPS = 1e-6
LRU_C = 8.0
ROPE_BASE = 10000.0
PAST_LEN = 16384
TOP_K = 2
LANES = 128
SUBLANES = 8
VMEM_LIMIT = 56 * 1024 * 1024


def _dot(a, b):
    return jnp.dot(a, b, preferred_element_type=F32)


def _dot_nt(a, b):
    return lax.dot_general(a, b, (((1,), (1,)), ((), ())), preferred_element_type=F32)


def _dot_tn(a, b):
    return lax.dot_general(a, b, (((0,), (0,)), ((), ())), preferred_element_type=F32)


def _rms(x, g):
    return x * lax.rsqrt(jnp.mean(x * x, axis=-1, keepdims=True) + EPS) * g


def _sigmoid(x):
    return 0.5 * jnp.tanh(0.5 * x) + 0.5


def _silu(x):
    return x * _sigmoid(x)


def _gelu_tanh(x):
    return 0.5 * x * (1.0 + jnp.tanh(math.sqrt(2.0 / math.pi) * (x + 0.044715 * (x * x * x))))


def _softplus(x):
    return jnp.maximum(x, 0.0) + jnp.log1p(jnp.exp(-jnp.abs(x)))


def _const_spec(shape):
    nd = len(shape)
    return pl.BlockSpec(shape, lambda *_: (0,) * nd, pipeline_mode=pl.Buffered(1))


def _params(sem):
    return pltpu.CompilerParams(dimension_semantics=sem, vmem_limit_bytes=VMEM_LIMIT)


def _lru_coeffs(uc, wr_ref, br, wi_ref, bi, lam):
    nb, bs = wr_ref.shape[0], wr_ref.shape[1]
    ucb = uc.astype(BF16)
    r_lin = jnp.concatenate([_dot(ucb[:, n * bs:(n + 1) * bs], wr_ref[n]) for n in range(nb)], axis=-1)
    i_lin = jnp.concatenate([_dot(ucb[:, n * bs:(n + 1) * bs], wi_ref[n]) for n in range(nb)], axis=-1)
    r = _sigmoid(r_lin + br)
    i = _sigmoid(i_lin + bi)
    log_a = (-LRU_C) * r * _softplus(-lam)
    a = jnp.exp(log_a)
    t = 1.0 - a * a
    mult = t * lax.rsqrt(jnp.maximum(t, 1e-37))
    return a, mult, i


def _scan_rows(a, b, h0):
    rows, width = a.shape
    grp = SUBLANES
    n_grp = rows // grp
    a = a.reshape(n_grp, grp, width)
    b = b.reshape(n_grp, grp, width)
    row = lax.broadcasted_iota(jnp.int32, a.shape, 1)
    d = 1
    while d < grp:
        a_sh = pltpu.roll(a, d, 1)
        b_sh = pltpu.roll(b, d, 1)
        keep = row >= d
        b = jnp.where(keep, b + a * b_sh, b)
        a = jnp.where(keep, a * a_sh, a)
        d *= 2
    out = []
    for g in range(n_grp):
        hg = a[g] * h0 + b[g]
        out.append(hg)
        h0 = hg[grp - 1:grp, :]
    return jnp.concatenate(out, axis=0)


def _rotary(t, cos, sin):
    half = t.shape[-1] // 2
    t1, t2 = t[:, :half], t[:, half:]
    return jnp.concatenate([t1 * cos - t2 * sin, t2 * cos + t1 * sin], axis=-1)


def _route(logits, run, n_groups, per_group):
    col = lax.broadcasted_iota(jnp.int32, logits.shape, 1)
    big = jnp.int32(1 << 20)
    neg = jnp.float32(-jnp.inf)
    gl = jnp.where(col < n_groups, logits, neg)
    gmax = jnp.max(gl, axis=-1, keepdims=True)
    g_idx = jnp.min(jnp.where(gl == gmax, col, big), axis=-1, keepdims=True)
    g_val = 1.0 / jnp.sum(jnp.exp(gl - gmax), axis=-1, keepdims=True)
    lo = n_groups + per_group * g_idx
    el = jnp.where((col >= lo) & (col < lo + per_group), logits, neg)
    m1 = jnp.max(el, axis=-1, keepdims=True)
    i1 = jnp.min(jnp.where(el == m1, col, big), axis=-1, keepdims=True)
    el2 = jnp.where(col == i1, neg, el)
    m2 = jnp.max(el2, axis=-1, keepdims=True)
    i2 = jnp.min(jnp.where(el2 == m2, col, big), axis=-1, keepdims=True)
    e2 = jnp.exp(m2 - m1)
    w1 = g_val / (1.0 + e2)
    w2 = g_val * e2 / (1.0 + e2)
    f1 = (i1 - n_groups).astype(F32)
    f2 = (i2 - n_groups).astype(F32)
    hit1, hit2 = col == i1, col == i2
    hits = jnp.where(hit1 | hit2, 1.0, 0.0)
    rows = logits.shape[0]
    earlier = lax.broadcasted_iota(jnp.int32, (rows, rows), 1) < lax.broadcasted_iota(jnp.int32, (rows, rows), 0)
    before = run + _dot(jnp.where(earlier, 1.0, 0.0).astype(BF16), hits.astype(BF16))
    r1 = jnp.sum(jnp.where(hit1, before, 0.0), axis=-1, keepdims=True)
    r2 = jnp.sum(jnp.where(hit2, before, 0.0), axis=-1, keepdims=True)
    route = f1
    for j, val in enumerate((f2, w1, w2, r1, r2), start=1):
        route = jnp.where(col == j, val, route)
    route = jnp.where(col > 5, 0.0, route)
    return route, run + jnp.sum(hits, axis=0, keepdims=True)


def _pack_pairs(x):
    half = x.shape[1] // 2
    lo = lax.bitcast_convert_type(x[:, :half].astype(F32), jnp.uint32)
    hi = lax.bitcast_convert_type(x[:, half:].astype(F32), jnp.uint32)
    return lax.bitcast_convert_type((lo >> 16) | (hi & jnp.uint32(0xFFFF0000)), F32)


def _unpack_pairs(w):
    u = lax.bitcast_convert_type(w, jnp.uint32)
    lo = lax.bitcast_convert_type(u << 16, F32)
    hi = lax.bitcast_convert_type(u & jnp.uint32(0xFFFF0000), F32)
    return jnp.concatenate([lo, hi], axis=1).astype(BF16)


def _merge_tail(x, merged, wout_ref, gffn, wrt_ref, brt, run, n_groups, per_group):
    x1 = x + _dot(merged.astype(BF16), wout_ref[...])
    xn2 = _rms(x1, gffn).astype(BF16)
    logits = _dot(xn2, wrt_ref[...]) + brt
    route, run = _route(logits, run, n_groups, per_group)
    return x1, _pack_pairs(xn2), route, run


def _lru_prompt_kernel(x_ref, g_ref, wxy_ref, wga_ref, cw_ref, cb_ref, wr_ref, br_ref, wi_ref, bi_ref,
                       lam_ref, wbr_ref, out_ref, conv_ref, h_ref, ccar, hcar):
    c = pl.program_id(1)
    tl = x_ref.shape[1]
    width = hcar.shape[1]
    kw = cw_ref.shape[0]

    @pl.when(c == 0)
    def _():
        ccar[...] = jnp.zeros_like(ccar)
        hcar[...] = jnp.zeros_like(hcar)

    xn = _rms(x_ref[0], g_ref[...]).astype(BF16)
    z = _dot(xn, wxy_ref[...])
    ux, uy = z[:, :width], z[:, width:]
    row = lax.broadcasted_iota(jnp.int32, ux.shape, 0)
    acc = cw_ref[0:1, :] * ux
    for j in range(1, kw):
        prev = jnp.where(row == 0, ccar[j - 1:j, :], pltpu.roll(acc, 1, 0))
        ccar[j - 1:j, :] = acc[tl - 1:tl, :]
        acc = prev + cw_ref[j:j + 1, :] * ux
    uc = acc + cb_ref[...]
    a, mult, gate = _lru_coeffs(uc, wr_ref, br_ref[...], wi_ref, bi_ref[...], lam_ref[...])
    mult = jnp.where(row + c * tl == 0, 1.0, mult)
    h = _scan_rows(a, mult * gate * uc, hcar[...])
    hcar[...] = h[tl - 1:tl, :]
    tail = ux[tl - (kw - 1):tl, :]
    a_out = _dot((h * _gelu_tanh(uy)).astype(BF16), wbr_ref[...])
    ga = _dot(xn, wga_ref[...])
    out_ref[0] = (_sigmoid(ga) * a_out).astype(out_ref.dtype)

    @pl.when(c == pl.num_programs(1) - 1)
    def _():
        conv_ref[0] = tail
        h_ref[0] = h[tl - 1:tl, :]


def _ret_prompt_kernel(x_ref, g_ref, wq_ref, wk_ref, wv_ref, wg_ref, wgb_ref, cos_ref, sin_ref, wbr_ref,
                       out_ref, s_out_ref, s_scr, y_scr, *, heads, chunk):
    c = pl.program_id(1)
    tl = x_ref.shape[1]
    dk = wq_ref.shape[1] // heads
    dv = wv_ref.shape[1] // heads

    @pl.when(c == 0)
    def _():
        s_scr[...] = jnp.zeros_like(s_scr)

    xn = _rms(x_ref[0], g_ref[...]).astype(BF16)
    q = _dot(xn, wq_ref[...])
    k = _dot(xn, wk_ref[...])
    v = _dot(xn, wv_ref[...])
    g = _dot(xn, wg_ref[...])
    cos, sin = cos_ref[...], sin_ref[...]
    n_i = lax.broadcasted_iota(jnp.int32, (chunk, chunk), 0)
    m_i = lax.broadcasted_iota(jnp.int32, (chunk, chunk), 1)
    diff = (n_i - m_i).astype(F32)
    rowk = lax.broadcasted_iota(jnp.int32, (chunk, dk), 0).astype(F32)
    for h in range(heads):
        log_g = math.log1p(-(2.0 ** (-5.0 - h)))
        dmask = jnp.where(diff >= 0, jnp.exp(jnp.maximum(diff, 0.0) * log_g), 0.0)
        q_decay = jnp.exp((rowk + 1.0) * log_g)
        k_decay = jnp.exp((chunk - 1.0 - rowk) * log_g)
        s_decay = math.exp(chunk * log_g)
        for sub in range(tl // chunk):
            r0 = sub * chunk
            cs, sn = cos[r0:r0 + chunk], sin[r0:r0 + chunk]
            qr = _rotary(q[r0:r0 + chunk, h * dk:(h + 1) * dk], cs, sn)
            kr = _rotary(k[r0:r0 + chunk, h * dk:(h + 1) * dk], cs, sn) * (dk ** -0.5)
            vh = v[r0:r0 + chunk, h * dv:(h + 1) * dv].astype(BF16)
            s = s_scr[h]
            att = _dot_nt(qr.astype(BF16), kr.astype(BF16)) * dmask
            o = _dot(att.astype(BF16), vh) + _dot((qr * q_decay).astype(BF16), s.astype(BF16))
            s_scr[h] = s * s_decay + _dot_tn((kr * k_decay).astype(BF16), vh)
            o = o * lax.rsqrt(jnp.mean(o * o, axis=-1, keepdims=True) + EPS)
            gh = g[r0:r0 + chunk, h * dv:(h + 1) * dv]
            y_scr[r0:r0 + chunk, h * dv:(h + 1) * dv] = (_silu(gh) * o).astype(BF16)
    b_out = _dot(y_scr[...], wbr_ref[...])
    gb = _dot(xn, wgb_ref[...])
    out_ref[0] = (_sigmoid(gb) * b_out).astype(out_ref.dtype)

    @pl.when(c == pl.num_programs(1) - 1)
    def _():
        s_out_ref[0] = s_scr[...]


def _memkv_kernel(m_ref, g_ref, w_ref, k_ref, v_ref):
    width = k_ref.shape[-1]
    mn = _rms(m_ref[0], g_ref[...]).astype(BF16)
    kv = _dot(mn, w_ref[...])
    k_ref[0] = kv[:, :width]
    v_ref[0] = kv[:, width:]


def _attn_merge_prompt_kernel(x_ref, ag_ref, bg_ref, mk_ref, mv_ref, g_ref, wmq_ref, wgc_ref, wbm_ref,
                              wout_ref, gffn_ref, wrt_ref, brt_ref, x1_ref, xn2_ref, route_ref, cnt_ref,
                              *, heads, n_groups, per_group):
    @pl.when((pl.program_id(0) == 0) & (pl.program_id(1) == 0))
    def _():
        cnt_ref[...] = jnp.zeros_like(cnt_ref)

    x = x_ref[0]
    xn = _rms(x, g_ref[...]).astype(BF16)
    mq = _dot(xn, wmq_ref[...]).astype(BF16)
    dh = mq.shape[1] // heads
    mk = mk_ref[0].astype(BF16)
    mv = mv_ref[0].astype(BF16)
    cs = []
    for h in range(heads):
        hs = slice(h * dh, (h + 1) * dh)
        lg = _dot_nt(mq[:, hs], mk[:, hs]) * (dh ** -0.5)
        p = jnp.exp(lg - jnp.max(lg, axis=-1, keepdims=True))
        den = jnp.sum(p, axis=-1, keepdims=True)
        cs.append(_dot(p.astype(BF16), mv[:, hs]) / den)
    c_out = _dot(jnp.concatenate(cs, axis=-1).astype(BF16), wbm_ref[...])
    gc = _dot(xn, wgc_ref[...])
    merged = _sigmoid(gc) * c_out + ag_ref[0].astype(F32) + bg_ref[0].astype(F32)
    x1, xn2, route, run = _merge_tail(x, merged, wout_ref, gffn_ref[...], wrt_ref, brt_ref[...], cnt_ref[...],
                                      n_groups, per_group)
    x1_ref[0] = x1
    xn2_ref[0] = xn2
    route_ref[0] = route
    cnt_ref[...] = run


def _proj_kernel(x_ref, g_ref, w_ref, z_ref):
    xn = _rms(x_ref[...], g_ref[...]).astype(BF16)
    z_ref[...] = _dot(xn, w_ref[...])


def _lru_sample_kernel(ux_ref, uy_ref, ga_ref, cprev_ref, hprev_ref, cw_ref, cb_ref, wr_ref, br_ref, wi_ref,
                       bi_ref, lam_ref, wbr_ref, out_ref, conv_ref, h_ref, *, at_start):
    kw = cw_ref.shape[0]
    ux = ux_ref[...]
    uc = cb_ref[...] + cw_ref[kw - 1:kw, :] * ux
    for j in range(kw - 1):
        uc = uc + cw_ref[j:j + 1, :] * cprev_ref[:, j, :]
    a, mult, gate = _lru_coeffs(uc, wr_ref, br_ref[...], wi_ref, bi_ref[...], lam_ref[...])
    if at_start:
        mult = jnp.ones_like(mult)
    h = a * hprev_ref[...] + mult * gate * uc
    h_ref[...] = h
    for j in range(kw - 2):
        conv_ref[:, j, :] = cprev_ref[:, j + 1, :]
    conv_ref[:, kw - 2, :] = ux
    a_out = _dot((h * _gelu_tanh(uy_ref[...])).astype(BF16), wbr_ref[...])
    out_ref[...] = _sigmoid(ga_ref[...]) * a_out


def _rot_sample_kernel(q_ref, k_ref, cos_ref, sin_ref, qo_ref, ko_ref, *, heads):
    dk = q_ref.shape[1] // heads
    cos, sin = cos_ref[...], sin_ref[...]
    for h in range(heads):
        hs = slice(h * dk, (h + 1) * dk)
        qo_ref[:, hs] = _rotary(q_ref[:, hs], cos, sin)
        ko_ref[:, hs] = _rotary(k_ref[:, hs], cos, sin) * (dk ** -0.5)


def _ret_sample_kernel(qt_ref, kt_ref, q_ref, k_ref, v_ref, s_ref, o_ref, s_out_ref, *, heads):
    bb = q_ref.shape[1]
    dk = s_ref.shape[2]
    dv = s_ref.shape[3]
    for j in range(bb):
        for h in range(heads):
            decay = 1.0 - 2.0 ** (-5.0 - h)
            qcol = qt_ref[0, h, :, j:j + 1]
            kcol = kt_ref[0, h, :, j:j + 1]
            qrow = q_ref[0, j:j + 1, h * dk:(h + 1) * dk]
            krow = k_ref[0, j:j + 1, h * dk:(h + 1) * dk]
            vrow = v_ref[0, j:j + 1, h * dv:(h + 1) * dv]
            s = s_ref[j, h]
            att = jnp.sum(qrow * krow, axis=-1, keepdims=True)
            o = att * vrow + jnp.sum((qcol * decay) * s, axis=0, keepdims=True)
            o_ref[0, j:j + 1, h * dv:(h + 1) * dv] = o
            s_out_ref[j, h] = s * decay + kcol * vrow


def _attn_sample_kernel(mq_ref, k_ref, v_ref, c_ref, *, heads):
    bb, _, dh = mq_ref.shape
    for j in range(bb):
        q = mq_ref[j]
        lg = jnp.sum(k_ref[j] * q[None], axis=-1, keepdims=True) * (dh ** -0.5)
        p = jnp.exp(lg - jnp.max(lg, axis=0, keepdims=True))
        den = jnp.sum(p, axis=0)
        c_ref[j] = jnp.sum(v_ref[j] * p, axis=0) / den


def _merge_sample_kernel(x_ref, ag_ref, o_ref, g_ref, c_ref, gb_ref, gc_ref, wbr_ref, wbm_ref, wout_ref,
                         gffn_ref, wrt_ref, brt_ref, run_ref, x1_ref, xn2_ref, route_ref, cnt_ref,
                         *, heads, n_groups, per_group):
    dv = o_ref.shape[1] // heads
    ys = []
    for h in range(heads):
        o = o_ref[:, h * dv:(h + 1) * dv]
        o = o * lax.rsqrt(jnp.mean(o * o, axis=-1, keepdims=True) + EPS)
        ys.append((_silu(g_ref[:, h * dv:(h + 1) * dv]) * o).astype(BF16))
    b_out = _dot(jnp.concatenate(ys, axis=-1), wbr_ref[...])
    c_out = _dot(c_ref[...].astype(BF16), wbm_ref[...])
    merged = ag_ref[...] + _sigmoid(gb_ref[...]) * b_out + _sigmoid(gc_ref[...]) * c_out
    x1, xn2, route, run = _merge_tail(x_ref[...], merged, wout_ref, gffn_ref[...], wrt_ref, brt_ref[...],
                                      run_ref[...], n_groups, per_group)
    x1_ref[...] = x1
    xn2_ref[...] = xn2
    route_ref[...] = route
    cnt_ref[...] = run


def _expert_kernel(te_ref, nu_ref, xa_ref, xb_ref, wg_ref, wu_ref, wd_ref, ys_ref, *, half):
    i = pl.program_id(0)

    @pl.when(i < nu_ref[0])
    def _():
        x = _unpack_pairs(jnp.where(i < half, xa_ref[...], xb_ref[...]))
        hg = _dot(x, wg_ref[0].astype(BF16))
        hu = _dot(x, wu_ref[0].astype(BF16))
        hid = (_silu(hg) * hu).astype(BF16)
        ys_ref[...] = _dot(hid, wd_ref[0].astype(BF16))

    @pl.when(i >= nu_ref[0])
    def _():
        ys_ref[...] = jnp.zeros_like(ys_ref)


def _final_kernel(x1_ref, y1_ref, y2_ref, route_ref, g_ref, out_ref):
    w1 = route_ref[:, 2:3]
    w2 = route_ref[:, 3:4]
    x2 = x1_ref[...] + (w1 * y1_ref[...] + w2 * y2_ref[...])
    out_ref[...] = _rms(x2, g_ref[...])


def _tile(n, target):
    t = min(n, target)
    while n % t:
        t //= 2
    return t


def _row(v):
    return v.reshape(1, -1).astype(F32)


def _prep_layer(p, dims):
    d, w, hk, hv, hm = dims['d'], dims['w'], dims['hk'], dims['hv'], dims['hm']
    w_in = p['w_in']
    cuts = [0, w, 2 * w, 2 * w + hk, 2 * w + 2 * hk, 2 * w + 2 * hk + hv, 2 * w + 2 * hk + 2 * hv,
            2 * w + 2 * hk + 2 * hv + hm, w_in.shape[1]]
    seg = [w_in[:, cuts[i]:cuts[i + 1]].astype(BF16) for i in range(8)]
    gates = seg[7]
    g, e = p['w_exp_router'].shape[0], p['w_exp_router'].shape[2]
    w_rt = jnp.concatenate([p['w_grp'], jnp.moveaxis(p['w_exp_router'], 0, 1).reshape(d, g * e)], axis=1)
    w_rt = jnp.pad(w_rt, ((0, 0), (0, LANES - w_rt.shape[1]))).astype(BF16)
    b_rt = jnp.pad(jnp.concatenate([p['b_grp'], p['b_exp_router'].reshape(-1)]), (0, LANES - g - g * e))
    return dict(
        w_in=w_in.astype(BF16),
        w_xy=jnp.concatenate([seg[0], seg[1]], axis=1), w_q=seg[2], w_k=seg[3], w_v=seg[4], w_g=seg[5],
        w_mq=seg[6], w_ga=gates[:, :d], w_gb=gates[:, d:2 * d], w_gc=gates[:, 2 * d:],
        norm_mix=_row(p['norm_mix']), norm_ffn=_row(p['norm_ffn']), norm_mem=_row(p['norm_mem']),
        w_mem_kv=p['w_mem_kv'].astype(BF16),
        conv_w=p['conv_w'].astype(F32), conv_b=_row(p['conv_b']),
        w_r=p['w_r'].astype(BF16), b_r=_row(p['b_r']), w_i=p['w_i'].astype(BF16), b_i=_row(p['b_i']),
        lam=_row(p['lru_lambda']),
        w_br_lru=p['w_branch_lru'].astype(BF16), w_br_ret=p['w_branch_ret'].astype(BF16),
        w_br_mem=p['w_branch_mem'].astype(BF16), w_out=p['w_out'].astype(BF16),
        w_rt=w_rt, b_rt=_row(b_rt),
        w_gate=p['w_gate'].reshape((g * e,) + p['w_gate'].shape[2:]),
        w_up=p['w_up'].reshape((g * e,) + p['w_up'].shape[2:]),
        w_down=p['w_down'].reshape((g * e,) + p['w_down'].shape[2:]),
    )


def _rope_tables(pos, dk):
    half = dk // 2
    inv = ROPE_BASE ** (-jnp.linspace(0.0, 1.0, half, dtype=F32))
    ang = pos.astype(F32)[:, None] * inv[None, :]
    return jnp.cos(ang), jnp.sin(ang)


def _prompt_mixer(x, mem, lw, dims):
    b, l, d = x.shape
    w, heads, dk, dv = dims['w'], dims['heads'], dims['dk'], dims['dv']
    kw = lw['conv_w'].shape[0]
    tl = _tile(l, 256)
    grid = (b, l // tl)
    xspec = pl.BlockSpec((1, tl, d), lambda i, c: (i, c, 0))
    sem = ("parallel", "arbitrary")

    a_g, conv_new, h_last = pl.pallas_call(
        _lru_prompt_kernel,
        grid=grid,
        in_specs=[xspec, _const_spec((1, d)), _const_spec((d, 2 * w)), _const_spec((d, d)),
                  _const_spec((kw, w)), _const_spec((1, w)), _const_spec(lw['w_r'].shape), _const_spec((1, w)),
                  _const_spec(lw['w_i'].shape), _const_spec((1, w)), _const_spec((1, w)), _const_spec((w, d))],
        out_specs=[pl.BlockSpec((1, tl, d), lambda i, c: (i, c, 0)),
                   pl.BlockSpec((1, kw - 1, w), lambda i, c: (i, 0, 0)),
                   pl.BlockSpec((1, 1, w), lambda i, c: (i, 0, 0))],
        out_shape=[jax.ShapeDtypeStruct((b, l, d), BF16), jax.ShapeDtypeStruct((b, kw - 1, w), F32),
                   jax.ShapeDtypeStruct((b, 1, w), F32)],
        scratch_shapes=[pltpu.VMEM((SUBLANES, w), F32), pltpu.VMEM((1, w), F32)],
        compiler_params=_params(sem),
    )(x, lw['norm_mix'], lw['w_xy'], lw['w_ga'], lw['conv_w'], lw['conv_b'], lw['w_r'], lw['b_r'],
      lw['w_i'], lw['b_i'], lw['lam'], lw['w_br_lru'])

    chunk = _tile(tl, 128)
    cos, sin = _rope_tables(jnp.arange(l, dtype=jnp.int32), dk)
    half = dk // 2
    b_g, s_new = pl.pallas_call(
        functools.partial(_ret_prompt_kernel, heads=heads, chunk=chunk),
        grid=grid,
        in_specs=[xspec, _const_spec((1, d)), _const_spec((d, heads * dk)), _const_spec((d, heads * dk)),
                  _const_spec((d, heads * dv)), _const_spec((d, heads * dv)), _const_spec((d, d)),
                  pl.BlockSpec((tl, half), lambda i, c: (c, 0)), pl.BlockSpec((tl, half), lambda i, c: (c, 0)),
                  _const_spec((heads * dv, d))],
        out_specs=[pl.BlockSpec((1, tl, d), lambda i, c: (i, c, 0)),
                   pl.BlockSpec((1, heads, dk, dv), lambda i, c: (i, 0, 0, 0))],
        out_shape=[jax.ShapeDtypeStruct((b, l, d), BF16), jax.ShapeDtypeStruct((b, heads, dk, dv), F32)],
        scratch_shapes=[pltpu.VMEM((heads, dk, dv), F32), pltpu.VMEM((tl, heads * dv), BF16)],
        compiler_params=_params(sem),
    )(x, lw['norm_mix'], lw['w_q'], lw['w_k'], lw['w_v'], lw['w_g'], lw['w_gb'], cos, sin, lw['w_br_ret'])

    m = mem.shape[1]
    hm = dims['hm']
    mk, mv = pl.pallas_call(
        _memkv_kernel,
        grid=(b,),
        in_specs=[pl.BlockSpec((1, m, d), lambda i: (i, 0, 0)), _const_spec((1, d)), _const_spec((d, 2 * hm))],
        out_specs=[pl.BlockSpec((1, m, hm), lambda i: (i, 0, 0)), pl.BlockSpec((1, m, hm), lambda i: (i, 0, 0))],
        out_shape=[jax.ShapeDtypeStruct((b, m, hm), F32), jax.ShapeDtypeStruct((b, m, hm), F32)],
        compiler_params=_params(("parallel",)),
    )(mem, lw['norm_mem'], lw['w_mem_kv'])

    x1, xn2, route, counts = pl.pallas_call(
        functools.partial(_attn_merge_prompt_kernel, heads=dims['mheads'], n_groups=dims['g'],
                          per_group=dims['e']),
        grid=grid,
        in_specs=[xspec, xspec, xspec,
                  pl.BlockSpec((1, m, hm), lambda i, c: (i, 0, 0)), pl.BlockSpec((1, m, hm), lambda i, c: (i, 0, 0)),
                  _const_spec((1, d)), _const_spec((d, hm)), _const_spec((d, d)), _const_spec((hm, d)),
                  _const_spec((d, d)), _const_spec((1, d)), _const_spec((d, LANES)), _const_spec((1, LANES))],
        out_specs=[xspec, pl.BlockSpec((1, tl, d // 2), lambda i, c: (i, c, 0)),
                   pl.BlockSpec((1, tl, LANES), lambda i, c: (i, c, 0)),
                   pl.BlockSpec((1, LANES), lambda i, c: (0, 0))],
        out_shape=[jax.ShapeDtypeStruct((b, l, d), F32), jax.ShapeDtypeStruct((b, l, d // 2), F32),
                   jax.ShapeDtypeStruct((b, l, LANES), F32), jax.ShapeDtypeStruct((1, LANES), F32)],
        compiler_params=_params(("arbitrary", "arbitrary")),
    )(x, a_g, b_g, mk, mv, lw['norm_mix'], lw['w_mq'], lw['w_gc'], lw['w_br_mem'], lw['w_out'],
      lw['norm_ffn'], lw['w_rt'], lw['b_rt'])
    return x1, xn2, route, counts, conv_new, h_last[:, 0], s_new, mk, mv


def _sample_mixer(x, conv_prev, h_prev, s_prev, mem_k, mem_v, counts0, lw, dims):
    n, d = x.shape
    w, heads, dk, dv, hm = dims['w'], dims['heads'], dims['dk'], dims['dv'], dims['hm']
    hk, hv = heads * dk, heads * dv
    kw = lw['conv_w'].shape[0]
    n_in = lw['w_in'].shape[1]
    tn = _tile(n_in, 1024)
    z = pl.pallas_call(
        _proj_kernel,
        grid=(n_in // tn,),
        in_specs=[_const_spec((n, d)), _const_spec((1, d)), pl.BlockSpec((d, tn), lambda j: (0, j))],
        out_specs=pl.BlockSpec((n, tn), lambda j: (0, j)),
        out_shape=jax.ShapeDtypeStruct((n, n_in), F32),
        compiler_params=_params(("parallel",)),
    )(x, lw['norm_mix'], lw['w_in'])
    o0 = 0
    parts = []
    for sz in (w, w, hk, hk, hv, hv, hm, d, d, d):
        parts.append(z[:, o0:o0 + sz])
        o0 += sz
    ux, uy, q, k, v, g, mq, ga, gb, gc = parts

    a_g, conv_new, h_new = pl.pallas_call(
        functools.partial(_lru_sample_kernel, at_start=(PAST_LEN == 0)),
        out_shape=[jax.ShapeDtypeStruct((n, d), F32), jax.ShapeDtypeStruct((n, kw - 1, w), F32),
                   jax.ShapeDtypeStruct((n, w), F32)],
        compiler_params=pltpu.CompilerParams(vmem_limit_bytes=VMEM_LIMIT),
    )(ux, uy, ga, conv_prev, h_prev, lw['conv_w'], lw['conv_b'], lw['w_r'], lw['b_r'], lw['w_i'], lw['b_i'],
      lw['lam'], lw['w_br_lru'])

    cos, sin = _rope_tables(PAST_LEN + jnp.arange(1, dtype=jnp.int32), dk)
    qr, kr = pl.pallas_call(
        functools.partial(_rot_sample_kernel, heads=heads),
        out_shape=[jax.ShapeDtypeStruct((n, hk), F32), jax.ShapeDtypeStruct((n, hk), F32)],
    )(q, k, cos, sin)

    bb = _tile(n, 4)

    def cols(t):
        return t.reshape(n // bb, bb, heads, dk).transpose(0, 2, 3, 1)

    def rows(t):
        return t.reshape(n // bb, bb, t.shape[1])

    o, s_new = pl.pallas_call(
        functools.partial(_ret_sample_kernel, heads=heads),
        grid=(n // bb,),
        in_specs=[pl.BlockSpec((1, heads, dk, bb), lambda i: (i, 0, 0, 0)),
                  pl.BlockSpec((1, heads, dk, bb), lambda i: (i, 0, 0, 0)),
                  pl.BlockSpec((1, bb, hk), lambda i: (i, 0, 0)), pl.BlockSpec((1, bb, hk), lambda i: (i, 0, 0)),
                  pl.BlockSpec((1, bb, hv), lambda i: (i, 0, 0)),
                  pl.BlockSpec((bb, heads, dk, dv), lambda i: (i, 0, 0, 0))],
        out_specs=[pl.BlockSpec((1, bb, hv), lambda i: (i, 0, 0)),
                   pl.BlockSpec((bb, heads, dk, dv), lambda i: (i, 0, 0, 0))],
        out_shape=[jax.ShapeDtypeStruct((n // bb, bb, hv), F32), jax.ShapeDtypeStruct(s_prev.shape, F32)],
        compiler_params=_params(("parallel",)),
    )(cols(qr), cols(kr), rows(qr), rows(kr), rows(v), s_prev)
    o = o.reshape(n, hv)
    bb = _tile(n, SUBLANES)

    m = mem_k.shape[1]
    c = pl.pallas_call(
        functools.partial(_attn_sample_kernel, heads=dims['mheads']),
        grid=(n // bb,),
        in_specs=[pl.BlockSpec((bb,) + mem_k.shape[2:], lambda i: (i, 0, 0)),
                  pl.BlockSpec((bb,) + mem_k.shape[1:], lambda i: (i, 0, 0, 0)),
                  pl.BlockSpec((bb,) + mem_v.shape[1:], lambda i: (i, 0, 0, 0))],
        out_specs=pl.BlockSpec((bb,) + mem_k.shape[2:], lambda i: (i, 0, 0)),
        out_shape=jax.ShapeDtypeStruct((n,) + mem_k.shape[2:], F32),
        compiler_params=_params(("parallel",)),
    )(mq.reshape((n,) + mem_k.shape[2:]), mem_k, mem_v).reshape(n, hm)

    x1, xn2, route, counts = pl.pallas_call(
        functools.partial(_merge_sample_kernel, heads=heads, n_groups=dims['g'], per_group=dims['e']),
        out_shape=[jax.ShapeDtypeStruct((n, d), F32), jax.ShapeDtypeStruct((n, d // 2), F32),
                   jax.ShapeDtypeStruct((n, LANES), F32), jax.ShapeDtypeStruct((1, LANES), F32)],
        compiler_params=pltpu.CompilerParams(vmem_limit_bytes=VMEM_LIMIT),
    )(x, a_g, o, g, c, gb, gc, lw['w_br_ret'], lw['w_br_mem'], lw['w_out'], lw['norm_ffn'], lw['w_rt'],
      lw['b_rt'], counts0)
    return x1, xn2, route, counts, conv_new, h_new, s_new


def _moe(xn2_parts, route_parts, counts, lw, dims, tm=256):
    xn2 = jnp.concatenate(xn2_parts, axis=0)
    t, d = xn2.shape[0], dims['d']
    g, n_exp = dims['g'], dims['g'] * dims['e']
    ff = lw['w_gate'].shape[2]
    counts = counts[0, g:g + n_exp].astype(jnp.int32)
    padded = ((counts + tm - 1) // tm) * tm
    ends = jnp.cumsum(padded)
    offs = ends - padded
    n_tiles = 2 * ((TOP_K * t + n_exp * tm + 2 * tm - 1) // (2 * tm))
    n_slots = n_tiles * tm
    half = n_tiles // 2
    dests = [offs.at[r[:, 0:TOP_K].astype(jnp.int32)].get(mode='promise_in_bounds')
             + r[:, 4:4 + TOP_K].astype(jnp.int32) for r in route_parts]
    dest = jnp.concatenate(dests, axis=0).reshape(-1)
    src = jnp.zeros((n_slots,), jnp.int32).at[dest].set(
        jnp.arange(TOP_K * t, dtype=jnp.int32) // TOP_K, mode='promise_in_bounds', unique_indices=True)
    n_used = (ends[-1] // tm).astype(jnp.int32)
    tile_row = jnp.arange(n_tiles, dtype=jnp.int32) * tm
    tile_e = jnp.sum((ends[None, :] <= tile_row[:, None]).astype(jnp.int32), axis=1)
    last_e = jnp.sum((ends <= (n_used - 1) * tm).astype(jnp.int32))
    tile_e = jnp.where(tile_row < ends[-1], tile_e, last_e)
    xs_a = xn2.at[src[:half * tm]].get(mode='promise_in_bounds')
    xs_b = xn2.at[src[half * tm:]].get(mode='promise_in_bounds')

    def map_a(i, te, nu):
        return (jnp.minimum(jnp.minimum(i, nu[0] - 1), half - 1), 0)

    def map_b(i, te, nu):
        return (jnp.clip(jnp.minimum(i, nu[0] - 1) - half, 0, half - 1), 0)

    ys = pl.pallas_call(
        functools.partial(_expert_kernel, half=half),
        grid_spec=pltpu.PrefetchScalarGridSpec(
            num_scalar_prefetch=2,
            grid=(n_tiles,),
            in_specs=[pl.BlockSpec((tm, d // 2), map_a), pl.BlockSpec((tm, d // 2), map_b),
                      pl.BlockSpec((1, d, ff), lambda i, te, nu: (te[i], 0, 0)),
                      pl.BlockSpec((1, d, ff), lambda i, te, nu: (te[i], 0, 0)),
                      pl.BlockSpec((1, ff, d), lambda i, te, nu: (te[i], 0, 0))],
            out_specs=pl.BlockSpec((tm, d), lambda i, te, nu: (i, 0)),
        ),
        out_shape=jax.ShapeDtypeStruct((n_slots, d), F32),
        compiler_params=_params(("arbitrary",)),
    )(tile_e, n_used.reshape(1), xs_a, xs_b, lw['w_gate'], lw['w_up'], lw['w_down'])
    return [tuple(ys.at[dp[:, j]].get(mode='promise_in_bounds') for j in range(TOP_K)) for dp in dests]


def _final(x1, y_pair, route, g):
    t, d = x1.shape
    tl = _tile(t, 512)
    row = pl.BlockSpec((tl, d), lambda i: (i, 0))
    return pl.pallas_call(
        _final_kernel,
        grid=(t // tl,),
        in_specs=[row, row, row, pl.BlockSpec((tl, LANES), lambda i: (i, 0)), _const_spec((1, d))],
        out_specs=row,
        out_shape=jax.ShapeDtypeStruct((t, d), F32),
        compiler_params=_params(("parallel",)),
    )(x1, y_pair[0], y_pair[1], route, g)


def kernel(x_prompt, x_sample, mem_prompt, state_conv, state_lru, state_ret, cache_mem_k, cache_mem_v, norm_mix, norm_ffn, norm_mem, norm_final, w_in, w_mem_kv, conv_w, conv_b, w_r, b_r, w_i, b_i, lru_lambda, w_branch_lru, w_branch_ret, w_branch_mem, w_out, w_grp, b_grp, w_exp_router, b_exp_router, w_gate, w_up, w_down):
    depth = w_in.shape[0]
    assert depth == 1, "the two request groups are chained per layer only for a single-layer trunk"
    assert x_sample.shape[1] == 1
    bp, lp, d = x_prompt.shape
    ns = x_sample.shape[0]
    heads, dk, dv = state_ret.shape[2], state_ret.shape[3], state_ret.shape[4]
    mheads, mdh = cache_mem_k.shape[3], cache_mem_k.shape[4]
    dims = dict(d=d, w=state_lru.shape[2], heads=heads, dk=dk, dv=dv, hk=heads * dk, hv=heads * dv,
                mheads=mheads, hm=mheads * mdh, g=w_exp_router.shape[1], e=w_exp_router.shape[3])
    layer = 0
    p = dict(norm_mix=norm_mix[layer], norm_ffn=norm_ffn[layer], norm_mem=norm_mem[layer], w_in=w_in[layer],
             w_mem_kv=w_mem_kv[layer], conv_w=conv_w[layer], conv_b=conv_b[layer], w_r=w_r[layer], b_r=b_r[layer],
             w_i=w_i[layer], b_i=b_i[layer], lru_lambda=lru_lambda[layer], w_branch_lru=w_branch_lru[layer],
             w_branch_ret=w_branch_ret[layer], w_branch_mem=w_branch_mem[layer], w_out=w_out[layer],
             w_grp=w_grp[layer], b_grp=b_grp[layer], w_exp_router=w_exp_router[layer],
             b_exp_router=b_exp_router[layer], w_gate=w_gate[layer], w_up=w_up[layer], w_down=w_down[layer])
    lw = _prep_layer(p, dims)

    x1p, xn2p, route_p, counts, conv_p, lru_p, ret_p, mk, mv = _prompt_mixer(x_prompt, mem_prompt, lw, dims)
    x1s, xn2s, route_s, counts, conv_s, lru_s, ret_s = _sample_mixer(
        x_sample[:, 0], state_conv[layer], state_lru[layer], state_ret[layer], cache_mem_k[layer],
        cache_mem_v[layer], counts, lw, dims)

    tp = bp * lp
    route_p = route_p.reshape(tp, LANES)
    yg_p, yg_s = _moe([xn2p.reshape(tp, d // 2), xn2s], [route_p, route_s], counts, lw, dims)
    yp = _final(x1p.reshape(tp, d), yg_p, route_p, _row(norm_final)).reshape(bp, lp, d)
    ys = _final(x1s, yg_s, route_s, _row(norm_final)).reshape(ns, 1, d)
    mshape = (1, bp, mem_prompt.shape[1], mheads, mdh)
    return (yp, ys, conv_p[None], lru_p[None], ret_p[None], mk.reshape(mshape), mv.reshape(mshape),
            conv_s[None], lru_s[None], ret_s[None])
```

```python
import functools
import math

import jax
import jax.numpy as jnp
from jax import lax
from jax.experimental import pallas as pl
from jax.experimental.pallas import tpu as pltpu

F32 = jnp.float32
BF16 = jnp.bfloat16

EPS = 1e-6
LRU_C = 8.0
ROPE_BASE = 10000.0
PAST_LEN = 16384
TOP_K = 2
LANES = 128
SUBLANES = 8
VMEM_LIMIT = 56 * 1024 * 1024


def _dot(a, b):
    return jnp.dot(a, b, preferred_element_type=F32)


def _dot_nt(a, b):
    return lax.dot_general(a, b, (((1,), (1,)), ((), ())), preferred_element_type=F32)


def _dot_tn(a, b):
    return lax.dot_general(a, b, (((0,), (0,)), ((), ())), preferred_element_type=F32)


def _rms(x, g):
    return x * lax.rsqrt(jnp.mean(x * x, axis=-1, keepdims=True) + EPS) * g


def _sigmoid(x):
    return 0.5 * jnp.tanh(0.5 * x) + 0.5


def _silu(x):
    return x * _sigmoid(x)


def _gelu_tanh(x):
    return 0.5 * x * (1.0 + jnp.tanh(math.sqrt(2.0 / math.pi) * (x + 0.044715 * (x * x * x))))


def _softplus(x):
    return jnp.maximum(x, 0.0) + jnp.log1p(jnp.exp(-jnp.abs(x)))


def _const_spec(shape):
    nd = len(shape)
    return pl.BlockSpec(shape, lambda *_: (0,) * nd, pipeline_mode=pl.Buffered(1))


def _params(sem):
    return pltpu.CompilerParams(dimension_semantics=sem, vmem_limit_bytes=VMEM_LIMIT)


def _lru_coeffs(uc, wr_ref, br, wi_ref, bi, lam):
    nb, bs = wr_ref.shape[0], wr_ref.shape[1]
    ucb = uc.astype(BF16)
    r_lin = jnp.concatenate([_dot(ucb[:, n * bs:(n + 1) * bs], wr_ref[n]) for n in range(nb)], axis=-1)
    i_lin = jnp.concatenate([_dot(ucb[:, n * bs:(n + 1) * bs], wi_ref[n]) for n in range(nb)], axis=-1)
    r = _sigmoid(r_lin + br)
    i = _sigmoid(i_lin + bi)
    log_a = (-LRU_C) * r * _softplus(-lam)
    a = jnp.exp(log_a)
    t = 1.0 - a * a
    mult = t * lax.rsqrt(jnp.maximum(t, 1e-37))
    return a, mult, i


def _scan_rows(a, b, h0):
    rows, width = a.shape
    grp = SUBLANES
    n_grp = rows // grp
    a = a.reshape(n_grp, grp, width)
    b = b.reshape(n_grp, grp, width)
    row = lax.broadcasted_iota(jnp.int32, a.shape, 1)
    d = 1
    while d < grp:
        a_sh = pltpu.roll(a, d, 1)
        b_sh = pltpu.roll(b, d, 1)
        keep = row >= d
        b = jnp.where(keep, b + a * b_sh, b)
        a = jnp.where(keep, a * a_sh, a)
        d *= 2
    out = []
    for g in range(n_grp):
        hg = a[g] * h0 + b[g]
        out.append(hg)
        h0 = hg[grp - 1:grp, :]
    return jnp.concatenate(out, axis=0)


def _rotary(t, cos, sin):
    half = t.shape[-1] // 2
    t1, t2 = t[:, :half], t[:, half:]
    return jnp.concatenate([t1 * cos - t2 * sin, t2 * cos + t1 * sin], axis=-1)


def _route(logits, run, n_groups, per_group):
    col = lax.broadcasted_iota(jnp.int32, logits.shape, 1)
    big = jnp.int32(1 << 20)
    neg = jnp.float32(-jnp.inf)
    gl = jnp.where(col < n_groups, logits, neg)
    gmax = jnp.max(gl, axis=-1, keepdims=True)
    g_idx = jnp.min(jnp.where(gl == gmax, col, big), axis=-1, keepdims=True)
    g_val = 1.0 / jnp.sum(jnp.exp(gl - gmax), axis=-1, keepdims=True)
    lo = n_groups + per_group * g_idx
    el = jnp.where((col >= lo) & (col < lo + per_group), logits, neg)
    m1 = jnp.max(el, axis=-1, keepdims=True)
    i1 = jnp.min(jnp.where(el == m1, col, big), axis=-1, keepdims=True)
    el2 = jnp.where(col == i1, neg, el)
    m2 = jnp.max(el2, axis=-1, keepdims=True)
    i2 = jnp.min(jnp.where(el2 == m2, col, big), axis=-1, keepdims=True)
    e2 = jnp.exp(m2 - m1)
    w1 = g_val / (1.0 + e2)
    w2 = g_val * e2 / (1.0 + e2)
    f1 = (i1 - n_groups).astype(F32)
    f2 = (i2 - n_groups).astype(F32)
    hit1, hit2 = col == i1, col == i2
    hits = jnp.where(hit1 | hit2, 1.0, 0.0)
    rows = logits.shape[0]
    earlier = lax.broadcasted_iota(jnp.int32, (rows, rows), 1) < lax.broadcasted_iota(jnp.int32, (rows, rows), 0)
    before = run + _dot(jnp.where(earlier, 1.0, 0.0).astype(BF16), hits.astype(BF16))
    r1 = jnp.sum(jnp.where(hit1, before, 0.0), axis=-1, keepdims=True)
    r2 = jnp.sum(jnp.where(hit2, before, 0.0), axis=-1, keepdims=True)
    route = f1
    for j, val in enumerate((f2, w1, w2, r1, r2), start=1):
        route = jnp.where(col == j, val, route)
    route = jnp.where(col > 5, 0.0, route)
    return route, run + jnp.sum(hits, axis=0, keepdims=True)


def _merge_tail(x, merged, wout_ref, gffn, wrt_ref, brt, run, n_groups, per_group):
    x1 = x + _dot(merged.astype(BF16), wout_ref[...])
    xn2 = _rms(x1, gffn).astype(BF16)
    logits = _dot(xn2, wrt_ref[...]) + brt
    route, run = _route(logits, run, n_groups, per_group)
    return x1, xn2.astype(F32), route, run


def _lru_prompt_kernel(x_ref, g_ref, wxy_ref, wga_ref, cw_ref, cb_ref, wr_ref, br_ref, wi_ref, bi_ref,
                       lam_ref, wbr_ref, out_ref, conv_ref, h_ref, ccar, hcar):
    c = pl.program_id(1)
    tl = x_ref.shape[1]
    width = hcar.shape[1]
    kw = cw_ref.shape[0]

    @pl.when(c == 0)
    def _():
        ccar[...] = jnp.zeros_like(ccar)
        hcar[...] = jnp.zeros_like(hcar)

    xn = _rms(x_ref[0], g_ref[...]).astype(BF16)
    z = _dot(xn, wxy_ref[...])
    ux, uy = z[:, :width], z[:, width:]
    row = lax.broadcasted_iota(jnp.int32, ux.shape, 0)
    acc = cw_ref[0:1, :] * ux
    for j in range(1, kw):
        prev = jnp.where(row == 0, ccar[j - 1:j, :], pltpu.roll(acc, 1, 0))
        ccar[j - 1:j, :] = acc[tl - 1:tl, :]
        acc = prev + cw_ref[j:j + 1, :] * ux
    uc = acc + cb_ref[...]
    a, mult, gate = _lru_coeffs(uc, wr_ref, br_ref[...], wi_ref, bi_ref[...], lam_ref[...])
    mult = jnp.where(row + c * tl == 0, 1.0, mult)
    h = _scan_rows(a, mult * gate * uc, hcar[...])
    hcar[...] = h[tl - 1:tl, :]
    tail = ux[tl - (kw - 1):tl, :]
    a_out = _dot((h * _gelu_tanh(uy)).astype(BF16), wbr_ref[...])
    ga = _dot(xn, wga_ref[...])
    out_ref[0] = (_sigmoid(ga) * a_out).astype(out_ref.dtype)

    @pl.when(c == pl.num_programs(1) - 1)
    def _():
        conv_ref[0] = tail
        h_ref[0] = h[tl - 1:tl, :]


def _ret_prompt_kernel(x_ref, g_ref, wq_ref, wk_ref, wv_ref, wg_ref, wgb_ref, cos_ref, sin_ref, wbr_ref,
                       out_ref, s_out_ref, s_scr, y_scr, *, heads, chunk):
    c = pl.program_id(1)
    tl = x_ref.shape[1]
    dk = wq_ref.shape[1] // heads
    dv = wv_ref.shape[1] // heads

    @pl.when(c == 0)
    def _():
        s_scr[...] = jnp.zeros_like(s_scr)

    xn = _rms(x_ref[0], g_ref[...]).astype(BF16)
    q = _dot(xn, wq_ref[...])
    k = _dot(xn, wk_ref[...])
    v = _dot(xn, wv_ref[...])
    g = _dot(xn, wg_ref[...])
    cos, sin = cos_ref[...], sin_ref[...]
    n_i = lax.broadcasted_iota(jnp.int32, (chunk, chunk), 0)
    m_i = lax.broadcasted_iota(jnp.int32, (chunk, chunk), 1)
    diff = (n_i - m_i).astype(F32)
    rowk = lax.broadcasted_iota(jnp.int32, (chunk, dk), 0).astype(F32)
    for h in range(heads):
        log_g = math.log1p(-(2.0 ** (-5.0 - h)))
        dmask = jnp.where(diff >= 0, jnp.exp(jnp.maximum(diff, 0.0) * log_g), 0.0)
        q_decay = jnp.exp((rowk + 1.0) * log_g)
        k_decay = jnp.exp((chunk - 1.0 - rowk) * log_g)
        s_decay = math.exp(chunk * log_g)
        for sub in range(tl // chunk):
            r0 = sub * chunk
            cs, sn = cos[r0:r0 + chunk], sin[r0:r0 + chunk]
            qr = _rotary(q[r0:r0 + chunk, h * dk:(h + 1) * dk], cs, sn)
            kr = _rotary(k[r0:r0 + chunk, h * dk:(h + 1) * dk], cs, sn) * (dk ** -0.5)
            vh = v[r0:r0 + chunk, h * dv:(h + 1) * dv].astype(BF16)
            s = s_scr[h]
            att = _dot_nt(qr.astype(BF16), kr.astype(BF16)) * dmask
            o = _dot(att.astype(BF16), vh) + _dot((qr * q_decay).astype(BF16), s.astype(BF16))
            s_scr[h] = s * s_decay + _dot_tn((kr * k_decay).astype(BF16), vh)
            o = o * lax.rsqrt(jnp.mean(o * o, axis=-1, keepdims=True) + EPS)
            gh = g[r0:r0 + chunk, h * dv:(h + 1) * dv]
            y_scr[r0:r0 + chunk, h * dv:(h + 1) * dv] = (_silu(gh) * o).astype(BF16)
    b_out = _dot(y_scr[...], wbr_ref[...])
    gb = _dot(xn, wgb_ref[...])
    out_ref[0] = (_sigmoid(gb) * b_out).astype(out_ref.dtype)

    @pl.when(c == pl.num_programs(1) - 1)
    def _():
        s_out_ref[0] = s_scr[...]


def _memkv_kernel(m_ref, g_ref, w_ref, k_ref, v_ref):
    width = k_ref.shape[-1]
    mn = _rms(m_ref[0], g_ref[...]).astype(BF16)
    kv = _dot(mn, w_ref[...])
    k_ref[0] = kv[:, :width]
    v_ref[0] = kv[:, width:]


def _attn_merge_prompt_kernel(x_ref, ag_ref, bg_ref, mk_ref, mv_ref, g_ref, wmq_ref, wgc_ref, wbm_ref,
                              wout_ref, gffn_ref, wrt_ref, brt_ref, x1_ref, xn2_ref, route_ref, cnt_ref,
                              *, heads, n_groups, per_group):
    @pl.when((pl.program_id(0) == 0) & (pl.program_id(1) == 0))
    def _():
        cnt_ref[...] = jnp.zeros_like(cnt_ref)

    x = x_ref[0]
    xn = _rms(x, g_ref[...]).astype(BF16)
    mq = _dot(xn, wmq_ref[...]).astype(BF16)
    dh = mq.shape[1] // heads
    mk = mk_ref[0].astype(BF16)
    mv = mv_ref[0].astype(BF16)
    cs = []
    for h in range(heads):
        hs = slice(h * dh, (h + 1) * dh)
        lg = _dot_nt(mq[:, hs], mk[:, hs]) * (dh ** -0.5)
        p = jnp.exp(lg - jnp.max(lg, axis=-1, keepdims=True))
        den = jnp.sum(p, axis=-1, keepdims=True)
        cs.append(_dot(p.astype(BF16), mv[:, hs]) / den)
    c_out = _dot(jnp.concatenate(cs, axis=-1).astype(BF16), wbm_ref[...])
    gc = _dot(xn, wgc_ref[...])
    merged = _sigmoid(gc) * c_out + ag_ref[0].astype(F32) + bg_ref[0].astype(F32)
    x1, xn2, route, run = _merge_tail(x, merged, wout_ref, gffn_ref[...], wrt_ref, brt_ref[...], cnt_ref[...],
                                      n_groups, per_group)
    x1_ref[0] = x1
    xn2_ref[...] = xn2
    route_ref[0] = route
    cnt_ref[...] = run


def _proj_kernel(x_ref, g_ref, w_ref, z_ref):
    xn = _rms(x_ref[...], g_ref[...]).astype(BF16)
    z_ref[...] = _dot(xn, w_ref[...])


def _lru_sample_kernel(ux_ref, uy_ref, ga_ref, cprev_ref, hprev_ref, cw_ref, cb_ref, wr_ref, br_ref, wi_ref,
                       bi_ref, lam_ref, wbr_ref, out_ref, conv_ref, h_ref, *, at_start):
    kw = cw_ref.shape[0]
    ux = ux_ref[...]
    uc = cb_ref[...] + cw_ref[kw - 1:kw, :] * ux
    for j in range(kw - 1):
        uc = uc + cw_ref[j:j + 1, :] * cprev_ref[:, j, :]
    a, mult, gate = _lru_coeffs(uc, wr_ref, br_ref[...], wi_ref, bi_ref[...], lam_ref[...])
    if at_start:
        mult = jnp.ones_like(mult)
    h = a * hprev_ref[...] + mult * gate * uc
    h_ref[...] = h
    for j in range(kw - 2):
        conv_ref[:, j, :] = cprev_ref[:, j + 1, :]
    conv_ref[:, kw - 2, :] = ux
    a_out = _dot((h * _gelu_tanh(uy_ref[...])).astype(BF16), wbr_ref[...])
    out_ref[...] = _sigmoid(ga_ref[...]) * a_out


def _rot_sample_kernel(q_ref, k_ref, cos_ref, sin_ref, qo_ref, ko_ref, *, heads):
    dk = q_ref.shape[1] // heads
    cos, sin = cos_ref[...], sin_ref[...]
    for h in range(heads):
        hs = slice(h * dk, (h + 1) * dk)
        qo_ref[:, hs] = _rotary(q_ref[:, hs], cos, sin)
        ko_ref[:, hs] = _rotary(k_ref[:, hs], cos, sin) * (dk ** -0.5)


def _ret_sample_kernel(qt_ref, kt_ref, q_ref, k_ref, v_ref, s_ref, o_ref, s_out_ref, *, heads):
    bb = q_ref.shape[1]
    dk = s_ref.shape[2]
    dv = s_ref.shape[3]
    for j in range(bb):
        for h in range(heads):
            decay = 1.0 - 2.0 ** (-5.0 - h)
            qcol = qt_ref[0, h, :, j:j + 1]
            kcol = kt_ref[0, h, :, j:j + 1]
            qrow = q_ref[0, j:j + 1, h * dk:(h + 1) * dk]
            krow = k_ref[0, j:j + 1, h * dk:(h + 1) * dk]
            vrow = v_ref[0, j:j + 1, h * dv:(h + 1) * dv]
            s = s_ref[j, h]
            att = jnp.sum(qrow * krow, axis=-1, keepdims=True)
            o = att * vrow + jnp.sum((qcol * decay) * s, axis=0, keepdims=True)
            o_ref[0, j:j + 1, h * dv:(h + 1) * dv] = o
            s_out_ref[j, h] = s * decay + kcol * vrow


def _attn_sample_kernel(mq_ref, k_ref, v_ref, c_ref, *, heads):
    bb, _, dh = mq_ref.shape
    for j in range(bb):
        q = mq_ref[j]
        lg = jnp.sum(k_ref[j] * q[None], axis=-1, keepdims=True) * (dh ** -0.5)
        p = jnp.exp(lg - jnp.max(lg, axis=0, keepdims=True))
        den = jnp.sum(p, axis=0)
        c_ref[j] = jnp.sum(v_ref[j] * p, axis=0) / den


def _merge_sample_kernel(x_ref, ag_ref, o_ref, g_ref, c_ref, gb_ref, gc_ref, wbr_ref, wbm_ref, wout_ref,
                         gffn_ref, wrt_ref, brt_ref, run_ref, xn2_table_ref, x1_ref, xn2_ref, route_ref, cnt_ref,
                         *, heads, n_groups, per_group):
    del xn2_table_ref
    dv = o_ref.shape[1] // heads
    ys = []
    for h in range(heads):
        o = o_ref[:, h * dv:(h + 1) * dv]
        o = o * lax.rsqrt(jnp.mean(o * o, axis=-1, keepdims=True) + EPS)
        ys.append((_silu(g_ref[:, h * dv:(h + 1) * dv]) * o).astype(BF16))
    b_out = _dot(jnp.concatenate(ys, axis=-1), wbr_ref[...])
    c_out = _dot(c_ref[...].astype(BF16), wbm_ref[...])
    merged = ag_ref[...] + _sigmoid(gb_ref[...]) * b_out + _sigmoid(gc_ref[...]) * c_out
    x1, xn2, route, run = _merge_tail(x_ref[...], merged, wout_ref, gffn_ref[...], wrt_ref, brt_ref[...],
                                      run_ref[...], n_groups, per_group)
    x1_ref[...] = x1
    xn2_ref[...] = xn2
    route_ref[...] = route
    cnt_ref[...] = run


def _expert_kernel(te_ref, nu_ref, xa_ref, xb_ref, wg_ref, wu_ref, wd_ref, ys_ref, *, half):
    i = pl.program_id(0)

    @pl.when(i < nu_ref[0])
    def _():
        x = jnp.where(i < half, xa_ref[...], xb_ref[...]).astype(BF16)
        hg = _dot(x, wg_ref[0].astype(BF16))
        hu = _dot(x, wu_ref[0].astype(BF16))
        hid = (_silu(hg) * hu).astype(BF16)
        ys_ref[...] = _dot(hid, wd_ref[0].astype(BF16))

    @pl.when(i >= nu_ref[0])
    def _():
        ys_ref[...] = jnp.zeros_like(ys_ref)


def _final_kernel(x1_ref, y1_ref, y2_ref, route_ref, g_ref, out_ref):
    w1 = route_ref[:, 2:3]
    w2 = route_ref[:, 3:4]
    x2 = x1_ref[...] + (w1 * y1_ref[...] + w2 * y2_ref[...])
    out_ref[...] = _rms(x2, g_ref[...])


def _tile(n, target):
    t = min(n, target)
    while n % t:
        t //= 2
    return t


def _row(v):
    return v.reshape(1, -1).astype(F32)


def _prep_layer(p, dims):
    d, w, hk, hv, hm = dims['d'], dims['w'], dims['hk'], dims['hv'], dims['hm']
    w_in = p['w_in']
    cuts = [0, w, 2 * w, 2 * w + hk, 2 * w + 2 * hk, 2 * w + 2 * hk + hv, 2 * w + 2 * hk + 2 * hv,
            2 * w + 2 * hk + 2 * hv + hm, w_in.shape[1]]
    seg = [w_in[:, cuts[i]:cuts[i + 1]].astype(BF16) for i in range(8)]
    gates = seg[7]
    g, e = p['w_exp_router'].shape[0], p['w_exp_router'].shape[2]
    w_rt = jnp.concatenate([p['w_grp'], jnp.moveaxis(p['w_exp_router'], 0, 1).reshape(d, g * e)], axis=1)
    w_rt = jnp.pad(w_rt, ((0, 0), (0, LANES - w_rt.shape[1]))).astype(BF16)
    b_rt = jnp.pad(jnp.concatenate([p['b_grp'], p['b_exp_router'].reshape(-1)]), (0, LANES - g - g * e))
    return dict(
        w_in=w_in.astype(BF16),
        w_xy=jnp.concatenate([seg[0], seg[1]], axis=1), w_q=seg[2], w_k=seg[3], w_v=seg[4], w_g=seg[5],
        w_mq=seg[6], w_ga=gates[:, :d], w_gb=gates[:, d:2 * d], w_gc=gates[:, 2 * d:],
        norm_mix=_row(p['norm_mix']), norm_ffn=_row(p['norm_ffn']), norm_mem=_row(p['norm_mem']),
        w_mem_kv=p['w_mem_kv'].astype(BF16),
        conv_w=p['conv_w'].astype(F32), conv_b=_row(p['conv_b']),
        w_r=p['w_r'].astype(BF16), b_r=_row(p['b_r']), w_i=p['w_i'].astype(BF16), b_i=_row(p['b_i']),
        lam=_row(p['lru_lambda']),
        w_br_lru=p['w_branch_lru'].astype(BF16), w_br_ret=p['w_branch_ret'].astype(BF16),
        w_br_mem=p['w_branch_mem'].astype(BF16), w_out=p['w_out'].astype(BF16),
        w_rt=w_rt, b_rt=_row(b_rt),
        w_gate=p['w_gate'].reshape((g * e,) + p['w_gate'].shape[2:]),
        w_up=p['w_up'].reshape((g * e,) + p['w_up'].shape[2:]),
        w_down=p['w_down'].reshape((g * e,) + p['w_down'].shape[2:]),
    )


def _rope_tables(pos, dk):
    half = dk // 2
    inv = ROPE_BASE ** (-jnp.linspace(0.0, 1.0, half, dtype=F32))
    ang = pos.astype(F32)[:, None] * inv[None, :]
    return jnp.cos(ang), jnp.sin(ang)


def _prompt_mixer(x, mem, lw, dims, extra_rows):
    b, l, d = x.shape
    w, heads, dk, dv = dims['w'], dims['heads'], dims['dk'], dims['dv']
    kw = lw['conv_w'].shape[0]
    tl = _tile(l, 256)
    grid = (b, l // tl)
    xspec = pl.BlockSpec((1, tl, d), lambda i, c: (i, c, 0))
    sem = ("parallel", "arbitrary")

    a_g, conv_new, h_last = pl.pallas_call(
        _lru_prompt_kernel,
        grid=grid,
        in_specs=[xspec, _const_spec((1, d)), _const_spec((d, 2 * w)), _const_spec((d, d)),
                  _const_spec((kw, w)), _const_spec((1, w)), _const_spec(lw['w_r'].shape), _const_spec((1, w)),
                  _const_spec(lw['w_i'].shape), _const_spec((1, w)), _const_spec((1, w)), _const_spec((w, d))],
        out_specs=[pl.BlockSpec((1, tl, d), lambda i, c: (i, c, 0)),
                   pl.BlockSpec((1, kw - 1, w), lambda i, c: (i, 0, 0)),
                   pl.BlockSpec((1, 1, w), lambda i, c: (i, 0, 0))],
        out_shape=[jax.ShapeDtypeStruct((b, l, d), BF16), jax.ShapeDtypeStruct((b, kw - 1, w), F32),
                   jax.ShapeDtypeStruct((b, 1, w), F32)],
        scratch_shapes=[pltpu.VMEM((SUBLANES, w), F32), pltpu.VMEM((1, w), F32)],
        compiler_params=_params(sem),
    )(x, lw['norm_mix'], lw['w_xy'], lw['w_ga'], lw['conv_w'], lw['conv_b'], lw['w_r'], lw['b_r'],
      lw['w_i'], lw['b_i'], lw['lam'], lw['w_br_lru'])

    chunk = _tile(tl, 128)
    cos, sin = _rope_tables(jnp.arange(l, dtype=jnp.int32), dk)
    half = dk // 2
    b_g, s_new = pl.pallas_call(
        functools.partial(_ret_prompt_kernel, heads=heads, chunk=chunk),
        grid=grid,
        in_specs=[xspec, _const_spec((1, d)), _const_spec((d, heads * dk)), _const_spec((d, heads * dk)),
                  _const_spec((d, heads * dv)), _const_spec((d, heads * dv)), _const_spec((d, d)),
                  pl.BlockSpec((tl, half), lambda i, c: (c, 0)), pl.BlockSpec((tl, half), lambda i, c: (c, 0)),
                  _const_spec((heads * dv, d))],
        out_specs=[pl.BlockSpec((1, tl, d), lambda i, c: (i, c, 0)),
                   pl.BlockSpec((1, heads, dk, dv), lambda i, c: (i, 0, 0, 0))],
        out_shape=[jax.ShapeDtypeStruct((b, l, d), BF16), jax.ShapeDtypeStruct((b, heads, dk, dv), F32)],
        scratch_shapes=[pltpu.VMEM((heads, dk, dv), F32), pltpu.VMEM((tl, heads * dv), BF16)],
        compiler_params=_params(sem),
    )(x, lw['norm_mix'], lw['w_q'], lw['w_k'], lw['w_v'], lw['w_g'], lw['w_gb'], cos, sin, lw['w_br_ret'])

    m = mem.shape[1]
    hm = dims['hm']
    mk, mv = pl.pallas_call(
        _memkv_kernel,
        grid=(b,),
        in_specs=[pl.BlockSpec((1, m, d), lambda i: (i, 0, 0)), _const_spec((1, d)), _const_spec((d, 2 * hm))],
        out_specs=[pl.BlockSpec((1, m, hm), lambda i: (i, 0, 0)), pl.BlockSpec((1, m, hm), lambda i: (i, 0, 0))],
        out_shape=[jax.ShapeDtypeStruct((b, m, hm), F32), jax.ShapeDtypeStruct((b, m, hm), F32)],
        compiler_params=_params(("parallel",)),
    )(mem, lw['norm_mem'], lw['w_mem_kv'])

    x1, xn2, route, counts = pl.pallas_call(
        functools.partial(_attn_merge_prompt_kernel, heads=dims['mheads'], n_groups=dims['g'],
                          per_group=dims['e']),
        grid=grid,
        in_specs=[xspec, xspec, xspec,
                  pl.BlockSpec((1, m, hm), lambda i, c: (i, 0, 0)), pl.BlockSpec((1, m, hm), lambda i, c: (i, 0, 0)),
                  _const_spec((1, d)), _const_spec((d, hm)), _const_spec((d, d)), _const_spec((hm, d)),
                  _const_spec((d, d)), _const_spec((1, d)), _const_spec((d, LANES)), _const_spec((1, LANES))],
        out_specs=[xspec, pl.BlockSpec((tl, d), lambda i, c: (i * (l // tl) + c, 0)),
                   pl.BlockSpec((1, tl, LANES), lambda i, c: (i, c, 0)),
                   pl.BlockSpec((1, LANES), lambda i, c: (0, 0))],
        out_shape=[jax.ShapeDtypeStruct((b, l, d), F32), jax.ShapeDtypeStruct((b * l + extra_rows, d), F32),
                   jax.ShapeDtypeStruct((b, l, LANES), F32), jax.ShapeDtypeStruct((1, LANES), F32)],
        compiler_params=_params(("arbitrary", "arbitrary")),
    )(x, a_g, b_g, mk, mv, lw['norm_mix'], lw['w_mq'], lw['w_gc'], lw['w_br_mem'], lw['w_out'],
      lw['norm_ffn'], lw['w_rt'], lw['b_rt'])
    return x1, xn2, route, counts, conv_new, h_last[:, 0], s_new, mk, mv


def _sample_mixer(x, conv_prev, h_prev, s_prev, mem_k, mem_v, counts0, xn2_all, lw, dims):
    n, d = x.shape
    w, heads, dk, dv, hm = dims['w'], dims['heads'], dims['dk'], dims['dv'], dims['hm']
    hk, hv = heads * dk, heads * dv
    kw = lw['conv_w'].shape[0]
    n_in = lw['w_in'].shape[1]
    tn = _tile(n_in, 1024)
    z = pl.pallas_call(
        _proj_kernel,
        grid=(n_in // tn,),
        in_specs=[_const_spec((n, d)), _const_spec((1, d)), pl.BlockSpec((d, tn), lambda j: (0, j))],
        out_specs=pl.BlockSpec((n, tn), lambda j: (0, j)),
        out_shape=jax.ShapeDtypeStruct((n, n_in), F32),
        compiler_params=_params(("parallel",)),
    )(x, lw['norm_mix'], lw['w_in'])
    o0 = 0
    parts = []
    for sz in (w, w, hk, hk, hv, hv, hm, d, d, d):
        parts.append(z[:, o0:o0 + sz])
        o0 += sz
    ux, uy, q, k, v, g, mq, ga, gb, gc = parts

    a_g, conv_new, h_new = pl.pallas_call(
        functools.partial(_lru_sample_kernel, at_start=(PAST_LEN == 0)),
        out_shape=[jax.ShapeDtypeStruct((n, d), F32), jax.ShapeDtypeStruct((n, kw - 1, w), F32),
                   jax.ShapeDtypeStruct((n, w), F32)],
        compiler_params=pltpu.CompilerParams(vmem_limit_bytes=VMEM_LIMIT),
    )(ux, uy, ga, conv_prev, h_prev, lw['conv_w'], lw['conv_b'], lw['w_r'], lw['b_r'], lw['w_i'], lw['b_i'],
      lw['lam'], lw['w_br_lru'])

    cos, sin = _rope_tables(PAST_LEN + jnp.arange(1, dtype=jnp.int32), dk)
    qr, kr = pl.pallas_call(
        functools.partial(_rot_sample_kernel, heads=heads),
        out_shape=[jax.ShapeDtypeStruct((n, hk), F32), jax.ShapeDtypeStruct((n, hk), F32)],
    )(q, k, cos, sin)

    bb = _tile(n, 4)

    def cols(t):
        return t.reshape(n // bb, bb, heads, dk).transpose(0, 2, 3, 1)

    def rows(t):
        return t.reshape(n // bb, bb, t.shape[1])

    o, s_new = pl.pallas_call(
        functools.partial(_ret_sample_kernel, heads=heads),
        grid=(n // bb,),
        in_specs=[pl.BlockSpec((1, heads, dk, bb), lambda i: (i, 0, 0, 0)),
                  pl.BlockSpec((1, heads, dk, bb), lambda i: (i, 0, 0, 0)),
                  pl.BlockSpec((1, bb, hk), lambda i: (i, 0, 0)), pl.BlockSpec((1, bb, hk), lambda i: (i, 0, 0)),
                  pl.BlockSpec((1, bb, hv), lambda i: (i, 0, 0)),
                  pl.BlockSpec((bb, heads, dk, dv), lambda i: (i, 0, 0, 0))],
        out_specs=[pl.BlockSpec((1, bb, hv), lambda i: (i, 0, 0)),
                   pl.BlockSpec((bb, heads, dk, dv), lambda i: (i, 0, 0, 0))],
        out_shape=[jax.ShapeDtypeStruct((n // bb, bb, hv), F32), jax.ShapeDtypeStruct(s_prev.shape, F32)],
        compiler_params=_params(("parallel",)),
    )(cols(qr), cols(kr), rows(qr), rows(kr), rows(v), s_prev)
    o = o.reshape(n, hv)
    bb = _tile(n, SUBLANES)

    m = mem_k.shape[1]
    c = pl.pallas_call(
        functools.partial(_attn_sample_kernel, heads=dims['mheads']),
        grid=(n // bb,),
        in_specs=[pl.BlockSpec((bb,) + mem_k.shape[2:], lambda i: (i, 0, 0)),
                  pl.BlockSpec((bb,) + mem_k.shape[1:], lambda i: (i, 0, 0, 0)),
                  pl.BlockSpec((bb,) + mem_v.shape[1:], lambda i: (i, 0, 0, 0))],
        out_specs=pl.BlockSpec((bb,) + mem_k.shape[2:], lambda i: (i, 0, 0)),
        out_shape=jax.ShapeDtypeStruct((n,) + mem_k.shape[2:], F32),
        compiler_params=_params(("parallel",)),
    )(mq.reshape((n,) + mem_k.shape[2:]), mem_k, mem_v).reshape(n, hm)

    first = xn2_all.shape[0] - n
    assert first % n == 0
    args = (x, a_g, o, g, c, gb, gc, lw['w_br_ret'], lw['w_br_mem'], lw['w_out'], lw['norm_ffn'], lw['w_rt'],
            lw['b_rt'], counts0)
    x1, xn2_all, route, counts = pl.pallas_call(
        functools.partial(_merge_sample_kernel, heads=heads, n_groups=dims['g'], per_group=dims['e']),
        grid=(1,),
        in_specs=[_const_spec(a.shape) for a in args] + [pl.BlockSpec(memory_space=pl.ANY)],
        out_specs=[pl.BlockSpec((n, d), lambda i: (0, 0)), pl.BlockSpec((n, d), lambda i: (first // n, 0)),
                   pl.BlockSpec((n, LANES), lambda i: (0, 0)), pl.BlockSpec((1, LANES), lambda i: (0, 0))],
        out_shape=[jax.ShapeDtypeStruct((n, d), F32), jax.ShapeDtypeStruct(xn2_all.shape, F32),
                   jax.ShapeDtypeStruct((n, LANES), F32), jax.ShapeDtypeStruct((1, LANES), F32)],
        input_output_aliases={len(args): 1},
        compiler_params=_params(("arbitrary",)),
    )(*args, xn2_all)
    return x1, xn2_all, route, counts, conv_new, h_new, s_new


def _moe(xn2, route_parts, counts, lw, dims, tm=256):
    t, d = xn2.shape
    g, n_exp = dims['g'], dims['g'] * dims['e']
    ff = lw['w_gate'].shape[2]
    counts = counts[0, g:g + n_exp].astype(jnp.int32)
    padded = ((counts + tm - 1) // tm) * tm
    ends = jnp.cumsum(padded)
    offs = ends - padded
    n_tiles = 2 * ((TOP_K * t + n_exp * tm + 2 * tm - 1) // (2 * tm))
    n_slots = n_tiles * tm
    half = n_tiles // 2
    experts = jnp.arange(n_exp, dtype=jnp.int32)[None, :]
    dests = []
    for j in range(TOP_K):
        per_part = []
        for r in route_parts:
            ids = r[:, j].astype(jnp.int32)
            base = jnp.sum(jnp.where(ids[:, None] == experts, offs[None, :], 0), axis=1)
            per_part.append(base + r[:, 4 + j].astype(jnp.int32))
        dests.append(per_part)
    dest = jnp.concatenate([dp for per_part in dests for dp in per_part], axis=0)
    tok = jnp.arange(t, dtype=jnp.int32)
    src = jnp.zeros((n_slots,), jnp.int32).at[dest].set(
        jnp.concatenate([tok] * TOP_K), mode='promise_in_bounds', unique_indices=True)
    n_used = (ends[-1] // tm).astype(jnp.int32)
    tile_row = jnp.arange(n_tiles, dtype=jnp.int32) * tm
    tile_e = jnp.sum((ends[None, :] <= tile_row[:, None]).astype(jnp.int32), axis=1)
    last_e = jnp.sum((ends <= (n_used - 1) * tm).astype(jnp.int32))
    tile_e = jnp.where(tile_row < ends[-1], tile_e, last_e)
    xs_a = xn2.at[src[:half * tm]].get(mode='promise_in_bounds')
    xs_b = xn2.at[src[half * tm:]].get(mode='promise_in_bounds')

    def map_a(i, te, nu):
        return (jnp.minimum(jnp.minimum(i, nu[0] - 1), half - 1), 0)

    def map_b(i, te, nu):
        return (jnp.clip(jnp.minimum(i, nu[0] - 1) - half, 0, half - 1), 0)

    ys = pl.pallas_call(
        functools.partial(_expert_kernel, half=half),
        grid_spec=pltpu.PrefetchScalarGridSpec(
            num_scalar_prefetch=2,
            grid=(n_tiles,),
            in_specs=[pl.BlockSpec((tm, d), map_a), pl.BlockSpec((tm, d), map_b),
                      pl.BlockSpec((1, d, ff), lambda i, te, nu: (te[i], 0, 0)),
                      pl.BlockSpec((1, d, ff), lambda i, te, nu: (te[i], 0, 0)),
                      pl.BlockSpec((1, ff, d), lambda i, te, nu: (te[i], 0, 0))],
            out_specs=pl.BlockSpec((tm, d), lambda i, te, nu: (i, 0)),
        ),
        out_shape=jax.ShapeDtypeStruct((n_slots, d), F32),
        compiler_params=_params(("arbitrary",)),
    )(tile_e, n_used.reshape(1), xs_a, xs_b, lw['w_gate'], lw['w_up'], lw['w_down'])
    return [tuple(ys.at[dests[j][p]].get(mode='promise_in_bounds') for j in range(TOP_K))
            for p in range(len(route_parts))]


def _final(x1, y_pair, route, g):
    t, d = x1.shape
    tl = _tile(t, 512)
    row = pl.BlockSpec((tl, d), lambda i: (i, 0))
    return pl.pallas_call(
        _final_kernel,
        grid=(t // tl,),
        in_specs=[row, row, row, pl.BlockSpec((tl, LANES), lambda i: (i, 0)), _const_spec((1, d))],
        out_specs=row,
        out_shape=jax.ShapeDtypeStruct((t, d), F32),
        compiler_params=_params(("parallel",)),
    )(x1, y_pair[0], y_pair[1], route, g)


def kernel(x_prompt, x_sample, mem_prompt, state_conv, state_lru, state_ret, cache_mem_k, cache_mem_v, norm_mix, norm_ffn, norm_mem, norm_final, w_in, w_mem_kv, conv_w, conv_b, w_r, b_r, w_i, b_i, lru_lambda, w_branch_lru, w_branch_ret, w_branch_mem, w_out, w_grp, b_grp, w_exp_router, b_exp_router, w_gate, w_up, w_down):
    depth = w_in.shape[0]
    assert depth == 1, "the two request groups are chained per layer only for a single-layer trunk"
    assert x_sample.shape[1] == 1
    bp, lp, d = x_prompt.shape
    ns = x_sample.shape[0]
    heads, dk, dv = state_ret.shape[2], state_ret.shape[3], state_ret.shape[4]
    mheads, mdh = cache_mem_k.shape[3], cache_mem_k.shape[4]
    dims = dict(d=d, w=state_lru.shape[2], heads=heads, dk=dk, dv=dv, hk=heads * dk, hv=heads * dv,
                mheads=mheads, hm=mheads * mdh, g=w_exp_router.shape[1], e=w_exp_router.shape[3])
    layer = 0
    p = dict(norm_mix=norm_mix[layer], norm_ffn=norm_ffn[layer], norm_mem=norm_mem[layer], w_in=w_in[layer],
             w_mem_kv=w_mem_kv[layer], conv_w=conv_w[layer], conv_b=conv_b[layer], w_r=w_r[layer], b_r=b_r[layer],
             w_i=w_i[layer], b_i=b_i[layer], lru_lambda=lru_lambda[layer], w_branch_lru=w_branch_lru[layer],
             w_branch_ret=w_branch_ret[layer], w_branch_mem=w_branch_mem[layer], w_out=w_out[layer],
             w_grp=w_grp[layer], b_grp=b_grp[layer], w_exp_router=w_exp_router[layer],
             b_exp_router=b_exp_router[layer], w_gate=w_gate[layer], w_up=w_up[layer], w_down=w_down[layer])
    lw = _prep_layer(p, dims)

    x1p, xn2, route_p, counts, conv_p, lru_p, ret_p, mk, mv = _prompt_mixer(x_prompt, mem_prompt, lw, dims, ns)
    x1s, xn2, route_s, counts, conv_s, lru_s, ret_s = _sample_mixer(
        x_sample[:, 0], state_conv[layer], state_lru[layer], state_ret[layer], cache_mem_k[layer],
        cache_mem_v[layer], counts, xn2, lw, dims)

    tp = bp * lp
    route_p = route_p.reshape(tp, LANES)
    yg_p, yg_s = _moe(xn2, [route_p, route_s], counts, lw, dims)
    yp = _final(x1p.reshape(tp, d), yg_p, route_p, _row(norm_final)).reshape(bp, lp, d)
    ys = _final(x1s, yg_s, route_s, _row(norm_final)).reshape(ns, 1, d)
    mshape = (1, bp, mem_prompt.shape[1], mheads, mdh)
    return (yp, ys, conv_p[None], lru_p[None], ret_p[None], mk.reshape(mshape), mv.reshape(mshape),
            conv_s[None], lru_s[None], ret_s[None])
```

```python
import functools
import math

import jax
import jax.numpy as jnp
from jax import lax
from jax.experimental import pallas as pl
from jax.experimental.pallas import tpu as pltpu

F32 = jnp.float32
BF16 = jnp.bfloat16

EPS = 1e-6
LRU_C = 8.0
ROPE_BASE = 10000.0
PAST_LEN = 16384
TOP_K = 2
LANES = 128
SUBLANES = 8
VMEM_LIMIT = 56 * 1024 * 1024


def _dot(a, b):
    return jnp.dot(a, b, preferred_element_type=F32)


def _dot_nt(a, b):
    return lax.dot_general(a, b, (((1,), (1,)), ((), ())), preferred_element_type=F32)


def _dot_tn(a, b):
    return lax.dot_general(a, b, (((0,), (0,)), ((), ())), preferred_element_type=F32)


def _rms(x, g):
    return x * lax.rsqrt(jnp.mean(x * x, axis=-1, keepdims=True) + EPS) * g


def _sigmoid(x):
    return 0.5 * jnp.tanh(0.5 * x) + 0.5


def _silu(x):
    return x * _sigmoid(x)


def _gelu_tanh(x):
    return 0.5 * x * (1.0 + jnp.tanh(math.sqrt(2.0 / math.pi) * (x + 0.044715 * (x * x * x))))


def _softplus(x):
    return jnp.maximum(x, 0.0) + jnp.log1p(jnp.exp(-jnp.abs(x)))


def _const_spec(shape):
    nd = len(shape)
    return pl.BlockSpec(shape, lambda *_: (0,) * nd, pipeline_mode=pl.Buffered(1))


def _params(sem):
    return pltpu.CompilerParams(dimension_semantics=sem, vmem_limit_bytes=VMEM_LIMIT)


def _lru_coeffs(uc, wr_ref, br, wi_ref, bi, lam):
    nb, bs = wr_ref.shape[0], wr_ref.shape[1]
    ucb = uc.astype(BF16)
    r_lin = jnp.concatenate([_dot(ucb[:, n * bs:(n + 1) * bs], wr_ref[n]) for n in range(nb)], axis=-1)
    i_lin = jnp.concatenate([_dot(ucb[:, n * bs:(n + 1) * bs], wi_ref[n]) for n in range(nb)], axis=-1)
    r = _sigmoid(r_lin + br)
    i = _sigmoid(i_lin + bi)
    log_a = (-LRU_C) * r * _softplus(-lam)
    a = jnp.exp(log_a)
    t = 1.0 - a * a
    mult = t * lax.rsqrt(jnp.maximum(t, 1e-37))
    return a, mult, i


def _scan_rows(a, b, h0):
    rows, width = a.shape
    grp = SUBLANES
    n_grp = rows // grp
    a = a.reshape(n_grp, grp, width)
    b = b.reshape(n_grp, grp, width)
    row = lax.broadcasted_iota(jnp.int32, a.shape, 1)
    d = 1
    while d < grp:
        a_sh = pltpu.roll(a, d, 1)
        b_sh = pltpu.roll(b, d, 1)
        keep = row >= d
        b = jnp.where(keep, b + a * b_sh, b)
        a = jnp.where(keep, a * a_sh, a)
        d *= 2
    out = []
    for g in range(n_grp):
        hg = a[g] * h0 + b[g]
        out.append(hg)
        h0 = hg[grp - 1:grp, :]
    return jnp.concatenate(out, axis=0)


def _rotary(t, cos, sin):
    half = t.shape[-1] // 2
    t1, t2 = t[:, :half], t[:, half:]
    return jnp.concatenate([t1 * cos - t2 * sin, t2 * cos + t1 * sin], axis=-1)


def _route(logits, run, n_groups, per_group):
    col = lax.broadcasted_iota(jnp.int32, logits.shape, 1)
    big = jnp.int32(1 << 20)
    neg = jnp.float32(-jnp.inf)
    gl = jnp.where(col < n_groups, logits, neg)
    gmax = jnp.max(gl, axis=-1, keepdims=True)
    g_idx = jnp.min(jnp.where(gl == gmax, col, big), axis=-1, keepdims=True)
    g_val = 1.0 / jnp.sum(jnp.exp(gl - gmax), axis=-1, keepdims=True)
    lo = n_groups + per_group * g_idx
    el = jnp.where((col >= lo) & (col < lo + per_group), logits, neg)
    m1 = jnp.max(el, axis=-1, keepdims=True)
    i1 = jnp.min(jnp.where(el == m1, col, big), axis=-1, keepdims=True)
    el2 = jnp.where(col == i1, neg, el)
    m2 = jnp.max(el2, axis=-1, keepdims=True)
    i2 = jnp.min(jnp.where(el2 == m2, col, big), axis=-1, keepdims=True)
    e2 = jnp.exp(m2 - m1)
    w1 = g_val / (1.0 + e2)
    w2 = g_val * e2 / (1.0 + e2)
    f1 = (i1 - n_groups).astype(F32)
    f2 = (i2 - n_groups).astype(F32)
    hit1, hit2 = col == i1, col == i2
    hits = jnp.where(hit1 | hit2, 1.0, 0.0)
    rows = logits.shape[0]
    earlier = lax.broadcasted_iota(jnp.int32, (rows, rows), 1) < lax.broadcasted_iota(jnp.int32, (rows, rows), 0)
    before = run + _dot(jnp.where(earlier, 1.0, 0.0).astype(BF16), hits.astype(BF16))
    r1 = jnp.sum(jnp.where(hit1, before, 0.0), axis=-1, keepdims=True)
    r2 = jnp.sum(jnp.where(hit2, before, 0.0), axis=-1, keepdims=True)
    route = f1
    for j, val in enumerate((f2, w1, w2, r1, r2), start=1):
        route = jnp.where(col == j, val, route)
    route = jnp.where(col > 5, 0.0, route)
    return route, run + jnp.sum(hits, axis=0, keepdims=True)


def _merge_tail(x, merged, wout_ref, gffn, wrt_ref, brt, run, n_groups, per_group):
    x1 = x + _dot(merged.astype(BF16), wout_ref[...])
    xn2 = _rms(x1, gffn).astype(BF16)
    logits = _dot(xn2, wrt_ref[...]) + brt
    route, run = _route(logits, run, n_groups, per_group)
    return x1, xn2.astype(F32), route, run


def _lru_prompt_kernel(x_ref, g_ref, wxy_ref, wga_ref, cw_ref, cb_ref, wr_ref, br_ref, wi_ref, bi_ref,
                       lam_ref, wbr_ref, out_ref, conv_ref, h_ref, ccar, hcar):
    c = pl.program_id(1)
    tl = x_ref.shape[1]
    width = hcar.shape[1]
    kw = cw_ref.shape[0]

    @pl.when(c == 0)
    def _():
        ccar[...] = jnp.zeros_like(ccar)
        hcar[...] = jnp.zeros_like(hcar)

    xn = _rms(x_ref[0], g_ref[...]).astype(BF16)
    z = _dot(xn, wxy_ref[...])
    ux, uy = z[:, :width], z[:, width:]
    row = lax.broadcasted_iota(jnp.int32, ux.shape, 0)
    acc = cw_ref[0:1, :] * ux
    for j in range(1, kw):
        prev = jnp.where(row == 0, ccar[j - 1:j, :], pltpu.roll(acc, 1, 0))
        ccar[j - 1:j, :] = acc[tl - 1:tl, :]
        acc = prev + cw_ref[j:j + 1, :] * ux
    uc = acc + cb_ref[...]
    a, mult, gate = _lru_coeffs(uc, wr_ref, br_ref[...], wi_ref, bi_ref[...], lam_ref[...])
    mult = jnp.where(row + c * tl == 0, 1.0, mult)
    h = _scan_rows(a, mult * gate * uc, hcar[...])
    hcar[...] = h[tl - 1:tl, :]
    tail = ux[tl - (kw - 1):tl, :]
    a_out = _dot((h * _gelu_tanh(uy)).astype(BF16), wbr_ref[...])
    ga = _dot(xn, wga_ref[...])
    out_ref[0] = (_sigmoid(ga) * a_out).astype(out_ref.dtype)

    @pl.when(c == pl.num_programs(1) - 1)
    def _():
        conv_ref[0] = tail
        h_ref[0] = h[tl - 1:tl, :]


def _ret_prompt_kernel(x_ref, g_ref, wq_ref, wk_ref, wv_ref, wg_ref, wgb_ref, cos_ref, sin_ref, wbr_ref,
                       out_ref, s_out_ref, s_scr, y_scr, *, heads, chunk):
    c = pl.program_id(1)
    tl = x_ref.shape[1]
    dk = wq_ref.shape[1] // heads
    dv = wv_ref.shape[1] // heads

    @pl.when(c == 0)
    def _():
        s_scr[...] = jnp.zeros_like(s_scr)

    xn = _rms(x_ref[0], g_ref[...]).astype(BF16)
    q = _dot(xn, wq_ref[...])
    k = _dot(xn, wk_ref[...])
    v = _dot(xn, wv_ref[...])
    g = _dot(xn, wg_ref[...])
    cos, sin = cos_ref[...], sin_ref[...]
    n_i = lax.broadcasted_iota(jnp.int32, (chunk, chunk), 0)
    m_i = lax.broadcasted_iota(jnp.int32, (chunk, chunk), 1)
    diff = (n_i - m_i).astype(F32)
    rowk = lax.broadcasted_iota(jnp.int32, (chunk, dk), 0).astype(F32)
    for h in range(heads):
        log_g = math.log1p(-(2.0 ** (-5.0 - h)))
        dmask = jnp.where(diff >= 0, jnp.exp(jnp.maximum(diff, 0.0) * log_g), 0.0)
        q_decay = jnp.exp((rowk + 1.0) * log_g)
        k_decay = jnp.exp((chunk - 1.0 - rowk) * log_g)
        s_decay = math.exp(chunk * log_g)
        for sub in range(tl // chunk):
            r0 = sub * chunk
            cs, sn = cos[r0:r0 + chunk], sin[r0:r0 + chunk]
            qr = _rotary(q[r0:r0 + chunk, h * dk:(h + 1) * dk], cs, sn)
            kr = _rotary(k[r0:r0 + chunk, h * dk:(h + 1) * dk], cs, sn) * (dk ** -0.5)
            vh = v[r0:r0 + chunk, h * dv:(h + 1) * dv].astype(BF16)
            s = s_scr[h]
            att = _dot_nt(qr.astype(BF16), kr.astype(BF16)) * dmask
            o = _dot(att.astype(BF16), vh) + _dot((qr * q_decay).astype(BF16), s.astype(BF16))
            s_scr[h] = s * s_decay + _dot_tn((kr * k_decay).astype(BF16), vh)
            o = o * lax.rsqrt(jnp.mean(o * o, axis=-1, keepdims=True) + EPS)
            gh = g[r0:r0 + chunk, h * dv:(h + 1) * dv]
            y_scr[r0:r0 + chunk, h * dv:(h + 1) * dv] = (_silu(gh) * o).astype(BF16)
    b_out = _dot(y_scr[...], wbr_ref[...])
    gb = _dot(xn, wgb_ref[...])
    out_ref[0] = (_sigmoid(gb) * b_out).astype(out_ref.dtype)

    @pl.when(c == pl.num_programs(1) - 1)
    def _():
        s_out_ref[0] = s_scr[...]


def _memkv_kernel(m_ref, g_ref, w_ref, k_ref, v_ref):
    width = k_ref.shape[-1]
    mn = _rms(m_ref[0], g_ref[...]).astype(BF16)
    kv = _dot(mn, w_ref[...])
    k_ref[0] = kv[:, :width]
    v_ref[0] = kv[:, width:]


def _attn_merge_prompt_kernel(x_ref, ag_ref, bg_ref, mk_ref, mv_ref, g_ref, wmq_ref, wgc_ref, wbm_ref,
                              wout_ref, gffn_ref, wrt_ref, brt_ref, x1_ref, xn2_ref, route_ref, cnt_ref,
                              *, heads, n_groups, per_group):
    @pl.when((pl.program_id(0) == 0) & (pl.program_id(1) == 0))
    def _():
        cnt_ref[...] = jnp.zeros_like(cnt_ref)

    x = x_ref[0]
    xn = _rms(x, g_ref[...]).astype(BF16)
    mq = _dot(xn, wmq_ref[...]).astype(BF16)
    dh = mq.shape[1] // heads
    mk = mk_ref[0].astype(BF16)
    mv = mv_ref[0].astype(BF16)
    cs = []
    for h in range(heads):
        hs = slice(h * dh, (h + 1) * dh)
        lg = _dot_nt(mq[:, hs], mk[:, hs]) * (dh ** -0.5)
        p = jnp.exp(lg - jnp.max(lg, axis=-1, keepdims=True))
        den = jnp.sum(p, axis=-1, keepdims=True)
        cs.append(_dot(p.astype(BF16), mv[:, hs]) / den)
    c_out = _dot(jnp.concatenate(cs, axis=-1).astype(BF16), wbm_ref[...])
    gc = _dot(xn, wgc_ref[...])
    merged = _sigmoid(gc) * c_out + ag_ref[0].astype(F32) + bg_ref[0].astype(F32)
    x1, xn2, route, run = _merge_tail(x, merged, wout_ref, gffn_ref[...], wrt_ref, brt_ref[...], cnt_ref[...],
                                      n_groups, per_group)
    x1_ref[0] = x1
    xn2_ref[...] = xn2
    route_ref[0] = route
    cnt_ref[...] = run


def _proj_kernel(x_ref, g_ref, w_ref, z_ref):
    xn = _rms(x_ref[...], g_ref[...]).astype(BF16)
    z_ref[...] = _dot(xn, w_ref[...])


def _lru_sample_kernel(ux_ref, uy_ref, ga_ref, cprev_ref, hprev_ref, cw_ref, cb_ref, wr_ref, br_ref, wi_ref,
                       bi_ref, lam_ref, wbr_ref, out_ref, conv_ref, h_ref, *, at_start):
    kw = cw_ref.shape[0]
    ux = ux_ref[...]
    uc = cb_ref[...] + cw_ref[kw - 1:kw, :] * ux
    for j in range(kw - 1):
        uc = uc + cw_ref[j:j + 1, :] * cprev_ref[:, j, :]
    a, mult, gate = _lru_coeffs(uc, wr_ref, br_ref[...], wi_ref, bi_ref[...], lam_ref[...])
    if at_start:
        mult = jnp.ones_like(mult)
    h = a * hprev_ref[...] + mult * gate * uc
    h_ref[...] = h
    for j in range(kw - 2):
        conv_ref[:, j, :] = cprev_ref[:, j + 1, :]
    conv_ref[:, kw - 2, :] = ux
    a_out = _dot((h * _gelu_tanh(uy_ref[...])).astype(BF16), wbr_ref[...])
    out_ref[...] = _sigmoid(ga_ref[...]) * a_out


def _rot_sample_kernel(q_ref, k_ref, cos_ref, sin_ref, qo_ref, ko_ref, *, heads):
    dk = q_ref.shape[1] // heads
    cos, sin = cos_ref[...], sin_ref[...]
    for h in range(heads):
        hs = slice(h * dk, (h + 1) * dk)
        qo_ref[:, hs] = _rotary(q_ref[:, hs], cos, sin)
        ko_ref[:, hs] = _rotary(k_ref[:, hs], cos, sin) * (dk ** -0.5)


def _ret_sample_kernel(qt_ref, kt_ref, q_ref, k_ref, v_ref, s_ref, o_ref, s_out_ref, *, heads):
    bb = q_ref.shape[1]
    dk = s_ref.shape[2]
    dv = s_ref.shape[3]
    for j in range(bb):
        for h in range(heads):
            decay = 1.0 - 2.0 ** (-5.0 - h)
            qcol = qt_ref[0, h, :, j:j + 1]
            kcol = kt_ref[0, h, :, j:j + 1]
            qrow = q_ref[0, j:j + 1, h * dk:(h + 1) * dk]
            krow = k_ref[0, j:j + 1, h * dk:(h + 1) * dk]
            vrow = v_ref[0, j:j + 1, h * dv:(h + 1) * dv]
            s = s_ref[j, h]
            att = jnp.sum(qrow * krow, axis=-1, keepdims=True)
            o = att * vrow + jnp.sum((qcol * decay) * s, axis=0, keepdims=True)
            o_ref[0, j:j + 1, h * dv:(h + 1) * dv] = o
            s_out_ref[j, h] = s * decay + kcol * vrow


def _attn_sample_kernel(mq_ref, k_ref, v_ref, c_ref, *, heads):
    bb, _, dh = mq_ref.shape
    for j in range(bb):
        q = mq_ref[j]
        lg = jnp.sum(k_ref[j] * q[None], axis=-1, keepdims=True) * (dh ** -0.5)
        p = jnp.exp(lg - jnp.max(lg, axis=0, keepdims=True))
        den = jnp.sum(p, axis=0)
        c_ref[j] = jnp.sum(v_ref[j] * p, axis=0) / den


def _merge_sample_kernel(x_ref, ag_ref, o_ref, g_ref, c_ref, gb_ref, gc_ref, wbr_ref, wbm_ref, wout_ref,
                         gffn_ref, wrt_ref, brt_ref, run_ref, xn2_table_ref, x1_ref, xn2_ref, route_ref, cnt_ref,
                         *, heads, n_groups, per_group):
    del xn2_table_ref
    dv = o_ref.shape[1] // heads
    ys = []
    for h in range(heads):
        o = o_ref[:, h * dv:(h + 1) * dv]
        o = o * lax.rsqrt(jnp.mean(o * o, axis=-1, keepdims=True) + EPS)
        ys.append((_silu(g_ref[:, h * dv:(h + 1) * dv]) * o).astype(BF16))
    b_out = _dot(jnp.concatenate(ys, axis=-1), wbr_ref[...])
    c_out = _dot(c_ref[...].astype(BF16), wbm_ref[...])
    merged = ag_ref[...] + _sigmoid(gb_ref[...]) * b_out + _sigmoid(gc_ref[...]) * c_out
    x1, xn2, route, run = _merge_tail(x_ref[...], merged, wout_ref, gffn_ref[...], wrt_ref, brt_ref[...],
                                      run_ref[...], n_groups, per_group)
    x1_ref[...] = x1
    xn2_ref[...] = xn2
    route_ref[...] = route
    cnt_ref[...] = run


def _expert_kernel(te_ref, nu_ref, xa_ref, xb_ref, wg_ref, wu_ref, wd_ref, ys_ref, *, half):
    i = pl.program_id(0)

    @pl.when(i < nu_ref[0])
    def _():
        x = jnp.where(i < half, xa_ref[...], xb_ref[...]).astype(BF16)
        hg = _dot(x, wg_ref[0].astype(BF16))
        hu = _dot(x, wu_ref[0].astype(BF16))
        hid = (_silu(hg) * hu).astype(BF16)
        ys_ref[...] = _dot(hid, wd_ref[0].astype(BF16))

    @pl.when(i >= nu_ref[0])
    def _():
        ys_ref[...] = jnp.zeros_like(ys_ref)


def _final_kernel(x1_ref, y1_ref, y2_ref, route_ref, g_ref, out_ref):
    w1 = route_ref[:, 2:3]
    w2 = route_ref[:, 3:4]
    x2 = x1_ref[...] + (w1 * y1_ref[...] + w2 * y2_ref[...])
    out_ref[...] = _rms(x2, g_ref[...])


def _tile(n, target):
    t = min(n, target)
    while n % t:
        t //= 2
    return t


def _row(v):
    return v.reshape(1, -1).astype(F32)


def _prep_layer(p, dims):
    d, w, hk, hv, hm = dims['d'], dims['w'], dims['hk'], dims['hv'], dims['hm']
    w_in = p['w_in']
    cuts = [0, w, 2 * w, 2 * w + hk, 2 * w + 2 * hk, 2 * w + 2 * hk + hv, 2 * w + 2 * hk + 2 * hv,
            2 * w + 2 * hk + 2 * hv + hm, w_in.shape[1]]
    seg = [w_in[:, cuts[i]:cuts[i + 1]].astype(BF16) for i in range(8)]
    gates = seg[7]
    g, e = p['w_exp_router'].shape[0], p['w_exp_router'].shape[2]
    w_rt = jnp.concatenate([p['w_grp'], jnp.moveaxis(p['w_exp_router'], 0, 1).reshape(d, g * e)], axis=1)
    w_rt = jnp.pad(w_rt, ((0, 0), (0, LANES - w_rt.shape[1]))).astype(BF16)
    b_rt = jnp.pad(jnp.concatenate([p['b_grp'], p['b_exp_router'].reshape(-1)]), (0, LANES - g - g * e))
    return dict(
        w_in=w_in.astype(BF16),
        w_xy=jnp.concatenate([seg[0], seg[1]], axis=1), w_q=seg[2], w_k=seg[3], w_v=seg[4], w_g=seg[5],
        w_mq=seg[6], w_ga=gates[:, :d], w_gb=gates[:, d:2 * d], w_gc=gates[:, 2 * d:],
        norm_mix=_row(p['norm_mix']), norm_ffn=_row(p['norm_ffn']), norm_mem=_row(p['norm_mem']),
        w_mem_kv=p['w_mem_kv'].astype(BF16),
        conv_w=p['conv_w'].astype(F32), conv_b=_row(p['conv_b']),
        w_r=p['w_r'].astype(BF16), b_r=_row(p['b_r']), w_i=p['w_i'].astype(BF16), b_i=_row(p['b_i']),
        lam=_row(p['lru_lambda']),
        w_br_lru=p['w_branch_lru'].astype(BF16), w_br_ret=p['w_branch_ret'].astype(BF16),
        w_br_mem=p['w_branch_mem'].astype(BF16), w_out=p['w_out'].astype(BF16),
        w_rt=w_rt, b_rt=_row(b_rt),
        w_gate=p['w_gate'].reshape((g * e,) + p['w_gate'].shape[2:]),
        w_up=p['w_up'].reshape((g * e,) + p['w_up'].shape[2:]),
        w_down=p['w_down'].reshape((g * e,) + p['w_down'].shape[2:]),
    )


def _rope_tables(pos, dk):
    half = dk // 2
    inv = ROPE_BASE ** (-jnp.linspace(0.0, 1.0, half, dtype=F32))
    ang = pos.astype(F32)[:, None] * inv[None, :]
    return jnp.cos(ang), jnp.sin(ang)


def _prompt_mixer(x, mem, lw, dims, extra_rows):
    b, l, d = x.shape
    w, heads, dk, dv = dims['w'], dims['heads'], dims['dk'], dims['dv']
    kw = lw['conv_w'].shape[0]
    tl = _tile(l, 256)
    grid = (b, l // tl)
    xspec = pl.BlockSpec((1, tl, d), lambda i, c: (i, c, 0))
    sem = ("parallel", "arbitrary")

    a_g, conv_new, h_last = pl.pallas_call(
        _lru_prompt_kernel,
        grid=grid,
        in_specs=[xspec, _const_spec((1, d)), _const_spec((d, 2 * w)), _const_spec((d, d)),
                  _const_spec((kw, w)), _const_spec((1, w)), _const_spec(lw['w_r'].shape), _const_spec((1, w)),
                  _const_spec(lw['w_i'].shape), _const_spec((1, w)), _const_spec((1, w)), _const_spec((w, d))],
        out_specs=[pl.BlockSpec((1, tl, d), lambda i, c: (i, c, 0)),
                   pl.BlockSpec((1, kw - 1, w), lambda i, c: (i, 0, 0)),
                   pl.BlockSpec((1, 1, w), lambda i, c: (i, 0, 0))],
        out_shape=[jax.ShapeDtypeStruct((b, l, d), BF16), jax.ShapeDtypeStruct((b, kw - 1, w), F32),
                   jax.ShapeDtypeStruct((b, 1, w), F32)],
        scratch_shapes=[pltpu.VMEM((SUBLANES, w), F32), pltpu.VMEM((1, w), F32)],
        compiler_params=_params(sem),
    )(x, lw['norm_mix'], lw['w_xy'], lw['w_ga'], lw['conv_w'], lw['conv_b'], lw['w_r'], lw['b_r'],
      lw['w_i'], lw['b_i'], lw['lam'], lw['w_br_lru'])

    chunk = _tile(tl, 128)
    cos, sin = _rope_tables(jnp.arange(l, dtype=jnp.int32), dk)
    half = dk // 2
    b_g, s_new = pl.pallas_call(
        functools.partial(_ret_prompt_kernel, heads=heads, chunk=chunk),
        grid=grid,
        in_specs=[xspec, _const_spec((1, d)), _const_spec((d, heads * dk)), _const_spec((d, heads * dk)),
                  _const_spec((d, heads * dv)), _const_spec((d, heads * dv)), _const_spec((d, d)),
                  pl.BlockSpec((tl, half), lambda i, c: (c, 0)), pl.BlockSpec((tl, half), lambda i, c: (c, 0)),
                  _const_spec((heads * dv, d))],
        out_specs=[pl.BlockSpec((1, tl, d), lambda i, c: (i, c, 0)),
                   pl.BlockSpec((1, heads, dk, dv), lambda i, c: (i, 0, 0, 0))],
        out_shape=[jax.ShapeDtypeStruct((b, l, d), BF16), jax.ShapeDtypeStruct((b, heads, dk, dv), F32)],
        scratch_shapes=[pltpu.VMEM((heads, dk, dv), F32), pltpu.VMEM((tl, heads * dv), BF16)],
        compiler_params=_params(sem),
    )(x, lw['norm_mix'], lw['w_q'], lw['w_k'], lw['w_v'], lw['w_g'], lw['w_gb'], cos, sin, lw['w_br_ret'])

    m = mem.shape[1]
    hm = dims['hm']
    mk, mv = pl.pallas_call(
        _memkv_kernel,
        grid=(b,),
        in_specs=[pl.BlockSpec((1, m, d), lambda i: (i, 0, 0)), _const_spec((1, d)), _const_spec((d, 2 * hm))],
        out_specs=[pl.BlockSpec((1, m, hm), lambda i: (i, 0, 0)), pl.BlockSpec((1, m, hm), lambda i: (i, 0, 0))],
        out_shape=[jax.ShapeDtypeStruct((b, m, hm), F32), jax.ShapeDtypeStruct((b, m, hm), F32)],
        compiler_params=_params(("parallel",)),
    )(mem, lw['norm_mem'], lw['w_mem_kv'])

    x1, xn2, route, counts = pl.pallas_call(
        functools.partial(_attn_merge_prompt_kernel, heads=dims['mheads'], n_groups=dims['g'],
                          per_group=dims['e']),
        grid=grid,
        in_specs=[xspec, xspec, xspec,
                  pl.BlockSpec((1, m, hm), lambda i, c: (i, 0, 0)), pl.BlockSpec((1, m, hm), lambda i, c: (i, 0, 0)),
                  _const_spec((1, d)), _const_spec((d, hm)), _const_spec((d, d)), _const_spec((hm, d)),
                  _const_spec((d, d)), _const_spec((1, d)), _const_spec((d, LANES)), _const_spec((1, LANES))],
        out_specs=[xspec, pl.BlockSpec((tl, d), lambda i, c: (i * (l // tl) + c, 0)),
                   pl.BlockSpec((1, tl, LANES), lambda i, c: (i, c, 0)),
                   pl.BlockSpec((1, LANES), lambda i, c: (0, 0))],
        out_shape=[jax.ShapeDtypeStruct((b, l, d), F32), jax.ShapeDtypeStruct((b * l + extra_rows, d), F32),
                   jax.ShapeDtypeStruct((b, l, LANES), F32), jax.ShapeDtypeStruct((1, LANES), F32)],
        compiler_params=_params(("arbitrary", "arbitrary")),
    )(x, a_g, b_g, mk, mv, lw['norm_mix'], lw['w_mq'], lw['w_gc'], lw['w_br_mem'], lw['w_out'],
      lw['norm_ffn'], lw['w_rt'], lw['b_rt'])
    return x1, xn2, route, counts, conv_new, h_last[:, 0], s_new, mk, mv


def _sample_mixer(x, conv_prev, h_prev, s_prev, mem_k, mem_v, counts0, xn2_all, lw, dims):
    n, d = x.shape
    w, heads, dk, dv, hm = dims['w'], dims['heads'], dims['dk'], dims['dv'], dims['hm']
    hk, hv = heads * dk, heads * dv
    kw = lw['conv_w'].shape[0]
    n_in = lw['w_in'].shape[1]
    tn = _tile(n_in, 1024)
    z = pl.pallas_call(
        _proj_kernel,
        grid=(n_in // tn,),
        in_specs=[_const_spec((n, d)), _const_spec((1, d)), pl.BlockSpec((d, tn), lambda j: (0, j))],
        out_specs=pl.BlockSpec((n, tn), lambda j: (0, j)),
        out_shape=jax.ShapeDtypeStruct((n, n_in), F32),
        compiler_params=_params(("parallel",)),
    )(x, lw['norm_mix'], lw['w_in'])
    o0 = 0
    parts = []
    for sz in (w, w, hk, hk, hv, hv, hm, d, d, d):
        parts.append(z[:, o0:o0 + sz])
        o0 += sz
    ux, uy, q, k, v, g, mq, ga, gb, gc = parts

    a_g, conv_new, h_new = pl.pallas_call(
        functools.partial(_lru_sample_kernel, at_start=(PAST_LEN == 0)),
        out_shape=[jax.ShapeDtypeStruct((n, d), F32), jax.ShapeDtypeStruct((n, kw - 1, w), F32),
                   jax.ShapeDtypeStruct((n, w), F32)],
        compiler_params=pltpu.CompilerParams(vmem_limit_bytes=VMEM_LIMIT),
    )(ux, uy, ga, conv_prev, h_prev, lw['conv_w'], lw['conv_b'], lw['w_r'], lw['b_r'], lw['w_i'], lw['b_i'],
      lw['lam'], lw['w_br_lru'])

    cos, sin = _rope_tables(PAST_LEN + jnp.arange(1, dtype=jnp.int32), dk)
    qr, kr = pl.pallas_call(
        functools.partial(_rot_sample_kernel, heads=heads),
        out_shape=[jax.ShapeDtypeStruct((n, hk), F32), jax.ShapeDtypeStruct((n, hk), F32)],
    )(q, k, cos, sin)

    bb = _tile(n, 4)

    def cols(t):
        return t.reshape(n // bb, bb, heads, dk).transpose(0, 2, 3, 1)

    def rows(t):
        return t.reshape(n // bb, bb, t.shape[1])

    o, s_new = pl.pallas_call(
        functools.partial(_ret_sample_kernel, heads=heads),
        grid=(n // bb,),
        in_specs=[pl.BlockSpec((1, heads, dk, bb), lambda i: (i, 0, 0, 0)),
                  pl.BlockSpec((1, heads, dk, bb), lambda i: (i, 0, 0, 0)),
                  pl.BlockSpec((1, bb, hk), lambda i: (i, 0, 0)), pl.BlockSpec((1, bb, hk), lambda i: (i, 0, 0)),
                  pl.BlockSpec((1, bb, hv), lambda i: (i, 0, 0)),
                  pl.BlockSpec((bb, heads, dk, dv), lambda i: (i, 0, 0, 0))],
        out_specs=[pl.BlockSpec((1, bb, hv), lambda i: (i, 0, 0)),
                   pl.BlockSpec((bb, heads, dk, dv), lambda i: (i, 0, 0, 0))],
        out_shape=[jax.ShapeDtypeStruct((n // bb, bb, hv), F32), jax.ShapeDtypeStruct(s_prev.shape, F32)],
        compiler_params=_params(("parallel",)),
    )(cols(qr), cols(kr), rows(qr), rows(kr), rows(v), s_prev)
    o = o.reshape(n, hv)
    bb = _tile(n, SUBLANES)

    m = mem_k.shape[1]
    c = pl.pallas_call(
        functools.partial(_attn_sample_kernel, heads=dims['mheads']),
        grid=(n // bb,),
        in_specs=[pl.BlockSpec((bb,) + mem_k.shape[2:], lambda i: (i, 0, 0)),
                  pl.BlockSpec((bb,) + mem_k.shape[1:], lambda i: (i, 0, 0, 0)),
                  pl.BlockSpec((bb,) + mem_v.shape[1:], lambda i: (i, 0, 0, 0))],
        out_specs=pl.BlockSpec((bb,) + mem_k.shape[2:], lambda i: (i, 0, 0)),
        out_shape=jax.ShapeDtypeStruct((n,) + mem_k.shape[2:], F32),
        compiler_params=_params(("parallel",)),
    )(mq.reshape((n,) + mem_k.shape[2:]), mem_k, mem_v).reshape(n, hm)

    first = xn2_all.shape[0] - n
    assert first % n == 0
    args = (x, a_g, o, g, c, gb, gc, lw['w_br_ret'], lw['w_br_mem'], lw['w_out'], lw['norm_ffn'], lw['w_rt'],
            lw['b_rt'], counts0)
    x1, xn2_all, route, counts = pl.pallas_call(
        functools.partial(_merge_sample_kernel, heads=heads, n_groups=dims['g'], per_group=dims['e']),
        grid=(1,),
        in_specs=[_const_spec(a.shape) for a in args] + [pl.BlockSpec(memory_space=pl.ANY)],
        out_specs=[pl.BlockSpec((n, d), lambda i: (0, 0)), pl.BlockSpec((n, d), lambda i: (first // n, 0)),
                   pl.BlockSpec((n, LANES), lambda i: (0, 0)), pl.BlockSpec((1, LANES), lambda i: (0, 0))],
        out_shape=[jax.ShapeDtypeStruct((n, d), F32), jax.ShapeDtypeStruct(xn2_all.shape, F32),
                   jax.ShapeDtypeStruct((n, LANES), F32), jax.ShapeDtypeStruct((1, LANES), F32)],
        input_output_aliases={len(args): 1},
        compiler_params=_params(("arbitrary",)),
    )(*args, xn2_all)
    return x1, xn2_all, route, counts, conv_new, h_new, s_new


def _moe(xn2, route_parts, counts, lw, dims, tm=256):
    t, d = xn2.shape
    g, n_exp = dims['g'], dims['g'] * dims['e']
    ff = lw['w_gate'].shape[2]
    counts = counts[0, g:g + n_exp].astype(jnp.int32)
    padded = ((counts + tm - 1) // tm) * tm
    ends = jnp.cumsum(padded)
    offs = ends - padded
    n_tiles = 2 * ((TOP_K * t + n_exp * tm + 2 * tm - 1) // (2 * tm))
    n_slots = n_tiles * tm
    half = n_tiles // 2
    experts = jnp.arange(n_exp, dtype=jnp.int32)[None, :]
    dests = []
    for j in range(TOP_K):
        per_part = []
        for r in route_parts:
            ids = r[:, j].astype(jnp.int32)
            base = jnp.sum(jnp.where(ids[:, None] == experts, offs[None, :], 0), axis=1)
            per_part.append(base + r[:, 4 + j].astype(jnp.int32))
        dests.append(per_part)
    dest = jnp.concatenate([dp for per_part in dests for dp in per_part], axis=0)
    tok = jnp.arange(t, dtype=jnp.int32)
    src = (jnp.arange(n_slots, dtype=jnp.int32) % t).at[dest].set(
        jnp.concatenate([tok] * TOP_K), mode='promise_in_bounds', unique_indices=True)
    n_used = (ends[-1] // tm).astype(jnp.int32)
    tile_row = jnp.arange(n_tiles, dtype=jnp.int32) * tm
    tile_e = jnp.sum((ends[None, :] <= tile_row[:, None]).astype(jnp.int32), axis=1)
    last_e = jnp.sum((ends <= (n_used - 1) * tm).astype(jnp.int32))
    tile_e = jnp.where(tile_row < ends[-1], tile_e, last_e)
    xs_a = xn2.at[src[:half * tm]].get(mode='promise_in_bounds')
    xs_b = xn2.at[src[half * tm:]].get(mode='promise_in_bounds')

    def map_a(i, te, nu):
        return (jnp.minimum(jnp.minimum(i, nu[0] - 1), half - 1), 0)

    def map_b(i, te, nu):
        return (jnp.clip(jnp.minimum(i, nu[0] - 1) - half, 0, half - 1), 0)

    ys = pl.pallas_call(
        functools.partial(_expert_kernel, half=half),
        grid_spec=pltpu.PrefetchScalarGridSpec(
            num_scalar_prefetch=2,
            grid=(n_tiles,),
            in_specs=[pl.BlockSpec((tm, d), map_a), pl.BlockSpec((tm, d), map_b),
                      pl.BlockSpec((1, d, ff), lambda i, te, nu: (te[i], 0, 0)),
                      pl.BlockSpec((1, d, ff), lambda i, te, nu: (te[i], 0, 0)),
                      pl.BlockSpec((1, ff, d), lambda i, te, nu: (te[i], 0, 0))],
            out_specs=pl.BlockSpec((tm, d), lambda i, te, nu: (i, 0)),
        ),
        out_shape=jax.ShapeDtypeStruct((n_slots, d), F32),
        compiler_params=_params(("arbitrary",)),
    )(tile_e, n_used.reshape(1), xs_a, xs_b, lw['w_gate'], lw['w_up'], lw['w_down'])
    return [tuple(ys.at[dests[j][p]].get(mode='promise_in_bounds') for j in range(TOP_K))
            for p in range(len(route_parts))]


def _final(x1, y_pair, route, g):
    t, d = x1.shape
    tl = _tile(t, 512)
    row = pl.BlockSpec((tl, d), lambda i: (i, 0))
    return pl.pallas_call(
        _final_kernel,
        grid=(t // tl,),
        in_specs=[row, row, row, pl.BlockSpec((tl, LANES), lambda i: (i, 0)), _const_spec((1, d))],
        out_specs=row,
        out_shape=jax.ShapeDtypeStruct((t, d), F32),
        compiler_params=_params(("parallel",)),
    )(x1, y_pair[0], y_pair[1], route, g)


def kernel(x_prompt, x_sample, mem_prompt, state_conv, state_lru, state_ret, cache_mem_k, cache_mem_v, norm_mix, norm_ffn, norm_mem, norm_final, w_in, w_mem_kv, conv_w, conv_b, w_r, b_r, w_i, b_i, lru_lambda, w_branch_lru, w_branch_ret, w_branch_mem, w_out, w_grp, b_grp, w_exp_router, b_exp_router, w_gate, w_up, w_down):
    depth = w_in.shape[0]
    assert depth == 1, "the two request groups are chained per layer only for a single-layer trunk"
    assert x_sample.shape[1] == 1
    bp, lp, d = x_prompt.shape
    ns = x_sample.shape[0]
    heads, dk, dv = state_ret.shape[2], state_ret.shape[3], state_ret.shape[4]
    mheads, mdh = cache_mem_k.shape[3], cache_mem_k.shape[4]
    dims = dict(d=d, w=state_lru.shape[2], heads=heads, dk=dk, dv=dv, hk=heads * dk, hv=heads * dv,
                mheads=mheads, hm=mheads * mdh, g=w_exp_router.shape[1], e=w_exp_router.shape[3])
    layer = 0
    p = dict(norm_mix=norm_mix[layer], norm_ffn=norm_ffn[layer], norm_mem=norm_mem[layer], w_in=w_in[layer],
             w_mem_kv=w_mem_kv[layer], conv_w=conv_w[layer], conv_b=conv_b[layer], w_r=w_r[layer], b_r=b_r[layer],
             w_i=w_i[layer], b_i=b_i[layer], lru_lambda=lru_lambda[layer], w_branch_lru=w_branch_lru[layer],
             w_branch_ret=w_branch_ret[layer], w_branch_mem=w_branch_mem[layer], w_out=w_out[layer],
             w_grp=w_grp[layer], b_grp=b_grp[layer], w_exp_router=w_exp_router[layer],
             b_exp_router=b_exp_router[layer], w_gate=w_gate[layer], w_up=w_up[layer], w_down=w_down[layer])
    lw = _prep_layer(p, dims)

    x1p, xn2, route_p, counts, conv_p, lru_p, ret_p, mk, mv = _prompt_mixer(x_prompt, mem_prompt, lw, dims, ns)
    x1s, xn2, route_s, counts, conv_s, lru_s, ret_s = _sample_mixer(
        x_sample[:, 0], state_conv[layer], state_lru[layer], state_ret[layer], cache_mem_k[layer],
        cache_mem_v[layer], counts, xn2, lw, dims)

    tp = bp * lp
    route_p = route_p.reshape(tp, LANES)
    yg_p, yg_s = _moe(xn2, [route_p, route_s], counts, lw, dims)
    yp = _final(x1p.reshape(tp, d), yg_p, route_p, _row(norm_final)).reshape(bp, lp, d)
    ys = _final(x1s, yg_s, route_s, _row(norm_final)).reshape(ns, 1, d)
    mshape = (1, bp, mem_prompt.shape[1], mheads, mdh)
    return (yp, ys, conv_p[None], lru_p[None], ret_p[None], mk.reshape(mshape), mv.reshape(mshape),
            conv_s[None], lru_s[None], ret_s[None])
```

```python
import functools
import math

import jax
import jax.numpy as jnp
from jax import lax
from jax.experimental import pallas as pl
from jax.experimental.pallas import tpu as pltpu

F32 = jnp.float32
BF16 = jnp.bfloat16

EPS = 1e-6
LRU_C = 8.0
ROPE_BASE = 10000.0
PAST_LEN = 16384
TOP_K = 2
LANES = 128
SUBLANES = 8
VMEM_LIMIT = 56 * 1024 * 1024


def _dot(a, b):
    return jnp.dot(a, b, preferred_element_type=F32)


def _dot_nt(a, b):
    return lax.dot_general(a, b, (((1,), (1,)), ((), ())), preferred_element_type=F32)


def _dot_tn(a, b):
    return lax.dot_general(a, b, (((0,), (0,)), ((), ())), preferred_element_type=F32)


def _rms(x, g):
    return x * lax.rsqrt(jnp.mean(x * x, axis=-1, keepdims=True) + EPS) * g


def _sigmoid(x):
    return 0.5 * jnp.tanh(0.5 * x) + 0.5


def _silu(x):
    return x * _sigmoid(x)


def _gelu_tanh(x):
    return 0.5 * x * (1.0 + jnp.tanh(math.sqrt(2.0 / math.pi) * (x + 0.044715 * (x * x * x))))


def _softplus(x):
    return jnp.maximum(x, 0.0) + jnp.log1p(jnp.exp(-jnp.abs(x)))


def _const_spec(shape):
    nd = len(shape)
    return pl.BlockSpec(shape, lambda *_: (0,) * nd, pipeline_mode=pl.Buffered(1))


def _params(sem):
    return pltpu.CompilerParams(dimension_semantics=sem, vmem_limit_bytes=VMEM_LIMIT)


def _lru_coeffs(uc, wr_ref, br, wi_ref, bi, lam):
    nb, bs = wr_ref.shape[0], wr_ref.shape[1]
    ucb = uc.astype(BF16)
    r_lin = jnp.concatenate([_dot(ucb[:, n * bs:(n + 1) * bs], wr_ref[n]) for n in range(nb)], axis=-1)
    i_lin = jnp.concatenate([_dot(ucb[:, n * bs:(n + 1) * bs], wi_ref[n]) for n in range(nb)], axis=-1)
    r = _sigmoid(r_lin + br)
    i = _sigmoid(i_lin + bi)
    log_a = (-LRU_C) * r * _softplus(-lam)
    a = jnp.exp(log_a)
    t = 1.0 - a * a
    mult = t * lax.rsqrt(jnp.maximum(t, 1e-37))
    return a, mult, i


def _scan_rows(a, b, h0):
    rows, width = a.shape
    grp = SUBLANES
    n_grp = rows // grp
    a = a.reshape(n_grp, grp, width)
    b = b.reshape(n_grp, grp, width)
    row = lax.broadcasted_iota(jnp.int32, a.shape, 1)
    d = 1
    while d < grp:
        a_sh = pltpu.roll(a, d, 1)
        b_sh = pltpu.roll(b, d, 1)
        keep = row >= d
        b = jnp.where(keep, b + a * b_sh, b)
        a = jnp.where(keep, a * a_sh, a)
        d *= 2
    out = []
    for g in range(n_grp):
        hg = a[g] * h0 + b[g]
        out.append(hg)
        h0 = hg[grp - 1:grp, :]
    return jnp.concatenate(out, axis=0)


def _rotary(t, cos, sin):
    half = t.shape[-1] // 2
    t1, t2 = t[:, :half], t[:, half:]
    return jnp.concatenate([t1 * cos - t2 * sin, t2 * cos + t1 * sin], axis=-1)


def _route(logits, run, n_groups, per_group):
    col = lax.broadcasted_iota(jnp.int32, logits.shape, 1)
    big = jnp.int32(1 << 20)
    neg = jnp.float32(-jnp.inf)
    gl = jnp.where(col < n_groups, logits, neg)
    gmax = jnp.max(gl, axis=-1, keepdims=True)
    g_idx = jnp.min(jnp.where(gl == gmax, col, big), axis=-1, keepdims=True)
    g_val = 1.0 / jnp.sum(jnp.exp(gl - gmax), axis=-1, keepdims=True)
    lo = n_groups + per_group * g_idx
    el = jnp.where((col >= lo) & (col < lo + per_group), logits, neg)
    m1 = jnp.max(el, axis=-1, keepdims=True)
    i1 = jnp.min(jnp.where(el == m1, col, big), axis=-1, keepdims=True)
    el2 = jnp.where(col == i1, neg, el)
    m2 = jnp.max(el2, axis=-1, keepdims=True)
    i2 = jnp.min(jnp.where(el2 == m2, col, big), axis=-1, keepdims=True)
    e2 = jnp.exp(m2 - m1)
    w1 = g_val / (1.0 + e2)
    w2 = g_val * e2 / (1.0 + e2)
    f1 = (i1 - n_groups).astype(F32)
    f2 = (i2 - n_groups).astype(F32)
    hit1, hit2 = col == i1, col == i2
    hits = jnp.where(hit1 | hit2, 1.0, 0.0)
    rows = logits.shape[0]
    earlier = lax.broadcasted_iota(jnp.int32, (rows, rows), 1) < lax.broadcasted_iota(jnp.int32, (rows, rows), 0)
    before = run + _dot(jnp.where(earlier, 1.0, 0.0).astype(BF16), hits.astype(BF16))
    r1 = jnp.sum(jnp.where(hit1, before, 0.0), axis=-1, keepdims=True)
    r2 = jnp.sum(jnp.where(hit2, before, 0.0), axis=-1, keepdims=True)
    route = f1
    for j, val in enumerate((f2, w1, w2, r1, r2), start=1):
        route = jnp.where(col == j, val, route)
    route = jnp.where(col > 5, 0.0, route)
    return route, run + jnp.sum(hits, axis=0, keepdims=True)


def _pack_pairs(x):
    half = x.shape[1] // 2
    lo = lax.bitcast_convert_type(x[:, :half].astype(F32), jnp.uint32)
    hi = lax.bitcast_convert_type(x[:, half:].astype(F32), jnp.uint32)
    return lax.bitcast_convert_type((lo >> 16) | (hi & jnp.uint32(0xFFFF0000)), F32)


def _unpack_pairs(w):
    u = lax.bitcast_convert_type(w, jnp.uint32)
    lo = lax.bitcast_convert_type(u << 16, F32)
    hi = lax.bitcast_convert_type(u & jnp.uint32(0xFFFF0000), F32)
    return jnp.concatenate([lo, hi], axis=1)


def _merge_tail(x, merged, wout_ref, gffn, wrt_ref, brt, run, n_groups, per_group):
    x1 = x + _dot(merged.astype(BF16), wout_ref[...])
    xn2 = _rms(x1, gffn).astype(BF16)
    logits = _dot(xn2, wrt_ref[...]) + brt
    route, run = _route(logits, run, n_groups, per_group)
    return x1, _pack_pairs(xn2), route, run


def _lru_prompt_kernel(x_ref, g_ref, wxy_ref, wga_ref, cw_ref, cb_ref, wr_ref, br_ref, wi_ref, bi_ref,
                       lam_ref, wbr_ref, out_ref, conv_ref, h_ref, ccar, hcar):
    c = pl.program_id(1)
    tl = x_ref.shape[1]
    width = hcar.shape[1]
    kw = cw_ref.shape[0]

    @pl.when(c == 0)
    def _():
        ccar[...] = jnp.zeros_like(ccar)
        hcar[...] = jnp.zeros_like(hcar)

    xn = _rms(x_ref[0], g_ref[...]).astype(BF16)
    z = _dot(xn, wxy_ref[...])
    ux, uy = z[:, :width], z[:, width:]
    row = lax.broadcasted_iota(jnp.int32, ux.shape, 0)
    acc = cw_ref[0:1, :] * ux
    for j in range(1, kw):
        prev = jnp.where(row == 0, ccar[j - 1:j, :], pltpu.roll(acc, 1, 0))
        ccar[j - 1:j, :] = acc[tl - 1:tl, :]
        acc = prev + cw_ref[j:j + 1, :] * ux
    uc = acc + cb_ref[...]
    a, mult, gate = _lru_coeffs(uc, wr_ref, br_ref[...], wi_ref, bi_ref[...], lam_ref[...])
    mult = jnp.where(row + c * tl == 0, 1.0, mult)
    h = _scan_rows(a, mult * gate * uc, hcar[...])
    hcar[...] = h[tl - 1:tl, :]
    tail = ux[tl - (kw - 1):tl, :]
    a_out = _dot((h * _gelu_tanh(uy)).astype(BF16), wbr_ref[...])
    ga = _dot(xn, wga_ref[...])
    out_ref[0] = (_sigmoid(ga) * a_out).astype(out_ref.dtype)

    @pl.when(c == pl.num_programs(1) - 1)
    def _():
        conv_ref[0] = tail
        h_ref[0] = h[tl - 1:tl, :]


def _ret_prompt_kernel(x_ref, g_ref, wq_ref, wk_ref, wv_ref, wg_ref, wgb_ref, cos_ref, sin_ref, wbr_ref,
                       out_ref, s_out_ref, s_scr, y_scr, *, heads, chunk):
    c = pl.program_id(1)
    tl = x_ref.shape[1]
    dk = wq_ref.shape[1] // heads
    dv = wv_ref.shape[1] // heads

    @pl.when(c == 0)
    def _():
        s_scr[...] = jnp.zeros_like(s_scr)

    xn = _rms(x_ref[0], g_ref[...]).astype(BF16)
    q = _dot(xn, wq_ref[...])
    k = _dot(xn, wk_ref[...])
    v = _dot(xn, wv_ref[...])
    g = _dot(xn, wg_ref[...])
    cos, sin = cos_ref[...], sin_ref[...]
    n_i = lax.broadcasted_iota(jnp.int32, (chunk, chunk), 0)
    m_i = lax.broadcasted_iota(jnp.int32, (chunk, chunk), 1)
    diff = (n_i - m_i).astype(F32)
    rowk = lax.broadcasted_iota(jnp.int32, (chunk, dk), 0).astype(F32)
    for h in range(heads):
        log_g = math.log1p(-(2.0 ** (-5.0 - h)))
        dmask = jnp.where(diff >= 0, jnp.exp(jnp.maximum(diff, 0.0) * log_g), 0.0)
        q_decay = jnp.exp((rowk + 1.0) * log_g)
        k_decay = jnp.exp((chunk - 1.0 - rowk) * log_g)
        s_decay = math.exp(chunk * log_g)
        for sub in range(tl // chunk):
            r0 = sub * chunk
            cs, sn = cos[r0:r0 + chunk], sin[r0:r0 + chunk]
            qr = _rotary(q[r0:r0 + chunk, h * dk:(h + 1) * dk], cs, sn)
            kr = _rotary(k[r0:r0 + chunk, h * dk:(h + 1) * dk], cs, sn) * (dk ** -0.5)
            vh = v[r0:r0 + chunk, h * dv:(h + 1) * dv].astype(BF16)
            s = s_scr[h]
            att = _dot_nt(qr.astype(BF16), kr.astype(BF16)) * dmask
            o = _dot(att.astype(BF16), vh) + _dot((qr * q_decay).astype(BF16), s.astype(BF16))
            s_scr[h] = s * s_decay + _dot_tn((kr * k_decay).astype(BF16), vh)
            o = o * lax.rsqrt(jnp.mean(o * o, axis=-1, keepdims=True) + EPS)
            gh = g[r0:r0 + chunk, h * dv:(h + 1) * dv]
            y_scr[r0:r0 + chunk, h * dv:(h + 1) * dv] = (_silu(gh) * o).astype(BF16)
    b_out = _dot(y_scr[...], wbr_ref[...])
    gb = _dot(xn, wgb_ref[...])
    out_ref[0] = (_sigmoid(gb) * b_out).astype(out_ref.dtype)

    @pl.when(c == pl.num_programs(1) - 1)
    def _():
        s_out_ref[0] = s_scr[...]


def _memkv_kernel(m_ref, g_ref, w_ref, k_ref, v_ref):
    width = k_ref.shape[-1]
    mn = _rms(m_ref[0], g_ref[...]).astype(BF16)
    kv = _dot(mn, w_ref[...])
    k_ref[0] = kv[:, :width]
    v_ref[0] = kv[:, width:]


def _attn_merge_prompt_kernel(x_ref, ag_ref, bg_ref, mk_ref, mv_ref, g_ref, wmq_ref, wgc_ref, wbm_ref,
                              wout_ref, gffn_ref, wrt_ref, brt_ref, x1_ref, xn2_ref, route_ref, cnt_ref,
                              *, heads, n_groups, per_group):
    @pl.when((pl.program_id(0) == 0) & (pl.program_id(1) == 0))
    def _():
        cnt_ref[...] = jnp.zeros_like(cnt_ref)

    x = x_ref[0]
    xn = _rms(x, g_ref[...]).astype(BF16)
    mq = _dot(xn, wmq_ref[...]).astype(BF16)
    dh = mq.shape[1] // heads
    mk = mk_ref[0].astype(BF16)
    mv = mv_ref[0].astype(BF16)
    cs = []
    for h in range(heads):
        hs = slice(h * dh, (h + 1) * dh)
        lg = _dot_nt(mq[:, hs], mk[:, hs]) * (dh ** -0.5)
        p = jnp.exp(lg - jnp.max(lg, axis=-1, keepdims=True))
        den = jnp.sum(p, axis=-1, keepdims=True)
        cs.append(_dot(p.astype(BF16), mv[:, hs]) / den)
    c_out = _dot(jnp.concatenate(cs, axis=-1).astype(BF16), wbm_ref[...])
    gc = _dot(xn, wgc_ref[...])
    merged = _sigmoid(gc) * c_out + ag_ref[0].astype(F32) + bg_ref[0].astype(F32)
    x1, xn2, route, run = _merge_tail(x, merged, wout_ref, gffn_ref[...], wrt_ref, brt_ref[...], cnt_ref[...],
                                      n_groups, per_group)
    x1_ref[0] = x1
    xn2_ref[...] = xn2
    route_ref[0] = route
    cnt_ref[...] = run


def _proj_kernel(x_ref, g_ref, w_ref, z_ref):
    xn = _rms(x_ref[...], g_ref[...]).astype(BF16)
    z_ref[...] = _dot(xn, w_ref[...])


def _lru_sample_kernel(ux_ref, uy_ref, ga_ref, cprev_ref, hprev_ref, cw_ref, cb_ref, wr_ref, br_ref, wi_ref,
                       bi_ref, lam_ref, wbr_ref, out_ref, conv_ref, h_ref, *, at_start):
    kw = cw_ref.shape[0]
    ux = ux_ref[...]
    uc = cb_ref[...] + cw_ref[kw - 1:kw, :] * ux
    for j in range(kw - 1):
        uc = uc + cw_ref[j:j + 1, :] * cprev_ref[:, j, :]
    a, mult, gate = _lru_coeffs(uc, wr_ref, br_ref[...], wi_ref, bi_ref[...], lam_ref[...])
    if at_start:
        mult = jnp.ones_like(mult)
    h = a * hprev_ref[...] + mult * gate * uc
    h_ref[...] = h
    for j in range(kw - 2):
        conv_ref[:, j, :] = cprev_ref[:, j + 1, :]
    conv_ref[:, kw - 2, :] = ux
    a_out = _dot((h * _gelu_tanh(uy_ref[...])).astype(BF16), wbr_ref[...])
    out_ref[...] = _sigmoid(ga_ref[...]) * a_out


def _rot_sample_kernel(q_ref, k_ref, cos_ref, sin_ref, qo_ref, ko_ref, *, heads):
    dk = q_ref.shape[1] // heads
    cos, sin = cos_ref[...], sin_ref[...]
    for h in range(heads):
        hs = slice(h * dk, (h + 1) * dk)
        qo_ref[:, hs] = _rotary(q_ref[:, hs], cos, sin)
        ko_ref[:, hs] = _rotary(k_ref[:, hs], cos, sin) * (dk ** -0.5)


def _ret_sample_kernel(qt_ref, kt_ref, q_ref, k_ref, v_ref, s_ref, o_ref, s_out_ref, *, heads):
    bb = q_ref.shape[1]
    dk = s_ref.shape[2]
    dv = s_ref.shape[3]
    for j in range(bb):
        for h in range(heads):
            decay = 1.0 - 2.0 ** (-5.0 - h)
            qcol = qt_ref[0, h, :, j:j + 1]
            kcol = kt_ref[0, h, :, j:j + 1]
            qrow = q_ref[0, j:j + 1, h * dk:(h + 1) * dk]
            krow = k_ref[0, j:j + 1, h * dk:(h + 1) * dk]
            vrow = v_ref[0, j:j + 1, h * dv:(h + 1) * dv]
            s = s_ref[j, h]
            att = jnp.sum(qrow * krow, axis=-1, keepdims=True)
            o = att * vrow + jnp.sum((qcol * decay) * s, axis=0, keepdims=True)
            o_ref[0, j:j + 1, h * dv:(h + 1) * dv] = o
            s_out_ref[j, h] = s * decay + kcol * vrow


def _attn_sample_kernel(mq_ref, k_ref, v_ref, c_ref, *, heads):
    bb, _, dh = mq_ref.shape
    for j in range(bb):
        q = mq_ref[j]
        lg = jnp.sum(k_ref[j] * q[None], axis=-1, keepdims=True) * (dh ** -0.5)
        p = jnp.exp(lg - jnp.max(lg, axis=0, keepdims=True))
        den = jnp.sum(p, axis=0)
        c_ref[j] = jnp.sum(v_ref[j] * p, axis=0) / den


def _merge_sample_kernel(x_ref, ag_ref, o_ref, g_ref, c_ref, gb_ref, gc_ref, wbr_ref, wbm_ref, wout_ref,
                         gffn_ref, wrt_ref, brt_ref, run_ref, xn2_table_ref, x1_ref, xn2_ref, route_ref, cnt_ref,
                         *, heads, n_groups, per_group):
    del xn2_table_ref
    dv = o_ref.shape[1] // heads
    ys = []
    for h in range(heads):
        o = o_ref[:, h * dv:(h + 1) * dv]
        o = o * lax.rsqrt(jnp.mean(o * o, axis=-1, keepdims=True) + EPS)
        ys.append((_silu(g_ref[:, h * dv:(h + 1) * dv]) * o).astype(BF16))
    b_out = _dot(jnp.concatenate(ys, axis=-1), wbr_ref[...])
    c_out = _dot(c_ref[...].astype(BF16), wbm_ref[...])
    merged = ag_ref[...] + _sigmoid(gb_ref[...]) * b_out + _sigmoid(gc_ref[...]) * c_out
    x1, xn2, route, run = _merge_tail(x_ref[...], merged, wout_ref, gffn_ref[...], wrt_ref, brt_ref[...],
                                      run_ref[...], n_groups, per_group)
    x1_ref[...] = x1
    xn2_ref[...] = xn2
    route_ref[...] = route
    cnt_ref[...] = run


def _expert_kernel(te_ref, nu_ref, xa_ref, xb_ref, wg_ref, wu_ref, wd_ref, ys_ref, *, half):
    i = pl.program_id(0)

    @pl.when(i < nu_ref[0])
    def _():
        x = _unpack_pairs(jnp.where(i < half, xa_ref[...], xb_ref[...])).astype(BF16)
        hg = _dot(x, wg_ref[0].astype(BF16))
        hu = _dot(x, wu_ref[0].astype(BF16))
        hid = (_silu(hg) * hu).astype(BF16)
        ys_ref[...] = _pack_pairs(_dot(hid, wd_ref[0].astype(BF16)).astype(BF16))

    @pl.when(i >= nu_ref[0])
    def _():
        ys_ref[...] = jnp.zeros_like(ys_ref)


def _final_kernel(x1_ref, y1_ref, y2_ref, route_ref, g_ref, out_ref):
    w1 = route_ref[:, 2:3]
    w2 = route_ref[:, 3:4]
    x2 = x1_ref[...] + (w1 * _unpack_pairs(y1_ref[...]) + w2 * _unpack_pairs(y2_ref[...]))
    out_ref[...] = _rms(x2, g_ref[...])


def _tile(n, target):
    t = min(n, target)
    while n % t:
        t //= 2
    return t


def _row(v):
    return v.reshape(1, -1).astype(F32)


def _prep_layer(p, dims):
    d, w, hk, hv, hm = dims['d'], dims['w'], dims['hk'], dims['hv'], dims['hm']
    w_in = p['w_in']
    cuts = [0, w, 2 * w, 2 * w + hk, 2 * w + 2 * hk, 2 * w + 2 * hk + hv, 2 * w + 2 * hk + 2 * hv,
            2 * w + 2 * hk + 2 * hv + hm, w_in.shape[1]]
    seg = [w_in[:, cuts[i]:cuts[i + 1]].astype(BF16) for i in range(8)]
    gates = seg[7]
    g, e = p['w_exp_router'].shape[0], p['w_exp_router'].shape[2]
    w_rt = jnp.concatenate([p['w_grp'], jnp.moveaxis(p['w_exp_router'], 0, 1).reshape(d, g * e)], axis=1)
    w_rt = jnp.pad(w_rt, ((0, 0), (0, LANES - w_rt.shape[1]))).astype(BF16)
    b_rt = jnp.pad(jnp.concatenate([p['b_grp'], p['b_exp_router'].reshape(-1)]), (0, LANES - g - g * e))
    return dict(
        w_in=w_in.astype(BF16),
        w_xy=jnp.concatenate([seg[0], seg[1]], axis=1), w_q=seg[2], w_k=seg[3], w_v=seg[4], w_g=seg[5],
        w_mq=seg[6], w_ga=gates[:, :d], w_gb=gates[:, d:2 * d], w_gc=gates[:, 2 * d:],
        norm_mix=_row(p['norm_mix']), norm_ffn=_row(p['norm_ffn']), norm_mem=_row(p['norm_mem']),
        w_mem_kv=p['w_mem_kv'].astype(BF16),
        conv_w=p['conv_w'].astype(F32), conv_b=_row(p['conv_b']),
        w_r=p['w_r'].astype(BF16), b_r=_row(p['b_r']), w_i=p['w_i'].astype(BF16), b_i=_row(p['b_i']),
        lam=_row(p['lru_lambda']),
        w_br_lru=p['w_branch_lru'].astype(BF16), w_br_ret=p['w_branch_ret'].astype(BF16),
        w_br_mem=p['w_branch_mem'].astype(BF16), w_out=p['w_out'].astype(BF16),
        w_rt=w_rt, b_rt=_row(b_rt),
        w_gate=p['w_gate'].reshape((g * e,) + p['w_gate'].shape[2:]),
        w_up=p['w_up'].reshape((g * e,) + p['w_up'].shape[2:]),
        w_down=p['w_down'].reshape((g * e,) + p['w_down'].shape[2:]),
    )


def _rope_tables(pos, dk):
    half = dk // 2
    inv = ROPE_BASE ** (-jnp.linspace(0.0, 1.0, half, dtype=F32))
    ang = pos.astype(F32)[:, None] * inv[None, :]
    return jnp.cos(ang), jnp.sin(ang)


def _prompt_mixer(x, mem, lw, dims, extra_rows):
    b, l, d = x.shape
    w, heads, dk, dv = dims['w'], dims['heads'], dims['dk'], dims['dv']
    kw = lw['conv_w'].shape[0]
    tl = _tile(l, 256)
    grid = (b, l // tl)
    xspec = pl.BlockSpec((1, tl, d), lambda i, c: (i, c, 0))
    sem = ("parallel", "arbitrary")

    a_g, conv_new, h_last = pl.pallas_call(
        _lru_prompt_kernel,
        grid=grid,
        in_specs=[xspec, _const_spec((1, d)), _const_spec((d, 2 * w)), _const_spec((d, d)),
                  _const_spec((kw, w)), _const_spec((1, w)), _const_spec(lw['w_r'].shape), _const_spec((1, w)),
                  _const_spec(lw['w_i'].shape), _const_spec((1, w)), _const_spec((1, w)), _const_spec((w, d))],
        out_specs=[pl.BlockSpec((1, tl, d), lambda i, c: (i, c, 0)),
                   pl.BlockSpec((1, kw - 1, w), lambda i, c: (i, 0, 0)),
                   pl.BlockSpec((1, 1, w), lambda i, c: (i, 0, 0))],
        out_shape=[jax.ShapeDtypeStruct((b, l, d), BF16), jax.ShapeDtypeStruct((b, kw - 1, w), F32),
                   jax.ShapeDtypeStruct((b, 1, w), F32)],
        scratch_shapes=[pltpu.VMEM((SUBLANES, w), F32), pltpu.VMEM((1, w), F32)],
        compiler_params=_params(sem),
    )(x, lw['norm_mix'], lw['w_xy'], lw['w_ga'], lw['conv_w'], lw['conv_b'], lw['w_r'], lw['b_r'],
      lw['w_i'], lw['b_i'], lw['lam'], lw['w_br_lru'])

    chunk = _tile(tl, 128)
    cos, sin = _rope_tables(jnp.arange(l, dtype=jnp.int32), dk)
    half = dk // 2
    b_g, s_new = pl.pallas_call(
        functools.partial(_ret_prompt_kernel, heads=heads, chunk=chunk),
        grid=grid,
        in_specs=[xspec, _const_spec((1, d)), _const_spec((d, heads * dk)), _const_spec((d, heads * dk)),
                  _const_spec((d, heads * dv)), _const_spec((d, heads * dv)), _const_spec((d, d)),
                  pl.BlockSpec((tl, half), lambda i, c: (c, 0)), pl.BlockSpec((tl, half), lambda i, c: (c, 0)),
                  _const_spec((heads * dv, d))],
        out_specs=[pl.BlockSpec((1, tl, d), lambda i, c: (i, c, 0)),
                   pl.BlockSpec((1, heads, dk, dv), lambda i, c: (i, 0, 0, 0))],
        out_shape=[jax.ShapeDtypeStruct((b, l, d), BF16), jax.ShapeDtypeStruct((b, heads, dk, dv), F32)],
        scratch_shapes=[pltpu.VMEM((heads, dk, dv), F32), pltpu.VMEM((tl, heads * dv), BF16)],
        compiler_params=_params(sem),
    )(x, lw['norm_mix'], lw['w_q'], lw['w_k'], lw['w_v'], lw['w_g'], lw['w_gb'], cos, sin, lw['w_br_ret'])

    m = mem.shape[1]
    hm = dims['hm']
    mk, mv = pl.pallas_call(
        _memkv_kernel,
        grid=(b,),
        in_specs=[pl.BlockSpec((1, m, d), lambda i: (i, 0, 0)), _const_spec((1, d)), _const_spec((d, 2 * hm))],
        out_specs=[pl.BlockSpec((1, m, hm), lambda i: (i, 0, 0)), pl.BlockSpec((1, m, hm), lambda i: (i, 0, 0))],
        out_shape=[jax.ShapeDtypeStruct((b, m, hm), F32), jax.ShapeDtypeStruct((b, m, hm), F32)],
        compiler_params=_params(("parallel",)),
    )(mem, lw['norm_mem'], lw['w_mem_kv'])

    x1, xn2, route, counts = pl.pallas_call(
        functools.partial(_attn_merge_prompt_kernel, heads=dims['mheads'], n_groups=dims['g'],
                          per_group=dims['e']),
        grid=grid,
        in_specs=[xspec, xspec, xspec,
                  pl.BlockSpec((1, m, hm), lambda i, c: (i, 0, 0)), pl.BlockSpec((1, m, hm), lambda i, c: (i, 0, 0)),
                  _const_spec((1, d)), _const_spec((d, hm)), _const_spec((d, d)), _const_spec((hm, d)),
                  _const_spec((d, d)), _const_spec((1, d)), _const_spec((d, LANES)), _const_spec((1, LANES))],
        out_specs=[xspec, pl.BlockSpec((tl, d // 2), lambda i, c: (i * (l // tl) + c, 0)),
                   pl.BlockSpec((1, tl, LANES), lambda i, c: (i, c, 0)),
                   pl.BlockSpec((1, LANES), lambda i, c: (0, 0))],
        out_shape=[jax.ShapeDtypeStruct((b, l, d), F32), jax.ShapeDtypeStruct((b * l + extra_rows, d // 2), F32),
                   jax.ShapeDtypeStruct((b, l, LANES), F32), jax.ShapeDtypeStruct((1, LANES), F32)],
        compiler_params=_params(("arbitrary", "arbitrary")),
    )(x, a_g, b_g, mk, mv, lw['norm_mix'], lw['w_mq'], lw['w_gc'], lw['w_br_mem'], lw['w_out'],
      lw['norm_ffn'], lw['w_rt'], lw['b_rt'])
    return x1, xn2, route, counts, conv_new, h_last[:, 0], s_new, mk, mv


def _sample_mixer(x, conv_prev, h_prev, s_prev, mem_k, mem_v, counts0, xn2_all, lw, dims):
    n, d = x.shape
    w, heads, dk, dv, hm = dims['w'], dims['heads'], dims['dk'], dims['dv'], dims['hm']
    hk, hv = heads * dk, heads * dv
    kw = lw['conv_w'].shape[0]
    n_in = lw['w_in'].shape[1]
    tn = _tile(n_in, 1024)
    z = pl.pallas_call(
        _proj_kernel,
        grid=(n_in // tn,),
        in_specs=[_const_spec((n, d)), _const_spec((1, d)), pl.BlockSpec((d, tn), lambda j: (0, j))],
        out_specs=pl.BlockSpec((n, tn), lambda j: (0, j)),
        out_shape=jax.ShapeDtypeStruct((n, n_in), F32),
        compiler_params=_params(("parallel",)),
    )(x, lw['norm_mix'], lw['w_in'])
    o0 = 0
    parts = []
    for sz in (w, w, hk, hk, hv, hv, hm, d, d, d):
        parts.append(z[:, o0:o0 + sz])
        o0 += sz
    ux, uy, q, k, v, g, mq, ga, gb, gc = parts

    a_g, conv_new, h_new = pl.pallas_call(
        functools.partial(_lru_sample_kernel, at_start=(PAST_LEN == 0)),
        out_shape=[jax.ShapeDtypeStruct((n, d), F32), jax.ShapeDtypeStruct((n, kw - 1, w), F32),
                   jax.ShapeDtypeStruct((n, w), F32)],
        compiler_params=pltpu.CompilerParams(vmem_limit_bytes=VMEM_LIMIT),
    )(ux, uy, ga, conv_prev, h_prev, lw['conv_w'], lw['conv_b'], lw['w_r'], lw['b_r'], lw['w_i'], lw['b_i'],
      lw['lam'], lw['w_br_lru'])

    cos, sin = _rope_tables(PAST_LEN + jnp.arange(1, dtype=jnp.int32), dk)
    qr, kr = pl.pallas_call(
        functools.partial(_rot_sample_kernel, heads=heads),
        out_shape=[jax.ShapeDtypeStruct((n, hk), F32), jax.ShapeDtypeStruct((n, hk), F32)],
    )(q, k, cos, sin)

    bb = _tile(n, 4)

    def cols(t):
        return t.reshape(n // bb, bb, heads, dk).transpose(0, 2, 3, 1)

    def rows(t):
        return t.reshape(n // bb, bb, t.shape[1])

    o, s_new = pl.pallas_call(
        functools.partial(_ret_sample_kernel, heads=heads),
        grid=(n // bb,),
        in_specs=[pl.BlockSpec((1, heads, dk, bb), lambda i: (i, 0, 0, 0)),
                  pl.BlockSpec((1, heads, dk, bb), lambda i: (i, 0, 0, 0)),
                  pl.BlockSpec((1, bb, hk), lambda i: (i, 0, 0)), pl.BlockSpec((1, bb, hk), lambda i: (i, 0, 0)),
                  pl.BlockSpec((1, bb, hv), lambda i: (i, 0, 0)),
                  pl.BlockSpec((bb, heads, dk, dv), lambda i: (i, 0, 0, 0))],
        out_specs=[pl.BlockSpec((1, bb, hv), lambda i: (i, 0, 0)),
                   pl.BlockSpec((bb, heads, dk, dv), lambda i: (i, 0, 0, 0))],
        out_shape=[jax.ShapeDtypeStruct((n // bb, bb, hv), F32), jax.ShapeDtypeStruct(s_prev.shape, F32)],
        compiler_params=_params(("parallel",)),
    )(cols(qr), cols(kr), rows(qr), rows(kr), rows(v), s_prev)
    o = o.reshape(n, hv)
    bb = _tile(n, SUBLANES)

    m = mem_k.shape[1]
    c = pl.pallas_call(
        functools.partial(_attn_sample_kernel, heads=dims['mheads']),
        grid=(n // bb,),
        in_specs=[pl.BlockSpec((bb,) + mem_k.shape[2:], lambda i: (i, 0, 0)),
                  pl.BlockSpec((bb,) + mem_k.shape[1:], lambda i: (i, 0, 0, 0)),
                  pl.BlockSpec((bb,) + mem_v.shape[1:], lambda i: (i, 0, 0, 0))],
        out_specs=pl.BlockSpec((bb,) + mem_k.shape[2:], lambda i: (i, 0, 0)),
        out_shape=jax.ShapeDtypeStruct((n,) + mem_k.shape[2:], F32),
        compiler_params=_params(("parallel",)),
    )(mq.reshape((n,) + mem_k.shape[2:]), mem_k, mem_v).reshape(n, hm)

    first = xn2_all.shape[0] - n
    assert first % n == 0
    args = (x, a_g, o, g, c, gb, gc, lw['w_br_ret'], lw['w_br_mem'], lw['w_out'], lw['norm_ffn'], lw['w_rt'],
            lw['b_rt'], counts0)
    x1, xn2_all, route, counts = pl.pallas_call(
        functools.partial(_merge_sample_kernel, heads=heads, n_groups=dims['g'], per_group=dims['e']),
        grid=(1,),
        in_specs=[_const_spec(a.shape) for a in args] + [pl.BlockSpec(memory_space=pl.ANY)],
        out_specs=[pl.BlockSpec((n, d), lambda i: (0, 0)), pl.BlockSpec((n, d // 2), lambda i: (first // n, 0)),
                   pl.BlockSpec((n, LANES), lambda i: (0, 0)), pl.BlockSpec((1, LANES), lambda i: (0, 0))],
        out_shape=[jax.ShapeDtypeStruct((n, d), F32), jax.ShapeDtypeStruct(xn2_all.shape, F32),
                   jax.ShapeDtypeStruct((n, LANES), F32), jax.ShapeDtypeStruct((1, LANES), F32)],
        input_output_aliases={len(args): 1},
        compiler_params=_params(("arbitrary",)),
    )(*args, xn2_all)
    return x1, xn2_all, route, counts, conv_new, h_new, s_new


def _moe(xn2, route_parts, counts, lw, dims, tm=256):
    t, d = xn2.shape[0], dims['d']
    g, n_exp = dims['g'], dims['g'] * dims['e']
    ff = lw['w_gate'].shape[2]
    counts = counts[0, g:g + n_exp].astype(jnp.int32)
    padded = ((counts + tm - 1) // tm) * tm
    ends = jnp.cumsum(padded)
    offs = ends - padded
    n_tiles = 2 * ((TOP_K * t + n_exp * tm + 2 * tm - 1) // (2 * tm))
    n_slots = n_tiles * tm
    half = n_tiles // 2
    experts = jnp.arange(n_exp, dtype=jnp.int32)[None, :]
    dests = []
    for j in range(TOP_K):
        per_part = []
        for r in route_parts:
            ids = r[:, j].astype(jnp.int32)
            base = jnp.sum(jnp.where(ids[:, None] == experts, offs[None, :], 0), axis=1)
            per_part.append(base + r[:, 4 + j].astype(jnp.int32))
        dests.append(per_part)
    dest = jnp.concatenate([dp for per_part in dests for dp in per_part], axis=0)
    tok = jnp.arange(t, dtype=jnp.int32)
    placed = jnp.zeros((n_slots,), F32).at[dest].add(jnp.concatenate([tok] * TOP_K).astype(F32) + 1.0,
                                                       mode='promise_in_bounds')
    placed = placed.astype(jnp.int32)
    src = jnp.where(placed > 0, placed - 1, jnp.arange(n_slots, dtype=jnp.int32) % t)
    n_used = (ends[-1] // tm).astype(jnp.int32)
    tile_row = jnp.arange(n_tiles, dtype=jnp.int32) * tm
    tile_e = jnp.sum((ends[None, :] <= tile_row[:, None]).astype(jnp.int32), axis=1)
    last_e = jnp.sum((ends <= (n_used - 1) * tm).astype(jnp.int32))
    tile_e = jnp.where(tile_row < ends[-1], tile_e, last_e)
    xs_a = xn2.at[src[:half * tm]].get(mode='promise_in_bounds')
    xs_b = xn2.at[src[half * tm:]].get(mode='promise_in_bounds')

    def map_a(i, te, nu):
        return (jnp.minimum(jnp.minimum(i, nu[0] - 1), half - 1), 0)

    def map_b(i, te, nu):
        return (jnp.clip(jnp.minimum(i, nu[0] - 1) - half, 0, half - 1), 0)

    ys = pl.pallas_call(
        functools.partial(_expert_kernel, half=half),
        grid_spec=pltpu.PrefetchScalarGridSpec(
            num_scalar_prefetch=2,
            grid=(n_tiles,),
            in_specs=[pl.BlockSpec((tm, d // 2), map_a), pl.BlockSpec((tm, d // 2), map_b),
                      pl.BlockSpec((1, d, ff), lambda i, te, nu: (te[i], 0, 0)),
                      pl.BlockSpec((1, d, ff), lambda i, te, nu: (te[i], 0, 0)),
                      pl.BlockSpec((1, ff, d), lambda i, te, nu: (te[i], 0, 0))],
            out_specs=pl.BlockSpec((tm, d // 2), lambda i, te, nu: (i, 0)),
        ),
        out_shape=jax.ShapeDtypeStruct((n_slots, d // 2), F32),
        compiler_params=_params(("arbitrary",)),
    )(tile_e, n_used.reshape(1), xs_a, xs_b, lw['w_gate'], lw['w_up'], lw['w_down'])
    return [tuple(ys.at[dests[j][p]].get(mode='promise_in_bounds') for j in range(TOP_K))
            for p in range(len(route_parts))]


def _final(x1, y_pair, route, g):
    t, d = x1.shape
    tl = _tile(t, 512)
    row = pl.BlockSpec((tl, d), lambda i: (i, 0))
    packed = pl.BlockSpec((tl, d // 2), lambda i: (i, 0))
    return pl.pallas_call(
        _final_kernel,
        grid=(t // tl,),
        in_specs=[row, packed, packed, pl.BlockSpec((tl, LANES), lambda i: (i, 0)), _const_spec((1, d))],
        out_specs=row,
        out_shape=jax.ShapeDtypeStruct((t, d), F32),
        compiler_params=_params(("parallel",)),
    )(x1, y_pair[0], y_pair[1], route, g)


def kernel(x_prompt, x_sample, mem_prompt, state_conv, state_lru, state_ret, cache_mem_k, cache_mem_v, norm_mix, norm_ffn, norm_mem, norm_final, w_in, w_mem_kv, conv_w, conv_b, w_r, b_r, w_i, b_i, lru_lambda, w_branch_lru, w_branch_ret, w_branch_mem, w_out, w_grp, b_grp, w_exp_router, b_exp_router, w_gate, w_up, w_down):
    depth = w_in.shape[0]
    assert depth == 1, "the two request groups are chained per layer only for a single-layer trunk"
    assert x_sample.shape[1] == 1
    bp, lp, d = x_prompt.shape
    ns = x_sample.shape[0]
    heads, dk, dv = state_ret.shape[2], state_ret.shape[3], state_ret.shape[4]
    mheads, mdh = cache_mem_k.shape[3], cache_mem_k.shape[4]
    dims = dict(d=d, w=state_lru.shape[2], heads=heads, dk=dk, dv=dv, hk=heads * dk, hv=heads * dv,
                mheads=mheads, hm=mheads * mdh, g=w_exp_router.shape[1], e=w_exp_router.shape[3])
    layer = 0
    p = dict(norm_mix=norm_mix[layer], norm_ffn=norm_ffn[layer], norm_mem=norm_mem[layer], w_in=w_in[layer],
             w_mem_kv=w_mem_kv[layer], conv_w=conv_w[layer], conv_b=conv_b[layer], w_r=w_r[layer], b_r=b_r[layer],
             w_i=w_i[layer], b_i=b_i[layer], lru_lambda=lru_lambda[layer], w_branch_lru=w_branch_lru[layer],
             w_branch_ret=w_branch_ret[layer], w_branch_mem=w_branch_mem[layer], w_out=w_out[layer],
             w_grp=w_grp[layer], b_grp=b_grp[layer], w_exp_router=w_exp_router[layer],
             b_exp_router=b_exp_router[layer], w_gate=w_gate[layer], w_up=w_up[layer], w_down=w_down[layer])
    lw = _prep_layer(p, dims)

    x1p, xn2, route_p, counts, conv_p, lru_p, ret_p, mk, mv = _prompt_mixer(x_prompt, mem_prompt, lw, dims, ns)
    x1s, xn2, route_s, counts, conv_s, lru_s, ret_s = _sample_mixer(
        x_sample[:, 0], state_conv[layer], state_lru[layer], state_ret[layer], cache_mem_k[layer],
        cache_mem_v[layer], counts, xn2, lw, dims)

    tp = bp * lp
    route_p = route_p.reshape(tp, LANES)
    yg_p, yg_s = _moe(xn2, [route_p, route_s], counts, lw, dims)
    yp = _final(x1p.reshape(tp, d), yg_p, route_p, _row(norm_final)).reshape(bp, lp, d)
    ys = _final(x1s, yg_s, route_s, _row(norm_final)).reshape(ns, 1, d)
    mshape = (1, bp, mem_prompt.shape[1], mheads, mdh)
    return (yp, ys, conv_p[None], lru_p[None], ret_p[None], mk.reshape(mshape), mv.reshape(mshape),
            conv_s[None], lru_s[None], ret_s[None])
```

```python
import functools
import itertools
import math

import jax
import jax.numpy as jnp
from jax import lax
from jax.experimental import pallas as pl
from jax.experimental.pallas import tpu as pltpu

F32 = jnp.float32
BF16 = jnp.bfloat16

EPS = 1e-6
LRU_C = 8.0
ROPE_BASE = 10000.0
PAST_LEN = 16384
TOP_K = 2
LANES = 128
SUBLANES = 8
VMEM_LIMIT = 56 * 1024 * 1024


def _dot(a, b):
    return jnp.dot(a, b, preferred_element_type=F32)


def _dot_nt(a, b):
    return lax.dot_general(a, b, (((1,), (1,)), ((), ())), preferred_element_type=F32)


def _dot_tn(a, b):
    return lax.dot_general(a, b, (((0,), (0,)), ((), ())), preferred_element_type=F32)


def _rms(x, g):
    return x * lax.rsqrt(jnp.mean(x * x, axis=-1, keepdims=True) + EPS) * g


def _sigmoid(x):
    return 0.5 * jnp.tanh(0.5 * x) + 0.5


def _silu(x):
    return x * _sigmoid(x)


def _gelu_tanh(x):
    return 0.5 * x * (1.0 + jnp.tanh(math.sqrt(2.0 / math.pi) * (x + 0.044715 * (x * x * x))))


def _softplus(x):
    return jnp.maximum(x, 0.0) + jnp.log1p(jnp.exp(-jnp.abs(x)))


def _const_spec(shape):
    nd = len(shape)
    return pl.BlockSpec(shape, lambda *_: (0,) * nd, pipeline_mode=pl.Buffered(1))


def _window_spec(rows, window):
    start, width = window
    return pl.BlockSpec((rows, width), lambda *_: (0, start // width), pipeline_mode=pl.Buffered(1))


def _params(sem):
    return pltpu.CompilerParams(dimension_semantics=sem, vmem_limit_bytes=VMEM_LIMIT)


def _lru_coeffs(uc, wr_ref, br, wi_ref, bi, lam):
    nb, bs = wr_ref.shape[0], wr_ref.shape[1]
    ucb = uc.astype(BF16)
    r_lin = jnp.concatenate([_dot(ucb[:, n * bs:(n + 1) * bs], wr_ref[n]) for n in range(nb)], axis=-1)
    i_lin = jnp.concatenate([_dot(ucb[:, n * bs:(n + 1) * bs], wi_ref[n]) for n in range(nb)], axis=-1)
    r = _sigmoid(r_lin + br)
    i = _sigmoid(i_lin + bi)
    log_a = (-LRU_C) * r * _softplus(-lam)
    a = jnp.exp(log_a)
    t = 1.0 - a * a
    mult = t * lax.rsqrt(jnp.maximum(t, 1e-37))
    return a, mult, i


def _scan_rows(a, b, h0):
    rows, width = a.shape
    grp = SUBLANES
    n_grp = rows // grp
    a = a.reshape(n_grp, grp, width)
    b = b.reshape(n_grp, grp, width)
    row = lax.broadcasted_iota(jnp.int32, a.shape, 1)
    d = 1
    while d < grp:
        a_sh = pltpu.roll(a, d, 1)
        b_sh = pltpu.roll(b, d, 1)
        keep = row >= d
        b = jnp.where(keep, b + a * b_sh, b)
        a = jnp.where(keep, a * a_sh, a)
        d *= 2
    out = []
    for g in range(n_grp):
        hg = a[g] * h0 + b[g]
        out.append(hg)
        h0 = hg[grp - 1:grp, :]
    return jnp.concatenate(out, axis=0)


def _rotary(t, cos, sin):
    half = t.shape[-1] // 2
    t1, t2 = t[:, :half], t[:, half:]
    return jnp.concatenate([t1 * cos - t2 * sin, t2 * cos + t1 * sin], axis=-1)


def _route(logits, run, n_groups, per_group):
    col = lax.broadcasted_iota(jnp.int32, logits.shape, 1)
    big = jnp.int32(1 << 20)
    neg = jnp.float32(-jnp.inf)
    gl = jnp.where(col < n_groups, logits, neg)
    gmax = jnp.max(gl, axis=-1, keepdims=True)
    g_idx = jnp.min(jnp.where(gl == gmax, col, big), axis=-1, keepdims=True)
    g_val = 1.0 / jnp.sum(jnp.exp(gl - gmax), axis=-1, keepdims=True)
    lo = n_groups + per_group * g_idx
    el = jnp.where((col >= lo) & (col < lo + per_group), logits, neg)
    m1 = jnp.max(el, axis=-1, keepdims=True)
    i1 = jnp.min(jnp.where(el == m1, col, big), axis=-1, keepdims=True)
    el2 = jnp.where(col == i1, neg, el)
    m2 = jnp.max(el2, axis=-1, keepdims=True)
    i2 = jnp.min(jnp.where(el2 == m2, col, big), axis=-1, keepdims=True)
    e2 = jnp.exp(m2 - m1)
    w1 = g_val / (1.0 + e2)
    w2 = g_val * e2 / (1.0 + e2)
    f1 = (i1 - n_groups).astype(F32)
    f2 = (i2 - n_groups).astype(F32)
    hit1, hit2 = col == i1, col == i2
    hits = jnp.where(hit1 | hit2, 1.0, 0.0)
    rows = logits.shape[0]
    earlier = lax.broadcasted_iota(jnp.int32, (rows, rows), 1) < lax.broadcasted_iota(jnp.int32, (rows, rows), 0)
    before = run + _dot(jnp.where(earlier, 1.0, 0.0).astype(BF16), hits.astype(BF16))
    r1 = jnp.sum(jnp.where(hit1, before, 0.0), axis=-1, keepdims=True)
    r2 = jnp.sum(jnp.where(hit2, before, 0.0), axis=-1, keepdims=True)
    route = f1
    for j, val in enumerate((f2, w1, w2, r1, r2), start=1):
        route = jnp.where(col == j, val, route)
    route = jnp.where(col > 5, 0.0, route)
    return route, run + jnp.sum(hits, axis=0, keepdims=True)


def _pack_pairs(x):
    half = x.shape[1] // 2
    lo = lax.bitcast_convert_type(x[:, :half].astype(F32), jnp.uint32)
    hi = lax.bitcast_convert_type(x[:, half:].astype(F32), jnp.uint32)
    return lax.bitcast_convert_type((lo >> 16) | (hi & jnp.uint32(0xFFFF0000)), F32)


def _unpack_pairs(w):
    u = lax.bitcast_convert_type(w, jnp.uint32)
    lo = lax.bitcast_convert_type(u << 16, F32)
    hi = lax.bitcast_convert_type(u & jnp.uint32(0xFFFF0000), F32)
    return jnp.concatenate([lo, hi], axis=1)


def _merge_tail(x, merged, wout_ref, gffn, wrt_ref, brt, run, n_groups, per_group):
    x1 = x + _dot(merged.astype(BF16), wout_ref[...])
    xn2 = _rms(x1, gffn).astype(BF16)
    logits = _dot(xn2, wrt_ref[...]) + brt
    route, run = _route(logits, run, n_groups, per_group)
    return x1, _pack_pairs(xn2), route, run


def _lru_prompt_steps(c, xn, wxy_ref, wga_ref, cw_ref, cb_ref, wr_ref, br_ref, wi_ref, bi_ref, lam_ref, wbr_ref,
                      out_ref, ccar, hcar, res):
    tl = xn.shape[0]
    width = hcar.shape[1]
    kw = cw_ref.shape[0]
    nb, bs = wr_ref.shape[0], wr_ref.shape[1]
    z = _dot(xn, wxy_ref[...])
    row = lax.broadcasted_iota(jnp.int32, (tl, bs), 0)
    hbs, lasts = [], []
    for n in range(nb):
        cs = slice(n * bs, (n + 1) * bs)
        ux = z[:, cs]
        acc = cw_ref[0:1, cs] * ux
        for j in range(1, kw):
            prev = jnp.where(row == 0, ccar[j - 1:j, cs], pltpu.roll(acc, 1, 0))
            ccar[j - 1:j, cs] = acc[tl - 1:tl, :]
            acc = prev + cw_ref[j:j + 1, cs] * ux
        uc = acc + cb_ref[:, cs]
        ucb = uc.astype(BF16)
        r = _sigmoid(_dot(ucb, wr_ref[n]) + br_ref[:, cs])
        gate = _sigmoid(_dot(ucb, wi_ref[n]) + bi_ref[:, cs])
        log_a = (-LRU_C) * r * _softplus(-lam_ref[:, cs])
        a = jnp.exp(log_a)
        t = 1.0 - a * a
        mult = t * lax.rsqrt(jnp.maximum(t, 1e-37))
        mult = jnp.where(row + c * tl == 0, 1.0, mult)
        h = _scan_rows(a, mult * gate * uc, hcar[:, cs])
        hcar[:, cs] = h[tl - 1:tl, :]
        lasts.append(h[tl - 1:tl, :])
        hbs.append((h * _gelu_tanh(z[:, width + n * bs: width + (n + 1) * bs])).astype(BF16))
        yield
    res['tail'] = z[tl - (kw - 1):tl, :width]
    res['h_last'] = jnp.concatenate(lasts, axis=1)
    a_out = _dot(jnp.concatenate(hbs, axis=1), wbr_ref[...])
    ga = _dot(xn, wga_ref[...])
    out_ref[0] = (_sigmoid(ga) * a_out).astype(out_ref.dtype)
    yield


def _ret_proj_steps(xn, w_refs, res, cols=512):
    for name, w_ref in w_refs:
        parts = []
        for j in range(0, w_ref.shape[1], cols):
            parts.append(_dot(xn, w_ref[:, j:j + cols]))
            yield
        res[name] = jnp.concatenate(parts, axis=1)


def _ret_prompt_steps(xn, proj, wgb_ref, cos_ref, sin_ref, wbr_ref, out_ref, s_scr, y_scr, heads, chunk):
    tl = xn.shape[0]
    q, k, v, g = proj['q'], proj['k'], proj['v'], proj['g']
    dk = q.shape[1] // heads
    dv = v.shape[1] // heads
    cos, sin = cos_ref[...], sin_ref[...]
    n_i = lax.broadcasted_iota(jnp.int32, (chunk, chunk), 0)
    m_i = lax.broadcasted_iota(jnp.int32, (chunk, chunk), 1)
    diff = (n_i - m_i).astype(F32)
    rowk = lax.broadcasted_iota(jnp.int32, (chunk, dk), 0).astype(F32)
    for h in range(heads):
        log_g = math.log1p(-(2.0 ** (-5.0 - h)))
        dmask = jnp.where(diff >= 0, jnp.exp(jnp.maximum(diff, 0.0) * log_g), 0.0)
        q_decay = jnp.exp((rowk + 1.0) * log_g)
        k_decay = jnp.exp((chunk - 1.0 - rowk) * log_g)
        s_decay = math.exp(chunk * log_g)
        for sub in range(tl // chunk):
            r0 = sub * chunk
            cs, sn = cos[r0:r0 + chunk], sin[r0:r0 + chunk]
            qr = _rotary(q[r0:r0 + chunk, h * dk:(h + 1) * dk], cs, sn)
            kr = _rotary(k[r0:r0 + chunk, h * dk:(h + 1) * dk], cs, sn) * (dk ** -0.5)
            vh = v[r0:r0 + chunk, h * dv:(h + 1) * dv].astype(BF16)
            s = s_scr[h]
            att = _dot_nt(qr.astype(BF16), kr.astype(BF16)) * dmask
            o = _dot(att.astype(BF16), vh) + _dot((qr * q_decay).astype(BF16), s.astype(BF16))
            s_scr[h] = s * s_decay + _dot_tn((kr * k_decay).astype(BF16), vh)
            o = o * lax.rsqrt(jnp.mean(o * o, axis=-1, keepdims=True) + EPS)
            gh = g[r0:r0 + chunk, h * dv:(h + 1) * dv]
            y_scr[r0:r0 + chunk, h * dv:(h + 1) * dv] = (_silu(gh) * o).astype(BF16)
            yield
    b_out = _dot(y_scr[...], wbr_ref[...])
    gb = _dot(xn, wgb_ref[...])
    out_ref[0] = (_sigmoid(gb) * b_out).astype(out_ref.dtype)


def _lru_ret_prompt_kernel(x_ref, g_ref, wxy_ref, wga_ref, cw_ref, cb_ref, wr_ref, br_ref, wi_ref, bi_ref,
                           lam_ref, wbl_ref, wq_ref, wk_ref, wv_ref, wg_ref, wgb_ref, cos_ref, sin_ref, wbr_ref,
                           a_ref, conv_ref, h_ref, b_ref, s_out_ref, ccar, hcar, s_scr, y_scr, *, heads, chunk):
    c = pl.program_id(1)

    @pl.when(c == 0)
    def _():
        ccar[...] = jnp.zeros_like(ccar)
        hcar[...] = jnp.zeros_like(hcar)
        s_scr[...] = jnp.zeros_like(s_scr)

    xn = _rms(x_ref[0], g_ref[...]).astype(BF16)
    lru, proj = {}, {}
    lru_steps = _lru_prompt_steps(c, xn, wxy_ref, wga_ref, cw_ref, cb_ref, wr_ref, br_ref, wi_ref, bi_ref,
                                  lam_ref, wbl_ref, a_ref, ccar, hcar, lru)
    proj_steps = _ret_proj_steps(xn, (('q', wq_ref), ('k', wk_ref), ('v', wv_ref), ('g', wg_ref)), proj)
    n_lru = wr_ref.shape[0] + 1
    n_proj = sum(-(-r.shape[1] // 512) for r in (wq_ref, wk_ref, wv_ref, wg_ref))
    n_ret = heads * (x_ref.shape[1] // chunk)
    lru_done = 0
    for i in range(n_proj):
        next(proj_steps)
        want = ((i + 1) * n_lru) // (n_proj + n_ret)
        while lru_done < want:
            next(lru_steps)
            lru_done += 1
    for _ in proj_steps:
        pass
    ret_steps = _ret_prompt_steps(xn, proj, wgb_ref, cos_ref, sin_ref, wbr_ref, b_ref, s_scr, y_scr, heads, chunk)
    for i in range(n_ret):
        next(ret_steps)
        want = ((n_proj + i + 1) * n_lru) // (n_proj + n_ret)
        while lru_done < want:
            next(lru_steps)
            lru_done += 1
    for _ in itertools.chain(lru_steps, ret_steps):
        pass

    @pl.when(c == pl.num_programs(1) - 1)
    def _():
        conv_ref[0] = lru['tail']
        h_ref[0] = lru['h_last']
        s_out_ref[0] = s_scr[...]


def _memkv_kernel(m_ref, g_ref, w_ref, k_ref, v_ref):
    width = k_ref.shape[-1]
    mn = _rms(m_ref[0], g_ref[...]).astype(BF16)
    kv = _dot(mn, w_ref[...])
    k_ref[0] = kv[:, :width]
    v_ref[0] = kv[:, width:]


def _attn_merge_prompt_kernel(x_ref, ag_ref, bg_ref, mk_ref, mv_ref, g_ref, wmq_ref, wgc_ref, wbm_ref,
                              wout_ref, gffn_ref, wrt_ref, brt_ref, x1_ref, xn2_ref, route_ref, cnt_ref,
                              *, heads, n_groups, per_group):
    @pl.when((pl.program_id(0) == 0) & (pl.program_id(1) == 0))
    def _():
        cnt_ref[...] = jnp.zeros_like(cnt_ref)

    x = x_ref[0]
    xn = _rms(x, g_ref[...]).astype(BF16)
    mq = _dot(xn, wmq_ref[...]).astype(BF16)
    dh = mq.shape[1] // heads
    mk = mk_ref[0].astype(BF16)
    mv = mv_ref[0].astype(BF16)
    cs = []
    for h in range(heads):
        hs = slice(h * dh, (h + 1) * dh)
        lg = _dot_nt(mq[:, hs], mk[:, hs]) * (dh ** -0.5)
        p = jnp.exp(lg - jnp.max(lg, axis=-1, keepdims=True))
        den = jnp.sum(p, axis=-1, keepdims=True)
        cs.append(_dot(p.astype(BF16), mv[:, hs]) / den)
    c_out = _dot(jnp.concatenate(cs, axis=-1).astype(BF16), wbm_ref[...])
    gc = _dot(xn, wgc_ref[...])
    merged = _sigmoid(gc) * c_out + ag_ref[0].astype(F32) + bg_ref[0].astype(F32)
    x1, xn2, route, run = _merge_tail(x, merged, wout_ref, gffn_ref[...], wrt_ref, brt_ref[...], cnt_ref[...],
                                      n_groups, per_group)
    x1_ref[0] = x1
    xn2_ref[...] = xn2
    route_ref[0] = route
    cnt_ref[...] = run


def _proj_kernel(x_ref, g_ref, w_ref, z_ref):
    xn = _rms(x_ref[...], g_ref[...]).astype(BF16)
    z_ref[...] = _dot(xn, w_ref[...])


def _lru_sample_kernel(ux_ref, uy_ref, ga_ref, cprev_ref, hprev_ref, cw_ref, cb_ref, wr_ref, br_ref, wi_ref,
                       bi_ref, lam_ref, wbr_ref, out_ref, conv_ref, h_ref, *, at_start):
    kw = cw_ref.shape[0]
    ux = ux_ref[...]
    uc = cb_ref[...] + cw_ref[kw - 1:kw, :] * ux
    for j in range(kw - 1):
        uc = uc + cw_ref[j:j + 1, :] * cprev_ref[:, j, :]
    a, mult, gate = _lru_coeffs(uc, wr_ref, br_ref[...], wi_ref, bi_ref[...], lam_ref[...])
    if at_start:
        mult = jnp.ones_like(mult)
    h = a * hprev_ref[...] + mult * gate * uc
    h_ref[...] = h
    for j in range(kw - 2):
        conv_ref[:, j, :] = cprev_ref[:, j + 1, :]
    conv_ref[:, kw - 2, :] = ux
    a_out = _dot((h * _gelu_tanh(uy_ref[...])).astype(BF16), wbr_ref[...])
    out_ref[...] = _sigmoid(ga_ref[...]) * a_out


def _rot_sample_kernel(q_ref, k_ref, cos_ref, sin_ref, qo_ref, ko_ref, *, heads):
    dk = q_ref.shape[1] // heads
    cos, sin = cos_ref[...], sin_ref[...]
    for h in range(heads):
        hs = slice(h * dk, (h + 1) * dk)
        qo_ref[:, hs] = _rotary(q_ref[:, hs], cos, sin)
        ko_ref[:, hs] = _rotary(k_ref[:, hs], cos, sin) * (dk ** -0.5)


def _ret_sample_kernel(qt_ref, kt_ref, q_ref, k_ref, v_ref, s_ref, o_ref, s_out_ref, *, heads):
    bb = q_ref.shape[1]
    dk = s_ref.shape[2]
    dv = s_ref.shape[3]
    for j in range(bb):
        for h in range(heads):
            decay = 1.0 - 2.0 ** (-5.0 - h)
            qcol = qt_ref[0, h, :, j:j + 1]
            kcol = kt_ref[0, h, :, j:j + 1]
            qrow = q_ref[0, j:j + 1, h * dk:(h + 1) * dk]
            krow = k_ref[0, j:j + 1, h * dk:(h + 1) * dk]
            vrow = v_ref[0, j:j + 1, h * dv:(h + 1) * dv]
            s = s_ref[j, h]
            att = jnp.sum(qrow * krow, axis=-1, keepdims=True)
            o = att * vrow + jnp.sum((qcol * decay) * s, axis=0, keepdims=True)
            o_ref[0, j:j + 1, h * dv:(h + 1) * dv] = o
            s_out_ref[j, h] = s * decay + kcol * vrow


def _attn_sample_kernel(mq_ref, k_ref, v_ref, c_ref, *, heads):
    bb, _, dh = mq_ref.shape
    for j in range(bb):
        q = mq_ref[j]
        lg = jnp.sum(k_ref[j] * q[None], axis=-1, keepdims=True) * (dh ** -0.5)
        p = jnp.exp(lg - jnp.max(lg, axis=0, keepdims=True))
        den = jnp.sum(p, axis=0)
        c_ref[j] = jnp.sum(v_ref[j] * p, axis=0) / den


def _merge_sample_kernel(x_ref, ag_ref, o_ref, g_ref, c_ref, gb_ref, gc_ref, wbr_ref, wbm_ref, wout_ref,
                         gffn_ref, wrt_ref, brt_ref, run_ref, xn2_table_ref, x1_ref, xn2_ref, route_ref, cnt_ref,
                         *, heads, n_groups, per_group):
    del xn2_table_ref
    dv = o_ref.shape[1] // heads
    ys = []
    for h in range(heads):
        o = o_ref[:, h * dv:(h + 1) * dv]
        o = o * lax.rsqrt(jnp.mean(o * o, axis=-1, keepdims=True) + EPS)
        ys.append((_silu(g_ref[:, h * dv:(h + 1) * dv]) * o).astype(BF16))
    b_out = _dot(jnp.concatenate(ys, axis=-1), wbr_ref[...])
    c_out = _dot(c_ref[...].astype(BF16), wbm_ref[...])
    merged = ag_ref[...] + _sigmoid(gb_ref[...]) * b_out + _sigmoid(gc_ref[...]) * c_out
    x1, xn2, route, run = _merge_tail(x_ref[...], merged, wout_ref, gffn_ref[...], wrt_ref, brt_ref[...],
                                      run_ref[...], n_groups, per_group)
    x1_ref[...] = x1
    xn2_ref[...] = xn2
    route_ref[...] = route
    cnt_ref[...] = run


def _expert_kernel(te_ref, nu_ref, xa_ref, xb_ref, wg_ref, wu_ref, wd_ref, ys_ref, *, half):
    i = pl.program_id(0)

    @pl.when(i < nu_ref[0])
    def _():
        x = _unpack_pairs(jnp.where(i < half, xa_ref[...], xb_ref[...])).astype(BF16)
        hg = _dot(x, wg_ref[0].astype(BF16))
        hu = _dot(x, wu_ref[0].astype(BF16))
        hid = (_silu(hg) * hu).astype(BF16)
        ys_ref[...] = _pack_pairs(_dot(hid, wd_ref[0].astype(BF16)).astype(BF16))

    @pl.when(i >= nu_ref[0])
    def _():
        ys_ref[...] = jnp.zeros_like(ys_ref)


def _final_kernel(x1_ref, y1_ref, y2_ref, route_ref, g_ref, out_ref):
    w1 = route_ref[:, 2:3]
    w2 = route_ref[:, 3:4]
    x2 = x1_ref[...] + (w1 * _unpack_pairs(y1_ref[...]) + w2 * _unpack_pairs(y2_ref[...]))
    out_ref[...] = _rms(x2, g_ref[...])


def _tile(n, target):
    t = min(n, target)
    while n % t:
        t //= 2
    return t


def _row(v):
    return v.reshape(1, -1).astype(F32)


def _prep_layer(p, dims):
    d, w, hk, hv, hm = dims['d'], dims['w'], dims['hk'], dims['hv'], dims['hm']
    w_in = p['w_in']
    starts = {}
    o0 = 0
    for name, width in (('xy', 2 * w), ('q', hk), ('k', hk), ('v', hv), ('g', hv), ('mq', hm),
                        ('ga', d), ('gb', d), ('gc', d)):
        assert o0 % width == 0, "each window must start on a multiple of its width"
        starts[name] = (o0, width)
        o0 += width
    g, e = p['w_exp_router'].shape[0], p['w_exp_router'].shape[2]
    w_rt = jnp.concatenate([p['w_grp'], jnp.moveaxis(p['w_exp_router'], 0, 1).reshape(d, g * e)], axis=1)
    w_rt = jnp.pad(w_rt, ((0, 0), (0, LANES - w_rt.shape[1]))).astype(BF16)
    b_rt = jnp.pad(jnp.concatenate([p['b_grp'], p['b_exp_router'].reshape(-1)]), (0, LANES - g - g * e))
    return dict(
        w_in=w_in.astype(BF16), win=starts,
        norm_mix=_row(p['norm_mix']), norm_ffn=_row(p['norm_ffn']), norm_mem=_row(p['norm_mem']),
        w_mem_kv=p['w_mem_kv'].astype(BF16),
        conv_w=p['conv_w'].astype(F32), conv_b=_row(p['conv_b']),
        w_r=p['w_r'].astype(BF16), b_r=_row(p['b_r']), w_i=p['w_i'].astype(BF16), b_i=_row(p['b_i']),
        lam=_row(p['lru_lambda']),
        w_br_lru=p['w_branch_lru'].astype(BF16), w_br_ret=p['w_branch_ret'].astype(BF16),
        w_br_mem=p['w_branch_mem'].astype(BF16), w_out=p['w_out'].astype(BF16),
        w_rt=w_rt, b_rt=_row(b_rt),
        w_gate=p['w_gate'].reshape((g * e,) + p['w_gate'].shape[2:]),
        w_up=p['w_up'].reshape((g * e,) + p['w_up'].shape[2:]),
        w_down=p['w_down'].reshape((g * e,) + p['w_down'].shape[2:]),
    )


def _rope_tables(pos, dk):
    half = dk // 2
    inv = ROPE_BASE ** (-jnp.linspace(0.0, 1.0, half, dtype=F32))
    ang = pos.astype(F32)[:, None] * inv[None, :]
    return jnp.cos(ang), jnp.sin(ang)


def _prompt_mixer(x, mem, lw, dims, extra_rows):
    b, l, d = x.shape
    w, heads, dk, dv = dims['w'], dims['heads'], dims['dk'], dims['dv']
    kw = lw['conv_w'].shape[0]
    tl = _tile(l, 256)
    grid = (b, l // tl)
    xspec = pl.BlockSpec((1, tl, d), lambda i, c: (i, c, 0))
    sem = ("parallel", "arbitrary")
    w_in, win = lw['w_in'], lw['win']

    chunk = _tile(tl, 256)
    cos, sin = _rope_tables(jnp.arange(l, dtype=jnp.int32), dk)
    half = dk // 2
    a_g, conv_new, h_last, b_g, s_new = pl.pallas_call(
        functools.partial(_lru_ret_prompt_kernel, heads=heads, chunk=chunk),
        grid=grid,
        in_specs=[xspec, _const_spec((1, d)), _window_spec(d, win['xy']), _window_spec(d, win['ga']),
                  _const_spec((kw, w)), _const_spec((1, w)), _const_spec(lw['w_r'].shape), _const_spec((1, w)),
                  _const_spec(lw['w_i'].shape), _const_spec((1, w)), _const_spec((1, w)), _const_spec((w, d)),
                  _window_spec(d, win['q']), _window_spec(d, win['k']),
                  _window_spec(d, win['v']), _window_spec(d, win['g']), _window_spec(d, win['gb']),
                  pl.BlockSpec((tl, half), lambda i, c: (c, 0)), pl.BlockSpec((tl, half), lambda i, c: (c, 0)),
                  _const_spec((heads * dv, d))],
        out_specs=[pl.BlockSpec((1, tl, d), lambda i, c: (i, c, 0)),
                   pl.BlockSpec((1, kw - 1, w), lambda i, c: (i, 0, 0)),
                   pl.BlockSpec((1, 1, w), lambda i, c: (i, 0, 0)),
                   pl.BlockSpec((1, tl, d), lambda i, c: (i, c, 0)),
                   pl.BlockSpec((1, heads, dk, dv), lambda i, c: (i, 0, 0, 0))],
        out_shape=[jax.ShapeDtypeStruct((b, l, d), BF16), jax.ShapeDtypeStruct((b, kw - 1, w), F32),
                   jax.ShapeDtypeStruct((b, 1, w), F32),
                   jax.ShapeDtypeStruct((b, l, d), BF16), jax.ShapeDtypeStruct((b, heads, dk, dv), F32)],
        scratch_shapes=[pltpu.VMEM((SUBLANES, w), F32), pltpu.VMEM((1, w), F32),
                        pltpu.VMEM((heads, dk, dv), F32), pltpu.VMEM((tl, heads * dv), BF16)],
        compiler_params=_params(sem),
    )(x, lw['norm_mix'], w_in, w_in, lw['conv_w'], lw['conv_b'], lw['w_r'], lw['b_r'],
      lw['w_i'], lw['b_i'], lw['lam'], lw['w_br_lru'],
      w_in, w_in, w_in, w_in, w_in, cos, sin, lw['w_br_ret'])

    m = mem.shape[1]
    hm = dims['hm']
    mk, mv = pl.pallas_call(
        _memkv_kernel,
        grid=(b,),
        in_specs=[pl.BlockSpec((1, m, d), lambda i: (i, 0, 0)), _const_spec((1, d)), _const_spec((d, 2 * hm))],
        out_specs=[pl.BlockSpec((1, m, hm), lambda i: (i, 0, 0)), pl.BlockSpec((1, m, hm), lambda i: (i, 0, 0))],
        out_shape=[jax.ShapeDtypeStruct((b, m, hm), F32), jax.ShapeDtypeStruct((b, m, hm), F32)],
        compiler_params=_params(("parallel",)),
    )(mem, lw['norm_mem'], lw['w_mem_kv'])

    x1, xn2, route, counts = pl.pallas_call(
        functools.partial(_attn_merge_prompt_kernel, heads=dims['mheads'], n_groups=dims['g'],
                          per_group=dims['e']),
        grid=grid,
        in_specs=[xspec, xspec, xspec,
                  pl.BlockSpec((1, m, hm), lambda i, c: (i, 0, 0)), pl.BlockSpec((1, m, hm), lambda i, c: (i, 0, 0)),
                  _const_spec((1, d)), _window_spec(d, win['mq']), _window_spec(d, win['gc']), _const_spec((hm, d)),
                  _const_spec((d, d)), _const_spec((1, d)), _const_spec((d, LANES)), _const_spec((1, LANES))],
        out_specs=[xspec, pl.BlockSpec((tl, d // 2), lambda i, c: (i * (l // tl) + c, 0)),
                   pl.BlockSpec((1, tl, LANES), lambda i, c: (i, c, 0)),
                   pl.BlockSpec((1, LANES), lambda i, c: (0, 0))],
        out_shape=[jax.ShapeDtypeStruct((b, l, d), F32), jax.ShapeDtypeStruct((b * l + extra_rows, d // 2), F32),
                   jax.ShapeDtypeStruct((b, l, LANES), F32), jax.ShapeDtypeStruct((1, LANES), F32)],
        compiler_params=_params(("arbitrary", "arbitrary")),
    )(x, a_g, b_g, mk, mv, lw['norm_mix'], w_in, w_in, lw['w_br_mem'], lw['w_out'],
      lw['norm_ffn'], lw['w_rt'], lw['b_rt'])
    return x1, xn2, route, counts, conv_new, h_last[:, 0], s_new, mk, mv


def _sample_mixer(x, conv_prev, h_prev, s_prev, mem_k, mem_v, counts0, xn2_all, lw, dims):
    n, d = x.shape
    w, heads, dk, dv, hm = dims['w'], dims['heads'], dims['dk'], dims['dv'], dims['hm']
    hk, hv = heads * dk, heads * dv
    kw = lw['conv_w'].shape[0]
    n_in = lw['w_in'].shape[1]
    tn = _tile(n_in, 1024)
    z = pl.pallas_call(
        _proj_kernel,
        grid=(n_in // tn,),
        in_specs=[_const_spec((n, d)), _const_spec((1, d)), pl.BlockSpec((d, tn), lambda j: (0, j))],
        out_specs=pl.BlockSpec((n, tn), lambda j: (0, j)),
        out_shape=jax.ShapeDtypeStruct((n, n_in), F32),
        compiler_params=_params(("parallel",)),
    )(x, lw['norm_mix'], lw['w_in'])
    o0 = 0
    parts = []
    for sz in (w, w, hk, hk, hv, hv, hm, d, d, d):
        parts.append(z[:, o0:o0 + sz])
        o0 += sz
    ux, uy, q, k, v, g, mq, ga, gb, gc = parts

    a_g, conv_new, h_new = pl.pallas_call(
        functools.partial(_lru_sample_kernel, at_start=(PAST_LEN == 0)),
        out_shape=[jax.ShapeDtypeStruct((n, d), F32), jax.ShapeDtypeStruct((n, kw - 1, w), F32),
                   jax.ShapeDtypeStruct((n, w), F32)],
        compiler_params=pltpu.CompilerParams(vmem_limit_bytes=VMEM_LIMIT),
    )(ux, uy, ga, conv_prev, h_prev, lw['conv_w'], lw['conv_b'], lw['w_r'], lw['b_r'], lw['w_i'], lw['b_i'],
      lw['lam'], lw['w_br_lru'])

    cos, sin = _rope_tables(PAST_LEN + jnp.arange(1, dtype=jnp.int32), dk)
    qr, kr = pl.pallas_call(
        functools.partial(_rot_sample_kernel, heads=heads),
        out_shape=[jax.ShapeDtypeStruct((n, hk), F32), jax.ShapeDtypeStruct((n, hk), F32)],
    )(q, k, cos, sin)

    bb = _tile(n, 4)

    def cols(t):
        return t.reshape(n // bb, bb, heads, dk).transpose(0, 2, 3, 1)

    def rows(t):
        return t.reshape(n // bb, bb, t.shape[1])

    o, s_new = pl.pallas_call(
        functools.partial(_ret_sample_kernel, heads=heads),
        grid=(n // bb,),
        in_specs=[pl.BlockSpec((1, heads, dk, bb), lambda i: (i, 0, 0, 0)),
                  pl.BlockSpec((1, heads, dk, bb), lambda i: (i, 0, 0, 0)),
                  pl.BlockSpec((1, bb, hk), lambda i: (i, 0, 0)), pl.BlockSpec((1, bb, hk), lambda i: (i, 0, 0)),
                  pl.BlockSpec((1, bb, hv), lambda i: (i, 0, 0)),
                  pl.BlockSpec((bb, heads, dk, dv), lambda i: (i, 0, 0, 0))],
        out_specs=[pl.BlockSpec((1, bb, hv), lambda i: (i, 0, 0)),
                   pl.BlockSpec((bb, heads, dk, dv), lambda i: (i, 0, 0, 0))],
        out_shape=[jax.ShapeDtypeStruct((n // bb, bb, hv), F32), jax.ShapeDtypeStruct(s_prev.shape, F32)],
        compiler_params=_params(("parallel",)),
    )(cols(qr), cols(kr), rows(qr), rows(kr), rows(v), s_prev)
    o = o.reshape(n, hv)
    bb = _tile(n, SUBLANES)

    m = mem_k.shape[1]
    c = pl.pallas_call(
        functools.partial(_attn_sample_kernel, heads=dims['mheads']),
        grid=(n // bb,),
        in_specs=[pl.BlockSpec((bb,) + mem_k.shape[2:], lambda i: (i, 0, 0)),
                  pl.BlockSpec((bb,) + mem_k.shape[1:], lambda i: (i, 0, 0, 0)),
                  pl.BlockSpec((bb,) + mem_v.shape[1:], lambda i: (i, 0, 0, 0))],
        out_specs=pl.BlockSpec((bb,) + mem_k.shape[2:], lambda i: (i, 0, 0)),
        out_shape=jax.ShapeDtypeStruct((n,) + mem_k.shape[2:], F32),
        compiler_params=_params(("parallel",)),
    )(mq.reshape((n,) + mem_k.shape[2:]), mem_k, mem_v).reshape(n, hm)

    first = xn2_all.shape[0] - n
    assert first % n == 0
    args = (x, a_g, o, g, c, gb, gc, lw['w_br_ret'], lw['w_br_mem'], lw['w_out'], lw['norm_ffn'], lw['w_rt'],
            lw['b_rt'], counts0)
    x1, xn2_all, route, counts = pl.pallas_call(
        functools.partial(_merge_sample_kernel, heads=heads, n_groups=dims['g'], per_group=dims['e']),
        grid=(1,),
        in_specs=[_const_spec(a.shape) for a in args] + [pl.BlockSpec(memory_space=pl.ANY)],
        out_specs=[pl.BlockSpec((n, d), lambda i: (0, 0)), pl.BlockSpec((n, d // 2), lambda i: (first // n, 0)),
                   pl.BlockSpec((n, LANES), lambda i: (0, 0)), pl.BlockSpec((1, LANES), lambda i: (0, 0))],
        out_shape=[jax.ShapeDtypeStruct((n, d), F32), jax.ShapeDtypeStruct(xn2_all.shape, F32),
                   jax.ShapeDtypeStruct((n, LANES), F32), jax.ShapeDtypeStruct((1, LANES), F32)],
        input_output_aliases={len(args): 1},
        compiler_params=_params(("arbitrary",)),
    )(*args, xn2_all)
    return x1, xn2_all, route, counts, conv_new, h_new, s_new


def _moe(xn2, route_parts, counts, lw, dims, tm=256):
    t, d = xn2.shape[0], dims['d']
    g, n_exp = dims['g'], dims['g'] * dims['e']
    ff = lw['w_gate'].shape[2]
    counts = counts[0, g:g + n_exp].astype(jnp.int32)
    padded = ((counts + tm - 1) // tm) * tm
    ends = jnp.cumsum(padded)
    offs = ends - padded
    n_tiles = 2 * ((TOP_K * t + n_exp * tm + 2 * tm - 1) // (2 * tm))
    n_slots = n_tiles * tm
    half = n_tiles // 2
    experts = jnp.arange(n_exp, dtype=jnp.int32)[None, :]
    dests = []
    for j in range(TOP_K):
        per_part = []
        for r in route_parts:
            ids = r[:, j].astype(jnp.int32)
            base = jnp.sum(jnp.where(ids[:, None] == experts, offs[None, :], 0), axis=1)
            per_part.append(base + r[:, 4 + j].astype(jnp.int32))
        dests.append(per_part)
    dest = jnp.concatenate([dp for per_part in dests for dp in per_part], axis=0)
    tok = jnp.arange(t, dtype=jnp.int32)
    src = (jnp.arange(n_slots, dtype=jnp.int32) % t).at[dest].set(
        jnp.concatenate([tok] * TOP_K), mode='promise_in_bounds', unique_indices=True)
    n_used = (ends[-1] // tm).astype(jnp.int32)
    tile_row = jnp.arange(n_tiles, dtype=jnp.int32) * tm
    tile_e = jnp.sum((ends[None, :] <= tile_row[:, None]).astype(jnp.int32), axis=1)
    last_e = jnp.sum((ends <= (n_used - 1) * tm).astype(jnp.int32))
    tile_e = jnp.where(tile_row < ends[-1], tile_e, last_e)
    xs_a = xn2.at[src[:half * tm]].get(mode='promise_in_bounds')
    xs_b = xn2.at[src[half * tm:]].get(mode='promise_in_bounds')

    def map_a(i, te, nu):
        return (jnp.minimum(jnp.minimum(i, nu[0] - 1), half - 1), 0)

    def map_b(i, te, nu):
        return (jnp.clip(jnp.minimum(i, nu[0] - 1) - half, 0, half - 1), 0)

    ys = pl.pallas_call(
        functools.partial(_expert_kernel, half=half),
        grid_spec=pltpu.PrefetchScalarGridSpec(
            num_scalar_prefetch=2,
            grid=(n_tiles,),
            in_specs=[pl.BlockSpec((tm, d // 2), map_a), pl.BlockSpec((tm, d // 2), map_b),
                      pl.BlockSpec((1, d, ff), lambda i, te, nu: (te[i], 0, 0)),
                      pl.BlockSpec((1, d, ff), lambda i, te, nu: (te[i], 0, 0)),
                      pl.BlockSpec((1, ff, d), lambda i, te, nu: (te[i], 0, 0))],
            out_specs=pl.BlockSpec((tm, d // 2), lambda i, te, nu: (i, 0)),
        ),
        out_shape=jax.ShapeDtypeStruct((n_slots, d // 2), F32),
        compiler_params=_params(("arbitrary",)),
    )(tile_e, n_used.reshape(1), xs_a, xs_b, lw['w_gate'], lw['w_up'], lw['w_down'])
    return [tuple(ys.at[dests[j][p]].get(mode='promise_in_bounds') for j in range(TOP_K))
            for p in range(len(route_parts))]


def _final(x1, y_pair, route, g):
    t, d = x1.shape
    tl = _tile(t, 512)
    row = pl.BlockSpec((tl, d), lambda i: (i, 0))
    packed = pl.BlockSpec((tl, d // 2), lambda i: (i, 0))
    return pl.pallas_call(
        _final_kernel,
        grid=(t // tl,),
        in_specs=[row, packed, packed, pl.BlockSpec((tl, LANES), lambda i: (i, 0)), _const_spec((1, d))],
        out_specs=row,
        out_shape=jax.ShapeDtypeStruct((t, d), F32),
        compiler_params=_params(("parallel",)),
    )(x1, y_pair[0], y_pair[1], route, g)


def kernel(x_prompt, x_sample, mem_prompt, state_conv, state_lru, state_ret, cache_mem_k, cache_mem_v, norm_mix, norm_ffn, norm_mem, norm_final, w_in, w_mem_kv, conv_w, conv_b, w_r, b_r, w_i, b_i, lru_lambda, w_branch_lru, w_branch_ret, w_branch_mem, w_out, w_grp, b_grp, w_exp_router, b_exp_router, w_gate, w_up, w_down):
    depth = w_in.shape[0]
    assert depth == 1, "the two request groups are chained per layer only for a single-layer trunk"
    assert x_sample.shape[1] == 1
    bp, lp, d = x_prompt.shape
    ns = x_sample.shape[0]
    heads, dk, dv = state_ret.shape[2], state_ret.shape[3], state_ret.shape[4]
    mheads, mdh = cache_mem_k.shape[3], cache_mem_k.shape[4]
    dims = dict(d=d, w=state_lru.shape[2], heads=heads, dk=dk, dv=dv, hk=heads * dk, hv=heads * dv,
                mheads=mheads, hm=mheads * mdh, g=w_exp_router.shape[1], e=w_exp_router.shape[3])
    layer = 0
    p = dict(norm_mix=norm_mix[layer], norm_ffn=norm_ffn[layer], norm_mem=norm_mem[layer], w_in=w_in[layer],
             w_mem_kv=w_mem_kv[layer], conv_w=conv_w[layer], conv_b=conv_b[layer], w_r=w_r[layer], b_r=b_r[layer],
             w_i=w_i[layer], b_i=b_i[layer], lru_lambda=lru_lambda[layer], w_branch_lru=w_branch_lru[layer],
             w_branch_ret=w_branch_ret[layer], w_branch_mem=w_branch_mem[layer], w_out=w_out[layer],
             w_grp=w_grp[layer], b_grp=b_grp[layer], w_exp_router=w_exp_router[layer],
             b_exp_router=b_exp_router[layer], w_gate=w_gate[layer], w_up=w_up[layer], w_down=w_down[layer])
    lw = _prep_layer(p, dims)

    x1p, xn2, route_p, counts, conv_p, lru_p, ret_p, mk, mv = _prompt_mixer(x_prompt, mem_prompt, lw, dims, ns)
    x1s, xn2, route_s, counts, conv_s, lru_s, ret_s = _sample_mixer(
        x_sample[:, 0], state_conv[layer], state_lru[layer], state_ret[layer], cache_mem_k[layer],
        cache_mem_v[layer], counts, xn2, lw, dims)

    tp = bp * lp
    route_p = route_p.reshape(tp, LANES)
    yg_p, yg_s = _moe(xn2, [route_p, route_s], counts, lw, dims)
    yp = _final(x1p.reshape(tp, d), yg_p, route_p, _row(norm_final)).reshape(bp, lp, d)
    ys = _final(x1s, yg_s, route_s, _row(norm_final)).reshape(ns, 1, d)
    mshape = (1, bp, mem_prompt.shape[1], mheads, mdh)
    return (yp, ys, conv_p[None], lru_p[None], ret_p[None], mk.reshape(mshape), mv.reshape(mshape),
            conv_s[None], lru_s[None], ret_s[None])
```

```python
import functools
import itertools
import math

import jax
import jax.numpy as jnp
from jax import lax
from jax.experimental import pallas as pl
from jax.experimental.pallas import tpu as pltpu

F32 = jnp.float32
BF16 = jnp.bfloat16

EPS = 1e-6
LRU_C = 8.0
ROPE_BASE = 10000.0
PAST_LEN = 16384
TOP_K = 2
LANES = 128
SUBLANES = 8
VMEM_LIMIT = 56 * 1024 * 1024


def _dot(a, b):
    return jnp.dot(a, b, preferred_element_type=F32)


def _dot_nt(a, b):
    return lax.dot_general(a, b, (((1,), (1,)), ((), ())), preferred_element_type=F32)


def _dot_tn(a, b):
    return lax.dot_general(a, b, (((0,), (0,)), ((), ())), preferred_element_type=F32)


def _rms(x, g):
    return x * lax.rsqrt(jnp.mean(x * x, axis=-1, keepdims=True) + EPS) * g


def _sigmoid(x):
    return 0.5 * jnp.tanh(0.5 * x) + 0.5


def _silu(x):
    return x * _sigmoid(x)


def _gelu_tanh(x):
    return 0.5 * x * (1.0 + jnp.tanh(math.sqrt(2.0 / math.pi) * (x + 0.044715 * (x * x * x))))


def _softplus(x):
    return jnp.maximum(x, 0.0) + jnp.log1p(jnp.exp(-jnp.abs(x)))


def _const_spec(shape):
    nd = len(shape)
    return pl.BlockSpec(shape, lambda *_: (0,) * nd, pipeline_mode=pl.Buffered(1))


def _window_spec(rows, window):
    start, width = window
    return pl.BlockSpec((rows, width), lambda *_: (0, start // width), pipeline_mode=pl.Buffered(1))


def _params(sem):
    return pltpu.CompilerParams(dimension_semantics=sem, vmem_limit_bytes=VMEM_LIMIT)


def _lru_coeffs(uc, wr_ref, br, wi_ref, bi, lam):
    nb, bs = wr_ref.shape[0], wr_ref.shape[1]
    ucb = uc.astype(BF16)
    r_lin = jnp.concatenate([_dot(ucb[:, n * bs:(n + 1) * bs], wr_ref[n]) for n in range(nb)], axis=-1)
    i_lin = jnp.concatenate([_dot(ucb[:, n * bs:(n + 1) * bs], wi_ref[n]) for n in range(nb)], axis=-1)
    r = _sigmoid(r_lin + br)
    i = _sigmoid(i_lin + bi)
    log_a = (-LRU_C) * r * _softplus(-lam)
    a = jnp.exp(log_a)
    t = 1.0 - a * a
    mult = t * lax.rsqrt(jnp.maximum(t, 1e-37))
    return a, mult, i


def _scan_rows(a, b, h0):
    rows, width = a.shape
    grp = SUBLANES
    n_grp = rows // grp
    a = a.reshape(n_grp, grp, width)
    b = b.reshape(n_grp, grp, width)
    row = lax.broadcasted_iota(jnp.int32, a.shape, 1)
    d = 1
    while d < grp:
        a_sh = pltpu.roll(a, d, 1)
        b_sh = pltpu.roll(b, d, 1)
        keep = row >= d
        b = jnp.where(keep, b + a * b_sh, b)
        a = jnp.where(keep, a * a_sh, a)
        d *= 2
    out = []
    for g in range(n_grp):
        hg = a[g] * h0 + b[g]
        out.append(hg)
        h0 = hg[grp - 1:grp, :]
    return jnp.concatenate(out, axis=0)


def _rotary(t, cos, sin):
    half = t.shape[-1] // 2
    t1, t2 = t[:, :half], t[:, half:]
    return jnp.concatenate([t1 * cos - t2 * sin, t2 * cos + t1 * sin], axis=-1)


def _route(logits, run, n_groups, per_group):
    col = lax.broadcasted_iota(jnp.int32, logits.shape, 1)
    big = jnp.int32(1 << 20)
    neg = jnp.float32(-jnp.inf)
    gl = jnp.where(col < n_groups, logits, neg)
    gmax = jnp.max(gl, axis=-1, keepdims=True)
    g_idx = jnp.min(jnp.where(gl == gmax, col, big), axis=-1, keepdims=True)
    g_val = 1.0 / jnp.sum(jnp.exp(gl - gmax), axis=-1, keepdims=True)
    lo = n_groups + per_group * g_idx
    el = jnp.where((col >= lo) & (col < lo + per_group), logits, neg)
    m1 = jnp.max(el, axis=-1, keepdims=True)
    i1 = jnp.min(jnp.where(el == m1, col, big), axis=-1, keepdims=True)
    el2 = jnp.where(col == i1, neg, el)
    m2 = jnp.max(el2, axis=-1, keepdims=True)
    i2 = jnp.min(jnp.where(el2 == m2, col, big), axis=-1, keepdims=True)
    e2 = jnp.exp(m2 - m1)
    w1 = g_val / (1.0 + e2)
    w2 = g_val * e2 / (1.0 + e2)
    f1 = (i1 - n_groups).astype(F32)
    f2 = (i2 - n_groups).astype(F32)
    hit1, hit2 = col == i1, col == i2
    hits = jnp.where(hit1 | hit2, 1.0, 0.0)
    rows = logits.shape[0]
    earlier = lax.broadcasted_iota(jnp.int32, (rows, rows), 1) < lax.broadcasted_iota(jnp.int32, (rows, rows), 0)
    before = run + _dot(jnp.where(earlier, 1.0, 0.0).astype(BF16), hits.astype(BF16))
    r1 = jnp.sum(jnp.where(hit1, before, 0.0), axis=-1, keepdims=True)
    r2 = jnp.sum(jnp.where(hit2, before, 0.0), axis=-1, keepdims=True)
    route = f1
    for j, val in enumerate((f2, w1, w2, r1, r2), start=1):
        route = jnp.where(col == j, val, route)
    route = jnp.where(col > 5, 0.0, route)
    return route, run + jnp.sum(hits, axis=0, keepdims=True)


def _pack_pairs(x):
    half = x.shape[1] // 2
    lo = lax.bitcast_convert_type(x[:, :half].astype(F32), jnp.uint32)
    hi = lax.bitcast_convert_type(x[:, half:].astype(F32), jnp.uint32)
    return lax.bitcast_convert_type((lo >> 16) | (hi & jnp.uint32(0xFFFF0000)), F32)


def _unpack_pairs(w):
    u = lax.bitcast_convert_type(w, jnp.uint32)
    lo = lax.bitcast_convert_type(u << 16, F32)
    hi = lax.bitcast_convert_type(u & jnp.uint32(0xFFFF0000), F32)
    return jnp.concatenate([lo, hi], axis=1)


def _merge_tail(x, merged, wout_ref, gffn, wrt_ref, brt, run, n_groups, per_group):
    x1 = x + _dot(merged.astype(BF16), wout_ref[...])
    xn2 = _rms(x1, gffn).astype(BF16)
    logits = _dot(xn2, wrt_ref[...]) + brt
    route, run = _route(logits, run, n_groups, per_group)
    return x1, _pack_pairs(xn2), route, run


def _lru_prompt_steps(c, xn, wxy_ref, wga_ref, cw_ref, cb_ref, wr_ref, br_ref, wi_ref, bi_ref, lam_ref, wbr_ref,
                      out_ref, ccar, hcar, res):
    tl = xn.shape[0]
    width = hcar.shape[1]
    kw = cw_ref.shape[0]
    nb, bs = wr_ref.shape[0], wr_ref.shape[1]
    z = _dot(xn, wxy_ref[...])
    row = lax.broadcasted_iota(jnp.int32, (tl, bs), 0)
    hbs, lasts = [], []
    for n in range(nb):
        cs = slice(n * bs, (n + 1) * bs)
        ux = z[:, cs]
        acc = cw_ref[0:1, cs] * ux
        for j in range(1, kw):
            prev = jnp.where(row == 0, ccar[j - 1:j, cs], pltpu.roll(acc, 1, 0))
            ccar[j - 1:j, cs] = acc[tl - 1:tl, :]
            acc = prev + cw_ref[j:j + 1, cs] * ux
        uc = acc + cb_ref[:, cs]
        ucb = uc.astype(BF16)
        r = _sigmoid(_dot(ucb, wr_ref[n]) + br_ref[:, cs])
        gate = _sigmoid(_dot(ucb, wi_ref[n]) + bi_ref[:, cs])
        log_a = (-LRU_C) * r * _softplus(-lam_ref[:, cs])
        a = jnp.exp(log_a)
        t = 1.0 - a * a
        mult = t * lax.rsqrt(jnp.maximum(t, 1e-37))
        mult = jnp.where(row + c * tl == 0, 1.0, mult)
        h = _scan_rows(a, mult * gate * uc, hcar[:, cs])
        hcar[:, cs] = h[tl - 1:tl, :]
        lasts.append(h[tl - 1:tl, :])
        hbs.append((h * _gelu_tanh(z[:, width + n * bs: width + (n + 1) * bs])).astype(BF16))
        yield
    res['tail'] = z[tl - (kw - 1):tl, :width]
    res['h_last'] = jnp.concatenate(lasts, axis=1)
    a_out = _dot(jnp.concatenate(hbs, axis=1), wbr_ref[...])
    ga = _dot(xn, wga_ref[...])
    out_ref[0] = (_sigmoid(ga) * a_out).astype(out_ref.dtype)
    yield


def _ret_proj_steps(xn, w_refs, res, cols=512):
    for name, w_ref in w_refs:
        parts = []
        for j in range(0, w_ref.shape[1], cols):
            parts.append(_dot(xn, w_ref[:, j:j + cols]))
            yield
        res[name] = jnp.concatenate(parts, axis=1)


def _ret_prompt_steps(xn, proj, wgb_ref, cos_ref, sin_ref, wbr_ref, out_ref, s_scr, y_scr, heads, chunk):
    tl = xn.shape[0]
    q, k, v, g = proj['q'], proj['k'], proj['v'], proj['g']
    dk = q.shape[1] // heads
    dv = v.shape[1] // heads
    cos, sin = cos_ref[...], sin_ref[...]
    n_i = lax.broadcasted_iota(jnp.int32, (chunk, chunk), 0)
    m_i = lax.broadcasted_iota(jnp.int32, (chunk, chunk), 1)
    diff = (n_i - m_i).astype(F32)
    rowk = lax.broadcasted_iota(jnp.int32, (chunk, dk), 0).astype(F32)
    for h in range(heads):
        log_g = math.log1p(-(2.0 ** (-5.0 - h)))
        dmask = jnp.where(diff >= 0, jnp.exp(jnp.maximum(diff, 0.0) * log_g), 0.0)
        q_decay = jnp.exp((rowk + 1.0) * log_g)
        k_decay = jnp.exp((chunk - 1.0 - rowk) * log_g)
        s_decay = math.exp(chunk * log_g)
        for sub in range(tl // chunk):
            r0 = sub * chunk
            cs, sn = cos[r0:r0 + chunk], sin[r0:r0 + chunk]
            qr = _rotary(q[r0:r0 + chunk, h * dk:(h + 1) * dk], cs, sn)
            kr = _rotary(k[r0:r0 + chunk, h * dk:(h + 1) * dk], cs, sn) * (dk ** -0.5)
            vh = v[r0:r0 + chunk, h * dv:(h + 1) * dv].astype(BF16)
            s = s_scr[h]
            att = _dot_nt(qr.astype(BF16), kr.astype(BF16)) * dmask
            o = _dot(att.astype(BF16), vh) + _dot((qr * q_decay).astype(BF16), s.astype(BF16))
            s_scr[h] = s * s_decay + _dot_tn((kr * k_decay).astype(BF16), vh)
            o = o * lax.rsqrt(jnp.mean(o * o, axis=-1, keepdims=True) + EPS)
            gh = g[r0:r0 + chunk, h * dv:(h + 1) * dv]
            y_scr[r0:r0 + chunk, h * dv:(h + 1) * dv] = (_silu(gh) * o).astype(BF16)
            yield
    b_out = _dot(y_scr[...], wbr_ref[...])
    gb = _dot(xn, wgb_ref[...])
    out_ref[0] = (_sigmoid(gb) * b_out).astype(out_ref.dtype)


def _lru_ret_prompt_kernel(x_ref, g_ref, wxy_ref, wga_ref, cw_ref, cb_ref, wr_ref, br_ref, wi_ref, bi_ref,
                           lam_ref, wbl_ref, wq_ref, wk_ref, wv_ref, wg_ref, wgb_ref, cos_ref, sin_ref, wbr_ref,
                           a_ref, conv_ref, h_ref, b_ref, s_out_ref, ccar, hcar, s_scr, y_scr, *, heads, chunk):
    c = pl.program_id(1)

    @pl.when(c == 0)
    def _():
        ccar[...] = jnp.zeros_like(ccar)
        hcar[...] = jnp.zeros_like(hcar)
        s_scr[...] = jnp.zeros_like(s_scr)

    xn = _rms(x_ref[0], g_ref[...]).astype(BF16)
    lru, proj = {}, {}
    lru_steps = _lru_prompt_steps(c, xn, wxy_ref, wga_ref, cw_ref, cb_ref, wr_ref, br_ref, wi_ref, bi_ref,
                                  lam_ref, wbl_ref, a_ref, ccar, hcar, lru)
    proj_steps = _ret_proj_steps(xn, (('q', wq_ref), ('k', wk_ref), ('v', wv_ref), ('g', wg_ref)), proj)
    n_lru = wr_ref.shape[0] + 1
    n_proj = sum(-(-r.shape[1] // 512) for r in (wq_ref, wk_ref, wv_ref, wg_ref))
    n_ret = heads * (x_ref.shape[1] // chunk)
    lru_done = 0
    for i in range(n_proj):
        next(proj_steps)
        want = ((i + 1) * n_lru) // (n_proj + n_ret)
        while lru_done < want:
            next(lru_steps)
            lru_done += 1
    for _ in proj_steps:
        pass
    ret_steps = _ret_prompt_steps(xn, proj, wgb_ref, cos_ref, sin_ref, wbr_ref, b_ref, s_scr, y_scr, heads, chunk)
    for i in range(n_ret):
        next(ret_steps)
        want = ((n_proj + i + 1) * n_lru) // (n_proj + n_ret)
        while lru_done < want:
            next(lru_steps)
            lru_done += 1
    for _ in itertools.chain(lru_steps, ret_steps):
        pass

    @pl.when(c == pl.num_programs(1) - 1)
    def _():
        conv_ref[0] = lru['tail']
        h_ref[0] = lru['h_last']
        s_out_ref[0] = s_scr[...]


def _memkv_kernel(m_ref, g_ref, w_ref, k_ref, v_ref, k4_ref, v4_ref):
    width = k_ref.shape[-1]
    heads, dh = k4_ref.shape[2], k4_ref.shape[3]
    mn = _rms(m_ref[0], g_ref[...]).astype(BF16)
    kv = _dot(mn, w_ref[...])
    k_ref[0] = kv[:, :width]
    v_ref[0] = kv[:, width:]
    for h in range(heads):
        k4_ref[0, :, h, :] = kv[:, h * dh:(h + 1) * dh]
        v4_ref[0, :, h, :] = kv[:, width + h * dh: width + (h + 1) * dh]


def _attn_merge_prompt_kernel(x_ref, ag_ref, bg_ref, mk_ref, mv_ref, g_ref, wmq_ref, wgc_ref, wbm_ref,
                              wout_ref, gffn_ref, wrt_ref, brt_ref, x1_ref, xn2_ref, route_ref, cnt_ref,
                              *, heads, n_groups, per_group):
    @pl.when((pl.program_id(0) == 0) & (pl.program_id(1) == 0))
    def _():
        cnt_ref[...] = jnp.zeros_like(cnt_ref)

    x = x_ref[0]
    xn = _rms(x, g_ref[...]).astype(BF16)
    mq = _dot(xn, wmq_ref[...]).astype(BF16)
    dh = mq.shape[1] // heads
    mk = mk_ref[0].astype(BF16)
    mv = mv_ref[0].astype(BF16)
    cs = []
    for h in range(heads):
        hs = slice(h * dh, (h + 1) * dh)
        lg = _dot_nt(mq[:, hs], mk[:, hs]) * (dh ** -0.5)
        p = jnp.exp(lg - jnp.max(lg, axis=-1, keepdims=True))
        den = jnp.sum(p, axis=-1, keepdims=True)
        cs.append(_dot(p.astype(BF16), mv[:, hs]) / den)
    c_out = _dot(jnp.concatenate(cs, axis=-1).astype(BF16), wbm_ref[...])
    gc = _dot(xn, wgc_ref[...])
    merged = _sigmoid(gc) * c_out + ag_ref[0].astype(F32) + bg_ref[0].astype(F32)
    x1, xn2, route, run = _merge_tail(x, merged, wout_ref, gffn_ref[...], wrt_ref, brt_ref[...], cnt_ref[...],
                                      n_groups, per_group)
    x1_ref[0] = x1
    xn2_ref[...] = xn2
    route_ref[0] = route
    cnt_ref[...] = run


def _proj_kernel(x_ref, g_ref, w_ref, z_ref):
    xn = _rms(x_ref[...], g_ref[...]).astype(BF16)
    z_ref[...] = _dot(xn, w_ref[...])


def _lru_sample_kernel(ux_ref, uy_ref, ga_ref, cprev_ref, hprev_ref, cw_ref, cb_ref, wr_ref, br_ref, wi_ref,
                       bi_ref, lam_ref, wbr_ref, out_ref, conv_ref, h_ref, *, at_start):
    kw = cw_ref.shape[0]
    ux = ux_ref[...]
    uc = cb_ref[...] + cw_ref[kw - 1:kw, :] * ux
    for j in range(kw - 1):
        uc = uc + cw_ref[j:j + 1, :] * cprev_ref[:, j, :]
    a, mult, gate = _lru_coeffs(uc, wr_ref, br_ref[...], wi_ref, bi_ref[...], lam_ref[...])
    if at_start:
        mult = jnp.ones_like(mult)
    h = a * hprev_ref[...] + mult * gate * uc
    h_ref[...] = h
    for j in range(kw - 2):
        conv_ref[:, j, :] = cprev_ref[:, j + 1, :]
    conv_ref[:, kw - 2, :] = ux
    a_out = _dot((h * _gelu_tanh(uy_ref[...])).astype(BF16), wbr_ref[...])
    out_ref[...] = _sigmoid(ga_ref[...]) * a_out


def _rot_sample_kernel(q_ref, k_ref, cos_ref, sin_ref, qo_ref, ko_ref, *, heads):
    dk = q_ref.shape[1] // heads
    cos, sin = cos_ref[...], sin_ref[...]
    for h in range(heads):
        hs = slice(h * dk, (h + 1) * dk)
        qo_ref[:, hs] = _rotary(q_ref[:, hs], cos, sin)
        ko_ref[:, hs] = _rotary(k_ref[:, hs], cos, sin) * (dk ** -0.5)


def _ret_sample_kernel(qt_ref, kt_ref, q_ref, k_ref, v_ref, s_ref, o_ref, s_out_ref, *, heads):
    bb = q_ref.shape[1]
    dk = s_ref.shape[2]
    dv = s_ref.shape[3]
    for j in range(bb):
        for h in range(heads):
            decay = 1.0 - 2.0 ** (-5.0 - h)
            qcol = qt_ref[0, h, :, j:j + 1]
            kcol = kt_ref[0, h, :, j:j + 1]
            qrow = q_ref[0, j:j + 1, h * dk:(h + 1) * dk]
            krow = k_ref[0, j:j + 1, h * dk:(h + 1) * dk]
            vrow = v_ref[0, j:j + 1, h * dv:(h + 1) * dv]
            s = s_ref[j, h]
            att = jnp.sum(qrow * krow, axis=-1, keepdims=True)
            o = att * vrow + jnp.sum((qcol * decay) * s, axis=0, keepdims=True)
            o_ref[0, j:j + 1, h * dv:(h + 1) * dv] = o
            s_out_ref[j, h] = s * decay + kcol * vrow


def _attn_sample_kernel(mq_ref, k_ref, v_ref, c_ref, *, heads):
    bb, _, dh = mq_ref.shape
    for j in range(bb):
        q = mq_ref[j]
        lg = jnp.sum(k_ref[j] * q[None], axis=-1, keepdims=True) * (dh ** -0.5)
        p = jnp.exp(lg - jnp.max(lg, axis=0, keepdims=True))
        den = jnp.sum(p, axis=0)
        c_ref[j] = jnp.sum(v_ref[j] * p, axis=0) / den


def _merge_sample_kernel(x_ref, ag_ref, o_ref, g_ref, c_ref, gb_ref, gc_ref, wbr_ref, wbm_ref, wout_ref,
                         gffn_ref, wrt_ref, brt_ref, run_ref, xn2_table_ref, x1_ref, xn2_ref, route_ref, cnt_ref,
                         *, heads, n_groups, per_group):
    del xn2_table_ref
    dv = o_ref.shape[1] // heads
    ys = []
    for h in range(heads):
        o = o_ref[:, h * dv:(h + 1) * dv]
        o = o * lax.rsqrt(jnp.mean(o * o, axis=-1, keepdims=True) + EPS)
        ys.append((_silu(g_ref[:, h * dv:(h + 1) * dv]) * o).astype(BF16))
    b_out = _dot(jnp.concatenate(ys, axis=-1), wbr_ref[...])
    c_out = _dot(c_ref[...].astype(BF16), wbm_ref[...])
    merged = ag_ref[...] + _sigmoid(gb_ref[...]) * b_out + _sigmoid(gc_ref[...]) * c_out
    x1, xn2, route, run = _merge_tail(x_ref[...], merged, wout_ref, gffn_ref[...], wrt_ref, brt_ref[...],
                                      run_ref[...], n_groups, per_group)
    x1_ref[...] = x1
    xn2_ref[...] = xn2
    route_ref[...] = route
    cnt_ref[...] = run


def _expert_kernel(te_ref, nu_ref, xa_ref, xb_ref, wg_ref, wu_ref, wd_ref, ys_ref, *, half):
    i = pl.program_id(0)

    @pl.when(i < nu_ref[0])
    def _():
        x = _unpack_pairs(jnp.where(i < half, xa_ref[...], xb_ref[...])).astype(BF16)
        hg = _dot(x, wg_ref[0].astype(BF16))
        hu = _dot(x, wu_ref[0].astype(BF16))
        hid = (_silu(hg) * hu).astype(BF16)
        ys_ref[...] = _pack_pairs(_dot(hid, wd_ref[0].astype(BF16)).astype(BF16))

    @pl.when(i >= nu_ref[0])
    def _():
        ys_ref[...] = jnp.zeros_like(ys_ref)


def _final_kernel(x1_ref, y1_ref, y2_ref, route_ref, g_ref, out_ref):
    w1 = route_ref[:, 2:3]
    w2 = route_ref[:, 3:4]
    x2 = x1_ref[...] + (w1 * _unpack_pairs(y1_ref[...]) + w2 * _unpack_pairs(y2_ref[...]))
    out_ref[...] = _rms(x2, g_ref[...])


def _tile(n, target):
    t = min(n, target)
    while n % t:
        t //= 2
    return t


def _row(v):
    return v.reshape(1, -1).astype(F32)


def _prep_layer(p, dims):
    d, w, hk, hv, hm = dims['d'], dims['w'], dims['hk'], dims['hv'], dims['hm']
    w_in = p['w_in']
    starts = {}
    o0 = 0
    for name, width in (('xy', 2 * w), ('q', hk), ('k', hk), ('v', hv), ('g', hv), ('mq', hm),
                        ('ga', d), ('gb', d), ('gc', d)):
        assert o0 % width == 0, "each window must start on a multiple of its width"
        starts[name] = (o0, width)
        o0 += width
    g, e = p['w_exp_router'].shape[0], p['w_exp_router'].shape[2]
    w_rt = jnp.concatenate([p['w_grp'], jnp.moveaxis(p['w_exp_router'], 0, 1).reshape(d, g * e)], axis=1)
    w_rt = jnp.pad(w_rt, ((0, 0), (0, LANES - w_rt.shape[1]))).astype(BF16)
    b_rt = jnp.pad(jnp.concatenate([p['b_grp'], p['b_exp_router'].reshape(-1)]), (0, LANES - g - g * e))
    return dict(
        w_in=w_in.astype(BF16), win=starts,
        norm_mix=_row(p['norm_mix']), norm_ffn=_row(p['norm_ffn']), norm_mem=_row(p['norm_mem']),
        w_mem_kv=p['w_mem_kv'].astype(BF16),
        conv_w=p['conv_w'].astype(F32), conv_b=_row(p['conv_b']),
        w_r=p['w_r'].astype(BF16), b_r=_row(p['b_r']), w_i=p['w_i'].astype(BF16), b_i=_row(p['b_i']),
        lam=_row(p['lru_lambda']),
        w_br_lru=p['w_branch_lru'].astype(BF16), w_br_ret=p['w_branch_ret'].astype(BF16),
        w_br_mem=p['w_branch_mem'].astype(BF16), w_out=p['w_out'].astype(BF16),
        w_rt=w_rt, b_rt=_row(b_rt),
        w_gate=p['w_gate'].reshape((g * e,) + p['w_gate'].shape[2:]),
        w_up=p['w_up'].reshape((g * e,) + p['w_up'].shape[2:]),
        w_down=p['w_down'].reshape((g * e,) + p['w_down'].shape[2:]),
    )


def _rope_tables(pos, dk):
    half = dk // 2
    inv = ROPE_BASE ** (-jnp.linspace(0.0, 1.0, half, dtype=F32))
    ang = pos.astype(F32)[:, None] * inv[None, :]
    return jnp.cos(ang), jnp.sin(ang)


def _prompt_mixer(x, mem, lw, dims, extra_rows):
    b, l, d = x.shape
    w, heads, dk, dv = dims['w'], dims['heads'], dims['dk'], dims['dv']
    kw = lw['conv_w'].shape[0]
    tl = _tile(l, 256)
    grid = (b, l // tl)
    xspec = pl.BlockSpec((1, tl, d), lambda i, c: (i, c, 0))
    sem = ("parallel", "arbitrary")
    w_in, win = lw['w_in'], lw['win']

    chunk = _tile(tl, 256)
    cos, sin = _rope_tables(jnp.arange(l, dtype=jnp.int32), dk)
    half = dk // 2
    a_g, conv_new, h_last, b_g, s_new = pl.pallas_call(
        functools.partial(_lru_ret_prompt_kernel, heads=heads, chunk=chunk),
        grid=grid,
        in_specs=[xspec, _const_spec((1, d)), _window_spec(d, win['xy']), _window_spec(d, win['ga']),
                  _const_spec((kw, w)), _const_spec((1, w)), _const_spec(lw['w_r'].shape), _const_spec((1, w)),
                  _const_spec(lw['w_i'].shape), _const_spec((1, w)), _const_spec((1, w)), _const_spec((w, d)),
                  _window_spec(d, win['q']), _window_spec(d, win['k']),
                  _window_spec(d, win['v']), _window_spec(d, win['g']), _window_spec(d, win['gb']),
                  pl.BlockSpec((tl, half), lambda i, c: (c, 0)), pl.BlockSpec((tl, half), lambda i, c: (c, 0)),
                  _const_spec((heads * dv, d))],
        out_specs=[pl.BlockSpec((1, tl, d), lambda i, c: (i, c, 0)),
                   pl.BlockSpec((1, kw - 1, w), lambda i, c: (i, 0, 0)),
                   pl.BlockSpec((1, 1, w), lambda i, c: (i, 0, 0)),
                   pl.BlockSpec((1, tl, d), lambda i, c: (i, c, 0)),
                   pl.BlockSpec((1, heads, dk, dv), lambda i, c: (i, 0, 0, 0))],
        out_shape=[jax.ShapeDtypeStruct((b, l, d), BF16), jax.ShapeDtypeStruct((b, kw - 1, w), F32),
                   jax.ShapeDtypeStruct((b, 1, w), F32),
                   jax.ShapeDtypeStruct((b, l, d), BF16), jax.ShapeDtypeStruct((b, heads, dk, dv), F32)],
        scratch_shapes=[pltpu.VMEM((SUBLANES, w), F32), pltpu.VMEM((1, w), F32),
                        pltpu.VMEM((heads, dk, dv), F32), pltpu.VMEM((tl, heads * dv), BF16)],
        compiler_params=_params(sem),
    )(x, lw['norm_mix'], w_in, w_in, lw['conv_w'], lw['conv_b'], lw['w_r'], lw['b_r'],
      lw['w_i'], lw['b_i'], lw['lam'], lw['w_br_lru'],
      w_in, w_in, w_in, w_in, w_in, cos, sin, lw['w_br_ret'])

    m = mem.shape[1]
    hm = dims['hm']
    mh = dims['mheads']
    flat = pl.BlockSpec((1, m, hm), lambda i: (i, 0, 0))
    per_head = pl.BlockSpec((1, m, mh, hm // mh), lambda i: (i, 0, 0, 0))
    mk, mv, mk4, mv4 = pl.pallas_call(
        _memkv_kernel,
        grid=(b,),
        in_specs=[pl.BlockSpec((1, m, d), lambda i: (i, 0, 0)), _const_spec((1, d)), _const_spec((d, 2 * hm))],
        out_specs=[flat, flat, per_head, per_head],
        out_shape=[jax.ShapeDtypeStruct((b, m, hm), F32), jax.ShapeDtypeStruct((b, m, hm), F32),
                   jax.ShapeDtypeStruct((b, m, mh, hm // mh), F32), jax.ShapeDtypeStruct((b, m, mh, hm // mh), F32)],
        compiler_params=_params(("parallel",)),
    )(mem, lw['norm_mem'], lw['w_mem_kv'])

    tm = _tile(l, 512)
    mspec = pl.BlockSpec((1, tm, d), lambda i, c: (i, c, 0))
    x1, xn2, route, counts = pl.pallas_call(
        functools.partial(_attn_merge_prompt_kernel, heads=dims['mheads'], n_groups=dims['g'],
                          per_group=dims['e']),
        grid=(b, l // tm),
        in_specs=[mspec, mspec, mspec,
                  pl.BlockSpec((1, m, hm), lambda i, c: (i, 0, 0)), pl.BlockSpec((1, m, hm), lambda i, c: (i, 0, 0)),
                  _const_spec((1, d)), _window_spec(d, win['mq']), _window_spec(d, win['gc']), _const_spec((hm, d)),
                  _const_spec((d, d)), _const_spec((1, d)), _const_spec((d, LANES)), _const_spec((1, LANES))],
        out_specs=[mspec, pl.BlockSpec((tm, d // 2), lambda i, c: (i * (l // tm) + c, 0)),
                   pl.BlockSpec((1, tm, LANES), lambda i, c: (i, c, 0)),
                   pl.BlockSpec((1, LANES), lambda i, c: (0, 0))],
        out_shape=[jax.ShapeDtypeStruct((b, l, d), F32), jax.ShapeDtypeStruct((b * l + extra_rows, d // 2), F32),
                   jax.ShapeDtypeStruct((b, l, LANES), F32), jax.ShapeDtypeStruct((1, LANES), F32)],
        compiler_params=_params(("arbitrary", "arbitrary")),
    )(x, a_g, b_g, mk, mv, lw['norm_mix'], w_in, w_in, lw['w_br_mem'], lw['w_out'],
      lw['norm_ffn'], lw['w_rt'], lw['b_rt'])
    return x1, xn2, route, counts, conv_new, h_last[:, 0], s_new, mk4, mv4


def _sample_mixer(x, conv_prev, h_prev, s_prev, mem_k, mem_v, counts0, xn2_all, lw, dims):
    n, d = x.shape
    w, heads, dk, dv, hm = dims['w'], dims['heads'], dims['dk'], dims['dv'], dims['hm']
    hk, hv = heads * dk, heads * dv
    kw = lw['conv_w'].shape[0]
    n_in = lw['w_in'].shape[1]
    tn = _tile(n_in, 1024)
    z = pl.pallas_call(
        _proj_kernel,
        grid=(n_in // tn,),
        in_specs=[_const_spec((n, d)), _const_spec((1, d)), pl.BlockSpec((d, tn), lambda j: (0, j))],
        out_specs=pl.BlockSpec((n, tn), lambda j: (0, j)),
        out_shape=jax.ShapeDtypeStruct((n, n_in), F32),
        compiler_params=_params(("parallel",)),
    )(x, lw['norm_mix'], lw['w_in'])
    o0 = 0
    parts = []
    for sz in (w, w, hk, hk, hv, hv, hm, d, d, d):
        parts.append(z[:, o0:o0 + sz])
        o0 += sz
    ux, uy, q, k, v, g, mq, ga, gb, gc = parts

    a_g, conv_new, h_new = pl.pallas_call(
        functools.partial(_lru_sample_kernel, at_start=(PAST_LEN == 0)),
        out_shape=[jax.ShapeDtypeStruct((n, d), F32), jax.ShapeDtypeStruct((n, kw - 1, w), F32),
                   jax.ShapeDtypeStruct((n, w), F32)],
        compiler_params=pltpu.CompilerParams(vmem_limit_bytes=VMEM_LIMIT),
    )(ux, uy, ga, conv_prev, h_prev, lw['conv_w'], lw['conv_b'], lw['w_r'], lw['b_r'], lw['w_i'], lw['b_i'],
      lw['lam'], lw['w_br_lru'])

    cos, sin = _rope_tables(PAST_LEN + jnp.arange(1, dtype=jnp.int32), dk)
    qr, kr = pl.pallas_call(
        functools.partial(_rot_sample_kernel, heads=heads),
        out_shape=[jax.ShapeDtypeStruct((n, hk), F32), jax.ShapeDtypeStruct((n, hk), F32)],
    )(q, k, cos, sin)

    bb = _tile(n, 4)

    def cols(t):
        return t.reshape(n // bb, bb, heads, dk).transpose(0, 2, 3, 1)

    def rows(t):
        return t.reshape(n // bb, bb, t.shape[1])

    o, s_new = pl.pallas_call(
        functools.partial(_ret_sample_kernel, heads=heads),
        grid=(n // bb,),
        in_specs=[pl.BlockSpec((1, heads, dk, bb), lambda i: (i, 0, 0, 0)),
                  pl.BlockSpec((1, heads, dk, bb), lambda i: (i, 0, 0, 0)),
                  pl.BlockSpec((1, bb, hk), lambda i: (i, 0, 0)), pl.BlockSpec((1, bb, hk), lambda i: (i, 0, 0)),
                  pl.BlockSpec((1, bb, hv), lambda i: (i, 0, 0)),
                  pl.BlockSpec((bb, heads, dk, dv), lambda i: (i, 0, 0, 0))],
        out_specs=[pl.BlockSpec((1, bb, hv), lambda i: (i, 0, 0)),
                   pl.BlockSpec((bb, heads, dk, dv), lambda i: (i, 0, 0, 0))],
        out_shape=[jax.ShapeDtypeStruct((n // bb, bb, hv), F32), jax.ShapeDtypeStruct(s_prev.shape, F32)],
        compiler_params=_params(("parallel",)),
    )(cols(qr), cols(kr), rows(qr), rows(kr), rows(v), s_prev)
    o = o.reshape(n, hv)
    bb = _tile(n, SUBLANES)

    m = mem_k.shape[1]
    c = pl.pallas_call(
        functools.partial(_attn_sample_kernel, heads=dims['mheads']),
        grid=(n // bb,),
        in_specs=[pl.BlockSpec((bb,) + mem_k.shape[2:], lambda i: (i, 0, 0)),
                  pl.BlockSpec((bb,) + mem_k.shape[1:], lambda i: (i, 0, 0, 0)),
                  pl.BlockSpec((bb,) + mem_v.shape[1:], lambda i: (i, 0, 0, 0))],
        out_specs=pl.BlockSpec((bb,) + mem_k.shape[2:], lambda i: (i, 0, 0)),
        out_shape=jax.ShapeDtypeStruct((n,) + mem_k.shape[2:], F32),
        compiler_params=_params(("parallel",)),
    )(mq.reshape((n,) + mem_k.shape[2:]), mem_k, mem_v).reshape(n, hm)

    first = xn2_all.shape[0] - n
    assert first % n == 0
    args = (x, a_g, o, g, c, gb, gc, lw['w_br_ret'], lw['w_br_mem'], lw['w_out'], lw['norm_ffn'], lw['w_rt'],
            lw['b_rt'], counts0)
    x1, xn2_all, route, counts = pl.pallas_call(
        functools.partial(_merge_sample_kernel, heads=heads, n_groups=dims['g'], per_group=dims['e']),
        grid=(1,),
        in_specs=[_const_spec(a.shape) for a in args] + [pl.BlockSpec(memory_space=pl.ANY)],
        out_specs=[pl.BlockSpec((n, d), lambda i: (0, 0)), pl.BlockSpec((n, d // 2), lambda i: (first // n, 0)),
                   pl.BlockSpec((n, LANES), lambda i: (0, 0)), pl.BlockSpec((1, LANES), lambda i: (0, 0))],
        out_shape=[jax.ShapeDtypeStruct((n, d), F32), jax.ShapeDtypeStruct(xn2_all.shape, F32),
                   jax.ShapeDtypeStruct((n, LANES), F32), jax.ShapeDtypeStruct((1, LANES), F32)],
        input_output_aliases={len(args): 1},
        compiler_params=_params(("arbitrary",)),
    )(*args, xn2_all)
    return x1, xn2_all, route, counts, conv_new, h_new, s_new


def _moe(xn2, route_parts, counts, lw, dims, tm=512):
    t, d = xn2.shape[0], dims['d']
    g, n_exp = dims['g'], dims['g'] * dims['e']
    ff = lw['w_gate'].shape[2]
    counts = counts[0, g:g + n_exp].astype(jnp.int32)
    padded = ((counts + tm - 1) // tm) * tm
    ends = jnp.cumsum(padded)
    offs = ends - padded
    n_tiles = 2 * ((TOP_K * t + n_exp * tm + 2 * tm - 1) // (2 * tm))
    n_slots = n_tiles * tm
    half = n_tiles // 2
    experts = jnp.arange(n_exp, dtype=jnp.int32)[None, :]
    dests = []
    for j in range(TOP_K):
        per_part = []
        for r in route_parts:
            ids = r[:, j].astype(jnp.int32)
            base = jnp.sum(jnp.where(ids[:, None] == experts, offs[None, :], 0), axis=1)
            per_part.append(base + r[:, 4 + j].astype(jnp.int32))
        dests.append(per_part)
    dest = jnp.concatenate([dp for per_part in dests for dp in per_part], axis=0)
    tok = jnp.arange(t, dtype=jnp.int32)
    src = (jnp.arange(n_slots, dtype=jnp.int32) % t).at[dest].set(
        jnp.concatenate([tok] * TOP_K), mode='promise_in_bounds', unique_indices=True)
    n_used = (ends[-1] // tm).astype(jnp.int32)
    tile_row = jnp.arange(n_tiles, dtype=jnp.int32) * tm
    tile_e = jnp.sum((ends[None, :] <= tile_row[:, None]).astype(jnp.int32), axis=1)
    last_e = jnp.sum((ends <= (n_used - 1) * tm).astype(jnp.int32))
    tile_e = jnp.where(tile_row < ends[-1], tile_e, last_e)
    xs_a = xn2.at[src[:half * tm]].get(mode='promise_in_bounds')
    xs_b = xn2.at[src[half * tm:]].get(mode='promise_in_bounds')

    def map_a(i, te, nu):
        return (jnp.minimum(jnp.minimum(i, nu[0] - 1), half - 1), 0)

    def map_b(i, te, nu):
        return (jnp.clip(jnp.minimum(i, nu[0] - 1) - half, 0, half - 1), 0)

    ys = pl.pallas_call(
        functools.partial(_expert_kernel, half=half),
        grid_spec=pltpu.PrefetchScalarGridSpec(
            num_scalar_prefetch=2,
            grid=(n_tiles,),
            in_specs=[pl.BlockSpec((tm, d // 2), map_a), pl.BlockSpec((tm, d // 2), map_b),
                      pl.BlockSpec((1, d, ff), lambda i, te, nu: (te[i], 0, 0)),
                      pl.BlockSpec((1, d, ff), lambda i, te, nu: (te[i], 0, 0)),
                      pl.BlockSpec((1, ff, d), lambda i, te, nu: (te[i], 0, 0))],
            out_specs=pl.BlockSpec((tm, d // 2), lambda i, te, nu: (i, 0)),
        ),
        out_shape=jax.ShapeDtypeStruct((n_slots, d // 2), F32),
        compiler_params=_params(("arbitrary",)),
    )(tile_e, n_used.reshape(1), xs_a, xs_b, lw['w_gate'], lw['w_up'], lw['w_down'])
    return [tuple(ys.at[dests[j][p]].get(mode='promise_in_bounds') for j in range(TOP_K))
            for p in range(len(route_parts))]


def _final(x1, y_pair, route, g):
    t, d = x1.shape
    tl = _tile(t, 512)
    row = pl.BlockSpec((tl, d), lambda i: (i, 0))
    packed = pl.BlockSpec((tl, d // 2), lambda i: (i, 0))
    return pl.pallas_call(
        _final_kernel,
        grid=(t // tl,),
        in_specs=[row, packed, packed, pl.BlockSpec((tl, LANES), lambda i: (i, 0)), _const_spec((1, d))],
        out_specs=row,
        out_shape=jax.ShapeDtypeStruct((t, d), F32),
        compiler_params=_params(("parallel",)),
    )(x1, y_pair[0], y_pair[1], route, g)


def kernel(x_prompt, x_sample, mem_prompt, state_conv, state_lru, state_ret, cache_mem_k, cache_mem_v, norm_mix, norm_ffn, norm_mem, norm_final, w_in, w_mem_kv, conv_w, conv_b, w_r, b_r, w_i, b_i, lru_lambda, w_branch_lru, w_branch_ret, w_branch_mem, w_out, w_grp, b_grp, w_exp_router, b_exp_router, w_gate, w_up, w_down):
    depth = w_in.shape[0]
    assert depth == 1, "the two request groups are chained per layer only for a single-layer trunk"
    assert x_sample.shape[1] == 1
    bp, lp, d = x_prompt.shape
    ns = x_sample.shape[0]
    heads, dk, dv = state_ret.shape[2], state_ret.shape[3], state_ret.shape[4]
    mheads, mdh = cache_mem_k.shape[3], cache_mem_k.shape[4]
    dims = dict(d=d, w=state_lru.shape[2], heads=heads, dk=dk, dv=dv, hk=heads * dk, hv=heads * dv,
                mheads=mheads, hm=mheads * mdh, g=w_exp_router.shape[1], e=w_exp_router.shape[3])
    layer = 0
    p = dict(norm_mix=norm_mix[layer], norm_ffn=norm_ffn[layer], norm_mem=norm_mem[layer], w_in=w_in[layer],
             w_mem_kv=w_mem_kv[layer], conv_w=conv_w[layer], conv_b=conv_b[layer], w_r=w_r[layer], b_r=b_r[layer],
             w_i=w_i[layer], b_i=b_i[layer], lru_lambda=lru_lambda[layer], w_branch_lru=w_branch_lru[layer],
             w_branch_ret=w_branch_ret[layer], w_branch_mem=w_branch_mem[layer], w_out=w_out[layer],
             w_grp=w_grp[layer], b_grp=b_grp[layer], w_exp_router=w_exp_router[layer],
             b_exp_router=b_exp_router[layer], w_gate=w_gate[layer], w_up=w_up[layer], w_down=w_down[layer])
    lw = _prep_layer(p, dims)

    x1p, xn2, route_p, counts, conv_p, lru_p, ret_p, mk, mv = _prompt_mixer(x_prompt, mem_prompt, lw, dims, ns)
    x1s, xn2, route_s, counts, conv_s, lru_s, ret_s = _sample_mixer(
        x_sample[:, 0], state_conv[layer], state_lru[layer], state_ret[layer], cache_mem_k[layer],
        cache_mem_v[layer], counts, xn2, lw, dims)

    tp = bp * lp
    route_p = route_p.reshape(tp, LANES)
    yg_p, yg_s = _moe(xn2, [route_p, route_s], counts, lw, dims)
    yp = _final(x1p.reshape(tp, d), yg_p, route_p, _row(norm_final)).reshape(bp, lp, d)
    ys = _final(x1s, yg_s, route_s, _row(norm_final)).reshape(ns, 1, d)
    return (yp, ys, conv_p[None], lru_p[None], ret_p[None], mk[None], mv[None],
            conv_s[None], lru_s[None], ret_s[None])
```

```python
import functools
import itertools
import math

import jax
import jax.numpy as jnp
from jax import lax
from jax.experimental import pallas as pl
from jax.experimental.pallas import tpu as pltpu

F32 = jnp.float32
BF16 = jnp.bfloat16

EPS = 1e-6
LRU_C = 8.0
ROPE_BASE = 10000.0
PAST_LEN = 16384
TOP_K = 2
LANES = 128
SUBLANES = 8
VMEM_LIMIT = 56 * 1024 * 1024


def _dot(a, b):
    return jnp.dot(a, b, preferred_element_type=F32)


def _dot_nt(a, b):
    return lax.dot_general(a, b, (((1,), (1,)), ((), ())), preferred_element_type=F32)


def _dot_tn(a, b):
    return lax.dot_general(a, b, (((0,), (0,)), ((), ())), preferred_element_type=F32)


def _rms(x, g):
    return x * lax.rsqrt(jnp.mean(x * x, axis=-1, keepdims=True) + EPS) * g


def _sigmoid(x):
    return 0.5 * jnp.tanh(0.5 * x) + 0.5


def _silu(x):
    return x * _sigmoid(x)


def _gelu_tanh(x):
    return 0.5 * x * (1.0 + jnp.tanh(math.sqrt(2.0 / math.pi) * (x + 0.044715 * (x * x * x))))


def _softplus(x):
    return jnp.maximum(x, 0.0) + jnp.log1p(jnp.exp(-jnp.abs(x)))


def _const_spec(shape):
    nd = len(shape)
    return pl.BlockSpec(shape, lambda *_: (0,) * nd, pipeline_mode=pl.Buffered(1))


def _window_spec(rows, window):
    start, width = window
    return pl.BlockSpec((rows, width), lambda *_: (0, start // width), pipeline_mode=pl.Buffered(1))


def _params(sem):
    return pltpu.CompilerParams(dimension_semantics=sem, vmem_limit_bytes=VMEM_LIMIT)


def _lru_coeffs(uc, wr_ref, br, wi_ref, bi, lam):
    nb, bs = wr_ref.shape[0], wr_ref.shape[1]
    ucb = uc.astype(BF16)
    r_lin = jnp.concatenate([_dot(ucb[:, n * bs:(n + 1) * bs], wr_ref[n]) for n in range(nb)], axis=-1)
    i_lin = jnp.concatenate([_dot(ucb[:, n * bs:(n + 1) * bs], wi_ref[n]) for n in range(nb)], axis=-1)
    r = _sigmoid(r_lin + br)
    i = _sigmoid(i_lin + bi)
    log_a = (-LRU_C) * r * _softplus(-lam)
    a = jnp.exp(log_a)
    t = 1.0 - a * a
    mult = t * lax.rsqrt(jnp.maximum(t, 1e-37))
    return a, mult, i


def _scan_rows(a, b, h0):
    rows, width = a.shape
    grp = SUBLANES
    n_grp = rows // grp
    a = a.reshape(n_grp, grp, width)
    b = b.reshape(n_grp, grp, width)
    row = lax.broadcasted_iota(jnp.int32, a.shape, 1)
    d = 1
    while d < grp:
        a_sh = pltpu.roll(a, d, 1)
        b_sh = pltpu.roll(b, d, 1)
        keep = row >= d
        b = jnp.where(keep, b + a * b_sh, b)
        a = jnp.where(keep, a * a_sh, a)
        d *= 2
    out = []
    for g in range(n_grp):
        hg = a[g] * h0 + b[g]
        out.append(hg)
        h0 = hg[grp - 1:grp, :]
    return jnp.concatenate(out, axis=0)


def _rotary(t, cos, sin):
    half = t.shape[-1] // 2
    t1, t2 = t[:, :half], t[:, half:]
    return jnp.concatenate([t1 * cos - t2 * sin, t2 * cos + t1 * sin], axis=-1)


def _route(logits, run, n_groups, per_group):
    col = lax.broadcasted_iota(jnp.int32, logits.shape, 1)
    big = jnp.int32(1 << 20)
    neg = jnp.float32(-jnp.inf)
    gl = jnp.where(col < n_groups, logits, neg)
    gmax = jnp.max(gl, axis=-1, keepdims=True)
    g_idx = jnp.min(jnp.where(gl == gmax, col, big), axis=-1, keepdims=True)
    g_val = 1.0 / jnp.sum(jnp.exp(gl - gmax), axis=-1, keepdims=True)
    lo = n_groups + per_group * g_idx
    el = jnp.where((col >= lo) & (col < lo + per_group), logits, neg)
    m1 = jnp.max(el, axis=-1, keepdims=True)
    i1 = jnp.min(jnp.where(el == m1, col, big), axis=-1, keepdims=True)
    el2 = jnp.where(col == i1, neg, el)
    m2 = jnp.max(el2, axis=-1, keepdims=True)
    i2 = jnp.min(jnp.where(el2 == m2, col, big), axis=-1, keepdims=True)
    e2 = jnp.exp(m2 - m1)
    w1 = g_val / (1.0 + e2)
    w2 = g_val * e2 / (1.0 + e2)
    f1 = (i1 - n_groups).astype(F32)
    f2 = (i2 - n_groups).astype(F32)
    hit1, hit2 = col == i1, col == i2
    hits = jnp.where(hit1 | hit2, 1.0, 0.0)
    rows = logits.shape[0]
    earlier = lax.broadcasted_iota(jnp.int32, (rows, rows), 1) < lax.broadcasted_iota(jnp.int32, (rows, rows), 0)
    before = run + _dot(jnp.where(earlier, 1.0, 0.0).astype(BF16), hits.astype(BF16))
    r1 = jnp.sum(jnp.where(hit1, before, 0.0), axis=-1, keepdims=True)
    r2 = jnp.sum(jnp.where(hit2, before, 0.0), axis=-1, keepdims=True)
    route = f1
    for j, val in enumerate((f2, w1, w2, r1, r2), start=1):
        route = jnp.where(col == j, val, route)
    route = jnp.where(col > 5, 0.0, route)
    return route, run + jnp.sum(hits, axis=0, keepdims=True)


def _pack_pairs(x):
    half = x.shape[1] // 2
    lo = lax.bitcast_convert_type(x[:, :half].astype(F32), jnp.uint32)
    hi = lax.bitcast_convert_type(x[:, half:].astype(F32), jnp.uint32)
    return lax.bitcast_convert_type((lo >> 16) | (hi & jnp.uint32(0xFFFF0000)), F32)


def _unpack_pairs(w):
    u = lax.bitcast_convert_type(w, jnp.uint32)
    lo = lax.bitcast_convert_type(u << 16, F32)
    hi = lax.bitcast_convert_type(u & jnp.uint32(0xFFFF0000), F32)
    return jnp.concatenate([lo, hi], axis=1)


def _merge_tail(x, merged, wout_ref, gffn, wrt_ref, brt, run, n_groups, per_group):
    x1 = x + _dot(merged.astype(BF16), wout_ref[...])
    xn2 = _rms(x1, gffn).astype(BF16)
    logits = _dot(xn2, wrt_ref[...]) + brt
    route, run = _route(logits, run, n_groups, per_group)
    return x1, _pack_pairs(xn2), route, run


def _lru_prompt_steps(c, xn, wxy_ref, wga_ref, cw_ref, cb_ref, wr_ref, br_ref, wi_ref, bi_ref, lam_ref, wbr_ref,
                      out_ref, ccar, hcar, res):
    tl = xn.shape[0]
    width = hcar.shape[1]
    kw = cw_ref.shape[0]
    nb, bs = wr_ref.shape[0], wr_ref.shape[1]
    z = _dot(xn, wxy_ref[...])
    row = lax.broadcasted_iota(jnp.int32, (tl, bs), 0)
    hbs, lasts = [], []
    for n in range(nb):
        cs = slice(n * bs, (n + 1) * bs)
        ux = z[:, cs]
        acc = cw_ref[0:1, cs] * ux
        for j in range(1, kw):
            prev = jnp.where(row == 0, ccar[j - 1:j, cs], pltpu.roll(acc, 1, 0))
            ccar[j - 1:j, cs] = acc[tl - 1:tl, :]
            acc = prev + cw_ref[j:j + 1, cs] * ux
        uc = acc + cb_ref[:, cs]
        ucb = uc.astype(BF16)
        r = _sigmoid(_dot(ucb, wr_ref[n]) + br_ref[:, cs])
        gate = _sigmoid(_dot(ucb, wi_ref[n]) + bi_ref[:, cs])
        log_a = (-LRU_C) * r * _softplus(-lam_ref[:, cs])
        a = jnp.exp(log_a)
        t = 1.0 - a * a
        mult = t * lax.rsqrt(jnp.maximum(t, 1e-37))
        mult = jnp.where(row + c * tl == 0, 1.0, mult)
        h = _scan_rows(a, mult * gate * uc, hcar[:, cs])
        hcar[:, cs] = h[tl - 1:tl, :]
        lasts.append(h[tl - 1:tl, :])
        hbs.append((h * _gelu_tanh(z[:, width + n * bs: width + (n + 1) * bs])).astype(BF16))
        yield
    res['tail'] = z[tl - (kw - 1):tl, :width]
    res['h_last'] = jnp.concatenate(lasts, axis=1)
    a_out = _dot(jnp.concatenate(hbs, axis=1), wbr_ref[...])
    ga = _dot(xn, wga_ref[...])
    out_ref[0] = (_sigmoid(ga) * a_out).astype(out_ref.dtype)
    yield


def _ret_proj_steps(xn, w_refs, res, cols=512):
    for name, w_ref in w_refs:
        parts = []
        for j in range(0, w_ref.shape[1], cols):
            parts.append(_dot(xn, w_ref[:, j:j + cols]))
            yield
        res[name] = jnp.concatenate(parts, axis=1)


def _ret_prompt_steps(xn, proj, wgb_ref, cos_ref, sin_ref, wbr_ref, out_ref, s_scr, y_scr, heads, chunk):
    tl = xn.shape[0]
    q, k, v, g = proj['q'], proj['k'], proj['v'], proj['g']
    dk = q.shape[1] // heads
    dv = v.shape[1] // heads
    cos, sin = cos_ref[...], sin_ref[...]
    n_i = lax.broadcasted_iota(jnp.int32, (chunk, chunk), 0)
    m_i = lax.broadcasted_iota(jnp.int32, (chunk, chunk), 1)
    diff = (n_i - m_i).astype(F32)
    rowk = lax.broadcasted_iota(jnp.int32, (chunk, dk), 0).astype(F32)
    for h in range(heads):
        log_g = math.log1p(-(2.0 ** (-5.0 - h)))
        dmask = jnp.where(diff >= 0, jnp.exp(jnp.maximum(diff, 0.0) * log_g), 0.0)
        q_decay = jnp.exp((rowk + 1.0) * log_g)
        k_decay = jnp.exp((chunk - 1.0 - rowk) * log_g)
        s_decay = math.exp(chunk * log_g)
        for sub in range(tl // chunk):
            r0 = sub * chunk
            cs, sn = cos[r0:r0 + chunk], sin[r0:r0 + chunk]
            qr = _rotary(q[r0:r0 + chunk, h * dk:(h + 1) * dk], cs, sn)
            kr = _rotary(k[r0:r0 + chunk, h * dk:(h + 1) * dk], cs, sn) * (dk ** -0.5)
            vh = v[r0:r0 + chunk, h * dv:(h + 1) * dv].astype(BF16)
            s = s_scr[h]
            att = _dot_nt(qr.astype(BF16), kr.astype(BF16)) * dmask
            o = _dot(att.astype(BF16), vh) + _dot((qr * q_decay).astype(BF16), s.astype(BF16))
            s_scr[h] = s * s_decay + _dot_tn((kr * k_decay).astype(BF16), vh)
            o = o * lax.rsqrt(jnp.mean(o * o, axis=-1, keepdims=True) + EPS)
            gh = g[r0:r0 + chunk, h * dv:(h + 1) * dv]
            y_scr[r0:r0 + chunk, h * dv:(h + 1) * dv] = (_silu(gh) * o).astype(BF16)
            yield
    b_out = _dot(y_scr[...], wbr_ref[...])
    gb = _dot(xn, wgb_ref[...])
    out_ref[0] = (_sigmoid(gb) * b_out).astype(out_ref.dtype)


def _lru_ret_prompt_kernel(x_ref, g_ref, wxy_ref, wga_ref, cw_ref, cb_ref, wr_ref, br_ref, wi_ref, bi_ref,
                           lam_ref, wbl_ref, wq_ref, wk_ref, wv_ref, wg_ref, wgb_ref, cos_ref, sin_ref, wbr_ref,
                           a_ref, conv_ref, h_ref, b_ref, s_out_ref, ccar, hcar, s_scr, y_scr, *, heads, chunk):
    c = pl.program_id(1)

    @pl.when(c == 0)
    def _():
        ccar[...] = jnp.zeros_like(ccar)
        hcar[...] = jnp.zeros_like(hcar)
        s_scr[...] = jnp.zeros_like(s_scr)

    xn = _rms(x_ref[0], g_ref[...]).astype(BF16)
    lru, proj = {}, {}
    lru_steps = _lru_prompt_steps(c, xn, wxy_ref, wga_ref, cw_ref, cb_ref, wr_ref, br_ref, wi_ref, bi_ref,
                                  lam_ref, wbl_ref, a_ref, ccar, hcar, lru)
    proj_steps = _ret_proj_steps(xn, (('q', wq_ref), ('k', wk_ref), ('v', wv_ref), ('g', wg_ref)), proj)
    n_lru = wr_ref.shape[0] + 1
    n_proj = sum(-(-r.shape[1] // 512) for r in (wq_ref, wk_ref, wv_ref, wg_ref))
    n_ret = heads * (x_ref.shape[1] // chunk)
    lru_done = 0
    for i in range(n_proj):
        next(proj_steps)
        want = ((i + 1) * n_lru) // (n_proj + n_ret)
        while lru_done < want:
            next(lru_steps)
            lru_done += 1
    for _ in proj_steps:
        pass
    ret_steps = _ret_prompt_steps(xn, proj, wgb_ref, cos_ref, sin_ref, wbr_ref, b_ref, s_scr, y_scr, heads, chunk)
    for i in range(n_ret):
        next(ret_steps)
        want = ((n_proj + i + 1) * n_lru) // (n_proj + n_ret)
        while lru_done < want:
            next(lru_steps)
            lru_done += 1
    for _ in itertools.chain(lru_steps, ret_steps):
        pass

    @pl.when(c == pl.num_programs(1) - 1)
    def _():
        conv_ref[0] = lru['tail']
        h_ref[0] = lru['h_last']
        s_out_ref[0] = s_scr[...]


def _memkv_kernel(m_ref, g_ref, w_ref, k_ref, v_ref, k4_ref, v4_ref):
    width = k_ref.shape[-1]
    heads, dh = k4_ref.shape[2], k4_ref.shape[3]
    mn = _rms(m_ref[0], g_ref[...]).astype(BF16)
    kv = _dot(mn, w_ref[...])
    k_ref[0] = kv[:, :width]
    v_ref[0] = kv[:, width:]
    for h in range(heads):
        k4_ref[0, :, h, :] = kv[:, h * dh:(h + 1) * dh]
        v4_ref[0, :, h, :] = kv[:, width + h * dh: width + (h + 1) * dh]


def _attn_merge_prompt_kernel(x_ref, ag_ref, bg_ref, mk_ref, mv_ref, g_ref, wmq_ref, wgc_ref, wbm_ref,
                              wout_ref, gffn_ref, wrt_ref, brt_ref, x1_ref, xn2_ref, route_ref, cnt_ref,
                              *, heads, n_groups, per_group):
    @pl.when((pl.program_id(0) == 0) & (pl.program_id(1) == 0))
    def _():
        cnt_ref[...] = jnp.zeros_like(cnt_ref)

    x = x_ref[0]
    xn = _rms(x, g_ref[...]).astype(BF16)
    mq = _dot(xn, wmq_ref[...]).astype(BF16)
    dh = mq.shape[1] // heads
    mk = mk_ref[0].astype(BF16)
    mv = mv_ref[0].astype(BF16)
    cs = []
    for h in range(heads):
        hs = slice(h * dh, (h + 1) * dh)
        lg = _dot_nt(mq[:, hs], mk[:, hs]) * (dh ** -0.5)
        p = jnp.exp(lg - jnp.max(lg, axis=-1, keepdims=True))
        den = jnp.sum(p, axis=-1, keepdims=True)
        cs.append(_dot(p.astype(BF16), mv[:, hs]) / den)
    c_out = _dot(jnp.concatenate(cs, axis=-1).astype(BF16), wbm_ref[...])
    gc = _dot(xn, wgc_ref[...])
    merged = _sigmoid(gc) * c_out + ag_ref[0].astype(F32) + bg_ref[0].astype(F32)
    x1, xn2, route, run = _merge_tail(x, merged, wout_ref, gffn_ref[...], wrt_ref, brt_ref[...], cnt_ref[...],
                                      n_groups, per_group)
    x1_ref[0] = x1
    xn2_ref[...] = xn2
    route_ref[0] = route
    cnt_ref[...] = run


def _proj_kernel(x_ref, g_ref, w_ref, z_ref):
    xn = _rms(x_ref[...], g_ref[...]).astype(BF16)
    z_ref[...] = _dot(xn, w_ref[...])


def _lru_sample_kernel(ux_ref, uy_ref, ga_ref, cprev_ref, hprev_ref, cw_ref, cb_ref, wr_ref, br_ref, wi_ref,
                       bi_ref, lam_ref, wbr_ref, out_ref, conv_ref, h_ref, *, at_start):
    kw = cw_ref.shape[0]
    ux = ux_ref[...]
    uc = cb_ref[...] + cw_ref[kw - 1:kw, :] * ux
    for j in range(kw - 1):
        uc = uc + cw_ref[j:j + 1, :] * cprev_ref[:, j, :]
    a, mult, gate = _lru_coeffs(uc, wr_ref, br_ref[...], wi_ref, bi_ref[...], lam_ref[...])
    if at_start:
        mult = jnp.ones_like(mult)
    h = a * hprev_ref[...] + mult * gate * uc
    h_ref[...] = h
    for j in range(kw - 2):
        conv_ref[:, j, :] = cprev_ref[:, j + 1, :]
    conv_ref[:, kw - 2, :] = ux
    a_out = _dot((h * _gelu_tanh(uy_ref[...])).astype(BF16), wbr_ref[...])
    out_ref[...] = _sigmoid(ga_ref[...]) * a_out


def _rot_sample_kernel(q_ref, k_ref, cos_ref, sin_ref, qo_ref, ko_ref, *, heads):
    dk = q_ref.shape[1] // heads
    cos, sin = cos_ref[...], sin_ref[...]
    for h in range(heads):
        hs = slice(h * dk, (h + 1) * dk)
        qo_ref[:, hs] = _rotary(q_ref[:, hs], cos, sin)
        ko_ref[:, hs] = _rotary(k_ref[:, hs], cos, sin) * (dk ** -0.5)


def _ret_sample_kernel(qt_ref, kt_ref, q_ref, k_ref, v_ref, s_ref, o_ref, s_out_ref, *, heads):
    bb = q_ref.shape[1]
    dk = s_ref.shape[2]
    dv = s_ref.shape[3]
    for j in range(bb):
        for h in range(heads):
            decay = 1.0 - 2.0 ** (-5.0 - h)
            qcol = qt_ref[0, h, :, j:j + 1]
            kcol = kt_ref[0, h, :, j:j + 1]
            qrow = q_ref[0, j:j + 1, h * dk:(h + 1) * dk]
            krow = k_ref[0, j:j + 1, h * dk:(h + 1) * dk]
            vrow = v_ref[0, j:j + 1, h * dv:(h + 1) * dv]
            s = s_ref[j, h]
            att = jnp.sum(qrow * krow, axis=-1, keepdims=True)
            o = att * vrow + jnp.sum((qcol * decay) * s, axis=0, keepdims=True)
            o_ref[0, j:j + 1, h * dv:(h + 1) * dv] = o
            s_out_ref[j, h] = s * decay + kcol * vrow


def _attn_sample_kernel(mq_ref, k_ref, v_ref, c_ref, *, heads):
    bb, _, dh = mq_ref.shape
    for j in range(bb):
        q = mq_ref[j]
        lg = jnp.sum(k_ref[j] * q[None], axis=-1, keepdims=True) * (dh ** -0.5)
        p = jnp.exp(lg - jnp.max(lg, axis=0, keepdims=True))
        den = jnp.sum(p, axis=0)
        c_ref[j] = jnp.sum(v_ref[j] * p, axis=0) / den


def _merge_sample_kernel(x_ref, ag_ref, o_ref, g_ref, c_ref, gb_ref, gc_ref, wbr_ref, wbm_ref, wout_ref,
                         gffn_ref, wrt_ref, brt_ref, run_ref, xn2_table_ref, x1_ref, xn2_ref, route_ref, cnt_ref,
                         *, heads, n_groups, per_group):
    del xn2_table_ref
    dv = o_ref.shape[1] // heads
    ys = []
    for h in range(heads):
        o = o_ref[:, h * dv:(h + 1) * dv]
        o = o * lax.rsqrt(jnp.mean(o * o, axis=-1, keepdims=True) + EPS)
        ys.append((_silu(g_ref[:, h * dv:(h + 1) * dv]) * o).astype(BF16))
    b_out = _dot(jnp.concatenate(ys, axis=-1), wbr_ref[...])
    c_out = _dot(c_ref[...].astype(BF16), wbm_ref[...])
    merged = ag_ref[...] + _sigmoid(gb_ref[...]) * b_out + _sigmoid(gc_ref[...]) * c_out
    x1, xn2, route, run = _merge_tail(x_ref[...], merged, wout_ref, gffn_ref[...], wrt_ref, brt_ref[...],
                                      run_ref[...], n_groups, per_group)
    x1_ref[...] = x1
    xn2_ref[...] = xn2
    route_ref[...] = route
    cnt_ref[...] = run


def _expert_kernel(te_ref, nu_ref, xs_ref, wg_ref, wu_ref, wd_ref, *rest):
    ys_ref = rest[-1]
    i = pl.program_id(0)

    @pl.when(i < nu_ref[0])
    def _():
        x = _unpack_pairs(xs_ref[...]).astype(BF16)
        hg = _dot(x, wg_ref[0].astype(BF16))
        hu = _dot(x, wu_ref[0].astype(BF16))
        hid = (_silu(hg) * hu).astype(BF16)
        ys_ref[...] = _pack_pairs(_dot(hid, wd_ref[0].astype(BF16)).astype(BF16))

    @pl.when(i >= nu_ref[0])
    def _():
        ys_ref[...] = jnp.zeros_like(ys_ref)


def _final_kernel(x1_ref, y1_ref, y2_ref, route_ref, g_ref, *rest):
    out_ref = rest[-1]
    w1 = route_ref[:, 2:3]
    w2 = route_ref[:, 3:4]
    x2 = x1_ref[...] + (w1 * _unpack_pairs(y1_ref[...]) + w2 * _unpack_pairs(y2_ref[...]))
    out_ref[...] = _rms(x2, g_ref[...])


def _tile(n, target):
    t = min(n, target)
    while n % t:
        t //= 2
    return t


def _row(v):
    return v.reshape(1, -1).astype(F32)


def _prep_layer(p, dims):
    d, w, hk, hv, hm = dims['d'], dims['w'], dims['hk'], dims['hv'], dims['hm']
    w_in = p['w_in']
    starts = {}
    o0 = 0
    for name, width in (('xy', 2 * w), ('q', hk), ('k', hk), ('v', hv), ('g', hv), ('mq', hm),
                        ('ga', d), ('gb', d), ('gc', d)):
        assert o0 % width == 0, "each window must start on a multiple of its width"
        starts[name] = (o0, width)
        o0 += width
    g, e = p['w_exp_router'].shape[0], p['w_exp_router'].shape[2]
    w_rt = jnp.concatenate([p['w_grp'], jnp.moveaxis(p['w_exp_router'], 0, 1).reshape(d, g * e)], axis=1)
    w_rt = jnp.pad(w_rt, ((0, 0), (0, LANES - w_rt.shape[1]))).astype(BF16)
    b_rt = jnp.pad(jnp.concatenate([p['b_grp'], p['b_exp_router'].reshape(-1)]), (0, LANES - g - g * e))
    return dict(
        w_in=w_in.astype(BF16), win=starts,
        norm_mix=_row(p['norm_mix']), norm_ffn=_row(p['norm_ffn']), norm_mem=_row(p['norm_mem']),
        w_mem_kv=p['w_mem_kv'].astype(BF16),
        conv_w=p['conv_w'].astype(F32), conv_b=_row(p['conv_b']),
        w_r=p['w_r'].astype(BF16), b_r=_row(p['b_r']), w_i=p['w_i'].astype(BF16), b_i=_row(p['b_i']),
        lam=_row(p['lru_lambda']),
        w_br_lru=p['w_branch_lru'].astype(BF16), w_br_ret=p['w_branch_ret'].astype(BF16),
        w_br_mem=p['w_branch_mem'].astype(BF16), w_out=p['w_out'].astype(BF16),
        w_rt=w_rt, b_rt=_row(b_rt),
        w_gate=p['w_gate'].reshape((g * e,) + p['w_gate'].shape[2:]),
        w_up=p['w_up'].reshape((g * e,) + p['w_up'].shape[2:]),
        w_down=p['w_down'].reshape((g * e,) + p['w_down'].shape[2:]),
    )


def _rope_tables(pos, dk):
    half = dk // 2
    inv = ROPE_BASE ** (-jnp.linspace(0.0, 1.0, half, dtype=F32))
    ang = pos.astype(F32)[:, None] * inv[None, :]
    return jnp.cos(ang), jnp.sin(ang)


def _prompt_mixer(x, mem, lw, dims, extra_rows):
    b, l, d = x.shape
    w, heads, dk, dv = dims['w'], dims['heads'], dims['dk'], dims['dv']
    kw = lw['conv_w'].shape[0]
    tl = _tile(l, 256)
    grid = (b, l // tl)
    xspec = pl.BlockSpec((1, tl, d), lambda i, c: (i, c, 0))
    sem = ("parallel", "arbitrary")
    w_in, win = lw['w_in'], lw['win']

    chunk = _tile(tl, 256)
    cos, sin = _rope_tables(jnp.arange(l, dtype=jnp.int32), dk)
    half = dk // 2
    a_g, conv_new, h_last, b_g, s_new = pl.pallas_call(
        functools.partial(_lru_ret_prompt_kernel, heads=heads, chunk=chunk),
        grid=grid,
        in_specs=[xspec, _const_spec((1, d)), _window_spec(d, win['xy']), _window_spec(d, win['ga']),
                  _const_spec((kw, w)), _const_spec((1, w)), _const_spec(lw['w_r'].shape), _const_spec((1, w)),
                  _const_spec(lw['w_i'].shape), _const_spec((1, w)), _const_spec((1, w)), _const_spec((w, d)),
                  _window_spec(d, win['q']), _window_spec(d, win['k']),
                  _window_spec(d, win['v']), _window_spec(d, win['g']), _window_spec(d, win['gb']),
                  pl.BlockSpec((tl, half), lambda i, c: (c, 0)), pl.BlockSpec((tl, half), lambda i, c: (c, 0)),
                  _const_spec((heads * dv, d))],
        out_specs=[pl.BlockSpec((1, tl, d), lambda i, c: (i, c, 0)),
                   pl.BlockSpec((1, kw - 1, w), lambda i, c: (i, 0, 0)),
                   pl.BlockSpec((1, 1, w), lambda i, c: (i, 0, 0)),
                   pl.BlockSpec((1, tl, d), lambda i, c: (i, c, 0)),
                   pl.BlockSpec((1, heads, dk, dv), lambda i, c: (i, 0, 0, 0))],
        out_shape=[jax.ShapeDtypeStruct((b, l, d), BF16), jax.ShapeDtypeStruct((b, kw - 1, w), F32),
                   jax.ShapeDtypeStruct((b, 1, w), F32),
                   jax.ShapeDtypeStruct((b, l, d), BF16), jax.ShapeDtypeStruct((b, heads, dk, dv), F32)],
        scratch_shapes=[pltpu.VMEM((SUBLANES, w), F32), pltpu.VMEM((1, w), F32),
                        pltpu.VMEM((heads, dk, dv), F32), pltpu.VMEM((tl, heads * dv), BF16)],
        compiler_params=_params(sem),
    )(x, lw['norm_mix'], w_in, w_in, lw['conv_w'], lw['conv_b'], lw['w_r'], lw['b_r'],
      lw['w_i'], lw['b_i'], lw['lam'], lw['w_br_lru'],
      w_in, w_in, w_in, w_in, w_in, cos, sin, lw['w_br_ret'])

    m = mem.shape[1]
    hm = dims['hm']
    mh = dims['mheads']
    flat = pl.BlockSpec((1, m, hm), lambda i: (i, 0, 0))
    per_head = pl.BlockSpec((1, m, mh, hm // mh), lambda i: (i, 0, 0, 0))
    mk, mv, mk4, mv4 = pl.pallas_call(
        _memkv_kernel,
        grid=(b,),
        in_specs=[pl.BlockSpec((1, m, d), lambda i: (i, 0, 0)), _const_spec((1, d)), _const_spec((d, 2 * hm))],
        out_specs=[flat, flat, per_head, per_head],
        out_shape=[jax.ShapeDtypeStruct((b, m, hm), F32), jax.ShapeDtypeStruct((b, m, hm), F32),
                   jax.ShapeDtypeStruct((b, m, mh, hm // mh), F32), jax.ShapeDtypeStruct((b, m, mh, hm // mh), F32)],
        compiler_params=_params(("parallel",)),
    )(mem, lw['norm_mem'], lw['w_mem_kv'])

    tm = _tile(l, 512)
    mspec = pl.BlockSpec((1, tm, d), lambda i, c: (i, c, 0))
    x1, xn2, route, counts = pl.pallas_call(
        functools.partial(_attn_merge_prompt_kernel, heads=dims['mheads'], n_groups=dims['g'],
                          per_group=dims['e']),
        grid=(b, l // tm),
        in_specs=[mspec, mspec, mspec,
                  pl.BlockSpec((1, m, hm), lambda i, c: (i, 0, 0)), pl.BlockSpec((1, m, hm), lambda i, c: (i, 0, 0)),
                  _const_spec((1, d)), _window_spec(d, win['mq']), _window_spec(d, win['gc']), _const_spec((hm, d)),
                  _const_spec((d, d)), _const_spec((1, d)), _const_spec((d, LANES)), _const_spec((1, LANES))],
        out_specs=[mspec, pl.BlockSpec((tm, d // 2), lambda i, c: (i * (l // tm) + c, 0)),
                   pl.BlockSpec((1, tm, LANES), lambda i, c: (i, c, 0)),
                   pl.BlockSpec((1, LANES), lambda i, c: (0, 0))],
        out_shape=[jax.ShapeDtypeStruct((b, l, d), F32), jax.ShapeDtypeStruct((b * l + extra_rows, d // 2), F32),
                   jax.ShapeDtypeStruct((b, l, LANES), F32), jax.ShapeDtypeStruct((1, LANES), F32)],
        compiler_params=_params(("arbitrary", "arbitrary")),
    )(x, a_g, b_g, mk, mv, lw['norm_mix'], w_in, w_in, lw['w_br_mem'], lw['w_out'],
      lw['norm_ffn'], lw['w_rt'], lw['b_rt'])
    return x1, xn2, route, counts, conv_new, h_last[:, 0], s_new, mk4, mv4


def _sample_mixer(x, conv_prev, h_prev, s_prev, mem_k, mem_v, counts0, xn2_all, lw, dims):
    n, d = x.shape
    w, heads, dk, dv, hm = dims['w'], dims['heads'], dims['dk'], dims['dv'], dims['hm']
    hk, hv = heads * dk, heads * dv
    kw = lw['conv_w'].shape[0]
    n_in = lw['w_in'].shape[1]
    tn = _tile(n_in, 1024)
    z = pl.pallas_call(
        _proj_kernel,
        grid=(n_in // tn,),
        in_specs=[_const_spec((n, d)), _const_spec((1, d)), pl.BlockSpec((d, tn), lambda j: (0, j))],
        out_specs=pl.BlockSpec((n, tn), lambda j: (0, j)),
        out_shape=jax.ShapeDtypeStruct((n, n_in), F32),
        compiler_params=_params(("parallel",)),
    )(x, lw['norm_mix'], lw['w_in'])
    o0 = 0
    parts = []
    for sz in (w, w, hk, hk, hv, hv, hm, d, d, d):
        parts.append(z[:, o0:o0 + sz])
        o0 += sz
    ux, uy, q, k, v, g, mq, ga, gb, gc = parts

    a_g, conv_new, h_new = pl.pallas_call(
        functools.partial(_lru_sample_kernel, at_start=(PAST_LEN == 0)),
        out_shape=[jax.ShapeDtypeStruct((n, d), F32), jax.ShapeDtypeStruct((n, kw - 1, w), F32),
                   jax.ShapeDtypeStruct((n, w), F32)],
        compiler_params=pltpu.CompilerParams(vmem_limit_bytes=VMEM_LIMIT),
    )(ux, uy, ga, conv_prev, h_prev, lw['conv_w'], lw['conv_b'], lw['w_r'], lw['b_r'], lw['w_i'], lw['b_i'],
      lw['lam'], lw['w_br_lru'])

    cos, sin = _rope_tables(PAST_LEN + jnp.arange(1, dtype=jnp.int32), dk)
    qr, kr = pl.pallas_call(
        functools.partial(_rot_sample_kernel, heads=heads),
        out_shape=[jax.ShapeDtypeStruct((n, hk), F32), jax.ShapeDtypeStruct((n, hk), F32)],
    )(q, k, cos, sin)

    bb = _tile(n, 4)

    def cols(t):
        return t.reshape(n // bb, bb, heads, dk).transpose(0, 2, 3, 1)

    def rows(t):
        return t.reshape(n // bb, bb, t.shape[1])

    o, s_new = pl.pallas_call(
        functools.partial(_ret_sample_kernel, heads=heads),
        grid=(n // bb,),
        in_specs=[pl.BlockSpec((1, heads, dk, bb), lambda i: (i, 0, 0, 0)),
                  pl.BlockSpec((1, heads, dk, bb), lambda i: (i, 0, 0, 0)),
                  pl.BlockSpec((1, bb, hk), lambda i: (i, 0, 0)), pl.BlockSpec((1, bb, hk), lambda i: (i, 0, 0)),
                  pl.BlockSpec((1, bb, hv), lambda i: (i, 0, 0)),
                  pl.BlockSpec((bb, heads, dk, dv), lambda i: (i, 0, 0, 0))],
        out_specs=[pl.BlockSpec((1, bb, hv), lambda i: (i, 0, 0)),
                   pl.BlockSpec((bb, heads, dk, dv), lambda i: (i, 0, 0, 0))],
        out_shape=[jax.ShapeDtypeStruct((n // bb, bb, hv), F32), jax.ShapeDtypeStruct(s_prev.shape, F32)],
        compiler_params=_params(("parallel",)),
    )(cols(qr), cols(kr), rows(qr), rows(kr), rows(v), s_prev)
    o = o.reshape(n, hv)
    bb = _tile(n, SUBLANES)

    m = mem_k.shape[1]
    c = pl.pallas_call(
        functools.partial(_attn_sample_kernel, heads=dims['mheads']),
        grid=(n // bb,),
        in_specs=[pl.BlockSpec((bb,) + mem_k.shape[2:], lambda i: (i, 0, 0)),
                  pl.BlockSpec((bb,) + mem_k.shape[1:], lambda i: (i, 0, 0, 0)),
                  pl.BlockSpec((bb,) + mem_v.shape[1:], lambda i: (i, 0, 0, 0))],
        out_specs=pl.BlockSpec((bb,) + mem_k.shape[2:], lambda i: (i, 0, 0)),
        out_shape=jax.ShapeDtypeStruct((n,) + mem_k.shape[2:], F32),
        compiler_params=_params(("parallel",)),
    )(mq.reshape((n,) + mem_k.shape[2:]), mem_k, mem_v).reshape(n, hm)

    first = xn2_all.shape[0] - n
    assert first % n == 0
    args = (x, a_g, o, g, c, gb, gc, lw['w_br_ret'], lw['w_br_mem'], lw['w_out'], lw['norm_ffn'], lw['w_rt'],
            lw['b_rt'], counts0)
    x1, xn2_all, route, counts = pl.pallas_call(
        functools.partial(_merge_sample_kernel, heads=heads, n_groups=dims['g'], per_group=dims['e']),
        grid=(1,),
        in_specs=[_const_spec(a.shape) for a in args] + [pl.BlockSpec(memory_space=pl.ANY)],
        out_specs=[pl.BlockSpec((n, d), lambda i: (0, 0)), pl.BlockSpec((n, d // 2), lambda i: (first // n, 0)),
                   pl.BlockSpec((n, LANES), lambda i: (0, 0)), pl.BlockSpec((1, LANES), lambda i: (0, 0))],
        out_shape=[jax.ShapeDtypeStruct((n, d), F32), jax.ShapeDtypeStruct(xn2_all.shape, F32),
                   jax.ShapeDtypeStruct((n, LANES), F32), jax.ShapeDtypeStruct((1, LANES), F32)],
        input_output_aliases={len(args): 1},
        compiler_params=_params(("arbitrary",)),
    )(*args, xn2_all)
    return x1, xn2_all, route, counts, conv_new, h_new, s_new


def _moe(xn2, route_parts, counts, lw, dims, tm=512):
    t, d = xn2.shape[0], dims['d']
    g, n_exp = dims['g'], dims['g'] * dims['e']
    ff = lw['w_gate'].shape[2]
    counts = counts[0, g:g + n_exp].astype(jnp.int32)
    padded = ((counts + tm - 1) // tm) * tm
    ends = jnp.cumsum(padded)
    offs = ends - padded
    n_tiles = 2 * ((TOP_K * t + n_exp * tm + 2 * tm - 1) // (2 * tm))
    n_slots = n_tiles * tm
    half = n_tiles // 2
    experts = jnp.arange(n_exp, dtype=jnp.int32)[None, :]
    dests = []
    for j in range(TOP_K):
        per_part = []
        for r in route_parts:
            ids = r[:, j].astype(jnp.int32)
            base = jnp.sum(jnp.where(ids[:, None] == experts, offs[None, :], 0), axis=1)
            per_part.append(base + r[:, 4 + j].astype(jnp.int32))
        dests.append(per_part)
    dest = jnp.concatenate([dp for per_part in dests for dp in per_part], axis=0)
    tok = jnp.arange(t, dtype=jnp.int32)
    n_pad = n_slots - TOP_K * t
    pad_cnt = jnp.concatenate([padded - counts, n_slots - ends[-1:]])
    pad_from = jnp.concatenate([offs + counts, ends[-1:]])
    pad_end = jnp.cumsum(pad_cnt)
    pad_i = jnp.arange(n_pad, dtype=jnp.int32)
    which = jnp.sum((pad_end[None, :] <= pad_i[:, None]).astype(jnp.int32), axis=1)
    shift = pad_from - (pad_end - pad_cnt)
    groups = jnp.arange(n_exp + 1, dtype=jnp.int32)[None, :]
    pad_slot = pad_i + jnp.sum(jnp.where(which[:, None] == groups, shift[None, :], 0), axis=1)
    _, src = lax.sort_key_val(jnp.concatenate([dest, pad_slot]),
                              jnp.concatenate([tok] * TOP_K + [pad_i % t]))
    n_used = (ends[-1] // tm).astype(jnp.int32)
    tile_row = jnp.arange(n_tiles, dtype=jnp.int32) * tm
    tile_e = jnp.sum((ends[None, :] <= tile_row[:, None]).astype(jnp.int32), axis=1)
    last_e = jnp.sum((ends <= (n_used - 1) * tm).astype(jnp.int32))
    tile_e = jnp.where(tile_row < ends[-1], tile_e, last_e)
    xs_a = xn2.at[src[:half * tm]].get(mode='promise_in_bounds')
    xs_b = xn2.at[src[half * tm:]].get(mode='promise_in_bounds')

    def expert_pass(xs, first_tile, ys_table):
        def rows(i, te, nu):
            return (jnp.maximum(jnp.minimum(i, nu[0] - 1), 0), 0)

        def weights(i, te, nu):
            return (te[i], 0, 0)

        in_specs = [pl.BlockSpec((tm, d // 2), rows), pl.BlockSpec((1, d, ff), weights),
                    pl.BlockSpec((1, d, ff), weights), pl.BlockSpec((1, ff, d), weights)]
        args = [xs, lw['w_gate'], lw['w_up'], lw['w_down']]
        aliases = {}
        if ys_table is not None:
            in_specs.append(pl.BlockSpec(memory_space=pl.ANY))
            args.append(ys_table)
            aliases = {2 + len(args) - 1: 0}
        return pl.pallas_call(
            _expert_kernel,
            grid_spec=pltpu.PrefetchScalarGridSpec(
                num_scalar_prefetch=2,
                grid=(half,),
                in_specs=in_specs,
                out_specs=pl.BlockSpec((tm, d // 2), lambda i, te, nu: (first_tile + i, 0)),
            ),
            out_shape=jax.ShapeDtypeStruct((n_slots, d // 2), F32),
            input_output_aliases=aliases,
            compiler_params=_params(("arbitrary",)),
        )(tile_e[first_tile:first_tile + half], jnp.clip(n_used - first_tile, 0, half).reshape(1), *args)

    ys = expert_pass(xs_a, 0, None)
    ys = expert_pass(xs_b, half, ys)
    return [(ys, tuple(dests[j][p] for j in range(TOP_K))) for p in range(len(route_parts))]


def _final(x1, ys, dest_pair, route, g, n_parts):
    t, d = x1.shape
    rows = t // n_parts
    tl = _tile(rows, 512)
    out = None
    for p in range(n_parts):
        first = p * rows // tl
        y1, y2 = (ys.at[dp[p * rows:(p + 1) * rows]].get(mode='promise_in_bounds') for dp in dest_pair)
        row = pl.BlockSpec((tl, d), lambda i, first=first: (first + i, 0))
        packed = pl.BlockSpec((tl, d // 2), lambda i: (i, 0))
        in_specs = [row, packed, packed, pl.BlockSpec((tl, LANES), lambda i, first=first: (first + i, 0)),
                    _const_spec((1, d))]
        args = [x1, y1, y2, route, g]
        aliases = {}
        if out is not None:
            in_specs.append(pl.BlockSpec(memory_space=pl.ANY))
            args.append(out)
            aliases = {len(args) - 1: 0}
        out = pl.pallas_call(
            _final_kernel,
            grid=(rows // tl,),
            in_specs=in_specs,
            out_specs=row,
            out_shape=jax.ShapeDtypeStruct((t, d), F32),
            input_output_aliases=aliases,
            compiler_params=_params(("parallel",)),
        )(*args)
    return out


def kernel(x_prompt, x_sample, mem_prompt, state_conv, state_lru, state_ret, cache_mem_k, cache_mem_v, norm_mix, norm_ffn, norm_mem, norm_final, w_in, w_mem_kv, conv_w, conv_b, w_r, b_r, w_i, b_i, lru_lambda, w_branch_lru, w_branch_ret, w_branch_mem, w_out, w_grp, b_grp, w_exp_router, b_exp_router, w_gate, w_up, w_down):
    depth = w_in.shape[0]
    assert depth == 1, "the two request groups are chained per layer only for a single-layer trunk"
    assert x_sample.shape[1] == 1
    bp, lp, d = x_prompt.shape
    ns = x_sample.shape[0]
    heads, dk, dv = state_ret.shape[2], state_ret.shape[3], state_ret.shape[4]
    mheads, mdh = cache_mem_k.shape[3], cache_mem_k.shape[4]
    dims = dict(d=d, w=state_lru.shape[2], heads=heads, dk=dk, dv=dv, hk=heads * dk, hv=heads * dv,
                mheads=mheads, hm=mheads * mdh, g=w_exp_router.shape[1], e=w_exp_router.shape[3])
    layer = 0
    p = dict(norm_mix=norm_mix[layer], norm_ffn=norm_ffn[layer], norm_mem=norm_mem[layer], w_in=w_in[layer],
             w_mem_kv=w_mem_kv[layer], conv_w=conv_w[layer], conv_b=conv_b[layer], w_r=w_r[layer], b_r=b_r[layer],
             w_i=w_i[layer], b_i=b_i[layer], lru_lambda=lru_lambda[layer], w_branch_lru=w_branch_lru[layer],
             w_branch_ret=w_branch_ret[layer], w_branch_mem=w_branch_mem[layer], w_out=w_out[layer],
             w_grp=w_grp[layer], b_grp=b_grp[layer], w_exp_router=w_exp_router[layer],
             b_exp_router=b_exp_router[layer], w_gate=w_gate[layer], w_up=w_up[layer], w_down=w_down[layer])
    lw = _prep_layer(p, dims)

    x1p, xn2, route_p, counts, conv_p, lru_p, ret_p, mk, mv = _prompt_mixer(x_prompt, mem_prompt, lw, dims, ns)
    x1s, xn2, route_s, counts, conv_s, lru_s, ret_s = _sample_mixer(
        x_sample[:, 0], state_conv[layer], state_lru[layer], state_ret[layer], cache_mem_k[layer],
        cache_mem_v[layer], counts, xn2, lw, dims)

    tp = bp * lp
    route_p = route_p.reshape(tp, LANES)
    (ys_p, dest_p), (ys_s, dest_s) = _moe(xn2, [route_p, route_s], counts, lw, dims)
    yp = _final(x1p.reshape(tp, d), ys_p, dest_p, route_p, _row(norm_final), 2).reshape(bp, lp, d)
    ys = _final(x1s, ys_s, dest_s, route_s, _row(norm_final), 1).reshape(ns, 1, d)
    return (yp, ys, conv_p[None], lru_p[None], ret_p[None], mk[None], mv[None],
            conv_s[None], lru_s[None], ret_s[None])
```

```python
import functools
import itertools
import math

import jax
import jax.numpy as jnp
from jax import lax
from jax.experimental import pallas as pl
from jax.experimental.pallas import tpu as pltpu

F32 = jnp.float32
BF16 = jnp.bfloat16

EPS = 1e-6
LRU_C = 8.0
ROPE_BASE = 10000.0
PAST_LEN = 16384
TOP_K = 2
LANES = 128
SUBLANES = 8
VMEM_LIMIT = 56 * 1024 * 1024


def _dot(a, b):
    return jnp.dot(a, b, preferred_element_type=F32)


def _dot_nt(a, b):
    return lax.dot_general(a, b, (((1,), (1,)), ((), ())), preferred_element_type=F32)


def _dot_tn(a, b):
    return lax.dot_general(a, b, (((0,), (0,)), ((), ())), preferred_element_type=F32)


def _rms(x, g):
    return x * lax.rsqrt(jnp.mean(x * x, axis=-1, keepdims=True) + EPS) * g


def _sigmoid(x):
    return 0.5 * jnp.tanh(0.5 * x) + 0.5


def _silu(x):
    return x * _sigmoid(x)


def _gelu_tanh(x):
    return 0.5 * x * (1.0 + jnp.tanh(math.sqrt(2.0 / math.pi) * (x + 0.044715 * (x * x * x))))


def _softplus(x):
    return jnp.maximum(x, 0.0) + jnp.log1p(jnp.exp(-jnp.abs(x)))


def _const_spec(shape):
    nd = len(shape)
    return pl.BlockSpec(shape, lambda *_: (0,) * nd, pipeline_mode=pl.Buffered(1))


def _window_spec(rows, window):
    start, width = window
    return pl.BlockSpec((rows, width), lambda *_: (0, start // width), pipeline_mode=pl.Buffered(1))


def _params(sem):
    return pltpu.CompilerParams(dimension_semantics=sem, vmem_limit_bytes=VMEM_LIMIT)


def _lru_coeffs(uc, wr_ref, br, wi_ref, bi, lam):
    nb, bs = wr_ref.shape[0], wr_ref.shape[1]
    ucb = uc.astype(BF16)
    r_lin = jnp.concatenate([_dot(ucb[:, n * bs:(n + 1) * bs], wr_ref[n]) for n in range(nb)], axis=-1)
    i_lin = jnp.concatenate([_dot(ucb[:, n * bs:(n + 1) * bs], wi_ref[n]) for n in range(nb)], axis=-1)
    r = _sigmoid(r_lin + br)
    i = _sigmoid(i_lin + bi)
    log_a = (-LRU_C) * r * _softplus(-lam)
    a = jnp.exp(log_a)
    t = 1.0 - a * a
    mult = t * lax.rsqrt(jnp.maximum(t, 1e-37))
    return a, mult, i


def _scan_rows(a, b, h0):
    rows, width = a.shape
    grp = SUBLANES
    n_grp = rows // grp
    a = a.reshape(n_grp, grp, width)
    b = b.reshape(n_grp, grp, width)
    row = lax.broadcasted_iota(jnp.int32, a.shape, 1)
    d = 1
    while d < grp:
        a_sh = pltpu.roll(a, d, 1)
        b_sh = pltpu.roll(b, d, 1)
        keep = row >= d
        b = jnp.where(keep, b + a * b_sh, b)
        a = jnp.where(keep, a * a_sh, a)
        d *= 2
    out = []
    for g in range(n_grp):
        hg = a[g] * h0 + b[g]
        out.append(hg)
        h0 = hg[grp - 1:grp, :]
    return jnp.concatenate(out, axis=0)


def _rotary(t, cos, sin):
    half = t.shape[-1] // 2
    t1, t2 = t[:, :half], t[:, half:]
    return jnp.concatenate([t1 * cos - t2 * sin, t2 * cos + t1 * sin], axis=-1)


def _route(logits, run, n_groups, per_group):
    col = lax.broadcasted_iota(jnp.int32, logits.shape, 1)
    big = jnp.int32(1 << 20)
    neg = jnp.float32(-jnp.inf)
    gl = jnp.where(col < n_groups, logits, neg)
    gmax = jnp.max(gl, axis=-1, keepdims=True)
    g_idx = jnp.min(jnp.where(gl == gmax, col, big), axis=-1, keepdims=True)
    g_val = 1.0 / jnp.sum(jnp.exp(gl - gmax), axis=-1, keepdims=True)
    lo = n_groups + per_group * g_idx
    el = jnp.where((col >= lo) & (col < lo + per_group), logits, neg)
    m1 = jnp.max(el, axis=-1, keepdims=True)
    i1 = jnp.min(jnp.where(el == m1, col, big), axis=-1, keepdims=True)
    el2 = jnp.where(col == i1, neg, el)
    m2 = jnp.max(el2, axis=-1, keepdims=True)
    i2 = jnp.min(jnp.where(el2 == m2, col, big), axis=-1, keepdims=True)
    e2 = jnp.exp(m2 - m1)
    w1 = g_val / (1.0 + e2)
    w2 = g_val * e2 / (1.0 + e2)
    f1 = (i1 - n_groups).astype(F32)
    f2 = (i2 - n_groups).astype(F32)
    hit1, hit2 = col == i1, col == i2
    hits = jnp.where(hit1 | hit2, 1.0, 0.0)
    rows = logits.shape[0]
    earlier = lax.broadcasted_iota(jnp.int32, (rows, rows), 1) < lax.broadcasted_iota(jnp.int32, (rows, rows), 0)
    before = run + _dot(jnp.where(earlier, 1.0, 0.0).astype(BF16), hits.astype(BF16))
    r1 = jnp.sum(jnp.where(hit1, before, 0.0), axis=-1, keepdims=True)
    r2 = jnp.sum(jnp.where(hit2, before, 0.0), axis=-1, keepdims=True)
    route = f1
    for j, val in enumerate((f2, w1, w2, r1, r2), start=1):
        route = jnp.where(col == j, val, route)
    route = jnp.where(col > 5, 0.0, route)
    return route, run + jnp.sum(hits, axis=0, keepdims=True)


def _pack_pairs(x):
    half = x.shape[1] // 2
    lo = lax.bitcast_convert_type(x[:, :half].astype(F32), jnp.uint32)
    hi = lax.bitcast_convert_type(x[:, half:].astype(F32), jnp.uint32)
    return lax.bitcast_convert_type((lo >> 16) | (hi & jnp.uint32(0xFFFF0000)), F32)


def _unpack_pairs(w):
    u = lax.bitcast_convert_type(w, jnp.uint32)
    lo = lax.bitcast_convert_type(u << 16, F32)
    hi = lax.bitcast_convert_type(u & jnp.uint32(0xFFFF0000), F32)
    return jnp.concatenate([lo, hi], axis=1)


def _merge_tail(x, merged, wout_ref, gffn, wrt_ref, brt, run, n_groups, per_group):
    x1 = x + _dot(merged.astype(BF16), wout_ref[...])
    xn2 = _rms(x1, gffn).astype(BF16)
    logits = _dot(xn2, wrt_ref[...]) + brt
    route, run = _route(logits, run, n_groups, per_group)
    return x1, _pack_pairs(xn2), route, run


def _lru_prompt_steps(c, xn, wxy_ref, wga_ref, cw_ref, cb_ref, wr_ref, br_ref, wi_ref, bi_ref, lam_ref, wbr_ref,
                      out_ref, ccar, hcar, res):
    tl = xn.shape[0]
    width = hcar.shape[1]
    kw = cw_ref.shape[0]
    nb, bs = wr_ref.shape[0], wr_ref.shape[1]
    z = _dot(xn, wxy_ref[...])
    row = lax.broadcasted_iota(jnp.int32, (tl, bs), 0)
    hbs, lasts = [], []
    for n in range(nb):
        cs = slice(n * bs, (n + 1) * bs)
        ux = z[:, cs]
        acc = cw_ref[0:1, cs] * ux
        for j in range(1, kw):
            prev = jnp.where(row == 0, ccar[j - 1:j, cs], pltpu.roll(acc, 1, 0))
            ccar[j - 1:j, cs] = acc[tl - 1:tl, :]
            acc = prev + cw_ref[j:j + 1, cs] * ux
        uc = acc + cb_ref[:, cs]
        ucb = uc.astype(BF16)
        r = _sigmoid(_dot(ucb, wr_ref[n]) + br_ref[:, cs])
        gate = _sigmoid(_dot(ucb, wi_ref[n]) + bi_ref[:, cs])
        log_a = (-LRU_C) * r * _softplus(-lam_ref[:, cs])
        a = jnp.exp(log_a)
        t = 1.0 - a * a
        mult = t * lax.rsqrt(jnp.maximum(t, 1e-37))
        mult = jnp.where(row + c * tl == 0, 1.0, mult)
        h = _scan_rows(a, mult * gate * uc, hcar[:, cs])
        hcar[:, cs] = h[tl - 1:tl, :]
        lasts.append(h[tl - 1:tl, :])
        hbs.append((h * _gelu_tanh(z[:, width + n * bs: width + (n + 1) * bs])).astype(BF16))
        yield
    res['tail'] = z[tl - (kw - 1):tl, :width]
    res['h_last'] = jnp.concatenate(lasts, axis=1)
    a_out = _dot(jnp.concatenate(hbs, axis=1), wbr_ref[...])
    ga = _dot(xn, wga_ref[...])
    out_ref[0] = (_sigmoid(ga) * a_out).astype(out_ref.dtype)
    yield


def _ret_proj_steps(xn, w_refs, res, cols=512):
    for name, w_ref in w_refs:
        parts = []
        for j in range(0, w_ref.shape[1], cols):
            parts.append(_dot(xn, w_ref[:, j:j + cols]))
            yield
        res[name] = jnp.concatenate(parts, axis=1)


def _ret_prompt_steps(xn, proj, wgb_ref, cos_ref, sin_ref, wbr_ref, out_ref, s_scr, y_scr, heads, chunk):
    tl = xn.shape[0]
    q, k, v, g = proj['q'], proj['k'], proj['v'], proj['g']
    dk = q.shape[1] // heads
    dv = v.shape[1] // heads
    cos, sin = cos_ref[...], sin_ref[...]
    n_i = lax.broadcasted_iota(jnp.int32, (chunk, chunk), 0)
    m_i = lax.broadcasted_iota(jnp.int32, (chunk, chunk), 1)
    diff = (n_i - m_i).astype(F32)
    rowk = lax.broadcasted_iota(jnp.int32, (chunk, dk), 0).astype(F32)
    for h in range(heads):
        log_g = math.log1p(-(2.0 ** (-5.0 - h)))
        dmask = jnp.where(diff >= 0, jnp.exp(jnp.maximum(diff, 0.0) * log_g), 0.0)
        q_decay = jnp.exp((rowk + 1.0) * log_g)
        k_decay = jnp.exp((chunk - 1.0 - rowk) * log_g)
        s_decay = math.exp(chunk * log_g)
        for sub in range(tl // chunk):
            r0 = sub * chunk
            cs, sn = cos[r0:r0 + chunk], sin[r0:r0 + chunk]
            qr = _rotary(q[r0:r0 + chunk, h * dk:(h + 1) * dk], cs, sn)
            kr = _rotary(k[r0:r0 + chunk, h * dk:(h + 1) * dk], cs, sn) * (dk ** -0.5)
            vh = v[r0:r0 + chunk, h * dv:(h + 1) * dv].astype(BF16)
            s = s_scr[h]
            att = _dot_nt(qr.astype(BF16), kr.astype(BF16)) * dmask
            o = _dot(att.astype(BF16), vh) + _dot((qr * q_decay).astype(BF16), s.astype(BF16))
            s_scr[h] = s * s_decay + _dot_tn((kr * k_decay).astype(BF16), vh)
            o = o * lax.rsqrt(jnp.mean(o * o, axis=-1, keepdims=True) + EPS)
            gh = g[r0:r0 + chunk, h * dv:(h + 1) * dv]
            y_scr[r0:r0 + chunk, h * dv:(h + 1) * dv] = (_silu(gh) * o).astype(BF16)
            yield
    b_out = _dot(y_scr[...], wbr_ref[...])
    gb = _dot(xn, wgb_ref[...])
    out_ref[0] = (_sigmoid(gb) * b_out).astype(out_ref.dtype)


def _lru_ret_prompt_kernel(x_ref, g_ref, wxy_ref, wga_ref, cw_ref, cb_ref, wr_ref, br_ref, wi_ref, bi_ref,
                           lam_ref, wbl_ref, wq_ref, wk_ref, wv_ref, wg_ref, wgb_ref, cos_ref, sin_ref, wbr_ref,
                           a_ref, conv_ref, h_ref, b_ref, s_out_ref, ccar, hcar, s_scr, y_scr, *, heads, chunk):
    c = pl.program_id(1)

    @pl.when(c == 0)
    def _():
        ccar[...] = jnp.zeros_like(ccar)
        hcar[...] = jnp.zeros_like(hcar)
        s_scr[...] = jnp.zeros_like(s_scr)

    xn = _rms(x_ref[0], g_ref[...]).astype(BF16)
    lru, proj = {}, {}
    lru_steps = _lru_prompt_steps(c, xn, wxy_ref, wga_ref, cw_ref, cb_ref, wr_ref, br_ref, wi_ref, bi_ref,
                                  lam_ref, wbl_ref, a_ref, ccar, hcar, lru)
    proj_steps = _ret_proj_steps(xn, (('q', wq_ref), ('k', wk_ref), ('v', wv_ref), ('g', wg_ref)), proj)
    n_lru = wr_ref.shape[0] + 1
    n_proj = sum(-(-r.shape[1] // 512) for r in (wq_ref, wk_ref, wv_ref, wg_ref))
    n_ret = heads * (x_ref.shape[1] // chunk)
    lru_done = 0
    for i in range(n_proj):
        next(proj_steps)
        want = ((i + 1) * n_lru) // (n_proj + n_ret)
        while lru_done < want:
            next(lru_steps)
            lru_done += 1
    for _ in proj_steps:
        pass
    ret_steps = _ret_prompt_steps(xn, proj, wgb_ref, cos_ref, sin_ref, wbr_ref, b_ref, s_scr, y_scr, heads, chunk)
    for i in range(n_ret):
        next(ret_steps)
        want = ((n_proj + i + 1) * n_lru) // (n_proj + n_ret)
        while lru_done < want:
            next(lru_steps)
            lru_done += 1
    for _ in itertools.chain(lru_steps, ret_steps):
        pass

    @pl.when(c == pl.num_programs(1) - 1)
    def _():
        conv_ref[0] = lru['tail']
        h_ref[0] = lru['h_last']
        s_out_ref[0] = s_scr[...]


def _memkv_kernel(m_ref, g_ref, w_ref, k_ref, v_ref, k4_ref, v4_ref):
    width = k_ref.shape[-1]
    heads, dh = k4_ref.shape[2], k4_ref.shape[3]
    mn = _rms(m_ref[0], g_ref[...]).astype(BF16)
    kv = _dot(mn, w_ref[...])
    k_ref[0] = kv[:, :width]
    v_ref[0] = kv[:, width:]
    for h in range(heads):
        k4_ref[0, :, h, :] = kv[:, h * dh:(h + 1) * dh]
        v4_ref[0, :, h, :] = kv[:, width + h * dh: width + (h + 1) * dh]


def _attn_merge_prompt_kernel(x_ref, ag_ref, bg_ref, mk_ref, mv_ref, g_ref, wmq_ref, wgc_ref, wbm_ref,
                              wout_ref, gffn_ref, wrt_ref, brt_ref, x1_ref, xn2_ref, route_ref, cnt_ref,
                              *, heads, n_groups, per_group):
    @pl.when((pl.program_id(0) == 0) & (pl.program_id(1) == 0))
    def _():
        cnt_ref[...] = jnp.zeros_like(cnt_ref)

    x = x_ref[0]
    xn = _rms(x, g_ref[...]).astype(BF16)
    mq = _dot(xn, wmq_ref[...]).astype(BF16)
    dh = mq.shape[1] // heads
    mk = mk_ref[0].astype(BF16)
    mv = mv_ref[0].astype(BF16)
    cs = []
    for h in range(heads):
        hs = slice(h * dh, (h + 1) * dh)
        lg = _dot_nt(mq[:, hs], mk[:, hs]) * (dh ** -0.5)
        p = jnp.exp(lg - jnp.max(lg, axis=-1, keepdims=True))
        den = jnp.sum(p, axis=-1, keepdims=True)
        cs.append(_dot(p.astype(BF16), mv[:, hs]) / den)
    c_out = _dot(jnp.concatenate(cs, axis=-1).astype(BF16), wbm_ref[...])
    gc = _dot(xn, wgc_ref[...])
    merged = _sigmoid(gc) * c_out + ag_ref[0].astype(F32) + bg_ref[0].astype(F32)
    x1, xn2, route, run = _merge_tail(x, merged, wout_ref, gffn_ref[...], wrt_ref, brt_ref[...], cnt_ref[...],
                                      n_groups, per_group)
    x1_ref[0] = x1
    xn2_ref[...] = xn2
    route_ref[0] = route
    cnt_ref[...] = run


def _proj_kernel(x_ref, g_ref, w_ref, z_ref):
    xn = _rms(x_ref[...], g_ref[...]).astype(BF16)
    z_ref[...] = _dot(xn, w_ref[...])


def _lru_sample_kernel(ux_ref, uy_ref, ga_ref, cprev_ref, hprev_ref, cw_ref, cb_ref, wr_ref, br_ref, wi_ref,
                       bi_ref, lam_ref, wbr_ref, out_ref, conv_ref, h_ref, *, at_start):
    kw = cw_ref.shape[0]
    ux = ux_ref[...]
    uc = cb_ref[...] + cw_ref[kw - 1:kw, :] * ux
    for j in range(kw - 1):
        uc = uc + cw_ref[j:j + 1, :] * cprev_ref[:, j, :]
    a, mult, gate = _lru_coeffs(uc, wr_ref, br_ref[...], wi_ref, bi_ref[...], lam_ref[...])
    if at_start:
        mult = jnp.ones_like(mult)
    h = a * hprev_ref[...] + mult * gate * uc
    h_ref[...] = h
    for j in range(kw - 2):
        conv_ref[:, j, :] = cprev_ref[:, j + 1, :]
    conv_ref[:, kw - 2, :] = ux
    a_out = _dot((h * _gelu_tanh(uy_ref[...])).astype(BF16), wbr_ref[...])
    out_ref[...] = _sigmoid(ga_ref[...]) * a_out


def _rot_sample_kernel(q_ref, k_ref, cos_ref, sin_ref, qo_ref, ko_ref, *, heads):
    dk = q_ref.shape[1] // heads
    cos, sin = cos_ref[...], sin_ref[...]
    for h in range(heads):
        hs = slice(h * dk, (h + 1) * dk)
        qo_ref[:, hs] = _rotary(q_ref[:, hs], cos, sin)
        ko_ref[:, hs] = _rotary(k_ref[:, hs], cos, sin) * (dk ** -0.5)


def _ret_sample_kernel(qt_ref, kt_ref, q_ref, k_ref, v_ref, s_ref, o_ref, s_out_ref, *, heads):
    bb = q_ref.shape[1]
    dk = s_ref.shape[2]
    dv = s_ref.shape[3]
    for j in range(bb):
        for h in range(heads):
            decay = 1.0 - 2.0 ** (-5.0 - h)
            qcol = qt_ref[0, h, :, j:j + 1]
            kcol = kt_ref[0, h, :, j:j + 1]
            qrow = q_ref[0, j:j + 1, h * dk:(h + 1) * dk]
            krow = k_ref[0, j:j + 1, h * dk:(h + 1) * dk]
            vrow = v_ref[0, j:j + 1, h * dv:(h + 1) * dv]
            s = s_ref[j, h]
            att = jnp.sum(qrow * krow, axis=-1, keepdims=True)
            o = att * vrow + jnp.sum((qcol * decay) * s, axis=0, keepdims=True)
            o_ref[0, j:j + 1, h * dv:(h + 1) * dv] = o
            s_out_ref[j, h] = s * decay + kcol * vrow


def _attn_sample_kernel(mq_ref, k_ref, v_ref, c_ref, *, heads):
    bb, _, dh = mq_ref.shape
    for j in range(bb):
        q = mq_ref[j]
        lg = jnp.sum(k_ref[j] * q[None], axis=-1, keepdims=True) * (dh ** -0.5)
        p = jnp.exp(lg - jnp.max(lg, axis=0, keepdims=True))
        den = jnp.sum(p, axis=0)
        c_ref[j] = jnp.sum(v_ref[j] * p, axis=0) / den


def _merge_sample_kernel(x_ref, ag_ref, o_ref, g_ref, c_ref, gb_ref, gc_ref, wbr_ref, wbm_ref, wout_ref,
                         gffn_ref, wrt_ref, brt_ref, run_ref, xn2_table_ref, x1_ref, xn2_ref, route_ref, cnt_ref,
                         *, heads, n_groups, per_group):
    del xn2_table_ref
    dv = o_ref.shape[1] // heads
    ys = []
    for h in range(heads):
        o = o_ref[:, h * dv:(h + 1) * dv]
        o = o * lax.rsqrt(jnp.mean(o * o, axis=-1, keepdims=True) + EPS)
        ys.append((_silu(g_ref[:, h * dv:(h + 1) * dv]) * o).astype(BF16))
    b_out = _dot(jnp.concatenate(ys, axis=-1), wbr_ref[...])
    c_out = _dot(c_ref[...].astype(BF16), wbm_ref[...])
    merged = ag_ref[...] + _sigmoid(gb_ref[...]) * b_out + _sigmoid(gc_ref[...]) * c_out
    x1, xn2, route, run = _merge_tail(x_ref[...], merged, wout_ref, gffn_ref[...], wrt_ref, brt_ref[...],
                                      run_ref[...], n_groups, per_group)
    x1_ref[...] = x1
    xn2_ref[...] = xn2
    route_ref[...] = route
    cnt_ref[...] = run


def _expert_kernel(te_ref, nu_ref, xs_ref, wg_ref, wu_ref, wd_ref, *rest):
    ys_ref = rest[-1]
    i = pl.program_id(0)

    @pl.when(i < nu_ref[0])
    def _():
        x = _unpack_pairs(xs_ref[...]).astype(BF16)
        hg = _dot(x, wg_ref[0].astype(BF16))
        hu = _dot(x, wu_ref[0].astype(BF16))
        hid = (_silu(hg) * hu).astype(BF16)
        ys_ref[...] = _pack_pairs(_dot(hid, wd_ref[0].astype(BF16)).astype(BF16))

    @pl.when(i >= nu_ref[0])
    def _():
        ys_ref[...] = jnp.zeros_like(ys_ref)


def _final_kernel(x1_ref, y1_ref, y2_ref, route_ref, g_ref, *rest):
    out_ref = rest[-1]
    w1 = route_ref[:, 2:3]
    w2 = route_ref[:, 3:4]
    x2 = x1_ref[...] + (w1 * _unpack_pairs(y1_ref[...]) + w2 * _unpack_pairs(y2_ref[...]))
    out_ref[...] = _rms(x2, g_ref[...])


def _tile(n, target):
    t = min(n, target)
    while n % t:
        t //= 2
    return t


def _row(v):
    return v.reshape(1, -1).astype(F32)


def _prep_layer(p, dims):
    d, w, hk, hv, hm = dims['d'], dims['w'], dims['hk'], dims['hv'], dims['hm']
    w_in = p['w_in']
    starts = {}
    o0 = 0
    for name, width in (('xy', 2 * w), ('q', hk), ('k', hk), ('v', hv), ('g', hv), ('mq', hm),
                        ('ga', d), ('gb', d), ('gc', d)):
        assert o0 % width == 0, "each window must start on a multiple of its width"
        starts[name] = (o0, width)
        o0 += width
    g, e = p['w_exp_router'].shape[0], p['w_exp_router'].shape[2]
    w_rt = jnp.concatenate([p['w_grp'], jnp.moveaxis(p['w_exp_router'], 0, 1).reshape(d, g * e)], axis=1)
    w_rt = jnp.pad(w_rt, ((0, 0), (0, LANES - w_rt.shape[1]))).astype(BF16)
    b_rt = jnp.pad(jnp.concatenate([p['b_grp'], p['b_exp_router'].reshape(-1)]), (0, LANES - g - g * e))
    return dict(
        w_in=w_in.astype(BF16), win=starts,
        norm_mix=_row(p['norm_mix']), norm_ffn=_row(p['norm_ffn']), norm_mem=_row(p['norm_mem']),
        w_mem_kv=p['w_mem_kv'].astype(BF16),
        conv_w=p['conv_w'].astype(F32), conv_b=_row(p['conv_b']),
        w_r=p['w_r'].astype(BF16), b_r=_row(p['b_r']), w_i=p['w_i'].astype(BF16), b_i=_row(p['b_i']),
        lam=_row(p['lru_lambda']),
        w_br_lru=p['w_branch_lru'].astype(BF16), w_br_ret=p['w_branch_ret'].astype(BF16),
        w_br_mem=p['w_branch_mem'].astype(BF16), w_out=p['w_out'].astype(BF16),
        w_rt=w_rt, b_rt=_row(b_rt),
        w_gate=p['w_gate'].reshape((g * e,) + p['w_gate'].shape[2:]),
        w_up=p['w_up'].reshape((g * e,) + p['w_up'].shape[2:]),
        w_down=p['w_down'].reshape((g * e,) + p['w_down'].shape[2:]),
    )


def _rope_tables(pos, dk):
    half = dk // 2
    inv = ROPE_BASE ** (-jnp.linspace(0.0, 1.0, half, dtype=F32))
    ang = pos.astype(F32)[:, None] * inv[None, :]
    return jnp.cos(ang), jnp.sin(ang)


def _prompt_mixer(x, mem, lw, dims, extra_rows):
    b, l, d = x.shape
    w, heads, dk, dv = dims['w'], dims['heads'], dims['dk'], dims['dv']
    kw = lw['conv_w'].shape[0]
    tl = _tile(l, 256)
    grid = (b, l // tl)
    xspec = pl.BlockSpec((1, tl, d), lambda i, c: (i, c, 0))
    sem = ("parallel", "arbitrary")
    w_in, win = lw['w_in'], lw['win']

    chunk = _tile(tl, 256)
    cos, sin = _rope_tables(jnp.arange(l, dtype=jnp.int32), dk)
    half = dk // 2
    a_g, conv_new, h_last, b_g, s_new = pl.pallas_call(
        functools.partial(_lru_ret_prompt_kernel, heads=heads, chunk=chunk),
        grid=grid,
        in_specs=[xspec, _const_spec((1, d)), _window_spec(d, win['xy']), _window_spec(d, win['ga']),
                  _const_spec((kw, w)), _const_spec((1, w)), _const_spec(lw['w_r'].shape), _const_spec((1, w)),
                  _const_spec(lw['w_i'].shape), _const_spec((1, w)), _const_spec((1, w)), _const_spec((w, d)),
                  _window_spec(d, win['q']), _window_spec(d, win['k']),
                  _window_spec(d, win['v']), _window_spec(d, win['g']), _window_spec(d, win['gb']),
                  pl.BlockSpec((tl, half), lambda i, c: (c, 0)), pl.BlockSpec((tl, half), lambda i, c: (c, 0)),
                  _const_spec((heads * dv, d))],
        out_specs=[pl.BlockSpec((1, tl, d), lambda i, c: (i, c, 0)),
                   pl.BlockSpec((1, kw - 1, w), lambda i, c: (i, 0, 0)),
                   pl.BlockSpec((1, 1, w), lambda i, c: (i, 0, 0)),
                   pl.BlockSpec((1, tl, d), lambda i, c: (i, c, 0)),
                   pl.BlockSpec((1, heads, dk, dv), lambda i, c: (i, 0, 0, 0))],
        out_shape=[jax.ShapeDtypeStruct((b, l, d), BF16), jax.ShapeDtypeStruct((b, kw - 1, w), F32),
                   jax.ShapeDtypeStruct((b, 1, w), F32),
                   jax.ShapeDtypeStruct((b, l, d), BF16), jax.ShapeDtypeStruct((b, heads, dk, dv), F32)],
        scratch_shapes=[pltpu.VMEM((SUBLANES, w), F32), pltpu.VMEM((1, w), F32),
                        pltpu.VMEM((heads, dk, dv), F32), pltpu.VMEM((tl, heads * dv), BF16)],
        compiler_params=_params(sem),
    )(x, lw['norm_mix'], w_in, w_in, lw['conv_w'], lw['conv_b'], lw['w_r'], lw['b_r'],
      lw['w_i'], lw['b_i'], lw['lam'], lw['w_br_lru'],
      w_in, w_in, w_in, w_in, w_in, cos, sin, lw['w_br_ret'])

    m = mem.shape[1]
    hm = dims['hm']
    mh = dims['mheads']
    flat = pl.BlockSpec((1, m, hm), lambda i: (i, 0, 0))
    per_head = pl.BlockSpec((1, m, mh, hm // mh), lambda i: (i, 0, 0, 0))
    mk, mv, mk4, mv4 = pl.pallas_call(
        _memkv_kernel,
        grid=(b,),
        in_specs=[pl.BlockSpec((1, m, d), lambda i: (i, 0, 0)), _const_spec((1, d)), _const_spec((d, 2 * hm))],
        out_specs=[flat, flat, per_head, per_head],
        out_shape=[jax.ShapeDtypeStruct((b, m, hm), F32), jax.ShapeDtypeStruct((b, m, hm), F32),
                   jax.ShapeDtypeStruct((b, m, mh, hm // mh), F32), jax.ShapeDtypeStruct((b, m, mh, hm // mh), F32)],
        compiler_params=_params(("parallel",)),
    )(mem, lw['norm_mem'], lw['w_mem_kv'])

    tm = _tile(l, 512)
    mspec = pl.BlockSpec((1, tm, d), lambda i, c: (i, c, 0))
    x1, xn2, route, counts = pl.pallas_call(
        functools.partial(_attn_merge_prompt_kernel, heads=dims['mheads'], n_groups=dims['g'],
                          per_group=dims['e']),
        grid=(b, l // tm),
        in_specs=[mspec, mspec, mspec,
                  pl.BlockSpec((1, m, hm), lambda i, c: (i, 0, 0)), pl.BlockSpec((1, m, hm), lambda i, c: (i, 0, 0)),
                  _const_spec((1, d)), _window_spec(d, win['mq']), _window_spec(d, win['gc']), _const_spec((hm, d)),
                  _const_spec((d, d)), _const_spec((1, d)), _const_spec((d, LANES)), _const_spec((1, LANES))],
        out_specs=[mspec, pl.BlockSpec((tm, d // 2), lambda i, c: (i * (l // tm) + c, 0)),
                   pl.BlockSpec((1, tm, LANES), lambda i, c: (i, c, 0)),
                   pl.BlockSpec((1, LANES), lambda i, c: (0, 0))],
        out_shape=[jax.ShapeDtypeStruct((b, l, d), F32), jax.ShapeDtypeStruct((b * l + extra_rows, d // 2), F32),
                   jax.ShapeDtypeStruct((b, l, LANES), F32), jax.ShapeDtypeStruct((1, LANES), F32)],
        compiler_params=_params(("arbitrary", "arbitrary")),
    )(x, a_g, b_g, mk, mv, lw['norm_mix'], w_in, w_in, lw['w_br_mem'], lw['w_out'],
      lw['norm_ffn'], lw['w_rt'], lw['b_rt'])
    return x1, xn2, route, counts, conv_new, h_last[:, 0], s_new, mk4, mv4


def _sample_mixer(x, conv_prev, h_prev, s_prev, mem_k, mem_v, counts0, xn2_all, lw, dims):
    n, d = x.shape
    w, heads, dk, dv, hm = dims['w'], dims['heads'], dims['dk'], dims['dv'], dims['hm']
    hk, hv = heads * dk, heads * dv
    kw = lw['conv_w'].shape[0]
    n_in = lw['w_in'].shape[1]
    tn = _tile(n_in, 1024)
    z = pl.pallas_call(
        _proj_kernel,
        grid=(n_in // tn,),
        in_specs=[_const_spec((n, d)), _const_spec((1, d)), pl.BlockSpec((d, tn), lambda j: (0, j))],
        out_specs=pl.BlockSpec((n, tn), lambda j: (0, j)),
        out_shape=jax.ShapeDtypeStruct((n, n_in), F32),
        compiler_params=_params(("parallel",)),
    )(x, lw['norm_mix'], lw['w_in'])
    o0 = 0
    parts = []
    for sz in (w, w, hk, hk, hv, hv, hm, d, d, d):
        parts.append(z[:, o0:o0 + sz])
        o0 += sz
    ux, uy, q, k, v, g, mq, ga, gb, gc = parts

    a_g, conv_new, h_new = pl.pallas_call(
        functools.partial(_lru_sample_kernel, at_start=(PAST_LEN == 0)),
        out_shape=[jax.ShapeDtypeStruct((n, d), F32), jax.ShapeDtypeStruct((n, kw - 1, w), F32),
                   jax.ShapeDtypeStruct((n, w), F32)],
        compiler_params=pltpu.CompilerParams(vmem_limit_bytes=VMEM_LIMIT),
    )(ux, uy, ga, conv_prev, h_prev, lw['conv_w'], lw['conv_b'], lw['w_r'], lw['b_r'], lw['w_i'], lw['b_i'],
      lw['lam'], lw['w_br_lru'])

    cos, sin = _rope_tables(PAST_LEN + jnp.arange(1, dtype=jnp.int32), dk)
    qr, kr = pl.pallas_call(
        functools.partial(_rot_sample_kernel, heads=heads),
        out_shape=[jax.ShapeDtypeStruct((n, hk), F32), jax.ShapeDtypeStruct((n, hk), F32)],
    )(q, k, cos, sin)

    bb = _tile(n, 4)

    def cols(t):
        return t.reshape(n // bb, bb, heads, dk).transpose(0, 2, 3, 1)

    def rows(t):
        return t.reshape(n // bb, bb, t.shape[1])

    o, s_new = pl.pallas_call(
        functools.partial(_ret_sample_kernel, heads=heads),
        grid=(n // bb,),
        in_specs=[pl.BlockSpec((1, heads, dk, bb), lambda i: (i, 0, 0, 0)),
                  pl.BlockSpec((1, heads, dk, bb), lambda i: (i, 0, 0, 0)),
                  pl.BlockSpec((1, bb, hk), lambda i: (i, 0, 0)), pl.BlockSpec((1, bb, hk), lambda i: (i, 0, 0)),
                  pl.BlockSpec((1, bb, hv), lambda i: (i, 0, 0)),
                  pl.BlockSpec((bb, heads, dk, dv), lambda i: (i, 0, 0, 0))],
        out_specs=[pl.BlockSpec((1, bb, hv), lambda i: (i, 0, 0)),
                   pl.BlockSpec((bb, heads, dk, dv), lambda i: (i, 0, 0, 0))],
        out_shape=[jax.ShapeDtypeStruct((n // bb, bb, hv), F32), jax.ShapeDtypeStruct(s_prev.shape, F32)],
        compiler_params=_params(("parallel",)),
    )(cols(qr), cols(kr), rows(qr), rows(kr), rows(v), s_prev)
    o = o.reshape(n, hv)
    bb = _tile(n, SUBLANES)

    m = mem_k.shape[1]
    c = pl.pallas_call(
        functools.partial(_attn_sample_kernel, heads=dims['mheads']),
        grid=(n // bb,),
        in_specs=[pl.BlockSpec((bb,) + mem_k.shape[2:], lambda i: (i, 0, 0)),
                  pl.BlockSpec((bb,) + mem_k.shape[1:], lambda i: (i, 0, 0, 0)),
                  pl.BlockSpec((bb,) + mem_v.shape[1:], lambda i: (i, 0, 0, 0))],
        out_specs=pl.BlockSpec((bb,) + mem_k.shape[2:], lambda i: (i, 0, 0)),
        out_shape=jax.ShapeDtypeStruct((n,) + mem_k.shape[2:], F32),
        compiler_params=_params(("parallel",)),
    )(mq.reshape((n,) + mem_k.shape[2:]), mem_k, mem_v).reshape(n, hm)

    first = xn2_all.shape[0] - n
    assert first % n == 0
    args = (x, a_g, o, g, c, gb, gc, lw['w_br_ret'], lw['w_br_mem'], lw['w_out'], lw['norm_ffn'], lw['w_rt'],
            lw['b_rt'], counts0)
    x1, xn2_all, route, counts = pl.pallas_call(
        functools.partial(_merge_sample_kernel, heads=heads, n_groups=dims['g'], per_group=dims['e']),
        grid=(1,),
        in_specs=[_const_spec(a.shape) for a in args] + [pl.BlockSpec(memory_space=pl.ANY)],
        out_specs=[pl.BlockSpec((n, d), lambda i: (0, 0)), pl.BlockSpec((n, d // 2), lambda i: (first // n, 0)),
                   pl.BlockSpec((n, LANES), lambda i: (0, 0)), pl.BlockSpec((1, LANES), lambda i: (0, 0))],
        out_shape=[jax.ShapeDtypeStruct((n, d), F32), jax.ShapeDtypeStruct(xn2_all.shape, F32),
                   jax.ShapeDtypeStruct((n, LANES), F32), jax.ShapeDtypeStruct((1, LANES), F32)],
        input_output_aliases={len(args): 1},
        compiler_params=_params(("arbitrary",)),
    )(*args, xn2_all)
    return x1, xn2_all, route, counts, conv_new, h_new, s_new


def _moe(xn2, route_parts, counts, lw, dims, tm=512, n_pass=4):
    t, d = xn2.shape[0], dims['d']
    g, n_exp = dims['g'], dims['g'] * dims['e']
    ff = lw['w_gate'].shape[2]
    counts = counts[0, g:g + n_exp].astype(jnp.int32)
    padded = ((counts + tm - 1) // tm) * tm
    ends = jnp.cumsum(padded)
    offs = ends - padded
    n_tiles = n_pass * ((TOP_K * t + n_exp * tm + n_pass * tm - 1) // (n_pass * tm))
    n_slots = n_tiles * tm
    per_pass = n_tiles // n_pass
    experts = jnp.arange(n_exp, dtype=jnp.int32)[None, :]
    dests = []
    for j in range(TOP_K):
        per_part = []
        for r in route_parts:
            ids = r[:, j].astype(jnp.int32)
            base = jnp.sum(jnp.where(ids[:, None] == experts, offs[None, :], 0), axis=1)
            per_part.append(base + r[:, 4 + j].astype(jnp.int32))
        dests.append(per_part)
    dest = jnp.concatenate([dp for per_part in dests for dp in per_part], axis=0)
    tok = jnp.arange(t, dtype=jnp.int32)
    n_pad = n_slots - TOP_K * t
    pad_cnt = jnp.concatenate([padded - counts, n_slots - ends[-1:]])
    pad_from = jnp.concatenate([offs + counts, ends[-1:]])
    pad_end = jnp.cumsum(pad_cnt)
    pad_i = jnp.arange(n_pad, dtype=jnp.int32)
    which = jnp.sum((pad_end[None, :] <= pad_i[:, None]).astype(jnp.int32), axis=1)
    shift = pad_from - (pad_end - pad_cnt)
    groups = jnp.arange(n_exp + 1, dtype=jnp.int32)[None, :]
    pad_slot = pad_i + jnp.sum(jnp.where(which[:, None] == groups, shift[None, :], 0), axis=1)
    _, src = lax.sort_key_val(jnp.concatenate([dest, pad_slot]),
                              jnp.concatenate([tok] * TOP_K + [pad_i % t]))
    n_used = (ends[-1] // tm).astype(jnp.int32)
    tile_row = jnp.arange(n_tiles, dtype=jnp.int32) * tm
    tile_e = jnp.sum((ends[None, :] <= tile_row[:, None]).astype(jnp.int32), axis=1)
    last_e = jnp.sum((ends <= (n_used - 1) * tm).astype(jnp.int32))
    tile_e = jnp.where(tile_row < ends[-1], tile_e, last_e)
    def expert_pass(first_tile, ys_table):
        xs = xn2.at[src[first_tile * tm:(first_tile + per_pass) * tm]].get(mode='promise_in_bounds')

        def rows(i, te, nu):
            return (jnp.maximum(jnp.minimum(i, nu[0] - 1), 0), 0)

        def weights(i, te, nu):
            return (te[i], 0, 0)

        in_specs = [pl.BlockSpec((tm, d // 2), rows), pl.BlockSpec((1, d, ff), weights),
                    pl.BlockSpec((1, d, ff), weights), pl.BlockSpec((1, ff, d), weights)]
        args = [xs, lw['w_gate'], lw['w_up'], lw['w_down']]
        aliases = {}
        if ys_table is not None:
            in_specs.append(pl.BlockSpec(memory_space=pl.ANY))
            args.append(ys_table)
            aliases = {2 + len(args) - 1: 0}
        return pl.pallas_call(
            _expert_kernel,
            grid_spec=pltpu.PrefetchScalarGridSpec(
                num_scalar_prefetch=2,
                grid=(per_pass,),
                in_specs=in_specs,
                out_specs=pl.BlockSpec((tm, d // 2), lambda i, te, nu: (first_tile + i, 0)),
            ),
            out_shape=jax.ShapeDtypeStruct((n_slots, d // 2), F32),
            input_output_aliases=aliases,
            compiler_params=_params(("arbitrary",)),
        )(tile_e[first_tile:first_tile + per_pass], jnp.clip(n_used - first_tile, 0, per_pass).reshape(1), *args)

    ys = None
    for p in range(n_pass):
        ys = expert_pass(p * per_pass, ys)
    return [(ys, tuple(dests[j][p] for j in range(TOP_K))) for p in range(len(route_parts))]


def _final(x1, ys, dest_pair, route, g, n_parts):
    t, d = x1.shape
    rows = t // n_parts
    tl = _tile(rows, 512)
    out = None
    for p in range(n_parts):
        first = p * rows // tl
        y1, y2 = (ys.at[dp[p * rows:(p + 1) * rows]].get(mode='promise_in_bounds') for dp in dest_pair)
        row = pl.BlockSpec((tl, d), lambda i, first=first: (first + i, 0))
        packed = pl.BlockSpec((tl, d // 2), lambda i: (i, 0))
        in_specs = [row, packed, packed, pl.BlockSpec((tl, LANES), lambda i, first=first: (first + i, 0)),
                    _const_spec((1, d))]
        args = [x1, y1, y2, route, g]
        aliases = {}
        if out is not None:
            in_specs.append(pl.BlockSpec(memory_space=pl.ANY))
            args.append(out)
            aliases = {len(args) - 1: 0}
        out = pl.pallas_call(
            _final_kernel,
            grid=(rows // tl,),
            in_specs=in_specs,
            out_specs=row,
            out_shape=jax.ShapeDtypeStruct((t, d), F32),
            input_output_aliases=aliases,
            compiler_params=_params(("parallel",)),
        )(*args)
    return out


def kernel(x_prompt, x_sample, mem_prompt, state_conv, state_lru, state_ret, cache_mem_k, cache_mem_v, norm_mix, norm_ffn, norm_mem, norm_final, w_in, w_mem_kv, conv_w, conv_b, w_r, b_r, w_i, b_i, lru_lambda, w_branch_lru, w_branch_ret, w_branch_mem, w_out, w_grp, b_grp, w_exp_router, b_exp_router, w_gate, w_up, w_down):
    depth = w_in.shape[0]
    assert depth == 1, "the two request groups are chained per layer only for a single-layer trunk"
    assert x_sample.shape[1] == 1
    bp, lp, d = x_prompt.shape
    ns = x_sample.shape[0]
    heads, dk, dv = state_ret.shape[2], state_ret.shape[3], state_ret.shape[4]
    mheads, mdh = cache_mem_k.shape[3], cache_mem_k.shape[4]
    dims = dict(d=d, w=state_lru.shape[2], heads=heads, dk=dk, dv=dv, hk=heads * dk, hv=heads * dv,
                mheads=mheads, hm=mheads * mdh, g=w_exp_router.shape[1], e=w_exp_router.shape[3])
    layer = 0
    p = dict(norm_mix=norm_mix[layer], norm_ffn=norm_ffn[layer], norm_mem=norm_mem[layer], w_in=w_in[layer],
             w_mem_kv=w_mem_kv[layer], conv_w=conv_w[layer], conv_b=conv_b[layer], w_r=w_r[layer], b_r=b_r[layer],
             w_i=w_i[layer], b_i=b_i[layer], lru_lambda=lru_lambda[layer], w_branch_lru=w_branch_lru[layer],
             w_branch_ret=w_branch_ret[layer], w_branch_mem=w_branch_mem[layer], w_out=w_out[layer],
             w_grp=w_grp[layer], b_grp=b_grp[layer], w_exp_router=w_exp_router[layer],
             b_exp_router=b_exp_router[layer], w_gate=w_gate[layer], w_up=w_up[layer], w_down=w_down[layer])
    lw = _prep_layer(p, dims)

    x1p, xn2, route_p, counts, conv_p, lru_p, ret_p, mk, mv = _prompt_mixer(x_prompt, mem_prompt, lw, dims, ns)
    x1s, xn2, route_s, counts, conv_s, lru_s, ret_s = _sample_mixer(
        x_sample[:, 0], state_conv[layer], state_lru[layer], state_ret[layer], cache_mem_k[layer],
        cache_mem_v[layer], counts, xn2, lw, dims)

    tp = bp * lp
    route_p = route_p.reshape(tp, LANES)
    (ys_p, dest_p), (ys_s, dest_s) = _moe(xn2, [route_p, route_s], counts, lw, dims)
    yp = _final(x1p.reshape(tp, d), ys_p, dest_p, route_p, _row(norm_final), 4).reshape(bp, lp, d)
    ys = _final(x1s, ys_s, dest_s, route_s, _row(norm_final), 1).reshape(ns, 1, d)
    return (yp, ys, conv_p[None], lru_p[None], ret_p[None], mk[None], mv[None],
            conv_s[None], lru_s[None], ret_s[None])
```

```python
import functools
import itertools
import math

import jax
import jax.numpy as jnp
from jax import lax
from jax.experimental import pallas as pl
from jax.experimental.pallas import tpu as pltpu

F32 = jnp.float32
BF16 = jnp.bfloat16

EPS = 1e-6
LRU_C = 8.0
ROPE_BASE = 10000.0
PAST_LEN = 16384
TOP_K = 2
LANES = 128
SUBLANES = 8
VMEM_LIMIT = 56 * 1024 * 1024


def _dot(a, b):
    return jnp.dot(a, b, preferred_element_type=F32)


def _dot_nt(a, b):
    return lax.dot_general(a, b, (((1,), (1,)), ((), ())), preferred_element_type=F32)


def _dot_tn(a, b):
    return lax.dot_general(a, b, (((0,), (0,)), ((), ())), preferred_element_type=F32)


def _rms(x, g):
    return x * lax.rsqrt(jnp.mean(x * x, axis=-1, keepdims=True) + EPS) * g


def _sigmoid(x):
    return 0.5 * jnp.tanh(0.5 * x) + 0.5


def _silu(x):
    return x * _sigmoid(x)


def _gelu_tanh(x):
    return 0.5 * x * (1.0 + jnp.tanh(math.sqrt(2.0 / math.pi) * (x + 0.044715 * (x * x * x))))


def _softplus(x):
    return jnp.maximum(x, 0.0) + jnp.log1p(jnp.exp(-jnp.abs(x)))


def _const_spec(shape):
    nd = len(shape)
    return pl.BlockSpec(shape, lambda *_: (0,) * nd, pipeline_mode=pl.Buffered(1))


def _window_spec(rows, window):
    start, width = window
    return pl.BlockSpec((rows, width), lambda *_: (0, start // width), pipeline_mode=pl.Buffered(1))


def _params(sem):
    return pltpu.CompilerParams(dimension_semantics=sem, vmem_limit_bytes=VMEM_LIMIT)


def _lru_coeffs(uc, wr_ref, br, wi_ref, bi, lam):
    nb, bs = wr_ref.shape[0], wr_ref.shape[1]
    ucb = uc.astype(BF16)
    r_lin = jnp.concatenate([_dot(ucb[:, n * bs:(n + 1) * bs], wr_ref[n]) for n in range(nb)], axis=-1)
    i_lin = jnp.concatenate([_dot(ucb[:, n * bs:(n + 1) * bs], wi_ref[n]) for n in range(nb)], axis=-1)
    r = _sigmoid(r_lin + br)
    i = _sigmoid(i_lin + bi)
    log_a = (-LRU_C) * r * _softplus(-lam)
    a = jnp.exp(log_a)
    t = 1.0 - a * a
    mult = t * lax.rsqrt(jnp.maximum(t, 1e-37))
    return a, mult, i


def _scan_rows(a, b, h0):
    rows, width = a.shape
    grp = SUBLANES
    n_grp = rows // grp
    a = a.reshape(n_grp, grp, width)
    b = b.reshape(n_grp, grp, width)
    row = lax.broadcasted_iota(jnp.int32, a.shape, 1)
    d = 1
    while d < grp:
        a_sh = pltpu.roll(a, d, 1)
        b_sh = pltpu.roll(b, d, 1)
        keep = row >= d
        b = jnp.where(keep, b + a * b_sh, b)
        a = jnp.where(keep, a * a_sh, a)
        d *= 2
    out = []
    for g in range(n_grp):
        hg = a[g] * h0 + b[g]
        out.append(hg)
        h0 = hg[grp - 1:grp, :]
    return jnp.concatenate(out, axis=0)


def _rotary(t, cos, sin):
    half = t.shape[-1] // 2
    t1, t2 = t[:, :half], t[:, half:]
    return jnp.concatenate([t1 * cos - t2 * sin, t2 * cos + t1 * sin], axis=-1)


def _route(logits, run, n_groups, per_group):
    col = lax.broadcasted_iota(jnp.int32, logits.shape, 1)
    big = jnp.int32(1 << 20)
    neg = jnp.float32(-jnp.inf)
    gl = jnp.where(col < n_groups, logits, neg)
    gmax = jnp.max(gl, axis=-1, keepdims=True)
    g_idx = jnp.min(jnp.where(gl == gmax, col, big), axis=-1, keepdims=True)
    g_val = 1.0 / jnp.sum(jnp.exp(gl - gmax), axis=-1, keepdims=True)
    lo = n_groups + per_group * g_idx
    el = jnp.where((col >= lo) & (col < lo + per_group), logits, neg)
    m1 = jnp.max(el, axis=-1, keepdims=True)
    i1 = jnp.min(jnp.where(el == m1, col, big), axis=-1, keepdims=True)
    el2 = jnp.where(col == i1, neg, el)
    m2 = jnp.max(el2, axis=-1, keepdims=True)
    i2 = jnp.min(jnp.where(el2 == m2, col, big), axis=-1, keepdims=True)
    e2 = jnp.exp(m2 - m1)
    w1 = g_val / (1.0 + e2)
    w2 = g_val * e2 / (1.0 + e2)
    f1 = (i1 - n_groups).astype(F32)
    f2 = (i2 - n_groups).astype(F32)
    hit1, hit2 = col == i1, col == i2
    hits = jnp.where(hit1 | hit2, 1.0, 0.0)
    rows = logits.shape[0]
    earlier = lax.broadcasted_iota(jnp.int32, (rows, rows), 1) < lax.broadcasted_iota(jnp.int32, (rows, rows), 0)
    before = run + _dot(jnp.where(earlier, 1.0, 0.0).astype(BF16), hits.astype(BF16))
    r1 = jnp.sum(jnp.where(hit1, before, 0.0), axis=-1, keepdims=True)
    r2 = jnp.sum(jnp.where(hit2, before, 0.0), axis=-1, keepdims=True)
    route = f1
    for j, val in enumerate((f2, w1, w2, r1, r2), start=1):
        route = jnp.where(col == j, val, route)
    route = jnp.where(col > 5, 0.0, route)
    return route, run + jnp.sum(hits, axis=0, keepdims=True)


def _pack_pairs(x):
    half = x.shape[1] // 2
    lo = lax.bitcast_convert_type(x[:, :half].astype(F32), jnp.uint32)
    hi = lax.bitcast_convert_type(x[:, half:].astype(F32), jnp.uint32)
    return lax.bitcast_convert_type((lo >> 16) | (hi & jnp.uint32(0xFFFF0000)), F32)


def _unpack_pairs(w):
    u = lax.bitcast_convert_type(w, jnp.uint32)
    lo = lax.bitcast_convert_type(u << 16, F32)
    hi = lax.bitcast_convert_type(u & jnp.uint32(0xFFFF0000), F32)
    return jnp.concatenate([lo, hi], axis=1)


def _merge_tail(x, merged, wout_ref, gffn, wrt_ref, brt, run, n_groups, per_group):
    x1 = x + _dot(merged.astype(BF16), wout_ref[...])
    xn2 = _rms(x1, gffn).astype(BF16)
    logits = _dot(xn2, wrt_ref[...]) + brt
    route, run = _route(logits, run, n_groups, per_group)
    return x1, _pack_pairs(xn2), route, run


def _lru_prompt_steps(c, xn, wxy_ref, wga_ref, cw_ref, cb_ref, wr_ref, br_ref, wi_ref, bi_ref, lam_ref, wbr_ref,
                      out_ref, ccar, hcar, res):
    tl = xn.shape[0]
    width = hcar.shape[1]
    kw = cw_ref.shape[0]
    nb, bs = wr_ref.shape[0], wr_ref.shape[1]
    z = _dot(xn, wxy_ref[...])
    row = lax.broadcasted_iota(jnp.int32, (tl, bs), 0)
    hbs, lasts = [], []
    for n in range(nb):
        cs = slice(n * bs, (n + 1) * bs)
        ux = z[:, cs]
        acc = cw_ref[0:1, cs] * ux
        for j in range(1, kw):
            prev = jnp.where(row == 0, ccar[j - 1:j, cs], pltpu.roll(acc, 1, 0))
            ccar[j - 1:j, cs] = acc[tl - 1:tl, :]
            acc = prev + cw_ref[j:j + 1, cs] * ux
        uc = acc + cb_ref[:, cs]
        ucb = uc.astype(BF16)
        r = _sigmoid(_dot(ucb, wr_ref[n]) + br_ref[:, cs])
        gate = _sigmoid(_dot(ucb, wi_ref[n]) + bi_ref[:, cs])
        log_a = (-LRU_C) * r * _softplus(-lam_ref[:, cs])
        a = jnp.exp(log_a)
        t = 1.0 - a * a
        mult = t * lax.rsqrt(jnp.maximum(t, 1e-37))
        mult = jnp.where(row + c * tl == 0, 1.0, mult)
        h = _scan_rows(a, mult * gate * uc, hcar[:, cs])
        hcar[:, cs] = h[tl - 1:tl, :]
        lasts.append(h[tl - 1:tl, :])
        hbs.append((h * _gelu_tanh(z[:, width + n * bs: width + (n + 1) * bs])).astype(BF16))
        yield
    res['tail'] = z[tl - (kw - 1):tl, :width]
    res['h_last'] = jnp.concatenate(lasts, axis=1)
    a_out = _dot(jnp.concatenate(hbs, axis=1), wbr_ref[...])
    ga = _dot(xn, wga_ref[...])
    out_ref[0] = (_sigmoid(ga) * a_out).astype(out_ref.dtype)
    yield


def _ret_proj_steps(xn, w_refs, res, cols=512):
    for name, w_ref in w_refs:
        parts = []
        for j in range(0, w_ref.shape[1], cols):
            parts.append(_dot(xn, w_ref[:, j:j + cols]))
            yield
        res[name] = jnp.concatenate(parts, axis=1)


def _ret_prompt_steps(xn, proj, wgb_ref, cos_ref, sin_ref, wbr_ref, out_ref, s_scr, y_scr, heads, chunk):
    tl = xn.shape[0]
    q, k, v, g = proj['q'], proj['k'], proj['v'], proj['g']
    dk = q.shape[1] // heads
    dv = v.shape[1] // heads
    cos, sin = cos_ref[...], sin_ref[...]
    n_i = lax.broadcasted_iota(jnp.int32, (chunk, chunk), 0)
    m_i = lax.broadcasted_iota(jnp.int32, (chunk, chunk), 1)
    diff = (n_i - m_i).astype(F32)
    rowk = lax.broadcasted_iota(jnp.int32, (chunk, dk), 0).astype(F32)
    for h in range(heads):
        log_g = math.log1p(-(2.0 ** (-5.0 - h)))
        dmask = jnp.where(diff >= 0, jnp.exp(jnp.maximum(diff, 0.0) * log_g), 0.0)
        q_decay = jnp.exp((rowk + 1.0) * log_g)
        k_decay = jnp.exp((chunk - 1.0 - rowk) * log_g)
        s_decay = math.exp(chunk * log_g)
        for sub in range(tl // chunk):
            r0 = sub * chunk
            cs, sn = cos[r0:r0 + chunk], sin[r0:r0 + chunk]
            qr = _rotary(q[r0:r0 + chunk, h * dk:(h + 1) * dk], cs, sn)
            kr = _rotary(k[r0:r0 + chunk, h * dk:(h + 1) * dk], cs, sn) * (dk ** -0.5)
            vh = v[r0:r0 + chunk, h * dv:(h + 1) * dv].astype(BF16)
            s = s_scr[h]
            att = _dot_nt(qr.astype(BF16), kr.astype(BF16)) * dmask
            o = _dot(att.astype(BF16), vh) + _dot((qr * q_decay).astype(BF16), s.astype(BF16))
            s_scr[h] = s * s_decay + _dot_tn((kr * k_decay).astype(BF16), vh)
            o = o * lax.rsqrt(jnp.mean(o * o, axis=-1, keepdims=True) + EPS)
            gh = g[r0:r0 + chunk, h * dv:(h + 1) * dv]
            y_scr[r0:r0 + chunk, h * dv:(h + 1) * dv] = (_silu(gh) * o).astype(BF16)
            yield
    b_out = _dot(y_scr[...], wbr_ref[...])
    gb = _dot(xn, wgb_ref[...])
    out_ref[0] = (_sigmoid(gb) * b_out).astype(out_ref.dtype)


def _lru_ret_prompt_kernel(x_ref, g_ref, wxy_ref, wga_ref, cw_ref, cb_ref, wr_ref, br_ref, wi_ref, bi_ref,
                           lam_ref, wbl_ref, wq_ref, wk_ref, wv_ref, wg_ref, wgb_ref, cos_ref, sin_ref, wbr_ref,
                           a_ref, conv_ref, h_ref, b_ref, s_out_ref, ccar, hcar, s_scr, y_scr, *, heads, chunk):
    c = pl.program_id(1)

    @pl.when(c == 0)
    def _():
        ccar[...] = jnp.zeros_like(ccar)
        hcar[...] = jnp.zeros_like(hcar)
        s_scr[...] = jnp.zeros_like(s_scr)

    xn = _rms(x_ref[0], g_ref[...]).astype(BF16)
    lru, proj = {}, {}
    lru_steps = _lru_prompt_steps(c, xn, wxy_ref, wga_ref, cw_ref, cb_ref, wr_ref, br_ref, wi_ref, bi_ref,
                                  lam_ref, wbl_ref, a_ref, ccar, hcar, lru)
    proj_steps = _ret_proj_steps(xn, (('q', wq_ref), ('k', wk_ref), ('v', wv_ref), ('g', wg_ref)), proj)
    n_lru = wr_ref.shape[0] + 1
    n_proj = sum(-(-r.shape[1] // 512) for r in (wq_ref, wk_ref, wv_ref, wg_ref))
    n_ret = heads * (x_ref.shape[1] // chunk)
    lru_done = 0
    for i in range(n_proj):
        next(proj_steps)
        want = ((i + 1) * n_lru) // (n_proj + n_ret)
        while lru_done < want:
            next(lru_steps)
            lru_done += 1
    for _ in proj_steps:
        pass
    ret_steps = _ret_prompt_steps(xn, proj, wgb_ref, cos_ref, sin_ref, wbr_ref, b_ref, s_scr, y_scr, heads, chunk)
    for i in range(n_ret):
        next(ret_steps)
        want = ((n_proj + i + 1) * n_lru) // (n_proj + n_ret)
        while lru_done < want:
            next(lru_steps)
            lru_done += 1
    for _ in itertools.chain(lru_steps, ret_steps):
        pass

    @pl.when(c == pl.num_programs(1) - 1)
    def _():
        conv_ref[0] = lru['tail']
        h_ref[0] = lru['h_last']
        s_out_ref[0] = s_scr[...]


def _memkv_kernel(m_ref, g_ref, w_ref, k_ref, v_ref, k4_ref, v4_ref):
    width = k_ref.shape[-1]
    heads, dh = k4_ref.shape[2], k4_ref.shape[3]
    mn = _rms(m_ref[0], g_ref[...]).astype(BF16)
    kv = _dot(mn, w_ref[...])
    k_ref[0] = kv[:, :width]
    v_ref[0] = kv[:, width:]
    for h in range(heads):
        k4_ref[0, :, h, :] = kv[:, h * dh:(h + 1) * dh]
        v4_ref[0, :, h, :] = kv[:, width + h * dh: width + (h + 1) * dh]


def _attn_merge_prompt_kernel(x_ref, ag_ref, bg_ref, mk_ref, mv_ref, g_ref, wmq_ref, wgc_ref, wbm_ref,
                              wout_ref, gffn_ref, wrt_ref, brt_ref, x1_ref, xn2_ref, route_ref, cnt_ref,
                              *, heads, n_groups, per_group):
    @pl.when((pl.program_id(0) == 0) & (pl.program_id(1) == 0))
    def _():
        cnt_ref[...] = jnp.zeros_like(cnt_ref)

    x = x_ref[0]
    xn = _rms(x, g_ref[...]).astype(BF16)
    mq = _dot(xn, wmq_ref[...]).astype(BF16)
    dh = mq.shape[1] // heads
    mk = mk_ref[0].astype(BF16)
    mv = mv_ref[0].astype(BF16)
    cs = []
    for h in range(heads):
        hs = slice(h * dh, (h + 1) * dh)
        lg = _dot_nt(mq[:, hs], mk[:, hs]) * (dh ** -0.5)
        p = jnp.exp(lg - jnp.max(lg, axis=-1, keepdims=True))
        den = jnp.sum(p, axis=-1, keepdims=True)
        cs.append(_dot(p.astype(BF16), mv[:, hs]) / den)
    c_out = _dot(jnp.concatenate(cs, axis=-1).astype(BF16), wbm_ref[...])
    gc = _dot(xn, wgc_ref[...])
    merged = _sigmoid(gc) * c_out + ag_ref[0].astype(F32) + bg_ref[0].astype(F32)
    x1, xn2, route, run = _merge_tail(x, merged, wout_ref, gffn_ref[...], wrt_ref, brt_ref[...], cnt_ref[...],
                                      n_groups, per_group)
    x1_ref[0] = x1
    xn2_ref[...] = xn2
    route_ref[0] = route
    cnt_ref[...] = run


def _proj_kernel(x_ref, g_ref, w_ref, z_ref):
    xn = _rms(x_ref[...], g_ref[...]).astype(BF16)
    z_ref[...] = _dot(xn, w_ref[...])


def _lru_sample_kernel(ux_ref, uy_ref, ga_ref, cprev_ref, hprev_ref, cw_ref, cb_ref, wr_ref, br_ref, wi_ref,
                       bi_ref, lam_ref, wbr_ref, out_ref, conv_ref, h_ref, *, at_start):
    kw = cw_ref.shape[0]
    ux = ux_ref[...]
    uc = cb_ref[...] + cw_ref[kw - 1:kw, :] * ux
    for j in range(kw - 1):
        uc = uc + cw_ref[j:j + 1, :] * cprev_ref[:, j, :]
    a, mult, gate = _lru_coeffs(uc, wr_ref, br_ref[...], wi_ref, bi_ref[...], lam_ref[...])
    if at_start:
        mult = jnp.ones_like(mult)
    h = a * hprev_ref[...] + mult * gate * uc
    h_ref[...] = h
    for j in range(kw - 2):
        conv_ref[:, j, :] = cprev_ref[:, j + 1, :]
    conv_ref[:, kw - 2, :] = ux
    a_out = _dot((h * _gelu_tanh(uy_ref[...])).astype(BF16), wbr_ref[...])
    out_ref[...] = _sigmoid(ga_ref[...]) * a_out


def _rot_sample_kernel(q_ref, k_ref, cos_ref, sin_ref, qo_ref, ko_ref, *, heads):
    dk = q_ref.shape[1] // heads
    cos, sin = cos_ref[...], sin_ref[...]
    for h in range(heads):
        hs = slice(h * dk, (h + 1) * dk)
        qo_ref[:, hs] = _rotary(q_ref[:, hs], cos, sin)
        ko_ref[:, hs] = _rotary(k_ref[:, hs], cos, sin) * (dk ** -0.5)


def _ret_sample_kernel(qt_ref, kt_ref, q_ref, k_ref, v_ref, s_ref, o_ref, s_out_ref, *, heads):
    bb = q_ref.shape[1]
    dk = s_ref.shape[2]
    dv = s_ref.shape[3]
    for j in range(bb):
        for h in range(heads):
            decay = 1.0 - 2.0 ** (-5.0 - h)
            qcol = qt_ref[0, h, :, j:j + 1]
            kcol = kt_ref[0, h, :, j:j + 1]
            qrow = q_ref[0, j:j + 1, h * dk:(h + 1) * dk]
            krow = k_ref[0, j:j + 1, h * dk:(h + 1) * dk]
            vrow = v_ref[0, j:j + 1, h * dv:(h + 1) * dv]
            s = s_ref[j, h]
            att = jnp.sum(qrow * krow, axis=-1, keepdims=True)
            o = att * vrow + jnp.sum((qcol * decay) * s, axis=0, keepdims=True)
            o_ref[0, j:j + 1, h * dv:(h + 1) * dv] = o
            s_out_ref[j, h] = s * decay + kcol * vrow


def _attn_sample_kernel(mq_ref, k_ref, v_ref, c_ref, *, heads):
    bb, _, dh = mq_ref.shape
    for j in range(bb):
        q = mq_ref[j]
        lg = jnp.sum(k_ref[j] * q[None], axis=-1, keepdims=True) * (dh ** -0.5)
        p = jnp.exp(lg - jnp.max(lg, axis=0, keepdims=True))
        den = jnp.sum(p, axis=0)
        c_ref[j] = jnp.sum(v_ref[j] * p, axis=0) / den


def _merge_sample_kernel(x_ref, ag_ref, o_ref, g_ref, c_ref, gb_ref, gc_ref, wbr_ref, wbm_ref, wout_ref,
                         gffn_ref, wrt_ref, brt_ref, run_ref, xn2_table_ref, x1_ref, xn2_ref, route_ref, cnt_ref,
                         *, heads, n_groups, per_group):
    del xn2_table_ref
    dv = o_ref.shape[1] // heads
    ys = []
    for h in range(heads):
        o = o_ref[:, h * dv:(h + 1) * dv]
        o = o * lax.rsqrt(jnp.mean(o * o, axis=-1, keepdims=True) + EPS)
        ys.append((_silu(g_ref[:, h * dv:(h + 1) * dv]) * o).astype(BF16))
    b_out = _dot(jnp.concatenate(ys, axis=-1), wbr_ref[...])
    c_out = _dot(c_ref[...].astype(BF16), wbm_ref[...])
    merged = ag_ref[...] + _sigmoid(gb_ref[...]) * b_out + _sigmoid(gc_ref[...]) * c_out
    x1, xn2, route, run = _merge_tail(x_ref[...], merged, wout_ref, gffn_ref[...], wrt_ref, brt_ref[...],
                                      run_ref[...], n_groups, per_group)
    x1_ref[...] = x1
    xn2_ref[...] = xn2
    route_ref[...] = route
    cnt_ref[...] = run


def _expert_kernel(plan_ref, nu_ref, xs_ref, wg_hbm, wu_hbm, wd_hbm, *rest):
    ys_ref, wg_buf, wu_buf, wd_buf, sem = rest[-5:]
    i = pl.program_id(0)
    expert, first, slot, nxt = plan_ref[0, i], plan_ref[1, i], plan_ref[2, i], plan_ref[3, i]
    used = i < nu_ref[0]

    def weight_copies(e, s):
        return (pltpu.make_async_copy(wg_hbm.at[e], wg_buf.at[s], sem.at[0, s]),
                pltpu.make_async_copy(wu_hbm.at[e], wu_buf.at[s], sem.at[1, s]),
                pltpu.make_async_copy(wd_hbm.at[e], wd_buf.at[s], sem.at[2, s]))

    @pl.when(used & (i == 0))
    def _():
        for cp in weight_copies(expert, slot):
            cp.start()

    @pl.when(used & (first == 1))
    def _():
        for cp in weight_copies(expert, slot):
            cp.wait()

    @pl.when(used & (first == 1) & (nxt >= 0))
    def _():
        for cp in weight_copies(nxt, 1 - slot):
            cp.start()

    @pl.when(used)
    def _():
        x = _unpack_pairs(xs_ref[...]).astype(BF16)
        hg = _dot(x, wg_buf[slot].astype(BF16))
        hu = _dot(x, wu_buf[slot].astype(BF16))
        hid = (_silu(hg) * hu).astype(BF16)
        ys_ref[...] = _pack_pairs(_dot(hid, wd_buf[slot].astype(BF16)).astype(BF16))

    @pl.when(jnp.logical_not(used))
    def _():
        ys_ref[...] = jnp.zeros_like(ys_ref)


def _final_kernel(x1_ref, y1_ref, y2_ref, route_ref, g_ref, *rest):
    out_ref = rest[-1]
    w1 = route_ref[:, 2:3]
    w2 = route_ref[:, 3:4]
    x2 = x1_ref[...] + (w1 * _unpack_pairs(y1_ref[...]) + w2 * _unpack_pairs(y2_ref[...]))
    out_ref[...] = _rms(x2, g_ref[...])


def _tile(n, target):
    t = min(n, target)
    while n % t:
        t //= 2
    return t


def _row(v):
    return v.reshape(1, -1).astype(F32)


def _prep_layer(p, dims):
    d, w, hk, hv, hm = dims['d'], dims['w'], dims['hk'], dims['hv'], dims['hm']
    w_in = p['w_in']
    starts = {}
    o0 = 0
    for name, width in (('xy', 2 * w), ('q', hk), ('k', hk), ('v', hv), ('g', hv), ('mq', hm),
                        ('ga', d), ('gb', d), ('gc', d)):
        assert o0 % width == 0, "each window must start on a multiple of its width"
        starts[name] = (o0, width)
        o0 += width
    g, e = p['w_exp_router'].shape[0], p['w_exp_router'].shape[2]
    w_rt = jnp.concatenate([p['w_grp'], jnp.moveaxis(p['w_exp_router'], 0, 1).reshape(d, g * e)], axis=1)
    w_rt = jnp.pad(w_rt, ((0, 0), (0, LANES - w_rt.shape[1]))).astype(BF16)
    b_rt = jnp.pad(jnp.concatenate([p['b_grp'], p['b_exp_router'].reshape(-1)]), (0, LANES - g - g * e))
    return dict(
        w_in=w_in.astype(BF16), win=starts,
        norm_mix=_row(p['norm_mix']), norm_ffn=_row(p['norm_ffn']), norm_mem=_row(p['norm_mem']),
        w_mem_kv=p['w_mem_kv'].astype(BF16),
        conv_w=p['conv_w'].astype(F32), conv_b=_row(p['conv_b']),
        w_r=p['w_r'].astype(BF16), b_r=_row(p['b_r']), w_i=p['w_i'].astype(BF16), b_i=_row(p['b_i']),
        lam=_row(p['lru_lambda']),
        w_br_lru=p['w_branch_lru'].astype(BF16), w_br_ret=p['w_branch_ret'].astype(BF16),
        w_br_mem=p['w_branch_mem'].astype(BF16), w_out=p['w_out'].astype(BF16),
        w_rt=w_rt, b_rt=_row(b_rt),
        w_gate=p['w_gate'].reshape((g * e,) + p['w_gate'].shape[2:]),
        w_up=p['w_up'].reshape((g * e,) + p['w_up'].shape[2:]),
        w_down=p['w_down'].reshape((g * e,) + p['w_down'].shape[2:]),
    )


def _rope_tables(pos, dk):
    half = dk // 2
    inv = ROPE_BASE ** (-jnp.linspace(0.0, 1.0, half, dtype=F32))
    ang = pos.astype(F32)[:, None] * inv[None, :]
    return jnp.cos(ang), jnp.sin(ang)


def _prompt_mixer(x, mem, lw, dims, extra_rows):
    b, l, d = x.shape
    w, heads, dk, dv = dims['w'], dims['heads'], dims['dk'], dims['dv']
    kw = lw['conv_w'].shape[0]
    tl = _tile(l, 256)
    grid = (b, l // tl)
    xspec = pl.BlockSpec((1, tl, d), lambda i, c: (i, c, 0))
    sem = ("parallel", "arbitrary")
    w_in, win = lw['w_in'], lw['win']

    chunk = _tile(tl, 256)
    cos, sin = _rope_tables(jnp.arange(l, dtype=jnp.int32), dk)
    half = dk // 2
    a_g, conv_new, h_last, b_g, s_new = pl.pallas_call(
        functools.partial(_lru_ret_prompt_kernel, heads=heads, chunk=chunk),
        grid=grid,
        in_specs=[xspec, _const_spec((1, d)), _window_spec(d, win['xy']), _window_spec(d, win['ga']),
                  _const_spec((kw, w)), _const_spec((1, w)), _const_spec(lw['w_r'].shape), _const_spec((1, w)),
                  _const_spec(lw['w_i'].shape), _const_spec((1, w)), _const_spec((1, w)), _const_spec((w, d)),
                  _window_spec(d, win['q']), _window_spec(d, win['k']),
                  _window_spec(d, win['v']), _window_spec(d, win['g']), _window_spec(d, win['gb']),
                  pl.BlockSpec((tl, half), lambda i, c: (c, 0)), pl.BlockSpec((tl, half), lambda i, c: (c, 0)),
                  _const_spec((heads * dv, d))],
        out_specs=[pl.BlockSpec((1, tl, d), lambda i, c: (i, c, 0)),
                   pl.BlockSpec((1, kw - 1, w), lambda i, c: (i, 0, 0)),
                   pl.BlockSpec((1, 1, w), lambda i, c: (i, 0, 0)),
                   pl.BlockSpec((1, tl, d), lambda i, c: (i, c, 0)),
                   pl.BlockSpec((1, heads, dk, dv), lambda i, c: (i, 0, 0, 0))],
        out_shape=[jax.ShapeDtypeStruct((b, l, d), BF16), jax.ShapeDtypeStruct((b, kw - 1, w), F32),
                   jax.ShapeDtypeStruct((b, 1, w), F32),
                   jax.ShapeDtypeStruct((b, l, d), BF16), jax.ShapeDtypeStruct((b, heads, dk, dv), F32)],
        scratch_shapes=[pltpu.VMEM((SUBLANES, w), F32), pltpu.VMEM((1, w), F32),
                        pltpu.VMEM((heads, dk, dv), F32), pltpu.VMEM((tl, heads * dv), BF16)],
        compiler_params=_params(sem),
    )(x, lw['norm_mix'], w_in, w_in, lw['conv_w'], lw['conv_b'], lw['w_r'], lw['b_r'],
      lw['w_i'], lw['b_i'], lw['lam'], lw['w_br_lru'],
      w_in, w_in, w_in, w_in, w_in, cos, sin, lw['w_br_ret'])

    m = mem.shape[1]
    hm = dims['hm']
    mh = dims['mheads']
    flat = pl.BlockSpec((1, m, hm), lambda i: (i, 0, 0))
    per_head = pl.BlockSpec((1, m, mh, hm // mh), lambda i: (i, 0, 0, 0))
    mk, mv, mk4, mv4 = pl.pallas_call(
        _memkv_kernel,
        grid=(b,),
        in_specs=[pl.BlockSpec((1, m, d), lambda i: (i, 0, 0)), _const_spec((1, d)), _const_spec((d, 2 * hm))],
        out_specs=[flat, flat, per_head, per_head],
        out_shape=[jax.ShapeDtypeStruct((b, m, hm), F32), jax.ShapeDtypeStruct((b, m, hm), F32),
                   jax.ShapeDtypeStruct((b, m, mh, hm // mh), F32), jax.ShapeDtypeStruct((b, m, mh, hm // mh), F32)],
        compiler_params=_params(("parallel",)),
    )(mem, lw['norm_mem'], lw['w_mem_kv'])

    tm = _tile(l, 512)
    mspec = pl.BlockSpec((1, tm, d), lambda i, c: (i, c, 0))
    x1, xn2, route, counts = pl.pallas_call(
        functools.partial(_attn_merge_prompt_kernel, heads=dims['mheads'], n_groups=dims['g'],
                          per_group=dims['e']),
        grid=(b, l // tm),
        in_specs=[mspec, mspec, mspec,
                  pl.BlockSpec((1, m, hm), lambda i, c: (i, 0, 0)), pl.BlockSpec((1, m, hm), lambda i, c: (i, 0, 0)),
                  _const_spec((1, d)), _window_spec(d, win['mq']), _window_spec(d, win['gc']), _const_spec((hm, d)),
                  _const_spec((d, d)), _const_spec((1, d)), _const_spec((d, LANES)), _const_spec((1, LANES))],
        out_specs=[mspec, pl.BlockSpec((tm, d // 2), lambda i, c: (i * (l // tm) + c, 0)),
                   pl.BlockSpec((1, tm, LANES), lambda i, c: (i, c, 0)),
                   pl.BlockSpec((1, LANES), lambda i, c: (0, 0))],
        out_shape=[jax.ShapeDtypeStruct((b, l, d), F32), jax.ShapeDtypeStruct((b * l + extra_rows, d // 2), F32),
                   jax.ShapeDtypeStruct((b, l, LANES), F32), jax.ShapeDtypeStruct((1, LANES), F32)],
        compiler_params=_params(("arbitrary", "arbitrary")),
    )(x, a_g, b_g, mk, mv, lw['norm_mix'], w_in, w_in, lw['w_br_mem'], lw['w_out'],
      lw['norm_ffn'], lw['w_rt'], lw['b_rt'])
    return x1, xn2, route, counts, conv_new, h_last[:, 0], s_new, mk4, mv4


def _sample_mixer(x, conv_prev, h_prev, s_prev, mem_k, mem_v, counts0, xn2_all, lw, dims):
    n, d = x.shape
    w, heads, dk, dv, hm = dims['w'], dims['heads'], dims['dk'], dims['dv'], dims['hm']
    hk, hv = heads * dk, heads * dv
    kw = lw['conv_w'].shape[0]
    n_in = lw['w_in'].shape[1]
    tn = _tile(n_in, 1024)
    z = pl.pallas_call(
        _proj_kernel,
        grid=(n_in // tn,),
        in_specs=[_const_spec((n, d)), _const_spec((1, d)), pl.BlockSpec((d, tn), lambda j: (0, j))],
        out_specs=pl.BlockSpec((n, tn), lambda j: (0, j)),
        out_shape=jax.ShapeDtypeStruct((n, n_in), F32),
        compiler_params=_params(("parallel",)),
    )(x, lw['norm_mix'], lw['w_in'])
    o0 = 0
    parts = []
    for sz in (w, w, hk, hk, hv, hv, hm, d, d, d):
        parts.append(z[:, o0:o0 + sz])
        o0 += sz
    ux, uy, q, k, v, g, mq, ga, gb, gc = parts

    a_g, conv_new, h_new = pl.pallas_call(
        functools.partial(_lru_sample_kernel, at_start=(PAST_LEN == 0)),
        out_shape=[jax.ShapeDtypeStruct((n, d), F32), jax.ShapeDtypeStruct((n, kw - 1, w), F32),
                   jax.ShapeDtypeStruct((n, w), F32)],
        compiler_params=pltpu.CompilerParams(vmem_limit_bytes=VMEM_LIMIT),
    )(ux, uy, ga, conv_prev, h_prev, lw['conv_w'], lw['conv_b'], lw['w_r'], lw['b_r'], lw['w_i'], lw['b_i'],
      lw['lam'], lw['w_br_lru'])

    cos, sin = _rope_tables(PAST_LEN + jnp.arange(1, dtype=jnp.int32), dk)
    qr, kr = pl.pallas_call(
        functools.partial(_rot_sample_kernel, heads=heads),
        out_shape=[jax.ShapeDtypeStruct((n, hk), F32), jax.ShapeDtypeStruct((n, hk), F32)],
    )(q, k, cos, sin)

    bb = _tile(n, 4)

    def cols(t):
        return t.reshape(n // bb, bb, heads, dk).transpose(0, 2, 3, 1)

    def rows(t):
        return t.reshape(n // bb, bb, t.shape[1])

    o, s_new = pl.pallas_call(
        functools.partial(_ret_sample_kernel, heads=heads),
        grid=(n // bb,),
        in_specs=[pl.BlockSpec((1, heads, dk, bb), lambda i: (i, 0, 0, 0)),
                  pl.BlockSpec((1, heads, dk, bb), lambda i: (i, 0, 0, 0)),
                  pl.BlockSpec((1, bb, hk), lambda i: (i, 0, 0)), pl.BlockSpec((1, bb, hk), lambda i: (i, 0, 0)),
                  pl.BlockSpec((1, bb, hv), lambda i: (i, 0, 0)),
                  pl.BlockSpec((bb, heads, dk, dv), lambda i: (i, 0, 0, 0))],
        out_specs=[pl.BlockSpec((1, bb, hv), lambda i: (i, 0, 0)),
                   pl.BlockSpec((bb, heads, dk, dv), lambda i: (i, 0, 0, 0))],
        out_shape=[jax.ShapeDtypeStruct((n // bb, bb, hv), F32), jax.ShapeDtypeStruct(s_prev.shape, F32)],
        compiler_params=_params(("parallel",)),
    )(cols(qr), cols(kr), rows(qr), rows(kr), rows(v), s_prev)
    o = o.reshape(n, hv)
    bb = _tile(n, SUBLANES)

    m = mem_k.shape[1]
    c = pl.pallas_call(
        functools.partial(_attn_sample_kernel, heads=dims['mheads']),
        grid=(n // bb,),
        in_specs=[pl.BlockSpec((bb,) + mem_k.shape[2:], lambda i: (i, 0, 0)),
                  pl.BlockSpec((bb,) + mem_k.shape[1:], lambda i: (i, 0, 0, 0)),
                  pl.BlockSpec((bb,) + mem_v.shape[1:], lambda i: (i, 0, 0, 0))],
        out_specs=pl.BlockSpec((bb,) + mem_k.shape[2:], lambda i: (i, 0, 0)),
        out_shape=jax.ShapeDtypeStruct((n,) + mem_k.shape[2:], F32),
        compiler_params=_params(("parallel",)),
    )(mq.reshape((n,) + mem_k.shape[2:]), mem_k, mem_v).reshape(n, hm)

    first = xn2_all.shape[0] - n
    assert first % n == 0
    args = (x, a_g, o, g, c, gb, gc, lw['w_br_ret'], lw['w_br_mem'], lw['w_out'], lw['norm_ffn'], lw['w_rt'],
            lw['b_rt'], counts0)
    x1, xn2_all, route, counts = pl.pallas_call(
        functools.partial(_merge_sample_kernel, heads=heads, n_groups=dims['g'], per_group=dims['e']),
        grid=(1,),
        in_specs=[_const_spec(a.shape) for a in args] + [pl.BlockSpec(memory_space=pl.ANY)],
        out_specs=[pl.BlockSpec((n, d), lambda i: (0, 0)), pl.BlockSpec((n, d // 2), lambda i: (first // n, 0)),
                   pl.BlockSpec((n, LANES), lambda i: (0, 0)), pl.BlockSpec((1, LANES), lambda i: (0, 0))],
        out_shape=[jax.ShapeDtypeStruct((n, d), F32), jax.ShapeDtypeStruct(xn2_all.shape, F32),
                   jax.ShapeDtypeStruct((n, LANES), F32), jax.ShapeDtypeStruct((1, LANES), F32)],
        input_output_aliases={len(args): 1},
        compiler_params=_params(("arbitrary",)),
    )(*args, xn2_all)
    return x1, xn2_all, route, counts, conv_new, h_new, s_new


def _moe(xn2, route_parts, counts, lw, dims, tm=512, n_pass=2):
    t, d = xn2.shape[0], dims['d']
    g, n_exp = dims['g'], dims['g'] * dims['e']
    ff = lw['w_gate'].shape[2]
    counts = counts[0, g:g + n_exp].astype(jnp.int32)
    padded = ((counts + tm - 1) // tm) * tm
    ends = jnp.cumsum(padded)
    offs = ends - padded
    n_tiles = n_pass * ((TOP_K * t + n_exp * tm + n_pass * tm - 1) // (n_pass * tm))
    n_slots = n_tiles * tm
    per_pass = n_tiles // n_pass
    experts = jnp.arange(n_exp, dtype=jnp.int32)[None, :]
    dests = []
    for j in range(TOP_K):
        per_part = []
        for r in route_parts:
            ids = r[:, j].astype(jnp.int32)
            base = jnp.sum(jnp.where(ids[:, None] == experts, offs[None, :], 0), axis=1)
            per_part.append(base + r[:, 4 + j].astype(jnp.int32))
        dests.append(per_part)
    dest = jnp.concatenate([dp for per_part in dests for dp in per_part], axis=0)
    tok = jnp.arange(t, dtype=jnp.int32)
    n_pad = n_slots - TOP_K * t
    pad_cnt = jnp.concatenate([padded - counts, n_slots - ends[-1:]])
    pad_from = jnp.concatenate([offs + counts, ends[-1:]])
    pad_end = jnp.cumsum(pad_cnt)
    pad_i = jnp.arange(n_pad, dtype=jnp.int32)
    which = jnp.sum((pad_end[None, :] <= pad_i[:, None]).astype(jnp.int32), axis=1)
    shift = pad_from - (pad_end - pad_cnt)
    groups = jnp.arange(n_exp + 1, dtype=jnp.int32)[None, :]
    pad_slot = pad_i + jnp.sum(jnp.where(which[:, None] == groups, shift[None, :], 0), axis=1)
    _, src = lax.sort_key_val(jnp.concatenate([dest, pad_slot]),
                              jnp.concatenate([tok] * TOP_K + [pad_i % t]))
    n_used = (ends[-1] // tm).astype(jnp.int32)
    tile_row = jnp.arange(n_tiles, dtype=jnp.int32) * tm
    tile_e = jnp.sum((ends[None, :] <= tile_row[:, None]).astype(jnp.int32), axis=1)
    last_e = jnp.sum((ends <= (n_used - 1) * tm).astype(jnp.int32))
    tile_e = jnp.where(tile_row < ends[-1], tile_e, last_e)
    def expert_pass(first_tile, ys_table):
        xs = xn2.at[src[first_tile * tm:(first_tile + per_pass) * tm]].get(mode='promise_in_bounds')

        te = tile_e[first_tile:first_tile + per_pass]
        nu = jnp.clip(n_used - first_tile, 0, per_pass)
        idx = jnp.arange(per_pass, dtype=jnp.int32)
        first = (idx < nu) & ((idx == 0) | (te != jnp.roll(te, 1)))
        slot = (jnp.cumsum(first.astype(jnp.int32)) - 1) % 2
        later = lax.cummin(jnp.where(first, idx, per_pass), reverse=True)
        nxt_at = jnp.concatenate([later[1:], jnp.full((1,), per_pass, jnp.int32)])
        nxt = jnp.where(nxt_at < per_pass, te[jnp.minimum(nxt_at, per_pass - 1)], -1)
        plan = jnp.stack([te, first.astype(jnp.int32), slot, nxt]).astype(jnp.int32)

        def rows(i, plan, nu):
            return (jnp.maximum(jnp.minimum(i, nu[0] - 1), 0), 0)

        hbm = pl.BlockSpec(memory_space=pl.ANY)
        in_specs = [pl.BlockSpec((tm, d // 2), rows), hbm, hbm, hbm]
        args = [xs, lw['w_gate'], lw['w_up'], lw['w_down']]
        aliases = {}
        if ys_table is not None:
            in_specs.append(hbm)
            args.append(ys_table)
            aliases = {2 + len(args) - 1: 0}
        return pl.pallas_call(
            _expert_kernel,
            grid_spec=pltpu.PrefetchScalarGridSpec(
                num_scalar_prefetch=2,
                grid=(per_pass,),
                in_specs=in_specs,
                out_specs=pl.BlockSpec((tm, d // 2), lambda i, plan, nu: (first_tile + i, 0)),
                scratch_shapes=[pltpu.VMEM((2, d, ff), F32), pltpu.VMEM((2, d, ff), F32),
                                pltpu.VMEM((2, ff, d), F32), pltpu.SemaphoreType.DMA((3, 2))],
            ),
            out_shape=jax.ShapeDtypeStruct((n_slots, d // 2), F32),
            input_output_aliases=aliases,
            compiler_params=_params(("arbitrary",)),
        )(plan, nu.reshape(1), *args)

    ys = None
    for p in range(n_pass):
        ys = expert_pass(p * per_pass, ys)
    return [(ys, tuple(dests[j][p] for j in range(TOP_K))) for p in range(len(route_parts))]


def _final(x1, ys, dest_pair, route, g, n_parts):
    t, d = x1.shape
    rows = t // n_parts
    tl = _tile(rows, 512)
    out = None
    for p in range(n_parts):
        first = p * rows // tl
        y1, y2 = (ys.at[dp[p * rows:(p + 1) * rows]].get(mode='promise_in_bounds') for dp in dest_pair)
        row = pl.BlockSpec((tl, d), lambda i, first=first: (first + i, 0))
        packed = pl.BlockSpec((tl, d // 2), lambda i: (i, 0))
        in_specs = [row, packed, packed, pl.BlockSpec((tl, LANES), lambda i, first=first: (first + i, 0)),
                    _const_spec((1, d))]
        args = [x1, y1, y2, route, g]
        aliases = {}
        if out is not None:
            in_specs.append(pl.BlockSpec(memory_space=pl.ANY))
            args.append(out)
            aliases = {len(args) - 1: 0}
        out = pl.pallas_call(
            _final_kernel,
            grid=(rows // tl,),
            in_specs=in_specs,
            out_specs=row,
            out_shape=jax.ShapeDtypeStruct((t, d), F32),
            input_output_aliases=aliases,
            compiler_params=_params(("parallel",)),
        )(*args)
    return out


def kernel(x_prompt, x_sample, mem_prompt, state_conv, state_lru, state_ret, cache_mem_k, cache_mem_v, norm_mix, norm_ffn, norm_mem, norm_final, w_in, w_mem_kv, conv_w, conv_b, w_r, b_r, w_i, b_i, lru_lambda, w_branch_lru, w_branch_ret, w_branch_mem, w_out, w_grp, b_grp, w_exp_router, b_exp_router, w_gate, w_up, w_down):
    depth = w_in.shape[0]
    assert depth == 1, "the two request groups are chained per layer only for a single-layer trunk"
    assert x_sample.shape[1] == 1
    bp, lp, d = x_prompt.shape
    ns = x_sample.shape[0]
    heads, dk, dv = state_ret.shape[2], state_ret.shape[3], state_ret.shape[4]
    mheads, mdh = cache_mem_k.shape[3], cache_mem_k.shape[4]
    dims = dict(d=d, w=state_lru.shape[2], heads=heads, dk=dk, dv=dv, hk=heads * dk, hv=heads * dv,
                mheads=mheads, hm=mheads * mdh, g=w_exp_router.shape[1], e=w_exp_router.shape[3])
    layer = 0
    p = dict(norm_mix=norm_mix[layer], norm_ffn=norm_ffn[layer], norm_mem=norm_mem[layer], w_in=w_in[layer],
             w_mem_kv=w_mem_kv[layer], conv_w=conv_w[layer], conv_b=conv_b[layer], w_r=w_r[layer], b_r=b_r[layer],
             w_i=w_i[layer], b_i=b_i[layer], lru_lambda=lru_lambda[layer], w_branch_lru=w_branch_lru[layer],
             w_branch_ret=w_branch_ret[layer], w_branch_mem=w_branch_mem[layer], w_out=w_out[layer],
             w_grp=w_grp[layer], b_grp=b_grp[layer], w_exp_router=w_exp_router[layer],
             b_exp_router=b_exp_router[layer], w_gate=w_gate[layer], w_up=w_up[layer], w_down=w_down[layer])
    lw = _prep_layer(p, dims)

    x1p, xn2, route_p, counts, conv_p, lru_p, ret_p, mk, mv = _prompt_mixer(x_prompt, mem_prompt, lw, dims, ns)
    x1s, xn2, route_s, counts, conv_s, lru_s, ret_s = _sample_mixer(
        x_sample[:, 0], state_conv[layer], state_lru[layer], state_ret[layer], cache_mem_k[layer],
        cache_mem_v[layer], counts, xn2, lw, dims)

    tp = bp * lp
    route_p = route_p.reshape(tp, LANES)
    (ys_p, dest_p), (ys_s, dest_s) = _moe(xn2, [route_p, route_s], counts, lw, dims)
    yp = _final(x1p.reshape(tp, d), ys_p, dest_p, route_p, _row(norm_final), 2).reshape(bp, lp, d)
    ys = _final(x1s, ys_s, dest_s, route_s, _row(norm_final), 1).reshape(ns, 1, d)
    return (yp, ys, conv_p[None], lru_p[None], ret_p[None], mk[None], mv[None],
            conv_s[None], lru_s[None], ret_s[None])
```

```python
import functools
import itertools
import math

import jax
import jax.numpy as jnp
from jax import lax
from jax.experimental import pallas as pl
from jax.experimental.pallas import tpu as pltpu

F32 = jnp.float32
BF16 = jnp.bfloat16

EPS = 1e-6
LRU_C = 8.0
ROPE_BASE = 10000.0
PAST_LEN = 16384
TOP_K = 2
LANES = 128
SUBLANES = 8
VMEM_LIMIT = 56 * 1024 * 1024


def _dot(a, b):
    return jnp.dot(a, b, preferred_element_type=F32)


def _dot_nt(a, b):
    return lax.dot_general(a, b, (((1,), (1,)), ((), ())), preferred_element_type=F32)


def _dot_tn(a, b):
    return lax.dot_general(a, b, (((0,), (0,)), ((), ())), preferred_element_type=F32)


def _rms(x, g):
    return x * lax.rsqrt(jnp.mean(x * x, axis=-1, keepdims=True) + EPS) * g


def _sigmoid(x):
    return 0.5 * jnp.tanh(0.5 * x) + 0.5


def _silu(x):
    return x * _sigmoid(x)


def _gelu_tanh(x):
    return 0.5 * x * (1.0 + jnp.tanh(math.sqrt(2.0 / math.pi) * (x + 0.044715 * (x * x * x))))


def _softplus(x):
    return jnp.maximum(x, 0.0) + jnp.log1p(jnp.exp(-jnp.abs(x)))


def _const_spec(shape):
    nd = len(shape)
    return pl.BlockSpec(shape, lambda *_: (0,) * nd, pipeline_mode=pl.Buffered(1))


def _window_spec(rows, window):
    start, width = window
    return pl.BlockSpec((rows, width), lambda *_: (0, start // width), pipeline_mode=pl.Buffered(1))


def _params(sem):
    return pltpu.CompilerParams(dimension_semantics=sem, vmem_limit_bytes=VMEM_LIMIT)


def _lru_coeffs(uc, wr_ref, br, wi_ref, bi, lam):
    nb, bs = wr_ref.shape[0], wr_ref.shape[1]
    ucb = uc.astype(BF16)
    r_lin = jnp.concatenate([_dot(ucb[:, n * bs:(n + 1) * bs], wr_ref[n]) for n in range(nb)], axis=-1)
    i_lin = jnp.concatenate([_dot(ucb[:, n * bs:(n + 1) * bs], wi_ref[n]) for n in range(nb)], axis=-1)
    r = _sigmoid(r_lin + br)
    i = _sigmoid(i_lin + bi)
    log_a = (-LRU_C) * r * _softplus(-lam)
    a = jnp.exp(log_a)
    t = 1.0 - a * a
    mult = t * lax.rsqrt(jnp.maximum(t, 1e-37))
    return a, mult, i


def _scan_rows(a, b, h0):
    rows, width = a.shape
    grp = SUBLANES
    n_grp = rows // grp
    a = a.reshape(n_grp, grp, width)
    b = b.reshape(n_grp, grp, width)
    row = lax.broadcasted_iota(jnp.int32, a.shape, 1)
    d = 1
    while d < grp:
        a_sh = pltpu.roll(a, d, 1)
        b_sh = pltpu.roll(b, d, 1)
        keep = row >= d
        b = jnp.where(keep, b + a * b_sh, b)
        a = jnp.where(keep, a * a_sh, a)
        d *= 2
    out = []
    for g in range(n_grp):
        hg = a[g] * h0 + b[g]
        out.append(hg)
        h0 = hg[grp - 1:grp, :]
    return jnp.concatenate(out, axis=0)


def _rotary(t, cos, sin):
    half = t.shape[-1] // 2
    t1, t2 = t[:, :half], t[:, half:]
    return jnp.concatenate([t1 * cos - t2 * sin, t2 * cos + t1 * sin], axis=-1)


def _route(logits, run, n_groups, per_group):
    col = lax.broadcasted_iota(jnp.int32, logits.shape, 1)
    big = jnp.int32(1 << 20)
    neg = jnp.float32(-jnp.inf)
    gl = jnp.where(col < n_groups, logits, neg)
    gmax = jnp.max(gl, axis=-1, keepdims=True)
    g_idx = jnp.min(jnp.where(gl == gmax, col, big), axis=-1, keepdims=True)
    g_val = 1.0 / jnp.sum(jnp.exp(gl - gmax), axis=-1, keepdims=True)
    lo = n_groups + per_group * g_idx
    el = jnp.where((col >= lo) & (col < lo + per_group), logits, neg)
    m1 = jnp.max(el, axis=-1, keepdims=True)
    i1 = jnp.min(jnp.where(el == m1, col, big), axis=-1, keepdims=True)
    el2 = jnp.where(col == i1, neg, el)
    m2 = jnp.max(el2, axis=-1, keepdims=True)
    i2 = jnp.min(jnp.where(el2 == m2, col, big), axis=-1, keepdims=True)
    e2 = jnp.exp(m2 - m1)
    w1 = g_val / (1.0 + e2)
    w2 = g_val * e2 / (1.0 + e2)
    f1 = (i1 - n_groups).astype(F32)
    f2 = (i2 - n_groups).astype(F32)
    hit1, hit2 = col == i1, col == i2
    hits = jnp.where(hit1 | hit2, 1.0, 0.0)
    rows = logits.shape[0]
    earlier = lax.broadcasted_iota(jnp.int32, (rows, rows), 1) < lax.broadcasted_iota(jnp.int32, (rows, rows), 0)
    before = run + _dot(jnp.where(earlier, 1.0, 0.0).astype(BF16), hits.astype(BF16))
    r1 = jnp.sum(jnp.where(hit1, before, 0.0), axis=-1, keepdims=True)
    r2 = jnp.sum(jnp.where(hit2, before, 0.0), axis=-1, keepdims=True)
    route = f1
    for j, val in enumerate((f2, w1, w2, r1, r2), start=1):
        route = jnp.where(col == j, val, route)
    route = jnp.where(col > 5, 0.0, route)
    return route, run + jnp.sum(hits, axis=0, keepdims=True)


def _pack_pairs(x):
    half = x.shape[1] // 2
    lo = lax.bitcast_convert_type(x[:, :half].astype(F32), jnp.uint32)
    hi = lax.bitcast_convert_type(x[:, half:].astype(F32), jnp.uint32)
    return lax.bitcast_convert_type((lo >> 16) | (hi & jnp.uint32(0xFFFF0000)), F32)


def _unpack_pairs(w):
    u = lax.bitcast_convert_type(w, jnp.uint32)
    lo = lax.bitcast_convert_type(u << 16, F32)
    hi = lax.bitcast_convert_type(u & jnp.uint32(0xFFFF0000), F32)
    return jnp.concatenate([lo, hi], axis=1)


def _merge_tail(x, merged, wout_ref, gffn, wrt_ref, brt, run, n_groups, per_group):
    x1 = x + _dot(merged.astype(BF16), wout_ref[...])
    xn2 = _rms(x1, gffn).astype(BF16)
    logits = _dot(xn2, wrt_ref[...]) + brt
    route, run = _route(logits, run, n_groups, per_group)
    return x1, _pack_pairs(xn2), route, run


def _lru_prompt_steps(c, xn, wxy_ref, wga_ref, cw_ref, cb_ref, wr_ref, br_ref, wi_ref, bi_ref, lam_ref, wbr_ref,
                      out_ref, ccar, hcar, res):
    tl = xn.shape[0]
    width = hcar.shape[1]
    kw = cw_ref.shape[0]
    nb, bs = wr_ref.shape[0], wr_ref.shape[1]
    z = _dot(xn, wxy_ref[...])
    row = lax.broadcasted_iota(jnp.int32, (tl, bs), 0)
    hbs, lasts = [], []
    for n in range(nb):
        cs = slice(n * bs, (n + 1) * bs)
        ux = z[:, cs]
        acc = cw_ref[0:1, cs] * ux
        for j in range(1, kw):
            prev = jnp.where(row == 0, ccar[j - 1:j, cs], pltpu.roll(acc, 1, 0))
            ccar[j - 1:j, cs] = acc[tl - 1:tl, :]
            acc = prev + cw_ref[j:j + 1, cs] * ux
        uc = acc + cb_ref[:, cs]
        ucb = uc.astype(BF16)
        r = _sigmoid(_dot(ucb, wr_ref[n]) + br_ref[:, cs])
        gate = _sigmoid(_dot(ucb, wi_ref[n]) + bi_ref[:, cs])
        log_a = (-LRU_C) * r * _softplus(-lam_ref[:, cs])
        a = jnp.exp(log_a)
        t = 1.0 - a * a
        mult = t * lax.rsqrt(jnp.maximum(t, 1e-37))
        mult = jnp.where(row + c * tl == 0, 1.0, mult)
        h = _scan_rows(a, mult * gate * uc, hcar[:, cs])
        hcar[:, cs] = h[tl - 1:tl, :]
        lasts.append(h[tl - 1:tl, :])
        hbs.append((h * _gelu_tanh(z[:, width + n * bs: width + (n + 1) * bs])).astype(BF16))
        yield
    res['tail'] = z[tl - (kw - 1):tl, :width]
    res['h_last'] = jnp.concatenate(lasts, axis=1)
    a_out = _dot(jnp.concatenate(hbs, axis=1), wbr_ref[...])
    ga = _dot(xn, wga_ref[...])
    out_ref[0] = (_sigmoid(ga) * a_out).astype(out_ref.dtype)
    yield


def _ret_proj_steps(xn, w_refs, res, cols=512):
    for name, w_ref in w_refs:
        parts = []
        for j in range(0, w_ref.shape[1], cols):
            parts.append(_dot(xn, w_ref[:, j:j + cols]))
            yield
        res[name] = jnp.concatenate(parts, axis=1)


def _ret_prompt_steps(xn, proj, wgb_ref, cos_ref, sin_ref, wbr_ref, out_ref, s_scr, y_scr, heads, chunk):
    tl = xn.shape[0]
    q, k, v, g = proj['q'], proj['k'], proj['v'], proj['g']
    dk = q.shape[1] // heads
    dv = v.shape[1] // heads
    cos, sin = cos_ref[...], sin_ref[...]
    n_i = lax.broadcasted_iota(jnp.int32, (chunk, chunk), 0)
    m_i = lax.broadcasted_iota(jnp.int32, (chunk, chunk), 1)
    diff = (n_i - m_i).astype(F32)
    rowk = lax.broadcasted_iota(jnp.int32, (chunk, dk), 0).astype(F32)
    for h in range(heads):
        log_g = math.log1p(-(2.0 ** (-5.0 - h)))
        dmask = jnp.where(diff >= 0, jnp.exp(jnp.maximum(diff, 0.0) * log_g), 0.0)
        q_decay = jnp.exp((rowk + 1.0) * log_g)
        k_decay = jnp.exp((chunk - 1.0 - rowk) * log_g)
        s_decay = math.exp(chunk * log_g)
        for sub in range(tl // chunk):
            r0 = sub * chunk
            cs, sn = cos[r0:r0 + chunk], sin[r0:r0 + chunk]
            qr = _rotary(q[r0:r0 + chunk, h * dk:(h + 1) * dk], cs, sn)
            kr = _rotary(k[r0:r0 + chunk, h * dk:(h + 1) * dk], cs, sn) * (dk ** -0.5)
            vh = v[r0:r0 + chunk, h * dv:(h + 1) * dv].astype(BF16)
            s = s_scr[h]
            att = _dot_nt(qr.astype(BF16), kr.astype(BF16)) * dmask
            o = _dot(att.astype(BF16), vh) + _dot((qr * q_decay).astype(BF16), s.astype(BF16))
            s_scr[h] = s * s_decay + _dot_tn((kr * k_decay).astype(BF16), vh)
            o = o * lax.rsqrt(jnp.mean(o * o, axis=-1, keepdims=True) + EPS)
            gh = g[r0:r0 + chunk, h * dv:(h + 1) * dv]
            y_scr[r0:r0 + chunk, h * dv:(h + 1) * dv] = (_silu(gh) * o).astype(BF16)
            yield
    b_out = _dot(y_scr[...], wbr_ref[...])
    gb = _dot(xn, wgb_ref[...])
    out_ref[0] = (_sigmoid(gb) * b_out).astype(out_ref.dtype)


def _lru_ret_prompt_kernel(x_ref, g_ref, wxy_ref, wga_ref, cw_ref, cb_ref, wr_ref, br_ref, wi_ref, bi_ref,
                           lam_ref, wbl_ref, wq_ref, wk_ref, wv_ref, wg_ref, wgb_ref, cos_ref, sin_ref, wbr_ref,
                           a_ref, conv_ref, h_ref, b_ref, s_out_ref, ccar, hcar, s_scr, y_scr, *, heads, chunk):
    c = pl.program_id(1)

    @pl.when(c == 0)
    def _():
        ccar[...] = jnp.zeros_like(ccar)
        hcar[...] = jnp.zeros_like(hcar)
        s_scr[...] = jnp.zeros_like(s_scr)

    xn = _rms(x_ref[0], g_ref[...]).astype(BF16)
    lru, proj = {}, {}
    lru_steps = _lru_prompt_steps(c, xn, wxy_ref, wga_ref, cw_ref, cb_ref, wr_ref, br_ref, wi_ref, bi_ref,
                                  lam_ref, wbl_ref, a_ref, ccar, hcar, lru)
    proj_steps = _ret_proj_steps(xn, (('q', wq_ref), ('k', wk_ref), ('v', wv_ref), ('g', wg_ref)), proj)
    n_lru = wr_ref.shape[0] + 1
    n_proj = sum(-(-r.shape[1] // 512) for r in (wq_ref, wk_ref, wv_ref, wg_ref))
    n_ret = heads * (x_ref.shape[1] // chunk)
    lru_done = 0
    for i in range(n_proj):
        next(proj_steps)
        want = ((i + 1) * n_lru) // (n_proj + n_ret)
        while lru_done < want:
            next(lru_steps)
            lru_done += 1
    for _ in proj_steps:
        pass
    ret_steps = _ret_prompt_steps(xn, proj, wgb_ref, cos_ref, sin_ref, wbr_ref, b_ref, s_scr, y_scr, heads, chunk)
    for i in range(n_ret):
        next(ret_steps)
        want = ((n_proj + i + 1) * n_lru) // (n_proj + n_ret)
        while lru_done < want:
            next(lru_steps)
            lru_done += 1
    for _ in itertools.chain(lru_steps, ret_steps):
        pass

    @pl.when(c == pl.num_programs(1) - 1)
    def _():
        conv_ref[0] = lru['tail']
        h_ref[0] = lru['h_last']
        s_out_ref[0] = s_scr[...]


def _memkv_kernel(m_ref, g_ref, w_ref, k_ref, v_ref, k4_ref, v4_ref):
    width = k_ref.shape[-1]
    heads, dh = k4_ref.shape[2], k4_ref.shape[3]
    mn = _rms(m_ref[0], g_ref[...]).astype(BF16)
    kv = _dot(mn, w_ref[...])
    k_ref[0] = kv[:, :width]
    v_ref[0] = kv[:, width:]
    for h in range(heads):
        k4_ref[0, :, h, :] = kv[:, h * dh:(h + 1) * dh]
        v4_ref[0, :, h, :] = kv[:, width + h * dh: width + (h + 1) * dh]


def _attn_merge_prompt_kernel(x_ref, ag_ref, bg_ref, mk_ref, mv_ref, g_ref, wmq_ref, wgc_ref, wbm_ref,
                              wout_ref, gffn_ref, wrt_ref, brt_ref, x1_ref, xn2_ref, route_ref, cnt_ref,
                              *, heads, n_groups, per_group):
    @pl.when((pl.program_id(0) == 0) & (pl.program_id(1) == 0))
    def _():
        cnt_ref[...] = jnp.zeros_like(cnt_ref)

    x = x_ref[0]
    xn = _rms(x, g_ref[...]).astype(BF16)
    mq = _dot(xn, wmq_ref[...]).astype(BF16)
    dh = mq.shape[1] // heads
    mk = mk_ref[0].astype(BF16)
    mv = mv_ref[0].astype(BF16)
    cs = []
    for h in range(heads):
        hs = slice(h * dh, (h + 1) * dh)
        lg = _dot_nt(mq[:, hs], mk[:, hs]) * (dh ** -0.5)
        p = jnp.exp(lg - jnp.max(lg, axis=-1, keepdims=True))
        den = jnp.sum(p, axis=-1, keepdims=True)
        cs.append(_dot(p.astype(BF16), mv[:, hs]) / den)
    c_out = _dot(jnp.concatenate(cs, axis=-1).astype(BF16), wbm_ref[...])
    gc = _dot(xn, wgc_ref[...])
    merged = _sigmoid(gc) * c_out + ag_ref[0].astype(F32) + bg_ref[0].astype(F32)
    x1, xn2, route, run = _merge_tail(x, merged, wout_ref, gffn_ref[...], wrt_ref, brt_ref[...], cnt_ref[...],
                                      n_groups, per_group)
    x1_ref[0] = x1
    xn2_ref[...] = xn2
    route_ref[0] = route
    cnt_ref[...] = run


def _proj_kernel(x_ref, g_ref, w_ref, z_ref):
    xn = _rms(x_ref[...], g_ref[...]).astype(BF16)
    z_ref[...] = _dot(xn, w_ref[...])


def _lru_sample_kernel(ux_ref, uy_ref, ga_ref, cprev_ref, hprev_ref, cw_ref, cb_ref, wr_ref, br_ref, wi_ref,
                       bi_ref, lam_ref, wbr_ref, out_ref, conv_ref, h_ref, *, at_start):
    kw = cw_ref.shape[0]
    ux = ux_ref[...]
    uc = cb_ref[...] + cw_ref[kw - 1:kw, :] * ux
    for j in range(kw - 1):
        uc = uc + cw_ref[j:j + 1, :] * cprev_ref[:, j, :]
    a, mult, gate = _lru_coeffs(uc, wr_ref, br_ref[...], wi_ref, bi_ref[...], lam_ref[...])
    if at_start:
        mult = jnp.ones_like(mult)
    h = a * hprev_ref[...] + mult * gate * uc
    h_ref[...] = h
    for j in range(kw - 2):
        conv_ref[:, j, :] = cprev_ref[:, j + 1, :]
    conv_ref[:, kw - 2, :] = ux
    a_out = _dot((h * _gelu_tanh(uy_ref[...])).astype(BF16), wbr_ref[...])
    out_ref[...] = _sigmoid(ga_ref[...]) * a_out


def _rot_sample_kernel(q_ref, k_ref, cos_ref, sin_ref, qo_ref, ko_ref, *, heads):
    dk = q_ref.shape[1] // heads
    cos, sin = cos_ref[...], sin_ref[...]
    for h in range(heads):
        hs = slice(h * dk, (h + 1) * dk)
        qo_ref[:, hs] = _rotary(q_ref[:, hs], cos, sin)
        ko_ref[:, hs] = _rotary(k_ref[:, hs], cos, sin) * (dk ** -0.5)


def _ret_sample_kernel(qt_ref, kt_ref, q_ref, k_ref, v_ref, s_ref, o_ref, s_out_ref, *, heads):
    bb = q_ref.shape[1]
    dk = s_ref.shape[2]
    dv = s_ref.shape[3]
    for j in range(bb):
        for h in range(heads):
            decay = 1.0 - 2.0 ** (-5.0 - h)
            qcol = qt_ref[0, h, :, j:j + 1]
            kcol = kt_ref[0, h, :, j:j + 1]
            qrow = q_ref[0, j:j + 1, h * dk:(h + 1) * dk]
            krow = k_ref[0, j:j + 1, h * dk:(h + 1) * dk]
            vrow = v_ref[0, j:j + 1, h * dv:(h + 1) * dv]
            s = s_ref[j, h]
            att = jnp.sum(qrow * krow, axis=-1, keepdims=True)
            o = att * vrow + jnp.sum((qcol * decay) * s, axis=0, keepdims=True)
            o_ref[0, j:j + 1, h * dv:(h + 1) * dv] = o
            s_out_ref[j, h] = s * decay + kcol * vrow


def _attn_sample_kernel(mq_ref, k_ref, v_ref, c_ref, *, heads):
    bb, _, dh = mq_ref.shape
    for j in range(bb):
        q = mq_ref[j]
        lg = jnp.sum(k_ref[j] * q[None], axis=-1, keepdims=True) * (dh ** -0.5)
        p = jnp.exp(lg - jnp.max(lg, axis=0, keepdims=True))
        den = jnp.sum(p, axis=0)
        c_ref[j] = jnp.sum(v_ref[j] * p, axis=0) / den


def _merge_sample_kernel(x_ref, ag_ref, o_ref, g_ref, c_ref, gb_ref, gc_ref, wbr_ref, wbm_ref, wout_ref,
                         gffn_ref, wrt_ref, brt_ref, run_ref, xn2_table_ref, x1_ref, xn2_ref, route_ref, cnt_ref,
                         *, heads, n_groups, per_group):
    del xn2_table_ref
    dv = o_ref.shape[1] // heads
    ys = []
    for h in range(heads):
        o = o_ref[:, h * dv:(h + 1) * dv]
        o = o * lax.rsqrt(jnp.mean(o * o, axis=-1, keepdims=True) + EPS)
        ys.append((_silu(g_ref[:, h * dv:(h + 1) * dv]) * o).astype(BF16))
    b_out = _dot(jnp.concatenate(ys, axis=-1), wbr_ref[...])
    c_out = _dot(c_ref[...].astype(BF16), wbm_ref[...])
    merged = ag_ref[...] + _sigmoid(gb_ref[...]) * b_out + _sigmoid(gc_ref[...]) * c_out
    x1, xn2, route, run = _merge_tail(x_ref[...], merged, wout_ref, gffn_ref[...], wrt_ref, brt_ref[...],
                                      run_ref[...], n_groups, per_group)
    x1_ref[...] = x1
    xn2_ref[...] = xn2
    route_ref[...] = route
    cnt_ref[...] = run


def _expert_kernel(plan_ref, nu_ref, xs_ref, wg_hbm, wu_hbm, wd_hbm, *rest):
    ys_ref, wg_buf, wu_buf, wd_buf, sem = rest[-5:]
    i = pl.program_id(0)
    expert, first, slot, nxt = plan_ref[0, i], plan_ref[1, i], plan_ref[2, i], plan_ref[3, i]
    used = i < nu_ref[0]

    def weight_copies(e, s):
        return (pltpu.make_async_copy(wg_hbm.at[e], wg_buf.at[s], sem.at[0, s]),
                pltpu.make_async_copy(wu_hbm.at[e], wu_buf.at[s], sem.at[1, s]),
                pltpu.make_async_copy(wd_hbm.at[e], wd_buf.at[s], sem.at[2, s]))

    @pl.when(used & (i == 0))
    def _():
        for cp in weight_copies(expert, slot):
            cp.start()

    @pl.when(used & (first == 1))
    def _():
        for cp in weight_copies(expert, slot):
            cp.wait()

    @pl.when(used & (first == 1) & (nxt >= 0))
    def _():
        for cp in weight_copies(nxt, 1 - slot):
            cp.start()

    @pl.when(used)
    def _():
        x = _unpack_pairs(xs_ref[...]).astype(BF16)
        hg = _dot(x, wg_buf[slot].astype(BF16))
        hu = _dot(x, wu_buf[slot].astype(BF16))
        hid = (_silu(hg) * hu).astype(BF16)
        ys_ref[...] = _pack_pairs(_dot(hid, wd_buf[slot].astype(BF16)).astype(BF16))

    @pl.when(jnp.logical_not(used))
    def _():
        ys_ref[...] = jnp.zeros_like(ys_ref)


def _final_kernel(x1_ref, y1_ref, y2_ref, route_ref, g_ref, *rest):
    out_ref = rest[-1]
    w1 = route_ref[:, 2:3]
    w2 = route_ref[:, 3:4]
    x2 = x1_ref[...] + (w1 * _unpack_pairs(y1_ref[...]) + w2 * _unpack_pairs(y2_ref[...]))
    out_ref[...] = _rms(x2, g_ref[...])


def _tile(n, target):
    t = min(n, target)
    while n % t:
        t //= 2
    return t


def _row(v):
    return v.reshape(1, -1).astype(F32)


def _prep_layer(p, dims):
    d, w, hk, hv, hm = dims['d'], dims['w'], dims['hk'], dims['hv'], dims['hm']
    w_in = p['w_in']
    starts = {}
    o0 = 0
    for name, width in (('xy', 2 * w), ('q', hk), ('k', hk), ('v', hv), ('g', hv), ('mq', hm),
                        ('ga', d), ('gb', d), ('gc', d)):
        assert o0 % width == 0, "each window must start on a multiple of its width"
        starts[name] = (o0, width)
        o0 += width
    g, e = p['w_exp_router'].shape[0], p['w_exp_router'].shape[2]
    w_rt = jnp.concatenate([p['w_grp'], jnp.moveaxis(p['w_exp_router'], 0, 1).reshape(d, g * e)], axis=1)
    w_rt = jnp.pad(w_rt, ((0, 0), (0, LANES - w_rt.shape[1]))).astype(BF16)
    b_rt = jnp.pad(jnp.concatenate([p['b_grp'], p['b_exp_router'].reshape(-1)]), (0, LANES - g - g * e))
    return dict(
        w_in=w_in.astype(BF16), win=starts,
        norm_mix=_row(p['norm_mix']), norm_ffn=_row(p['norm_ffn']), norm_mem=_row(p['norm_mem']),
        w_mem_kv=p['w_mem_kv'].astype(BF16),
        conv_w=p['conv_w'].astype(F32), conv_b=_row(p['conv_b']),
        w_r=p['w_r'].astype(BF16), b_r=_row(p['b_r']), w_i=p['w_i'].astype(BF16), b_i=_row(p['b_i']),
        lam=_row(p['lru_lambda']),
        w_br_lru=p['w_branch_lru'].astype(BF16), w_br_ret=p['w_branch_ret'].astype(BF16),
        w_br_mem=p['w_branch_mem'].astype(BF16), w_out=p['w_out'].astype(BF16),
        w_rt=w_rt, b_rt=_row(b_rt),
        w_gate=p['w_gate'].reshape((g * e,) + p['w_gate'].shape[2:]),
        w_up=p['w_up'].reshape((g * e,) + p['w_up'].shape[2:]),
        w_down=p['w_down'].reshape((g * e,) + p['w_down'].shape[2:]),
    )


def _rope_tables(pos, dk):
    half = dk // 2
    inv = ROPE_BASE ** (-jnp.linspace(0.0, 1.0, half, dtype=F32))
    ang = pos.astype(F32)[:, None] * inv[None, :]
    return jnp.cos(ang), jnp.sin(ang)


def _prompt_mixer(x, mem, lw, dims, extra_rows):
    b, l, d = x.shape
    w, heads, dk, dv = dims['w'], dims['heads'], dims['dk'], dims['dv']
    kw = lw['conv_w'].shape[0]
    tl = _tile(l, 256)
    grid = (b, l // tl)
    xspec = pl.BlockSpec((1, tl, d), lambda i, c: (i, c, 0))
    sem = ("parallel", "arbitrary")
    w_in, win = lw['w_in'], lw['win']

    chunk = _tile(tl, 256)
    cos, sin = _rope_tables(jnp.arange(l, dtype=jnp.int32), dk)
    half = dk // 2
    a_g, conv_new, h_last, b_g, s_new = pl.pallas_call(
        functools.partial(_lru_ret_prompt_kernel, heads=heads, chunk=chunk),
        grid=grid,
        in_specs=[xspec, _const_spec((1, d)), _window_spec(d, win['xy']), _window_spec(d, win['ga']),
                  _const_spec((kw, w)), _const_spec((1, w)), _const_spec(lw['w_r'].shape), _const_spec((1, w)),
                  _const_spec(lw['w_i'].shape), _const_spec((1, w)), _const_spec((1, w)), _const_spec((w, d)),
                  _window_spec(d, win['q']), _window_spec(d, win['k']),
                  _window_spec(d, win['v']), _window_spec(d, win['g']), _window_spec(d, win['gb']),
                  pl.BlockSpec((tl, half), lambda i, c: (c, 0)), pl.BlockSpec((tl, half), lambda i, c: (c, 0)),
                  _const_spec((heads * dv, d))],
        out_specs=[pl.BlockSpec((1, tl, d), lambda i, c: (i, c, 0)),
                   pl.BlockSpec((1, kw - 1, w), lambda i, c: (i, 0, 0)),
                   pl.BlockSpec((1, 1, w), lambda i, c: (i, 0, 0)),
                   pl.BlockSpec((1, tl, d), lambda i, c: (i, c, 0)),
                   pl.BlockSpec((1, heads, dk, dv), lambda i, c: (i, 0, 0, 0))],
        out_shape=[jax.ShapeDtypeStruct((b, l, d), BF16), jax.ShapeDtypeStruct((b, kw - 1, w), F32),
                   jax.ShapeDtypeStruct((b, 1, w), F32),
                   jax.ShapeDtypeStruct((b, l, d), BF16), jax.ShapeDtypeStruct((b, heads, dk, dv), F32)],
        scratch_shapes=[pltpu.VMEM((SUBLANES, w), F32), pltpu.VMEM((1, w), F32),
                        pltpu.VMEM((heads, dk, dv), F32), pltpu.VMEM((tl, heads * dv), BF16)],
        compiler_params=_params(sem),
    )(x, lw['norm_mix'], w_in, w_in, lw['conv_w'], lw['conv_b'], lw['w_r'], lw['b_r'],
      lw['w_i'], lw['b_i'], lw['lam'], lw['w_br_lru'],
      w_in, w_in, w_in, w_in, w_in, cos, sin, lw['w_br_ret'])

    m = mem.shape[1]
    hm = dims['hm']
    mh = dims['mheads']
    flat = pl.BlockSpec((1, m, hm), lambda i: (i, 0, 0))
    per_head = pl.BlockSpec((1, m, mh, hm // mh), lambda i: (i, 0, 0, 0))
    mk, mv, mk4, mv4 = pl.pallas_call(
        _memkv_kernel,
        grid=(b,),
        in_specs=[pl.BlockSpec((1, m, d), lambda i: (i, 0, 0)), _const_spec((1, d)), _const_spec((d, 2 * hm))],
        out_specs=[flat, flat, per_head, per_head],
        out_shape=[jax.ShapeDtypeStruct((b, m, hm), F32), jax.ShapeDtypeStruct((b, m, hm), F32),
                   jax.ShapeDtypeStruct((b, m, mh, hm // mh), F32), jax.ShapeDtypeStruct((b, m, mh, hm // mh), F32)],
        compiler_params=_params(("parallel",)),
    )(mem, lw['norm_mem'], lw['w_mem_kv'])

    tm = _tile(l, 1024)
    mspec = pl.BlockSpec((1, tm, d), lambda i, c: (i, c, 0))
    x1, xn2, route, counts = pl.pallas_call(
        functools.partial(_attn_merge_prompt_kernel, heads=dims['mheads'], n_groups=dims['g'],
                          per_group=dims['e']),
        grid=(b, l // tm),
        in_specs=[mspec, mspec, mspec,
                  pl.BlockSpec((1, m, hm), lambda i, c: (i, 0, 0)), pl.BlockSpec((1, m, hm), lambda i, c: (i, 0, 0)),
                  _const_spec((1, d)), _window_spec(d, win['mq']), _window_spec(d, win['gc']), _const_spec((hm, d)),
                  _const_spec((d, d)), _const_spec((1, d)), _const_spec((d, LANES)), _const_spec((1, LANES))],
        out_specs=[mspec, pl.BlockSpec((tm, d // 2), lambda i, c: (i * (l // tm) + c, 0)),
                   pl.BlockSpec((1, tm, LANES), lambda i, c: (i, c, 0)),
                   pl.BlockSpec((1, LANES), lambda i, c: (0, 0))],
        out_shape=[jax.ShapeDtypeStruct((b, l, d), F32), jax.ShapeDtypeStruct((b * l + extra_rows, d // 2), F32),
                   jax.ShapeDtypeStruct((b, l, LANES), F32), jax.ShapeDtypeStruct((1, LANES), F32)],
        compiler_params=_params(("arbitrary", "arbitrary")),
    )(x, a_g, b_g, mk, mv, lw['norm_mix'], w_in, w_in, lw['w_br_mem'], lw['w_out'],
      lw['norm_ffn'], lw['w_rt'], lw['b_rt'])
    return x1, xn2, route, counts, conv_new, h_last[:, 0], s_new, mk4, mv4


def _sample_mixer(x, conv_prev, h_prev, s_prev, mem_k, mem_v, counts0, xn2_all, lw, dims):
    n, d = x.shape
    w, heads, dk, dv, hm = dims['w'], dims['heads'], dims['dk'], dims['dv'], dims['hm']
    hk, hv = heads * dk, heads * dv
    kw = lw['conv_w'].shape[0]
    n_in = lw['w_in'].shape[1]
    tn = _tile(n_in, 1024)
    z = pl.pallas_call(
        _proj_kernel,
        grid=(n_in // tn,),
        in_specs=[_const_spec((n, d)), _const_spec((1, d)), pl.BlockSpec((d, tn), lambda j: (0, j))],
        out_specs=pl.BlockSpec((n, tn), lambda j: (0, j)),
        out_shape=jax.ShapeDtypeStruct((n, n_in), F32),
        compiler_params=_params(("parallel",)),
    )(x, lw['norm_mix'], lw['w_in'])
    o0 = 0
    parts = []
    for sz in (w, w, hk, hk, hv, hv, hm, d, d, d):
        parts.append(z[:, o0:o0 + sz])
        o0 += sz
    ux, uy, q, k, v, g, mq, ga, gb, gc = parts

    a_g, conv_new, h_new = pl.pallas_call(
        functools.partial(_lru_sample_kernel, at_start=(PAST_LEN == 0)),
        out_shape=[jax.ShapeDtypeStruct((n, d), F32), jax.ShapeDtypeStruct((n, kw - 1, w), F32),
                   jax.ShapeDtypeStruct((n, w), F32)],
        compiler_params=pltpu.CompilerParams(vmem_limit_bytes=VMEM_LIMIT),
    )(ux, uy, ga, conv_prev, h_prev, lw['conv_w'], lw['conv_b'], lw['w_r'], lw['b_r'], lw['w_i'], lw['b_i'],
      lw['lam'], lw['w_br_lru'])

    cos, sin = _rope_tables(PAST_LEN + jnp.arange(1, dtype=jnp.int32), dk)
    qr, kr = pl.pallas_call(
        functools.partial(_rot_sample_kernel, heads=heads),
        out_shape=[jax.ShapeDtypeStruct((n, hk), F32), jax.ShapeDtypeStruct((n, hk), F32)],
    )(q, k, cos, sin)

    bb = _tile(n, 4)

    def cols(t):
        return t.reshape(n // bb, bb, heads, dk).transpose(0, 2, 3, 1)

    def rows(t):
        return t.reshape(n // bb, bb, t.shape[1])

    o, s_new = pl.pallas_call(
        functools.partial(_ret_sample_kernel, heads=heads),
        grid=(n // bb,),
        in_specs=[pl.BlockSpec((1, heads, dk, bb), lambda i: (i, 0, 0, 0)),
                  pl.BlockSpec((1, heads, dk, bb), lambda i: (i, 0, 0, 0)),
                  pl.BlockSpec((1, bb, hk), lambda i: (i, 0, 0)), pl.BlockSpec((1, bb, hk), lambda i: (i, 0, 0)),
                  pl.BlockSpec((1, bb, hv), lambda i: (i, 0, 0)),
                  pl.BlockSpec((bb, heads, dk, dv), lambda i: (i, 0, 0, 0))],
        out_specs=[pl.BlockSpec((1, bb, hv), lambda i: (i, 0, 0)),
                   pl.BlockSpec((bb, heads, dk, dv), lambda i: (i, 0, 0, 0))],
        out_shape=[jax.ShapeDtypeStruct((n // bb, bb, hv), F32), jax.ShapeDtypeStruct(s_prev.shape, F32)],
        compiler_params=_params(("parallel",)),
    )(cols(qr), cols(kr), rows(qr), rows(kr), rows(v), s_prev)
    o = o.reshape(n, hv)
    bb = _tile(n, SUBLANES)

    m = mem_k.shape[1]
    c = pl.pallas_call(
        functools.partial(_attn_sample_kernel, heads=dims['mheads']),
        grid=(n // bb,),
        in_specs=[pl.BlockSpec((bb,) + mem_k.shape[2:], lambda i: (i, 0, 0)),
                  pl.BlockSpec((bb,) + mem_k.shape[1:], lambda i: (i, 0, 0, 0)),
                  pl.BlockSpec((bb,) + mem_v.shape[1:], lambda i: (i, 0, 0, 0))],
        out_specs=pl.BlockSpec((bb,) + mem_k.shape[2:], lambda i: (i, 0, 0)),
        out_shape=jax.ShapeDtypeStruct((n,) + mem_k.shape[2:], F32),
        compiler_params=_params(("parallel",)),
    )(mq.reshape((n,) + mem_k.shape[2:]), mem_k, mem_v).reshape(n, hm)

    first = xn2_all.shape[0] - n
    assert first % n == 0
    args = (x, a_g, o, g, c, gb, gc, lw['w_br_ret'], lw['w_br_mem'], lw['w_out'], lw['norm_ffn'], lw['w_rt'],
            lw['b_rt'], counts0)
    x1, xn2_all, route, counts = pl.pallas_call(
        functools.partial(_merge_sample_kernel, heads=heads, n_groups=dims['g'], per_group=dims['e']),
        grid=(1,),
        in_specs=[_const_spec(a.shape) for a in args] + [pl.BlockSpec(memory_space=pl.ANY)],
        out_specs=[pl.BlockSpec((n, d), lambda i: (0, 0)), pl.BlockSpec((n, d // 2), lambda i: (first // n, 0)),
                   pl.BlockSpec((n, LANES), lambda i: (0, 0)), pl.BlockSpec((1, LANES), lambda i: (0, 0))],
        out_shape=[jax.ShapeDtypeStruct((n, d), F32), jax.ShapeDtypeStruct(xn2_all.shape, F32),
                   jax.ShapeDtypeStruct((n, LANES), F32), jax.ShapeDtypeStruct((1, LANES), F32)],
        input_output_aliases={len(args): 1},
        compiler_params=_params(("arbitrary",)),
    )(*args, xn2_all)
    return x1, xn2_all, route, counts, conv_new, h_new, s_new


def _moe(xn2, route_parts, counts, lw, dims, tm=512, n_pass=4):
    t, d = xn2.shape[0], dims['d']
    g, n_exp = dims['g'], dims['g'] * dims['e']
    ff = lw['w_gate'].shape[2]
    counts = counts[0, g:g + n_exp].astype(jnp.int32)
    padded = ((counts + tm - 1) // tm) * tm
    ends = jnp.cumsum(padded)
    offs = ends - padded
    n_tiles = n_pass * ((TOP_K * t + n_exp * tm + n_pass * tm - 1) // (n_pass * tm))
    n_slots = n_tiles * tm
    per_pass = n_tiles // n_pass
    experts = jnp.arange(n_exp, dtype=jnp.int32)[None, :]
    dests = []
    for j in range(TOP_K):
        per_part = []
        for r in route_parts:
            ids = r[:, j].astype(jnp.int32)
            base = jnp.sum(jnp.where(ids[:, None] == experts, offs[None, :], 0), axis=1)
            per_part.append(base + r[:, 4 + j].astype(jnp.int32))
        dests.append(per_part)
    dest = jnp.concatenate([dp for per_part in dests for dp in per_part], axis=0)
    tok = jnp.arange(t, dtype=jnp.int32)
    n_pad = n_slots - TOP_K * t
    pad_cnt = jnp.concatenate([padded - counts, n_slots - ends[-1:]])
    pad_from = jnp.concatenate([offs + counts, ends[-1:]])
    pad_end = jnp.cumsum(pad_cnt)
    pad_i = jnp.arange(n_pad, dtype=jnp.int32)
    which = jnp.sum((pad_end[None, :] <= pad_i[:, None]).astype(jnp.int32), axis=1)
    shift = pad_from - (pad_end - pad_cnt)
    groups = jnp.arange(n_exp + 1, dtype=jnp.int32)[None, :]
    pad_slot = pad_i + jnp.sum(jnp.where(which[:, None] == groups, shift[None, :], 0), axis=1)
    _, src = lax.sort_key_val(jnp.concatenate([dest, pad_slot]),
                              jnp.concatenate([tok] * TOP_K + [pad_i % t]))
    n_used = (ends[-1] // tm).astype(jnp.int32)
    tile_row = jnp.arange(n_tiles, dtype=jnp.int32) * tm
    tile_e = jnp.sum((ends[None, :] <= tile_row[:, None]).astype(jnp.int32), axis=1)
    last_e = jnp.sum((ends <= (n_used - 1) * tm).astype(jnp.int32))
    tile_e = jnp.where(tile_row < ends[-1], tile_e, last_e)
    def expert_pass(first_tile, ys_table):
        xs = xn2.at[src[first_tile * tm:(first_tile + per_pass) * tm]].get(mode='promise_in_bounds')

        te = tile_e[first_tile:first_tile + per_pass]
        nu = jnp.clip(n_used - first_tile, 0, per_pass)
        idx = jnp.arange(per_pass, dtype=jnp.int32)
        first = (idx < nu) & ((idx == 0) | (te != jnp.roll(te, 1)))
        slot = (jnp.cumsum(first.astype(jnp.int32)) - 1) % 2
        later = lax.cummin(jnp.where(first, idx, per_pass), reverse=True)
        nxt_at = jnp.concatenate([later[1:], jnp.full((1,), per_pass, jnp.int32)])
        nxt = jnp.where(nxt_at < per_pass, te[jnp.minimum(nxt_at, per_pass - 1)], -1)
        plan = jnp.stack([te, first.astype(jnp.int32), slot, nxt]).astype(jnp.int32)

        def rows(i, plan, nu):
            return (jnp.maximum(jnp.minimum(i, nu[0] - 1), 0), 0)

        hbm = pl.BlockSpec(memory_space=pl.ANY)
        in_specs = [pl.BlockSpec((tm, d // 2), rows), hbm, hbm, hbm]
        args = [xs, lw['w_gate'], lw['w_up'], lw['w_down']]
        aliases = {}
        if ys_table is not None:
            in_specs.append(hbm)
            args.append(ys_table)
            aliases = {2 + len(args) - 1: 0}
        return pl.pallas_call(
            _expert_kernel,
            grid_spec=pltpu.PrefetchScalarGridSpec(
                num_scalar_prefetch=2,
                grid=(per_pass,),
                in_specs=in_specs,
                out_specs=pl.BlockSpec((tm, d // 2), lambda i, plan, nu: (first_tile + i, 0)),
                scratch_shapes=[pltpu.VMEM((2, d, ff), F32), pltpu.VMEM((2, d, ff), F32),
                                pltpu.VMEM((2, ff, d), F32), pltpu.SemaphoreType.DMA((3, 2))],
            ),
            out_shape=jax.ShapeDtypeStruct((n_slots, d // 2), F32),
            input_output_aliases=aliases,
            compiler_params=_params(("arbitrary",)),
        )(plan, nu.reshape(1), *args)

    ys = None
    for p in range(n_pass):
        ys = expert_pass(p * per_pass, ys)
    return [(ys, tuple(dests[j][p] for j in range(TOP_K))) for p in range(len(route_parts))]


def _final(x1, ys, dest_pair, route, g, n_parts):
    t, d = x1.shape
    rows = t // n_parts
    tl = _tile(rows, 512)
    out = None
    for p in range(n_parts):
        first = p * rows // tl
        y1, y2 = (ys.at[dp[p * rows:(p + 1) * rows]].get(mode='promise_in_bounds') for dp in dest_pair)
        row = pl.BlockSpec((tl, d), lambda i, first=first: (first + i, 0))
        packed = pl.BlockSpec((tl, d // 2), lambda i: (i, 0))
        in_specs = [row, packed, packed, pl.BlockSpec((tl, LANES), lambda i, first=first: (first + i, 0)),
                    _const_spec((1, d))]
        args = [x1, y1, y2, route, g]
        aliases = {}
        if out is not None:
            in_specs.append(pl.BlockSpec(memory_space=pl.ANY))
            args.append(out)
            aliases = {len(args) - 1: 0}
        out = pl.pallas_call(
            _final_kernel,
            grid=(rows // tl,),
            in_specs=in_specs,
            out_specs=row,
            out_shape=jax.ShapeDtypeStruct((t, d), F32),
            input_output_aliases=aliases,
            compiler_params=_params(("parallel",)),
        )(*args)
    return out


def kernel(x_prompt, x_sample, mem_prompt, state_conv, state_lru, state_ret, cache_mem_k, cache_mem_v, norm_mix, norm_ffn, norm_mem, norm_final, w_in, w_mem_kv, conv_w, conv_b, w_r, b_r, w_i, b_i, lru_lambda, w_branch_lru, w_branch_ret, w_branch_mem, w_out, w_grp, b_grp, w_exp_router, b_exp_router, w_gate, w_up, w_down):
    depth = w_in.shape[0]
    assert depth == 1, "the two request groups are chained per layer only for a single-layer trunk"
    assert x_sample.shape[1] == 1
    bp, lp, d = x_prompt.shape
    ns = x_sample.shape[0]
    heads, dk, dv = state_ret.shape[2], state_ret.shape[3], state_ret.shape[4]
    mheads, mdh = cache_mem_k.shape[3], cache_mem_k.shape[4]
    dims = dict(d=d, w=state_lru.shape[2], heads=heads, dk=dk, dv=dv, hk=heads * dk, hv=heads * dv,
                mheads=mheads, hm=mheads * mdh, g=w_exp_router.shape[1], e=w_exp_router.shape[3])
    layer = 0
    p = dict(norm_mix=norm_mix[layer], norm_ffn=norm_ffn[layer], norm_mem=norm_mem[layer], w_in=w_in[layer],
             w_mem_kv=w_mem_kv[layer], conv_w=conv_w[layer], conv_b=conv_b[layer], w_r=w_r[layer], b_r=b_r[layer],
             w_i=w_i[layer], b_i=b_i[layer], lru_lambda=lru_lambda[layer], w_branch_lru=w_branch_lru[layer],
             w_branch_ret=w_branch_ret[layer], w_branch_mem=w_branch_mem[layer], w_out=w_out[layer],
             w_grp=w_grp[layer], b_grp=b_grp[layer], w_exp_router=w_exp_router[layer],
             b_exp_router=b_exp_router[layer], w_gate=w_gate[layer], w_up=w_up[layer], w_down=w_down[layer])
    lw = _prep_layer(p, dims)

    x1p, xn2, route_p, counts, conv_p, lru_p, ret_p, mk, mv = _prompt_mixer(x_prompt, mem_prompt, lw, dims, ns)
    x1s, xn2, route_s, counts, conv_s, lru_s, ret_s = _sample_mixer(
        x_sample[:, 0], state_conv[layer], state_lru[layer], state_ret[layer], cache_mem_k[layer],
        cache_mem_v[layer], counts, xn2, lw, dims)

    tp = bp * lp
    route_p = route_p.reshape(tp, LANES)
    (ys_p, dest_p), (ys_s, dest_s) = _moe(xn2, [route_p, route_s], counts, lw, dims)
    yp = _final(x1p.reshape(tp, d), ys_p, dest_p, route_p, _row(norm_final), 2).reshape(bp, lp, d)
    ys = _final(x1s, ys_s, dest_s, route_s, _row(norm_final), 1).reshape(ns, 1, d)
    return (yp, ys, conv_p[None], lru_p[None], ret_p[None], mk[None], mv[None],
            conv_s[None], lru_s[None], ret_s[None])
```

```python
import functools
import itertools
import math

import jax
import jax.numpy as jnp
from jax import lax
from jax.experimental import pallas as pl
from jax.experimental.pallas import tpu as pltpu

F32 = jnp.float32
BF16 = jnp.bfloat16

EPS = 1e-6
LRU_C = 8.0
ROPE_BASE = 10000.0
PAST_LEN = 16384
TOP_K = 2
LANES = 128
SUBLANES = 8
VMEM_LIMIT = 56 * 1024 * 1024


def _dot(a, b):
    return jnp.dot(a, b, preferred_element_type=F32)


def _dot_nt(a, b):
    return lax.dot_general(a, b, (((1,), (1,)), ((), ())), preferred_element_type=F32)


def _dot_tn(a, b):
    return lax.dot_general(a, b, (((0,), (0,)), ((), ())), preferred_element_type=F32)


def _rms(x, g):
    return x * lax.rsqrt(jnp.mean(x * x, axis=-1, keepdims=True) + EPS) * g


def _sigmoid(x):
    return 0.5 * jnp.tanh(0.5 * x) + 0.5


def _silu(x):
    return x * _sigmoid(x)


def _gelu_tanh(x):
    return 0.5 * x * (1.0 + jnp.tanh(math.sqrt(2.0 / math.pi) * (x + 0.044715 * (x * x * x))))


def _softplus(x):
    return jnp.maximum(x, 0.0) + jnp.log1p(jnp.exp(-jnp.abs(x)))


def _const_spec(shape):
    nd = len(shape)
    return pl.BlockSpec(shape, lambda *_: (0,) * nd, pipeline_mode=pl.Buffered(1))


def _window_spec(rows, window):
    start, width = window
    return pl.BlockSpec((rows, width), lambda *_: (0, start // width), pipeline_mode=pl.Buffered(1))


def _params(sem):
    return pltpu.CompilerParams(dimension_semantics=sem, vmem_limit_bytes=VMEM_LIMIT)


def _lru_coeffs(uc, wr_ref, br, wi_ref, bi, lam):
    nb, bs = wr_ref.shape[0], wr_ref.shape[1]
    ucb = uc.astype(BF16)
    r_lin = jnp.concatenate([_dot(ucb[:, n * bs:(n + 1) * bs], wr_ref[n]) for n in range(nb)], axis=-1)
    i_lin = jnp.concatenate([_dot(ucb[:, n * bs:(n + 1) * bs], wi_ref[n]) for n in range(nb)], axis=-1)
    r = _sigmoid(r_lin + br)
    i = _sigmoid(i_lin + bi)
    log_a = (-LRU_C) * r * _softplus(-lam)
    a = jnp.exp(log_a)
    t = 1.0 - a * a
    mult = t * lax.rsqrt(jnp.maximum(t, 1e-37))
    return a, mult, i


def _scan_rows(a, b, h0):
    rows, width = a.shape
    grp = SUBLANES
    n_grp = rows // grp
    a = a.reshape(n_grp, grp, width)
    b = b.reshape(n_grp, grp, width)
    row = lax.broadcasted_iota(jnp.int32, a.shape, 1)
    d = 1
    while d < grp:
        a_sh = pltpu.roll(a, d, 1)
        b_sh = pltpu.roll(b, d, 1)
        keep = row >= d
        b = jnp.where(keep, b + a * b_sh, b)
        a = jnp.where(keep, a * a_sh, a)
        d *= 2
    out = []
    for g in range(n_grp):
        hg = a[g] * h0 + b[g]
        out.append(hg)
        h0 = hg[grp - 1:grp, :]
    return jnp.concatenate(out, axis=0)


def _rotary(t, cos, sin):
    half = t.shape[-1] // 2
    t1, t2 = t[:, :half], t[:, half:]
    return jnp.concatenate([t1 * cos - t2 * sin, t2 * cos + t1 * sin], axis=-1)


def _route(logits, run, n_groups, per_group):
    col = lax.broadcasted_iota(jnp.int32, logits.shape, 1)
    big = jnp.int32(1 << 20)
    neg = jnp.float32(-jnp.inf)
    gl = jnp.where(col < n_groups, logits, neg)
    gmax = jnp.max(gl, axis=-1, keepdims=True)
    g_idx = jnp.min(jnp.where(gl == gmax, col, big), axis=-1, keepdims=True)
    g_val = 1.0 / jnp.sum(jnp.exp(gl - gmax), axis=-1, keepdims=True)
    lo = n_groups + per_group * g_idx
    el = jnp.where((col >= lo) & (col < lo + per_group), logits, neg)
    m1 = jnp.max(el, axis=-1, keepdims=True)
    i1 = jnp.min(jnp.where(el == m1, col, big), axis=-1, keepdims=True)
    el2 = jnp.where(col == i1, neg, el)
    m2 = jnp.max(el2, axis=-1, keepdims=True)
    i2 = jnp.min(jnp.where(el2 == m2, col, big), axis=-1, keepdims=True)
    e2 = jnp.exp(m2 - m1)
    w1 = g_val / (1.0 + e2)
    w2 = g_val * e2 / (1.0 + e2)
    f1 = (i1 - n_groups).astype(F32)
    f2 = (i2 - n_groups).astype(F32)
    hit1, hit2 = col == i1, col == i2
    hits = jnp.where(hit1 | hit2, 1.0, 0.0)
    rows = logits.shape[0]
    earlier = lax.broadcasted_iota(jnp.int32, (rows, rows), 1) < lax.broadcasted_iota(jnp.int32, (rows, rows), 0)
    before = run + _dot(jnp.where(earlier, 1.0, 0.0).astype(BF16), hits.astype(BF16))
    r1 = jnp.sum(jnp.where(hit1, before, 0.0), axis=-1, keepdims=True)
    r2 = jnp.sum(jnp.where(hit2, before, 0.0), axis=-1, keepdims=True)
    route = f1
    for j, val in enumerate((f2, w1, w2, r1, r2), start=1):
        route = jnp.where(col == j, val, route)
    route = jnp.where(col > 5, 0.0, route)
    return route, run + jnp.sum(hits, axis=0, keepdims=True)


def _pack_pairs(x):
    half = x.shape[1] // 2
    lo = lax.bitcast_convert_type(x[:, :half].astype(F32), jnp.uint32)
    hi = lax.bitcast_convert_type(x[:, half:].astype(F32), jnp.uint32)
    return lax.bitcast_convert_type((lo >> 16) | (hi & jnp.uint32(0xFFFF0000)), F32)


def _unpack_pairs(w):
    u = lax.bitcast_convert_type(w, jnp.uint32)
    lo = lax.bitcast_convert_type(u << 16, F32)
    hi = lax.bitcast_convert_type(u & jnp.uint32(0xFFFF0000), F32)
    return jnp.concatenate([lo, hi], axis=1)


def _merge_tail(x, merged, wout_ref, gffn, wrt_ref, brt, run, n_groups, per_group):
    x1 = x + _dot(merged.astype(BF16), wout_ref[...])
    xn2 = _rms(x1, gffn).astype(BF16)
    logits = _dot(xn2, wrt_ref[...]) + brt
    route, run = _route(logits, run, n_groups, per_group)
    return x1, _pack_pairs(xn2), route, run


def _lru_prompt_steps(c, xn, wxy_ref, wga_ref, cw_ref, cb_ref, wr_ref, br_ref, wi_ref, bi_ref, lam_ref, wbr_ref,
                      out_ref, ccar, hcar, res):
    tl = xn.shape[0]
    width = hcar.shape[1]
    kw = cw_ref.shape[0]
    nb, bs = wr_ref.shape[0], wr_ref.shape[1]
    z = _dot(xn, wxy_ref[...])
    row = lax.broadcasted_iota(jnp.int32, (tl, bs), 0)
    hbs, lasts = [], []
    for n in range(nb):
        cs = slice(n * bs, (n + 1) * bs)
        ux = z[:, cs]
        acc = cw_ref[0:1, cs] * ux
        for j in range(1, kw):
            prev = jnp.where(row == 0, ccar[j - 1:j, cs], pltpu.roll(acc, 1, 0))
            ccar[j - 1:j, cs] = acc[tl - 1:tl, :]
            acc = prev + cw_ref[j:j + 1, cs] * ux
        uc = acc + cb_ref[:, cs]
        ucb = uc.astype(BF16)
        r = _sigmoid(_dot(ucb, wr_ref[n]) + br_ref[:, cs])
        gate = _sigmoid(_dot(ucb, wi_ref[n]) + bi_ref[:, cs])
        log_a = (-LRU_C) * r * _softplus(-lam_ref[:, cs])
        a = jnp.exp(log_a)
        t = 1.0 - a * a
        mult = t * lax.rsqrt(jnp.maximum(t, 1e-37))
        mult = jnp.where(row + c * tl == 0, 1.0, mult)
        h = _scan_rows(a, mult * gate * uc, hcar[:, cs])
        hcar[:, cs] = h[tl - 1:tl, :]
        lasts.append(h[tl - 1:tl, :])
        hbs.append((h * _gelu_tanh(z[:, width + n * bs: width + (n + 1) * bs])).astype(BF16))
        yield
    res['tail'] = z[tl - (kw - 1):tl, :width]
    res['h_last'] = jnp.concatenate(lasts, axis=1)
    a_out = _dot(jnp.concatenate(hbs, axis=1), wbr_ref[...])
    ga = _dot(xn, wga_ref[...])
    out_ref[0] = (_sigmoid(ga) * a_out).astype(out_ref.dtype)
    yield


def _ret_proj_steps(xn, w_refs, res, cols=512):
    for name, w_ref in w_refs:
        parts = []
        for j in range(0, w_ref.shape[1], cols):
            parts.append(_dot(xn, w_ref[:, j:j + cols]))
            yield
        res[name] = jnp.concatenate(parts, axis=1)


def _ret_prompt_steps(xn, proj, wgb_ref, cos_ref, sin_ref, wbr_ref, out_ref, s_scr, y_scr, heads, chunk):
    tl = xn.shape[0]
    q, k, v, g = proj['q'], proj['k'], proj['v'], proj['g']
    dk = q.shape[1] // heads
    dv = v.shape[1] // heads
    cos, sin = cos_ref[...], sin_ref[...]
    n_i = lax.broadcasted_iota(jnp.int32, (chunk, chunk), 0)
    m_i = lax.broadcasted_iota(jnp.int32, (chunk, chunk), 1)
    diff = (n_i - m_i).astype(F32)
    rowk = lax.broadcasted_iota(jnp.int32, (chunk, dk), 0).astype(F32)
    for h in range(heads):
        log_g = math.log1p(-(2.0 ** (-5.0 - h)))
        dmask = jnp.where(diff >= 0, jnp.exp(jnp.maximum(diff, 0.0) * log_g), 0.0)
        q_decay = jnp.exp((rowk + 1.0) * log_g)
        k_decay = jnp.exp((chunk - 1.0 - rowk) * log_g)
        s_decay = math.exp(chunk * log_g)
        for sub in range(tl // chunk):
            r0 = sub * chunk
            cs, sn = cos[r0:r0 + chunk], sin[r0:r0 + chunk]
            qr = _rotary(q[r0:r0 + chunk, h * dk:(h + 1) * dk], cs, sn)
            kr = _rotary(k[r0:r0 + chunk, h * dk:(h + 1) * dk], cs, sn) * (dk ** -0.5)
            vh = v[r0:r0 + chunk, h * dv:(h + 1) * dv].astype(BF16)
            s = s_scr[h]
            att = _dot_nt(qr.astype(BF16), kr.astype(BF16)) * dmask
            o = _dot(att.astype(BF16), vh) + _dot((qr * q_decay).astype(BF16), s.astype(BF16))
            s_scr[h] = s * s_decay + _dot_tn((kr * k_decay).astype(BF16), vh)
            o = o * lax.rsqrt(jnp.mean(o * o, axis=-1, keepdims=True) + EPS)
            gh = g[r0:r0 + chunk, h * dv:(h + 1) * dv]
            y_scr[r0:r0 + chunk, h * dv:(h + 1) * dv] = (_silu(gh) * o).astype(BF16)
            yield
    b_out = _dot(y_scr[...], wbr_ref[...])
    gb = _dot(xn, wgb_ref[...])
    out_ref[0] = (_sigmoid(gb) * b_out).astype(out_ref.dtype)


def _lru_ret_prompt_kernel(x_ref, g_ref, wxy_ref, wga_ref, cw_ref, cb_ref, wr_ref, br_ref, wi_ref, bi_ref,
                           lam_ref, wbl_ref, wq_ref, wk_ref, wv_ref, wg_ref, wgb_ref, cos_ref, sin_ref, wbr_ref,
                           a_ref, conv_ref, h_ref, b_ref, s_out_ref, ccar, hcar, s_scr, y_scr, *, heads, chunk):
    c = pl.program_id(1)

    @pl.when(c == 0)
    def _():
        ccar[...] = jnp.zeros_like(ccar)
        hcar[...] = jnp.zeros_like(hcar)
        s_scr[...] = jnp.zeros_like(s_scr)

    xn = _rms(x_ref[0], g_ref[...]).astype(BF16)
    lru, proj = {}, {}
    lru_steps = _lru_prompt_steps(c, xn, wxy_ref, wga_ref, cw_ref, cb_ref, wr_ref, br_ref, wi_ref, bi_ref,
                                  lam_ref, wbl_ref, a_ref, ccar, hcar, lru)
    proj_steps = _ret_proj_steps(xn, (('q', wq_ref), ('k', wk_ref), ('v', wv_ref), ('g', wg_ref)), proj)
    n_lru = wr_ref.shape[0] + 1
    n_proj = sum(-(-r.shape[1] // 512) for r in (wq_ref, wk_ref, wv_ref, wg_ref))
    n_ret = heads * (x_ref.shape[1] // chunk)
    lru_done = 0
    for i in range(n_proj):
        next(proj_steps)
        want = ((i + 1) * n_lru) // (n_proj + n_ret)
        while lru_done < want:
            next(lru_steps)
            lru_done += 1
    for _ in proj_steps:
        pass
    ret_steps = _ret_prompt_steps(xn, proj, wgb_ref, cos_ref, sin_ref, wbr_ref, b_ref, s_scr, y_scr, heads, chunk)
    for i in range(n_ret):
        next(ret_steps)
        want = ((n_proj + i + 1) * n_lru) // (n_proj + n_ret)
        while lru_done < want:
            next(lru_steps)
            lru_done += 1
    for _ in itertools.chain(lru_steps, ret_steps):
        pass

    @pl.when(c == pl.num_programs(1) - 1)
    def _():
        conv_ref[0] = lru['tail']
        h_ref[0] = lru['h_last']
        s_out_ref[0] = s_scr[...]


def _memkv_kernel(m_ref, g_ref, w_ref, k_ref, v_ref, k4_ref, v4_ref):
    width = k_ref.shape[-1]
    heads, dh = k4_ref.shape[2], k4_ref.shape[3]
    mn = _rms(m_ref[0], g_ref[...]).astype(BF16)
    kv = _dot(mn, w_ref[...])
    k_ref[0] = kv[:, :width]
    v_ref[0] = kv[:, width:]
    for h in range(heads):
        k4_ref[0, :, h, :] = kv[:, h * dh:(h + 1) * dh]
        v4_ref[0, :, h, :] = kv[:, width + h * dh: width + (h + 1) * dh]


def _attn_merge_prompt_kernel(x_ref, ag_ref, bg_ref, mk_ref, mv_ref, g_ref, wmq_ref, wgc_ref, wbm_ref,
                              wout_ref, gffn_ref, wrt_ref, brt_ref, x1_ref, xn2_ref, route_ref, cnt_ref,
                              *, heads, n_groups, per_group):
    @pl.when((pl.program_id(0) == 0) & (pl.program_id(1) == 0))
    def _():
        cnt_ref[...] = jnp.zeros_like(cnt_ref)

    x = x_ref[0]
    xn = _rms(x, g_ref[...]).astype(BF16)
    mq = _dot(xn, wmq_ref[...]).astype(BF16)
    dh = mq.shape[1] // heads
    mk = mk_ref[0].astype(BF16)
    mv = mv_ref[0].astype(BF16)
    cs = []
    for h in range(heads):
        hs = slice(h * dh, (h + 1) * dh)
        lg = _dot_nt(mq[:, hs], mk[:, hs]) * (dh ** -0.5)
        p = jnp.exp(lg - jnp.max(lg, axis=-1, keepdims=True))
        den = jnp.sum(p, axis=-1, keepdims=True)
        cs.append(_dot(p.astype(BF16), mv[:, hs]) / den)
    c_out = _dot(jnp.concatenate(cs, axis=-1).astype(BF16), wbm_ref[...])
    gc = _dot(xn, wgc_ref[...])
    merged = _sigmoid(gc) * c_out + ag_ref[0].astype(F32) + bg_ref[0].astype(F32)
    x1, xn2, route, run = _merge_tail(x, merged, wout_ref, gffn_ref[...], wrt_ref, brt_ref[...], cnt_ref[...],
                                      n_groups, per_group)
    x1_ref[0] = x1
    xn2_ref[...] = xn2
    route_ref[0] = route
    cnt_ref[...] = run


def _proj_kernel(x_ref, g_ref, w_ref, z_ref):
    xn = _rms(x_ref[...], g_ref[...]).astype(BF16)
    z_ref[...] = _dot(xn, w_ref[...])


def _lru_sample_kernel(ux_ref, uy_ref, ga_ref, cprev_ref, hprev_ref, cw_ref, cb_ref, wr_ref, br_ref, wi_ref,
                       bi_ref, lam_ref, wbr_ref, out_ref, conv_ref, h_ref, *, at_start):
    kw = cw_ref.shape[0]
    ux = ux_ref[...]
    uc = cb_ref[...] + cw_ref[kw - 1:kw, :] * ux
    for j in range(kw - 1):
        uc = uc + cw_ref[j:j + 1, :] * cprev_ref[:, j, :]
    a, mult, gate = _lru_coeffs(uc, wr_ref, br_ref[...], wi_ref, bi_ref[...], lam_ref[...])
    if at_start:
        mult = jnp.ones_like(mult)
    h = a * hprev_ref[...] + mult * gate * uc
    h_ref[...] = h
    for j in range(kw - 2):
        conv_ref[:, j, :] = cprev_ref[:, j + 1, :]
    conv_ref[:, kw - 2, :] = ux
    a_out = _dot((h * _gelu_tanh(uy_ref[...])).astype(BF16), wbr_ref[...])
    out_ref[...] = _sigmoid(ga_ref[...]) * a_out


def _rot_sample_kernel(q_ref, k_ref, cos_ref, sin_ref, qo_ref, ko_ref, *, heads):
    dk = q_ref.shape[1] // heads
    cos, sin = cos_ref[...], sin_ref[...]
    for h in range(heads):
        hs = slice(h * dk, (h + 1) * dk)
        qo_ref[:, hs] = _rotary(q_ref[:, hs], cos, sin)
        ko_ref[:, hs] = _rotary(k_ref[:, hs], cos, sin) * (dk ** -0.5)


def _ret_sample_kernel(qt_ref, kt_ref, q_ref, k_ref, v_ref, s_ref, o_ref, s_out_ref, *, heads):
    bb = q_ref.shape[1]
    dk = s_ref.shape[2]
    dv = s_ref.shape[3]
    for j in range(bb):
        for h in range(heads):
            decay = 1.0 - 2.0 ** (-5.0 - h)
            qcol = qt_ref[0, h, :, j:j + 1]
            kcol = kt_ref[0, h, :, j:j + 1]
            qrow = q_ref[0, j:j + 1, h * dk:(h + 1) * dk]
            krow = k_ref[0, j:j + 1, h * dk:(h + 1) * dk]
            vrow = v_ref[0, j:j + 1, h * dv:(h + 1) * dv]
            s = s_ref[j, h]
            att = jnp.sum(qrow * krow, axis=-1, keepdims=True)
            o = att * vrow + jnp.sum((qcol * decay) * s, axis=0, keepdims=True)
            o_ref[0, j:j + 1, h * dv:(h + 1) * dv] = o
            s_out_ref[j, h] = s * decay + kcol * vrow


def _attn_sample_kernel(mq_ref, k_ref, v_ref, c_ref, *, heads):
    bb, _, dh = mq_ref.shape
    for j in range(bb):
        q = mq_ref[j]
        lg = jnp.sum(k_ref[j] * q[None], axis=-1, keepdims=True) * (dh ** -0.5)
        p = jnp.exp(lg - jnp.max(lg, axis=0, keepdims=True))
        den = jnp.sum(p, axis=0)
        c_ref[j] = jnp.sum(v_ref[j] * p, axis=0) / den


def _merge_sample_kernel(x_ref, ag_ref, o_ref, g_ref, c_ref, gb_ref, gc_ref, wbr_ref, wbm_ref, wout_ref,
                         gffn_ref, wrt_ref, brt_ref, run_ref, xn2_table_ref, x1_ref, xn2_ref, route_ref, cnt_ref,
                         *, heads, n_groups, per_group):
    del xn2_table_ref
    dv = o_ref.shape[1] // heads
    ys = []
    for h in range(heads):
        o = o_ref[:, h * dv:(h + 1) * dv]
        o = o * lax.rsqrt(jnp.mean(o * o, axis=-1, keepdims=True) + EPS)
        ys.append((_silu(g_ref[:, h * dv:(h + 1) * dv]) * o).astype(BF16))
    b_out = _dot(jnp.concatenate(ys, axis=-1), wbr_ref[...])
    c_out = _dot(c_ref[...].astype(BF16), wbm_ref[...])
    merged = ag_ref[...] + _sigmoid(gb_ref[...]) * b_out + _sigmoid(gc_ref[...]) * c_out
    x1, xn2, route, run = _merge_tail(x_ref[...], merged, wout_ref, gffn_ref[...], wrt_ref, brt_ref[...],
                                      run_ref[...], n_groups, per_group)
    x1_ref[...] = x1
    xn2_ref[...] = xn2
    route_ref[...] = route
    cnt_ref[...] = run


def _expert_kernel(plan_ref, nu_ref, xs_ref, wg_hbm, wu_hbm, wd_hbm, *rest):
    ys_ref, wg_buf, wu_buf, wd_buf, sem = rest[-5:]
    i = pl.program_id(0)
    expert, first, slot, nxt = plan_ref[0, i], plan_ref[1, i], plan_ref[2, i], plan_ref[3, i]
    used = i < nu_ref[0]

    def weight_copies(e, s):
        return (pltpu.make_async_copy(wg_hbm.at[e], wg_buf.at[s], sem.at[0, s]),
                pltpu.make_async_copy(wu_hbm.at[e], wu_buf.at[s], sem.at[1, s]),
                pltpu.make_async_copy(wd_hbm.at[e], wd_buf.at[s], sem.at[2, s]))

    @pl.when(used & (i == 0))
    def _():
        for cp in weight_copies(expert, slot):
            cp.start()

    @pl.when(used & (first == 1))
    def _():
        for cp in weight_copies(expert, slot):
            cp.wait()

    @pl.when(used & (first == 1) & (nxt >= 0))
    def _():
        for cp in weight_copies(nxt, 1 - slot):
            cp.start()

    @pl.when(used)
    def _():
        x = _unpack_pairs(xs_ref[...]).astype(BF16)
        hg = _dot(x, wg_buf[slot].astype(BF16))
        hu = _dot(x, wu_buf[slot].astype(BF16))
        hid = (_silu(hg) * hu).astype(BF16)
        ys_ref[...] = _pack_pairs(_dot(hid, wd_buf[slot].astype(BF16)).astype(BF16))

    @pl.when(jnp.logical_not(used))
    def _():
        ys_ref[...] = jnp.zeros_like(ys_ref)


def _final_kernel(x1_ref, y1_ref, y2_ref, route_ref, g_ref, *rest):
    out_ref = rest[-1]
    w1 = route_ref[:, 2:3]
    w2 = route_ref[:, 3:4]
    x2 = x1_ref[...] + (w1 * _unpack_pairs(y1_ref[...]) + w2 * _unpack_pairs(y2_ref[...]))
    out_ref[...] = _rms(x2, g_ref[...])


def _tile(n, target):
    t = min(n, target)
    while n % t:
        t //= 2
    return t


def _row(v):
    return v.reshape(1, -1).astype(F32)


def _prep_layer(p, dims):
    d, w, hk, hv, hm = dims['d'], dims['w'], dims['hk'], dims['hv'], dims['hm']
    w_in = p['w_in']
    starts = {}
    o0 = 0
    for name, width in (('xy', 2 * w), ('q', hk), ('k', hk), ('v', hv), ('g', hv), ('mq', hm),
                        ('ga', d), ('gb', d), ('gc', d)):
        assert o0 % width == 0, "each window must start on a multiple of its width"
        starts[name] = (o0, width)
        o0 += width
    g, e = p['w_exp_router'].shape[0], p['w_exp_router'].shape[2]
    w_rt = jnp.concatenate([p['w_grp'], jnp.moveaxis(p['w_exp_router'], 0, 1).reshape(d, g * e)], axis=1)
    w_rt = jnp.pad(w_rt, ((0, 0), (0, LANES - w_rt.shape[1]))).astype(BF16)
    b_rt = jnp.pad(jnp.concatenate([p['b_grp'], p['b_exp_router'].reshape(-1)]), (0, LANES - g - g * e))
    return dict(
        w_in=w_in.astype(BF16), win=starts,
        norm_mix=_row(p['norm_mix']), norm_ffn=_row(p['norm_ffn']), norm_mem=_row(p['norm_mem']),
        w_mem_kv=p['w_mem_kv'].astype(BF16),
        conv_w=p['conv_w'].astype(F32), conv_b=_row(p['conv_b']),
        w_r=p['w_r'].astype(BF16), b_r=_row(p['b_r']), w_i=p['w_i'].astype(BF16), b_i=_row(p['b_i']),
        lam=_row(p['lru_lambda']),
        w_br_lru=p['w_branch_lru'].astype(BF16), w_br_ret=p['w_branch_ret'].astype(BF16),
        w_br_mem=p['w_branch_mem'].astype(BF16), w_out=p['w_out'].astype(BF16),
        w_rt=w_rt, b_rt=_row(b_rt),
        w_gate=p['w_gate'].reshape((g * e,) + p['w_gate'].shape[2:]),
        w_up=p['w_up'].reshape((g * e,) + p['w_up'].shape[2:]),
        w_down=p['w_down'].reshape((g * e,) + p['w_down'].shape[2:]),
    )


def _rope_tables(pos, dk):
    half = dk // 2
    inv = ROPE_BASE ** (-jnp.linspace(0.0, 1.0, half, dtype=F32))
    ang = pos.astype(F32)[:, None] * inv[None, :]
    return jnp.cos(ang), jnp.sin(ang)


def _prompt_mixer(x, mem, lw, dims, extra_rows):
    b, l, d = x.shape
    w, heads, dk, dv = dims['w'], dims['heads'], dims['dk'], dims['dv']
    kw = lw['conv_w'].shape[0]
    tl = _tile(l, 256)
    grid = (b, l // tl)
    xspec = pl.BlockSpec((1, tl, d), lambda i, c: (i, c, 0))
    sem = ("parallel", "arbitrary")
    w_in, win = lw['w_in'], lw['win']

    chunk = _tile(tl, 256)
    cos, sin = _rope_tables(jnp.arange(l, dtype=jnp.int32), dk)
    half = dk // 2
    a_g, conv_new, h_last, b_g, s_new = pl.pallas_call(
        functools.partial(_lru_ret_prompt_kernel, heads=heads, chunk=chunk),
        grid=grid,
        in_specs=[xspec, _const_spec((1, d)), _window_spec(d, win['xy']), _window_spec(d, win['ga']),
                  _const_spec((kw, w)), _const_spec((1, w)), _const_spec(lw['w_r'].shape), _const_spec((1, w)),
                  _const_spec(lw['w_i'].shape), _const_spec((1, w)), _const_spec((1, w)), _const_spec((w, d)),
                  _window_spec(d, win['q']), _window_spec(d, win['k']),
                  _window_spec(d, win['v']), _window_spec(d, win['g']), _window_spec(d, win['gb']),
                  pl.BlockSpec((tl, half), lambda i, c: (c, 0)), pl.BlockSpec((tl, half), lambda i, c: (c, 0)),
                  _const_spec((heads * dv, d))],
        out_specs=[pl.BlockSpec((1, tl, d), lambda i, c: (i, c, 0)),
                   pl.BlockSpec((1, kw - 1, w), lambda i, c: (i, 0, 0)),
                   pl.BlockSpec((1, 1, w), lambda i, c: (i, 0, 0)),
                   pl.BlockSpec((1, tl, d), lambda i, c: (i, c, 0)),
                   pl.BlockSpec((1, heads, dk, dv), lambda i, c: (i, 0, 0, 0))],
        out_shape=[jax.ShapeDtypeStruct((b, l, d), BF16), jax.ShapeDtypeStruct((b, kw - 1, w), F32),
                   jax.ShapeDtypeStruct((b, 1, w), F32),
                   jax.ShapeDtypeStruct((b, l, d), BF16), jax.ShapeDtypeStruct((b, heads, dk, dv), F32)],
        scratch_shapes=[pltpu.VMEM((SUBLANES, w), F32), pltpu.VMEM((1, w), F32),
                        pltpu.VMEM((heads, dk, dv), F32), pltpu.VMEM((tl, heads * dv), BF16)],
        compiler_params=_params(sem),
    )(x, lw['norm_mix'], w_in, w_in, lw['conv_w'], lw['conv_b'], lw['w_r'], lw['b_r'],
      lw['w_i'], lw['b_i'], lw['lam'], lw['w_br_lru'],
      w_in, w_in, w_in, w_in, w_in, cos, sin, lw['w_br_ret'])

    m = mem.shape[1]
    hm = dims['hm']
    mh = dims['mheads']
    flat = pl.BlockSpec((1, m, hm), lambda i: (i, 0, 0))
    per_head = pl.BlockSpec((1, m, mh, hm // mh), lambda i: (i, 0, 0, 0))
    mk, mv, mk4, mv4 = pl.pallas_call(
        _memkv_kernel,
        grid=(b,),
        in_specs=[pl.BlockSpec((1, m, d), lambda i: (i, 0, 0)), _const_spec((1, d)), _const_spec((d, 2 * hm))],
        out_specs=[flat, flat, per_head, per_head],
        out_shape=[jax.ShapeDtypeStruct((b, m, hm), F32), jax.ShapeDtypeStruct((b, m, hm), F32),
                   jax.ShapeDtypeStruct((b, m, mh, hm // mh), F32), jax.ShapeDtypeStruct((b, m, mh, hm // mh), F32)],
        compiler_params=_params(("parallel",)),
    )(mem, lw['norm_mem'], lw['w_mem_kv'])

    tm = _tile(l, 1024)
    mspec = pl.BlockSpec((1, tm, d), lambda i, c: (i, c, 0))
    x1, xn2, route, counts = pl.pallas_call(
        functools.partial(_attn_merge_prompt_kernel, heads=dims['mheads'], n_groups=dims['g'],
                          per_group=dims['e']),
        grid=(b, l // tm),
        in_specs=[mspec, mspec, mspec,
                  pl.BlockSpec((1, m, hm), lambda i, c: (i, 0, 0)), pl.BlockSpec((1, m, hm), lambda i, c: (i, 0, 0)),
                  _const_spec((1, d)), _window_spec(d, win['mq']), _window_spec(d, win['gc']), _const_spec((hm, d)),
                  _const_spec((d, d)), _const_spec((1, d)), _const_spec((d, LANES)), _const_spec((1, LANES))],
        out_specs=[mspec, pl.BlockSpec((tm, d // 2), lambda i, c: (i * (l // tm) + c, 0)),
                   pl.BlockSpec((1, tm, LANES), lambda i, c: (i, c, 0)),
                   pl.BlockSpec((1, LANES), lambda i, c: (0, 0))],
        out_shape=[jax.ShapeDtypeStruct((b, l, d), F32), jax.ShapeDtypeStruct((b * l + extra_rows, d // 2), F32),
                   jax.ShapeDtypeStruct((b, l, LANES), F32), jax.ShapeDtypeStruct((1, LANES), F32)],
        compiler_params=_params(("arbitrary", "arbitrary")),
    )(x, a_g, b_g, mk, mv, lw['norm_mix'], w_in, w_in, lw['w_br_mem'], lw['w_out'],
      lw['norm_ffn'], lw['w_rt'], lw['b_rt'])
    return x1, xn2, route, counts, conv_new, h_last[:, 0], s_new, mk4, mv4


def _sample_mixer(x, conv_prev, h_prev, s_prev, mem_k, mem_v, counts0, xn2_all, lw, dims):
    n, d = x.shape
    w, heads, dk, dv, hm = dims['w'], dims['heads'], dims['dk'], dims['dv'], dims['hm']
    hk, hv = heads * dk, heads * dv
    kw = lw['conv_w'].shape[0]
    n_in = lw['w_in'].shape[1]
    tn = _tile(n_in, 1024)
    z = pl.pallas_call(
        _proj_kernel,
        grid=(n_in // tn,),
        in_specs=[_const_spec((n, d)), _const_spec((1, d)), pl.BlockSpec((d, tn), lambda j: (0, j))],
        out_specs=pl.BlockSpec((n, tn), lambda j: (0, j)),
        out_shape=jax.ShapeDtypeStruct((n, n_in), F32),
        compiler_params=_params(("parallel",)),
    )(x, lw['norm_mix'], lw['w_in'])
    o0 = 0
    parts = []
    for sz in (w, w, hk, hk, hv, hv, hm, d, d, d):
        parts.append(z[:, o0:o0 + sz])
        o0 += sz
    ux, uy, q, k, v, g, mq, ga, gb, gc = parts

    a_g, conv_new, h_new = pl.pallas_call(
        functools.partial(_lru_sample_kernel, at_start=(PAST_LEN == 0)),
        out_shape=[jax.ShapeDtypeStruct((n, d), F32), jax.ShapeDtypeStruct((n, kw - 1, w), F32),
                   jax.ShapeDtypeStruct((n, w), F32)],
        compiler_params=pltpu.CompilerParams(vmem_limit_bytes=VMEM_LIMIT),
    )(ux, uy, ga, conv_prev, h_prev, lw['conv_w'], lw['conv_b'], lw['w_r'], lw['b_r'], lw['w_i'], lw['b_i'],
      lw['lam'], lw['w_br_lru'])

    cos, sin = _rope_tables(PAST_LEN + jnp.arange(1, dtype=jnp.int32), dk)
    qr, kr = pl.pallas_call(
        functools.partial(_rot_sample_kernel, heads=heads),
        out_shape=[jax.ShapeDtypeStruct((n, hk), F32), jax.ShapeDtypeStruct((n, hk), F32)],
    )(q, k, cos, sin)

    bb = _tile(n, 4)

    def cols(t):
        return t.reshape(n // bb, bb, heads, dk).transpose(0, 2, 3, 1)

    def rows(t):
        return t.reshape(n // bb, bb, t.shape[1])

    o, s_new = pl.pallas_call(
        functools.partial(_ret_sample_kernel, heads=heads),
        grid=(n // bb,),
        in_specs=[pl.BlockSpec((1, heads, dk, bb), lambda i: (i, 0, 0, 0)),
                  pl.BlockSpec((1, heads, dk, bb), lambda i: (i, 0, 0, 0)),
                  pl.BlockSpec((1, bb, hk), lambda i: (i, 0, 0)), pl.BlockSpec((1, bb, hk), lambda i: (i, 0, 0)),
                  pl.BlockSpec((1, bb, hv), lambda i: (i, 0, 0)),
                  pl.BlockSpec((bb, heads, dk, dv), lambda i: (i, 0, 0, 0))],
        out_specs=[pl.BlockSpec((1, bb, hv), lambda i: (i, 0, 0)),
                   pl.BlockSpec((bb, heads, dk, dv), lambda i: (i, 0, 0, 0))],
        out_shape=[jax.ShapeDtypeStruct((n // bb, bb, hv), F32), jax.ShapeDtypeStruct(s_prev.shape, F32)],
        compiler_params=_params(("parallel",)),
    )(cols(qr), cols(kr), rows(qr), rows(kr), rows(v), s_prev)
    o = o.reshape(n, hv)
    bb = _tile(n, SUBLANES)

    m = mem_k.shape[1]
    c = pl.pallas_call(
        functools.partial(_attn_sample_kernel, heads=dims['mheads']),
        grid=(n // bb,),
        in_specs=[pl.BlockSpec((bb,) + mem_k.shape[2:], lambda i: (i, 0, 0)),
                  pl.BlockSpec((bb,) + mem_k.shape[1:], lambda i: (i, 0, 0, 0)),
                  pl.BlockSpec((bb,) + mem_v.shape[1:], lambda i: (i, 0, 0, 0))],
        out_specs=pl.BlockSpec((bb,) + mem_k.shape[2:], lambda i: (i, 0, 0)),
        out_shape=jax.ShapeDtypeStruct((n,) + mem_k.shape[2:], F32),
        compiler_params=_params(("parallel",)),
    )(mq.reshape((n,) + mem_k.shape[2:]), mem_k, mem_v).reshape(n, hm)

    first = xn2_all.shape[0] - n
    assert first % n == 0
    args = (x, a_g, o, g, c, gb, gc, lw['w_br_ret'], lw['w_br_mem'], lw['w_out'], lw['norm_ffn'], lw['w_rt'],
            lw['b_rt'], counts0)
    x1, xn2_all, route, counts = pl.pallas_call(
        functools.partial(_merge_sample_kernel, heads=heads, n_groups=dims['g'], per_group=dims['e']),
        grid=(1,),
        in_specs=[_const_spec(a.shape) for a in args] + [pl.BlockSpec(memory_space=pl.ANY)],
        out_specs=[pl.BlockSpec((n, d), lambda i: (0, 0)), pl.BlockSpec((n, d // 2), lambda i: (first // n, 0)),
                   pl.BlockSpec((n, LANES), lambda i: (0, 0)), pl.BlockSpec((1, LANES), lambda i: (0, 0))],
        out_shape=[jax.ShapeDtypeStruct((n, d), F32), jax.ShapeDtypeStruct(xn2_all.shape, F32),
                   jax.ShapeDtypeStruct((n, LANES), F32), jax.ShapeDtypeStruct((1, LANES), F32)],
        input_output_aliases={len(args): 1},
        compiler_params=_params(("arbitrary",)),
    )(*args, xn2_all)
    return x1, xn2_all, route, counts, conv_new, h_new, s_new


def _moe(xn2, route_parts, counts, lw, dims, tm=512, n_pass=2):
    t, d = xn2.shape[0], dims['d']
    g, n_exp = dims['g'], dims['g'] * dims['e']
    ff = lw['w_gate'].shape[2]
    counts = counts[0, g:g + n_exp].astype(jnp.int32)
    padded = ((counts + tm - 1) // tm) * tm
    ends = jnp.cumsum(padded)
    offs = ends - padded
    n_tiles = n_pass * ((TOP_K * t + n_exp * tm + n_pass * tm - 1) // (n_pass * tm))
    n_slots = n_tiles * tm
    per_pass = n_tiles // n_pass
    experts = jnp.arange(n_exp, dtype=jnp.int32)[None, :]
    dests = []
    for j in range(TOP_K):
        per_part = []
        for r in route_parts:
            ids = r[:, j].astype(jnp.int32)
            base = jnp.sum(jnp.where(ids[:, None] == experts, offs[None, :], 0), axis=1)
            per_part.append(base + r[:, 4 + j].astype(jnp.int32))
        dests.append(per_part)
    dest = jnp.concatenate([dp for per_part in dests for dp in per_part], axis=0)
    tok = jnp.arange(t, dtype=jnp.int32)
    n_pad = n_slots - TOP_K * t
    pad_cnt = jnp.concatenate([padded - counts, n_slots - ends[-1:]])
    pad_from = jnp.concatenate([offs + counts, ends[-1:]])
    pad_end = jnp.cumsum(pad_cnt)
    pad_i = jnp.arange(n_pad, dtype=jnp.int32)
    which = jnp.sum((pad_end[None, :] <= pad_i[:, None]).astype(jnp.int32), axis=1)
    shift = pad_from - (pad_end - pad_cnt)
    groups = jnp.arange(n_exp + 1, dtype=jnp.int32)[None, :]
    pad_slot = pad_i + jnp.sum(jnp.where(which[:, None] == groups, shift[None, :], 0), axis=1)
    _, src = lax.sort_key_val(jnp.concatenate([dest, pad_slot]),
                              jnp.concatenate([tok] * TOP_K + [pad_i % t]))
    n_used = (ends[-1] // tm).astype(jnp.int32)
    tile_row = jnp.arange(n_tiles, dtype=jnp.int32) * tm
    tile_e = jnp.sum((ends[None, :] <= tile_row[:, None]).astype(jnp.int32), axis=1)
    last_e = jnp.sum((ends <= (n_used - 1) * tm).astype(jnp.int32))
    tile_e = jnp.where(tile_row < ends[-1], tile_e, last_e)
    def expert_pass(first_tile, ys_table):
        xs = xn2.at[src[first_tile * tm:(first_tile + per_pass) * tm]].get(mode='promise_in_bounds')

        te = tile_e[first_tile:first_tile + per_pass]
        nu = jnp.clip(n_used - first_tile, 0, per_pass)
        idx = jnp.arange(per_pass, dtype=jnp.int32)
        first = (idx < nu) & ((idx == 0) | (te != jnp.roll(te, 1)))
        slot = (jnp.cumsum(first.astype(jnp.int32)) - 1) % 2
        later = lax.cummin(jnp.where(first, idx, per_pass), reverse=True)
        nxt_at = jnp.concatenate([later[1:], jnp.full((1,), per_pass, jnp.int32)])
        nxt = jnp.where(nxt_at < per_pass, te[jnp.minimum(nxt_at, per_pass - 1)], -1)
        plan = jnp.stack([te, first.astype(jnp.int32), slot, nxt]).astype(jnp.int32)

        def rows(i, plan, nu):
            return (jnp.maximum(jnp.minimum(i, nu[0] - 1), 0), 0)

        hbm = pl.BlockSpec(memory_space=pl.ANY)
        in_specs = [pl.BlockSpec((tm, d // 2), rows), hbm, hbm, hbm]
        args = [xs, lw['w_gate'], lw['w_up'], lw['w_down']]
        aliases = {}
        if ys_table is not None:
            in_specs.append(hbm)
            args.append(ys_table)
            aliases = {2 + len(args) - 1: 0}
        return pl.pallas_call(
            _expert_kernel,
            grid_spec=pltpu.PrefetchScalarGridSpec(
                num_scalar_prefetch=2,
                grid=(per_pass,),
                in_specs=in_specs,
                out_specs=pl.BlockSpec((tm, d // 2), lambda i, plan, nu: (first_tile + i, 0)),
                scratch_shapes=[pltpu.VMEM((2, d, ff), F32), pltpu.VMEM((2, d, ff), F32),
                                pltpu.VMEM((2, ff, d), F32), pltpu.SemaphoreType.DMA((3, 2))],
            ),
            out_shape=jax.ShapeDtypeStruct((n_slots, d // 2), F32),
            input_output_aliases=aliases,
            compiler_params=_params(("arbitrary",)),
        )(plan, nu.reshape(1), *args)

    ys = None
    for p in range(n_pass):
        ys = expert_pass(p * per_pass, ys)
    return [(ys, tuple(dests[j][p] for j in range(TOP_K))) for p in range(len(route_parts))]


def _final(x1, ys, dest_pair, route, g, n_parts):
    t, d = x1.shape
    rows = t // n_parts
    tl = _tile(rows, 512)
    out = None
    for p in range(n_parts):
        first = p * rows // tl
        y1, y2 = (ys.at[dp[p * rows:(p + 1) * rows]].get(mode='promise_in_bounds') for dp in dest_pair)
        row = pl.BlockSpec((tl, d), lambda i, first=first: (first + i, 0))
        packed = pl.BlockSpec((tl, d // 2), lambda i: (i, 0))
        in_specs = [row, packed, packed, pl.BlockSpec((tl, LANES), lambda i, first=first: (first + i, 0)),
                    _const_spec((1, d))]
        args = [x1, y1, y2, route, g]
        aliases = {}
        if out is not None:
            in_specs.append(pl.BlockSpec(memory_space=pl.ANY))
            args.append(out)
            aliases = {len(args) - 1: 0}
        out = pl.pallas_call(
            _final_kernel,
            grid=(rows // tl,),
            in_specs=in_specs,
            out_specs=row,
            out_shape=jax.ShapeDtypeStruct((t, d), F32),
            input_output_aliases=aliases,
            compiler_params=_params(("parallel",)),
        )(*args)
    return out


def kernel(x_prompt, x_sample, mem_prompt, state_conv, state_lru, state_ret, cache_mem_k, cache_mem_v, norm_mix, norm_ffn, norm_mem, norm_final, w_in, w_mem_kv, conv_w, conv_b, w_r, b_r, w_i, b_i, lru_lambda, w_branch_lru, w_branch_ret, w_branch_mem, w_out, w_grp, b_grp, w_exp_router, b_exp_router, w_gate, w_up, w_down):
    depth = w_in.shape[0]
    assert depth == 1, "the two request groups are chained per layer only for a single-layer trunk"
    assert x_sample.shape[1] == 1
    bp, lp, d = x_prompt.shape
    ns = x_sample.shape[0]
    heads, dk, dv = state_ret.shape[2], state_ret.shape[3], state_ret.shape[4]
    mheads, mdh = cache_mem_k.shape[3], cache_mem_k.shape[4]
    dims = dict(d=d, w=state_lru.shape[2], heads=heads, dk=dk, dv=dv, hk=heads * dk, hv=heads * dv,
                mheads=mheads, hm=mheads * mdh, g=w_exp_router.shape[1], e=w_exp_router.shape[3])
    layer = 0
    p = dict(norm_mix=norm_mix[layer], norm_ffn=norm_ffn[layer], norm_mem=norm_mem[layer], w_in=w_in[layer],
             w_mem_kv=w_mem_kv[layer], conv_w=conv_w[layer], conv_b=conv_b[layer], w_r=w_r[layer], b_r=b_r[layer],
             w_i=w_i[layer], b_i=b_i[layer], lru_lambda=lru_lambda[layer], w_branch_lru=w_branch_lru[layer],
             w_branch_ret=w_branch_ret[layer], w_branch_mem=w_branch_mem[layer], w_out=w_out[layer],
             w_grp=w_grp[layer], b_grp=b_grp[layer], w_exp_router=w_exp_router[layer],
             b_exp_router=b_exp_router[layer], w_gate=w_gate[layer], w_up=w_up[layer], w_down=w_down[layer])
    lw = _prep_layer(p, dims)

    x1p, xn2, route_p, counts, conv_p, lru_p, ret_p, mk, mv = _prompt_mixer(x_prompt, mem_prompt, lw, dims, ns)
    x1s, xn2, route_s, counts, conv_s, lru_s, ret_s = _sample_mixer(
        x_sample[:, 0], state_conv[layer], state_lru[layer], state_ret[layer], cache_mem_k[layer],
        cache_mem_v[layer], counts, xn2, lw, dims)

    tp = bp * lp
    route_p = route_p.reshape(tp, LANES)
    (ys_p, dest_p), (ys_s, dest_s) = _moe(xn2, [route_p, route_s], counts, lw, dims)
    yp = _final(x1p.reshape(tp, d), ys_p, dest_p, route_p, _row(norm_final), 2).reshape(bp, lp, d)
    ys = _final(x1s, ys_s, dest_s, route_s, _row(norm_final), 1).reshape(ns, 1, d)
    return (yp, ys, conv_p[None], lru_p[None], ret_p[None], mk[None], mv[None],
            conv_s[None], lru_s[None], ret_s[None])
```

```python
import functools
import itertools
import math

import jax
import jax.numpy as jnp
from jax import lax
from jax.experimental import pallas as pl
from jax.experimental.pallas import tpu as pltpu

F32 = jnp.float32
BF16 = jnp.bfloat16

EPS = 1e-6
LRU_C = 8.0
ROPE_BASE = 10000.0
PAST_LEN = 16384
TOP_K = 2
LANES = 128
SUBLANES = 8
VMEM_LIMIT = 56 * 1024 * 1024


def _dot(a, b):
    return jnp.dot(a, b, preferred_element_type=F32)


def _dot_nt(a, b):
    return lax.dot_general(a, b, (((1,), (1,)), ((), ())), preferred_element_type=F32)


def _dot_tn(a, b):
    return lax.dot_general(a, b, (((0,), (0,)), ((), ())), preferred_element_type=F32)


def _rms(x, g):
    return x * lax.rsqrt(jnp.mean(x * x, axis=-1, keepdims=True) + EPS) * g


def _sigmoid(x):
    return 0.5 * jnp.tanh(0.5 * x) + 0.5


def _silu(x):
    return x * _sigmoid(x)


def _gelu_tanh(x):
    return 0.5 * x * (1.0 + jnp.tanh(math.sqrt(2.0 / math.pi) * (x + 0.044715 * (x * x * x))))


def _softplus(x):
    return jnp.maximum(x, 0.0) + jnp.log1p(jnp.exp(-jnp.abs(x)))


def _const_spec(shape):
    nd = len(shape)
    return pl.BlockSpec(shape, lambda *_: (0,) * nd, pipeline_mode=pl.Buffered(1))


def _window_spec(rows, window):
    start, width = window
    return pl.BlockSpec((rows, width), lambda *_: (0, start // width), pipeline_mode=pl.Buffered(1))


def _params(sem):
    return pltpu.CompilerParams(dimension_semantics=sem, vmem_limit_bytes=VMEM_LIMIT)


def _lru_coeffs(uc, wr_ref, br, wi_ref, bi, lam):
    nb, bs = wr_ref.shape[0], wr_ref.shape[1]
    ucb = uc.astype(BF16)
    r_lin = jnp.concatenate([_dot(ucb[:, n * bs:(n + 1) * bs], wr_ref[n]) for n in range(nb)], axis=-1)
    i_lin = jnp.concatenate([_dot(ucb[:, n * bs:(n + 1) * bs], wi_ref[n]) for n in range(nb)], axis=-1)
    r = _sigmoid(r_lin + br)
    i = _sigmoid(i_lin + bi)
    log_a = (-LRU_C) * r * _softplus(-lam)
    a = jnp.exp(log_a)
    t = 1.0 - a * a
    mult = t * lax.rsqrt(jnp.maximum(t, 1e-37))
    return a, mult, i


def _scan_rows(a, b, h0):
    rows, width = a.shape
    grp = SUBLANES
    n_grp = rows // grp
    a = a.reshape(n_grp, grp, width)
    b = b.reshape(n_grp, grp, width)
    row = lax.broadcasted_iota(jnp.int32, a.shape, 1)
    d = 1
    while d < grp:
        a_sh = pltpu.roll(a, d, 1)
        b_sh = pltpu.roll(b, d, 1)
        keep = row >= d
        b = jnp.where(keep, b + a * b_sh, b)
        a = jnp.where(keep, a * a_sh, a)
        d *= 2
    out = []
    for g in range(n_grp):
        hg = a[g] * h0 + b[g]
        out.append(hg)
        h0 = hg[grp - 1:grp, :]
    return jnp.concatenate(out, axis=0)


def _rotary(t, cos, sin):
    half = t.shape[-1] // 2
    t1, t2 = t[:, :half], t[:, half:]
    return jnp.concatenate([t1 * cos - t2 * sin, t2 * cos + t1 * sin], axis=-1)


def _route(logits, run, n_groups, per_group):
    col = lax.broadcasted_iota(jnp.int32, logits.shape, 1)
    big = jnp.int32(1 << 20)
    neg = jnp.float32(-jnp.inf)
    gl = jnp.where(col < n_groups, logits, neg)
    gmax = jnp.max(gl, axis=-1, keepdims=True)
    g_idx = jnp.min(jnp.where(gl == gmax, col, big), axis=-1, keepdims=True)
    g_val = 1.0 / jnp.sum(jnp.exp(gl - gmax), axis=-1, keepdims=True)
    lo = n_groups + per_group * g_idx
    el = jnp.where((col >= lo) & (col < lo + per_group), logits, neg)
    m1 = jnp.max(el, axis=-1, keepdims=True)
    i1 = jnp.min(jnp.where(el == m1, col, big), axis=-1, keepdims=True)
    el2 = jnp.where(col == i1, neg, el)
    m2 = jnp.max(el2, axis=-1, keepdims=True)
    i2 = jnp.min(jnp.where(el2 == m2, col, big), axis=-1, keepdims=True)
    e2 = jnp.exp(m2 - m1)
    w1 = g_val / (1.0 + e2)
    w2 = g_val * e2 / (1.0 + e2)
    f1 = (i1 - n_groups).astype(F32)
    f2 = (i2 - n_groups).astype(F32)
    hit1, hit2 = col == i1, col == i2
    hits = jnp.where(hit1 | hit2, 1.0, 0.0)
    rows = logits.shape[0]
    earlier = lax.broadcasted_iota(jnp.int32, (rows, rows), 1) < lax.broadcasted_iota(jnp.int32, (rows, rows), 0)
    before = run + _dot(jnp.where(earlier, 1.0, 0.0).astype(BF16), hits.astype(BF16))
    r1 = jnp.sum(jnp.where(hit1, before, 0.0), axis=-1, keepdims=True)
    r2 = jnp.sum(jnp.where(hit2, before, 0.0), axis=-1, keepdims=True)
    route = f1
    for j, val in enumerate((f2, w1, w2, r1, r2), start=1):
        route = jnp.where(col == j, val, route)
    route = jnp.where(col > 5, 0.0, route)
    return route, run + jnp.sum(hits, axis=0, keepdims=True)


def _pack_pairs(x):
    half = x.shape[1] // 2
    lo = lax.bitcast_convert_type(x[:, :half].astype(F32), jnp.uint32)
    hi = lax.bitcast_convert_type(x[:, half:].astype(F32), jnp.uint32)
    return lax.bitcast_convert_type((lo >> 16) | (hi & jnp.uint32(0xFFFF0000)), F32)


def _unpack_pairs(w):
    u = lax.bitcast_convert_type(w, jnp.uint32)
    lo = lax.bitcast_convert_type(u << 16, F32)
    hi = lax.bitcast_convert_type(u & jnp.uint32(0xFFFF0000), F32)
    return jnp.concatenate([lo, hi], axis=1)


def _merge_tail(x, merged, wout_ref, gffn, wrt_ref, brt, run, n_groups, per_group):
    x1 = x + _dot(merged.astype(BF16), wout_ref[...])
    xn2 = _rms(x1, gffn).astype(BF16)
    logits = _dot(xn2, wrt_ref[...]) + brt
    route, run = _route(logits, run, n_groups, per_group)
    return x1, _pack_pairs(xn2), route, run


def _lru_prompt_steps(c, xn, wxy_ref, wga_ref, cw_ref, cb_ref, wr_ref, br_ref, wi_ref, bi_ref, lam_ref, wbr_ref,
                      out_ref, ccar, hcar, res):
    tl = xn.shape[0]
    width = hcar.shape[1]
    kw = cw_ref.shape[0]
    nb, bs = wr_ref.shape[0], wr_ref.shape[1]
    z = _dot(xn, wxy_ref[...])
    row = lax.broadcasted_iota(jnp.int32, (tl, bs), 0)
    hbs, lasts = [], []
    for n in range(nb):
        cs = slice(n * bs, (n + 1) * bs)
        ux = z[:, cs]
        acc = cw_ref[0:1, cs] * ux
        for j in range(1, kw):
            prev = jnp.where(row == 0, ccar[j - 1:j, cs], pltpu.roll(acc, 1, 0))
            ccar[j - 1:j, cs] = acc[tl - 1:tl, :]
            acc = prev + cw_ref[j:j + 1, cs] * ux
        uc = acc + cb_ref[:, cs]
        ucb = uc.astype(BF16)
        r = _sigmoid(_dot(ucb, wr_ref[n]) + br_ref[:, cs])
        gate = _sigmoid(_dot(ucb, wi_ref[n]) + bi_ref[:, cs])
        log_a = (-LRU_C) * r * _softplus(-lam_ref[:, cs])
        a = jnp.exp(log_a)
        t = 1.0 - a * a
        mult = t * lax.rsqrt(jnp.maximum(t, 1e-37))
        mult = jnp.where(row + c * tl == 0, 1.0, mult)
        h = _scan_rows(a, mult * gate * uc, hcar[:, cs])
        hcar[:, cs] = h[tl - 1:tl, :]
        lasts.append(h[tl - 1:tl, :])
        hbs.append((h * _gelu_tanh(z[:, width + n * bs: width + (n + 1) * bs])).astype(BF16))
        yield
    res['tail'] = z[tl - (kw - 1):tl, :width]
    res['h_last'] = jnp.concatenate(lasts, axis=1)
    a_out = _dot(jnp.concatenate(hbs, axis=1), wbr_ref[...])
    ga = _dot(xn, wga_ref[...])
    out_ref[0] = (_sigmoid(ga) * a_out).astype(out_ref.dtype)
    yield


def _ret_proj_steps(xn, w_refs, res, cols=512):
    for name, w_ref in w_refs:
        parts = []
        for j in range(0, w_ref.shape[1], cols):
            parts.append(_dot(xn, w_ref[:, j:j + cols]))
            yield
        res[name] = jnp.concatenate(parts, axis=1)


def _ret_prompt_steps(xn, proj, wgb_ref, cos_ref, sin_ref, wbr_ref, out_ref, s_scr, y_scr, heads, chunk):
    tl = xn.shape[0]
    q, k, v, g = proj['q'], proj['k'], proj['v'], proj['g']
    dk = q.shape[1] // heads
    dv = v.shape[1] // heads
    cos, sin = cos_ref[...], sin_ref[...]
    n_i = lax.broadcasted_iota(jnp.int32, (chunk, chunk), 0)
    m_i = lax.broadcasted_iota(jnp.int32, (chunk, chunk), 1)
    diff = (n_i - m_i).astype(F32)
    rowk = lax.broadcasted_iota(jnp.int32, (chunk, dk), 0).astype(F32)
    for h in range(heads):
        log_g = math.log1p(-(2.0 ** (-5.0 - h)))
        dmask = jnp.where(diff >= 0, jnp.exp(jnp.maximum(diff, 0.0) * log_g), 0.0)
        q_decay = jnp.exp((rowk + 1.0) * log_g)
        k_decay = jnp.exp((chunk - 1.0 - rowk) * log_g)
        s_decay = math.exp(chunk * log_g)
        for sub in range(tl // chunk):
            r0 = sub * chunk
            cs, sn = cos[r0:r0 + chunk], sin[r0:r0 + chunk]
            qr = _rotary(q[r0:r0 + chunk, h * dk:(h + 1) * dk], cs, sn)
            kr = _rotary(k[r0:r0 + chunk, h * dk:(h + 1) * dk], cs, sn) * (dk ** -0.5)
            vh = v[r0:r0 + chunk, h * dv:(h + 1) * dv].astype(BF16)
            s = s_scr[h]
            att = _dot_nt(qr.astype(BF16), kr.astype(BF16)) * dmask
            o = _dot(att.astype(BF16), vh) + _dot((qr * q_decay).astype(BF16), s.astype(BF16))
            s_scr[h] = s * s_decay + _dot_tn((kr * k_decay).astype(BF16), vh)
            o = o * lax.rsqrt(jnp.mean(o * o, axis=-1, keepdims=True) + EPS)
            gh = g[r0:r0 + chunk, h * dv:(h + 1) * dv]
            y_scr[r0:r0 + chunk, h * dv:(h + 1) * dv] = (_silu(gh) * o).astype(BF16)
            yield
    b_out = _dot(y_scr[...], wbr_ref[...])
    gb = _dot(xn, wgb_ref[...])
    out_ref[0] = (_sigmoid(gb) * b_out).astype(out_ref.dtype)


def _lru_ret_prompt_kernel(x_ref, g_ref, wxy_ref, wga_ref, cw_ref, cb_ref, wr_ref, br_ref, wi_ref, bi_ref,
                           lam_ref, wbl_ref, wq_ref, wk_ref, wv_ref, wg_ref, wgb_ref, cos_ref, sin_ref, wbr_ref,
                           a_ref, conv_ref, h_ref, b_ref, s_out_ref, ccar, hcar, s_scr, y_scr, *, heads, chunk):
    c = pl.program_id(1)

    @pl.when(c == 0)
    def _():
        ccar[...] = jnp.zeros_like(ccar)
        hcar[...] = jnp.zeros_like(hcar)
        s_scr[...] = jnp.zeros_like(s_scr)

    xn = _rms(x_ref[0], g_ref[...]).astype(BF16)
    lru, proj = {}, {}
    lru_steps = _lru_prompt_steps(c, xn, wxy_ref, wga_ref, cw_ref, cb_ref, wr_ref, br_ref, wi_ref, bi_ref,
                                  lam_ref, wbl_ref, a_ref, ccar, hcar, lru)
    proj_steps = _ret_proj_steps(xn, (('q', wq_ref), ('k', wk_ref), ('v', wv_ref), ('g', wg_ref)), proj)
    n_lru = wr_ref.shape[0] + 1
    n_proj = sum(-(-r.shape[1] // 512) for r in (wq_ref, wk_ref, wv_ref, wg_ref))
    n_ret = heads * (x_ref.shape[1] // chunk)
    lru_done = 0
    for i in range(n_proj):
        next(proj_steps)
        want = ((i + 1) * n_lru) // (n_proj + n_ret)
        while lru_done < want:
            next(lru_steps)
            lru_done += 1
    for _ in proj_steps:
        pass
    ret_steps = _ret_prompt_steps(xn, proj, wgb_ref, cos_ref, sin_ref, wbr_ref, b_ref, s_scr, y_scr, heads, chunk)
    for i in range(n_ret):
        next(ret_steps)
        want = ((n_proj + i + 1) * n_lru) // (n_proj + n_ret)
        while lru_done < want:
            next(lru_steps)
            lru_done += 1
    for _ in itertools.chain(lru_steps, ret_steps):
        pass

    @pl.when(c == pl.num_programs(1) - 1)
    def _():
        conv_ref[0] = lru['tail']
        h_ref[0] = lru['h_last']
        s_out_ref[0] = s_scr[...]


def _memkv_kernel(m_ref, g_ref, w_ref, k_ref, v_ref, k4_ref, v4_ref):
    width = k_ref.shape[-1]
    heads, dh = k4_ref.shape[2], k4_ref.shape[3]
    mn = _rms(m_ref[0], g_ref[...]).astype(BF16)
    kv = _dot(mn, w_ref[...])
    k_ref[0] = kv[:, :width]
    v_ref[0] = kv[:, width:]
    for h in range(heads):
        k4_ref[0, :, h, :] = kv[:, h * dh:(h + 1) * dh]
        v4_ref[0, :, h, :] = kv[:, width + h * dh: width + (h + 1) * dh]


def _attn_merge_prompt_kernel(x_ref, ag_ref, bg_ref, mk_ref, mv_ref, g_ref, wmq_ref, wgc_ref, wbm_ref,
                              wout_ref, gffn_ref, wrt_ref, brt_ref, x1_ref, xn2_ref, route_ref, cnt_ref,
                              *, heads, n_groups, per_group):
    @pl.when((pl.program_id(0) == 0) & (pl.program_id(1) == 0))
    def _():
        cnt_ref[...] = jnp.zeros_like(cnt_ref)

    x = x_ref[0]
    xn = _rms(x, g_ref[...]).astype(BF16)
    mq = _dot(xn, wmq_ref[...]).astype(BF16)
    dh = mq.shape[1] // heads
    mk = mk_ref[0].astype(BF16)
    mv = mv_ref[0].astype(BF16)
    cs = []
    for h in range(heads):
        hs = slice(h * dh, (h + 1) * dh)
        lg = _dot_nt(mq[:, hs], mk[:, hs]) * (dh ** -0.5)
        p = jnp.exp(lg - jnp.max(lg, axis=-1, keepdims=True))
        den = jnp.sum(p, axis=-1, keepdims=True)
        cs.append(_dot(p.astype(BF16), mv[:, hs]) / den)
    c_out = _dot(jnp.concatenate(cs, axis=-1).astype(BF16), wbm_ref[...])
    gc = _dot(xn, wgc_ref[...])
    merged = _sigmoid(gc) * c_out + ag_ref[0].astype(F32) + bg_ref[0].astype(F32)
    x1, xn2, route, run = _merge_tail(x, merged, wout_ref, gffn_ref[...], wrt_ref, brt_ref[...], cnt_ref[...],
                                      n_groups, per_group)
    x1_ref[0] = x1
    xn2_ref[...] = xn2
    route_ref[0] = route
    cnt_ref[...] = run


def _proj_kernel(x_ref, g_ref, w_ref, z_ref):
    xn = _rms(x_ref[...], g_ref[...]).astype(BF16)
    z_ref[...] = _dot(xn, w_ref[...])


def _lru_sample_kernel(ux_ref, uy_ref, ga_ref, cprev_ref, hprev_ref, cw_ref, cb_ref, wr_ref, br_ref, wi_ref,
                       bi_ref, lam_ref, wbr_ref, out_ref, conv_ref, h_ref, *, at_start):
    kw = cw_ref.shape[0]
    ux = ux_ref[...]
    uc = cb_ref[...] + cw_ref[kw - 1:kw, :] * ux
    for j in range(kw - 1):
        uc = uc + cw_ref[j:j + 1, :] * cprev_ref[:, j, :]
    a, mult, gate = _lru_coeffs(uc, wr_ref, br_ref[...], wi_ref, bi_ref[...], lam_ref[...])
    if at_start:
        mult = jnp.ones_like(mult)
    h = a * hprev_ref[...] + mult * gate * uc
    h_ref[...] = h
    for j in range(kw - 2):
        conv_ref[:, j, :] = cprev_ref[:, j + 1, :]
    conv_ref[:, kw - 2, :] = ux
    a_out = _dot((h * _gelu_tanh(uy_ref[...])).astype(BF16), wbr_ref[...])
    out_ref[...] = _sigmoid(ga_ref[...]) * a_out


def _rot_sample_kernel(q_ref, k_ref, cos_ref, sin_ref, qo_ref, ko_ref, *, heads):
    dk = q_ref.shape[1] // heads
    cos, sin = cos_ref[...], sin_ref[...]
    for h in range(heads):
        hs = slice(h * dk, (h + 1) * dk)
        qo_ref[:, hs] = _rotary(q_ref[:, hs], cos, sin)
        ko_ref[:, hs] = _rotary(k_ref[:, hs], cos, sin) * (dk ** -0.5)


def _ret_sample_kernel(qt_ref, kt_ref, q_ref, k_ref, v_ref, s_ref, o_ref, s_out_ref, *, heads):
    bb = q_ref.shape[1]
    dk = s_ref.shape[2]
    dv = s_ref.shape[3]
    for j in range(bb):
        for h in range(heads):
            decay = 1.0 - 2.0 ** (-5.0 - h)
            qcol = qt_ref[0, h, :, j:j + 1]
            kcol = kt_ref[0, h, :, j:j + 1]
            qrow = q_ref[0, j:j + 1, h * dk:(h + 1) * dk]
            krow = k_ref[0, j:j + 1, h * dk:(h + 1) * dk]
            vrow = v_ref[0, j:j + 1, h * dv:(h + 1) * dv]
            s = s_ref[j, h]
            att = jnp.sum(qrow * krow, axis=-1, keepdims=True)
            o = att * vrow + jnp.sum((qcol * decay) * s, axis=0, keepdims=True)
            o_ref[0, j:j + 1, h * dv:(h + 1) * dv] = o
            s_out_ref[j, h] = s * decay + kcol * vrow


def _attn_sample_kernel(mq_ref, k_ref, v_ref, c_ref, *, heads):
    bb, _, dh = mq_ref.shape
    for j in range(bb):
        q = mq_ref[j]
        lg = jnp.sum(k_ref[j] * q[None], axis=-1, keepdims=True) * (dh ** -0.5)
        p = jnp.exp(lg - jnp.max(lg, axis=0, keepdims=True))
        den = jnp.sum(p, axis=0)
        c_ref[j] = jnp.sum(v_ref[j] * p, axis=0) / den


def _merge_sample_kernel(x_ref, ag_ref, o_ref, g_ref, c_ref, gb_ref, gc_ref, wbr_ref, wbm_ref, wout_ref,
                         gffn_ref, wrt_ref, brt_ref, run_ref, xn2_table_ref, x1_ref, xn2_ref, route_ref, cnt_ref,
                         *, heads, n_groups, per_group):
    del xn2_table_ref
    dv = o_ref.shape[1] // heads
    ys = []
    for h in range(heads):
        o = o_ref[:, h * dv:(h + 1) * dv]
        o = o * lax.rsqrt(jnp.mean(o * o, axis=-1, keepdims=True) + EPS)
        ys.append((_silu(g_ref[:, h * dv:(h + 1) * dv]) * o).astype(BF16))
    b_out = _dot(jnp.concatenate(ys, axis=-1), wbr_ref[...])
    c_out = _dot(c_ref[...].astype(BF16), wbm_ref[...])
    merged = ag_ref[...] + _sigmoid(gb_ref[...]) * b_out + _sigmoid(gc_ref[...]) * c_out
    x1, xn2, route, run = _merge_tail(x_ref[...], merged, wout_ref, gffn_ref[...], wrt_ref, brt_ref[...],
                                      run_ref[...], n_groups, per_group)
    x1_ref[...] = x1
    xn2_ref[...] = xn2
    route_ref[...] = route
    cnt_ref[...] = run


def _expert_kernel(plan_ref, nu_ref, xs_ref, wg_hbm, wu_hbm, wd_hbm, *rest):
    ys_ref, wg_buf, wu_buf, wd_buf, sem = rest[-5:]
    i = pl.program_id(0)
    expert, first, slot, nxt = plan_ref[0, i], plan_ref[1, i], plan_ref[2, i], plan_ref[3, i]
    used = i < nu_ref[0]

    def weight_copies(e, s):
        return (pltpu.make_async_copy(wg_hbm.at[e], wg_buf.at[s], sem.at[0, s]),
                pltpu.make_async_copy(wu_hbm.at[e], wu_buf.at[s], sem.at[1, s]),
                pltpu.make_async_copy(wd_hbm.at[e], wd_buf.at[s], sem.at[2, s]))

    @pl.when(used & (i == 0))
    def _():
        for cp in weight_copies(expert, slot):
            cp.start()

    @pl.when(used & (first == 1))
    def _():
        for cp in weight_copies(expert, slot):
            cp.wait()

    @pl.when(used & (first == 1) & (nxt >= 0))
    def _():
        for cp in weight_copies(nxt, 1 - slot):
            cp.start()

    @pl.when(used)
    def _():
        x = _unpack_pairs(xs_ref[...]).astype(BF16)
        hg = _dot(x, wg_buf[slot].astype(BF16))
        hu = _dot(x, wu_buf[slot].astype(BF16))
        hid = (_silu(hg) * hu).astype(BF16)
        ys_ref[...] = _pack_pairs(_dot(hid, wd_buf[slot].astype(BF16)).astype(BF16))

    @pl.when(jnp.logical_not(used))
    def _():
        ys_ref[...] = jnp.zeros_like(ys_ref)


def _final_kernel(x1_ref, y1_ref, y2_ref, route_ref, g_ref, *rest):
    out_ref = rest[-1]
    w1 = route_ref[:, 2:3]
    w2 = route_ref[:, 3:4]
    x2 = x1_ref[...] + (w1 * _unpack_pairs(y1_ref[...]) + w2 * _unpack_pairs(y2_ref[...]))
    out_ref[...] = _rms(x2, g_ref[...])


def _tile(n, target):
    t = min(n, target)
    while n % t:
        t //= 2
    return t


def _row(v):
    return v.reshape(1, -1).astype(F32)


def _prep_layer(p, dims):
    d, w, hk, hv, hm = dims['d'], dims['w'], dims['hk'], dims['hv'], dims['hm']
    w_in = p['w_in']
    starts = {}
    o0 = 0
    for name, width in (('xy', 2 * w), ('q', hk), ('k', hk), ('v', hv), ('g', hv), ('mq', hm),
                        ('ga', d), ('gb', d), ('gc', d)):
        assert o0 % width == 0, "each window must start on a multiple of its width"
        starts[name] = (o0, width)
        o0 += width
    g, e = p['w_exp_router'].shape[0], p['w_exp_router'].shape[2]
    w_rt = jnp.concatenate([p['w_grp'], jnp.moveaxis(p['w_exp_router'], 0, 1).reshape(d, g * e)], axis=1)
    w_rt = jnp.pad(w_rt, ((0, 0), (0, LANES - w_rt.shape[1]))).astype(BF16)
    b_rt = jnp.pad(jnp.concatenate([p['b_grp'], p['b_exp_router'].reshape(-1)]), (0, LANES - g - g * e))
    return dict(
        w_in=w_in.astype(BF16), win=starts,
        norm_mix=_row(p['norm_mix']), norm_ffn=_row(p['norm_ffn']), norm_mem=_row(p['norm_mem']),
        w_mem_kv=p['w_mem_kv'].astype(BF16),
        conv_w=p['conv_w'].astype(F32), conv_b=_row(p['conv_b']),
        w_r=p['w_r'].astype(BF16), b_r=_row(p['b_r']), w_i=p['w_i'].astype(BF16), b_i=_row(p['b_i']),
        lam=_row(p['lru_lambda']),
        w_br_lru=p['w_branch_lru'].astype(BF16), w_br_ret=p['w_branch_ret'].astype(BF16),
        w_br_mem=p['w_branch_mem'].astype(BF16), w_out=p['w_out'].astype(BF16),
        w_rt=w_rt, b_rt=_row(b_rt),
        w_gate=p['w_gate'].reshape((g * e,) + p['w_gate'].shape[2:]),
        w_up=p['w_up'].reshape((g * e,) + p['w_up'].shape[2:]),
        w_down=p['w_down'].reshape((g * e,) + p['w_down'].shape[2:]),
    )


def _rope_tables(pos, dk):
    half = dk // 2
    inv = ROPE_BASE ** (-jnp.linspace(0.0, 1.0, half, dtype=F32))
    ang = pos.astype(F32)[:, None] * inv[None, :]
    return jnp.cos(ang), jnp.sin(ang)


def _prompt_mixer(x, mem, lw, dims, extra_rows):
    b, l, d = x.shape
    w, heads, dk, dv = dims['w'], dims['heads'], dims['dk'], dims['dv']
    kw = lw['conv_w'].shape[0]
    tl = _tile(l, 256)
    grid = (b, l // tl)
    xspec = pl.BlockSpec((1, tl, d), lambda i, c: (i, c, 0))
    sem = ("parallel", "arbitrary")
    w_in, win = lw['w_in'], lw['win']

    chunk = _tile(tl, 256)
    cos, sin = _rope_tables(jnp.arange(l, dtype=jnp.int32), dk)
    half = dk // 2
    a_g, conv_new, h_last, b_g, s_new = pl.pallas_call(
        functools.partial(_lru_ret_prompt_kernel, heads=heads, chunk=chunk),
        grid=grid,
        in_specs=[xspec, _const_spec((1, d)), _window_spec(d, win['xy']), _window_spec(d, win['ga']),
                  _const_spec((kw, w)), _const_spec((1, w)), _const_spec(lw['w_r'].shape), _const_spec((1, w)),
                  _const_spec(lw['w_i'].shape), _const_spec((1, w)), _const_spec((1, w)), _const_spec((w, d)),
                  _window_spec(d, win['q']), _window_spec(d, win['k']),
                  _window_spec(d, win['v']), _window_spec(d, win['g']), _window_spec(d, win['gb']),
                  pl.BlockSpec((tl, half), lambda i, c: (c, 0)), pl.BlockSpec((tl, half), lambda i, c: (c, 0)),
                  _const_spec((heads * dv, d))],
        out_specs=[pl.BlockSpec((1, tl, d), lambda i, c: (i, c, 0)),
                   pl.BlockSpec((1, kw - 1, w), lambda i, c: (i, 0, 0)),
                   pl.BlockSpec((1, 1, w), lambda i, c: (i, 0, 0)),
                   pl.BlockSpec((1, tl, d), lambda i, c: (i, c, 0)),
                   pl.BlockSpec((1, heads, dk, dv), lambda i, c: (i, 0, 0, 0))],
        out_shape=[jax.ShapeDtypeStruct((b, l, d), BF16), jax.ShapeDtypeStruct((b, kw - 1, w), F32),
                   jax.ShapeDtypeStruct((b, 1, w), F32),
                   jax.ShapeDtypeStruct((b, l, d), BF16), jax.ShapeDtypeStruct((b, heads, dk, dv), F32)],
        scratch_shapes=[pltpu.VMEM((SUBLANES, w), F32), pltpu.VMEM((1, w), F32),
                        pltpu.VMEM((heads, dk, dv), F32), pltpu.VMEM((tl, heads * dv), BF16)],
        compiler_params=_params(sem),
    )(x, lw['norm_mix'], w_in, w_in, lw['conv_w'], lw['conv_b'], lw['w_r'], lw['b_r'],
      lw['w_i'], lw['b_i'], lw['lam'], lw['w_br_lru'],
      w_in, w_in, w_in, w_in, w_in, cos, sin, lw['w_br_ret'])

    m = mem.shape[1]
    hm = dims['hm']
    mh = dims['mheads']
    flat = pl.BlockSpec((1, m, hm), lambda i: (i, 0, 0))
    per_head = pl.BlockSpec((1, m, mh, hm // mh), lambda i: (i, 0, 0, 0))
    mk, mv, mk4, mv4 = pl.pallas_call(
        _memkv_kernel,
        grid=(b,),
        in_specs=[pl.BlockSpec((1, m, d), lambda i: (i, 0, 0)), _const_spec((1, d)), _const_spec((d, 2 * hm))],
        out_specs=[flat, flat, per_head, per_head],
        out_shape=[jax.ShapeDtypeStruct((b, m, hm), F32), jax.ShapeDtypeStruct((b, m, hm), F32),
                   jax.ShapeDtypeStruct((b, m, mh, hm // mh), F32), jax.ShapeDtypeStruct((b, m, mh, hm // mh), F32)],
        compiler_params=_params(("parallel",)),
    )(mem, lw['norm_mem'], lw['w_mem_kv'])

    tm = _tile(l, 1024)
    mspec = pl.BlockSpec((1, tm, d), lambda i, c: (i, c, 0))
    x1, xn2, route, counts = pl.pallas_call(
        functools.partial(_attn_merge_prompt_kernel, heads=dims['mheads'], n_groups=dims['g'],
                          per_group=dims['e']),
        grid=(b, l // tm),
        in_specs=[mspec, mspec, mspec,
                  pl.BlockSpec((1, m, hm), lambda i, c: (i, 0, 0)), pl.BlockSpec((1, m, hm), lambda i, c: (i, 0, 0)),
                  _const_spec((1, d)), _window_spec(d, win['mq']), _window_spec(d, win['gc']), _const_spec((hm, d)),
                  _const_spec((d, d)), _const_spec((1, d)), _const_spec((d, LANES)), _const_spec((1, LANES))],
        out_specs=[mspec, pl.BlockSpec((tm, d // 2), lambda i, c: (i * (l // tm) + c, 0)),
                   pl.BlockSpec((1, tm, LANES), lambda i, c: (i, c, 0)),
                   pl.BlockSpec((1, LANES), lambda i, c: (0, 0))],
        out_shape=[jax.ShapeDtypeStruct((b, l, d), F32), jax.ShapeDtypeStruct((b * l + extra_rows, d // 2), F32),
                   jax.ShapeDtypeStruct((b, l, LANES), F32), jax.ShapeDtypeStruct((1, LANES), F32)],
        compiler_params=_params(("arbitrary", "arbitrary")),
    )(x, a_g, b_g, mk, mv, lw['norm_mix'], w_in, w_in, lw['w_br_mem'], lw['w_out'],
      lw['norm_ffn'], lw['w_rt'], lw['b_rt'])
    return x1, xn2, route, counts, conv_new, h_last[:, 0], s_new, mk4, mv4


def _sample_mixer(x, conv_prev, h_prev, s_prev, mem_k, mem_v, counts0, xn2_all, lw, dims):
    n, d = x.shape
    w, heads, dk, dv, hm = dims['w'], dims['heads'], dims['dk'], dims['dv'], dims['hm']
    hk, hv = heads * dk, heads * dv
    kw = lw['conv_w'].shape[0]
    n_in = lw['w_in'].shape[1]
    tn = _tile(n_in, 1024)
    z = pl.pallas_call(
        _proj_kernel,
        grid=(n_in // tn,),
        in_specs=[_const_spec((n, d)), _const_spec((1, d)), pl.BlockSpec((d, tn), lambda j: (0, j))],
        out_specs=pl.BlockSpec((n, tn), lambda j: (0, j)),
        out_shape=jax.ShapeDtypeStruct((n, n_in), F32),
        compiler_params=_params(("parallel",)),
    )(x, lw['norm_mix'], lw['w_in'])
    o0 = 0
    parts = []
    for sz in (w, w, hk, hk, hv, hv, hm, d, d, d):
        parts.append(z[:, o0:o0 + sz])
        o0 += sz
    ux, uy, q, k, v, g, mq, ga, gb, gc = parts

    a_g, conv_new, h_new = pl.pallas_call(
        functools.partial(_lru_sample_kernel, at_start=(PAST_LEN == 0)),
        out_shape=[jax.ShapeDtypeStruct((n, d), F32), jax.ShapeDtypeStruct((n, kw - 1, w), F32),
                   jax.ShapeDtypeStruct((n, w), F32)],
        compiler_params=pltpu.CompilerParams(vmem_limit_bytes=VMEM_LIMIT),
    )(ux, uy, ga, conv_prev, h_prev, lw['conv_w'], lw['conv_b'], lw['w_r'], lw['b_r'], lw['w_i'], lw['b_i'],
      lw['lam'], lw['w_br_lru'])

    cos, sin = _rope_tables(PAST_LEN + jnp.arange(1, dtype=jnp.int32), dk)
    qr, kr = pl.pallas_call(
        functools.partial(_rot_sample_kernel, heads=heads),
        out_shape=[jax.ShapeDtypeStruct((n, hk), F32), jax.ShapeDtypeStruct((n, hk), F32)],
    )(q, k, cos, sin)

    bb = _tile(n, 4)

    def cols(t):
        return t.reshape(n // bb, bb, heads, dk).transpose(0, 2, 3, 1)

    def rows(t):
        return t.reshape(n // bb, bb, t.shape[1])

    o, s_new = pl.pallas_call(
        functools.partial(_ret_sample_kernel, heads=heads),
        grid=(n // bb,),
        in_specs=[pl.BlockSpec((1, heads, dk, bb), lambda i: (i, 0, 0, 0)),
                  pl.BlockSpec((1, heads, dk, bb), lambda i: (i, 0, 0, 0)),
                  pl.BlockSpec((1, bb, hk), lambda i: (i, 0, 0)), pl.BlockSpec((1, bb, hk), lambda i: (i, 0, 0)),
                  pl.BlockSpec((1, bb, hv), lambda i: (i, 0, 0)),
                  pl.BlockSpec((bb, heads, dk, dv), lambda i: (i, 0, 0, 0))],
        out_specs=[pl.BlockSpec((1, bb, hv), lambda i: (i, 0, 0)),
                   pl.BlockSpec((bb, heads, dk, dv), lambda i: (i, 0, 0, 0))],
        out_shape=[jax.ShapeDtypeStruct((n // bb, bb, hv), F32), jax.ShapeDtypeStruct(s_prev.shape, F32)],
        compiler_params=_params(("parallel",)),
    )(cols(qr), cols(kr), rows(qr), rows(kr), rows(v), s_prev)
    o = o.reshape(n, hv)
    bb = _tile(n, SUBLANES)

    m = mem_k.shape[1]
    c = pl.pallas_call(
        functools.partial(_attn_sample_kernel, heads=dims['mheads']),
        grid=(n // bb,),
        in_specs=[pl.BlockSpec((bb,) + mem_k.shape[2:], lambda i: (i, 0, 0)),
                  pl.BlockSpec((bb,) + mem_k.shape[1:], lambda i: (i, 0, 0, 0)),
                  pl.BlockSpec((bb,) + mem_v.shape[1:], lambda i: (i, 0, 0, 0))],
        out_specs=pl.BlockSpec((bb,) + mem_k.shape[2:], lambda i: (i, 0, 0)),
        out_shape=jax.ShapeDtypeStruct((n,) + mem_k.shape[2:], F32),
        compiler_params=_params(("parallel",)),
    )(mq.reshape((n,) + mem_k.shape[2:]), mem_k, mem_v).reshape(n, hm)

    first = xn2_all.shape[0] - n
    assert first % n == 0
    args = (x, a_g, o, g, c, gb, gc, lw['w_br_ret'], lw['w_br_mem'], lw['w_out'], lw['norm_ffn'], lw['w_rt'],
            lw['b_rt'], counts0)
    x1, xn2_all, route, counts = pl.pallas_call(
        functools.partial(_merge_sample_kernel, heads=heads, n_groups=dims['g'], per_group=dims['e']),
        grid=(1,),
        in_specs=[_const_spec(a.shape) for a in args] + [pl.BlockSpec(memory_space=pl.ANY)],
        out_specs=[pl.BlockSpec((n, d), lambda i: (0, 0)), pl.BlockSpec((n, d // 2), lambda i: (first // n, 0)),
                   pl.BlockSpec((n, LANES), lambda i: (0, 0)), pl.BlockSpec((1, LANES), lambda i: (0, 0))],
        out_shape=[jax.ShapeDtypeStruct((n, d), F32), jax.ShapeDtypeStruct(xn2_all.shape, F32),
                   jax.ShapeDtypeStruct((n, LANES), F32), jax.ShapeDtypeStruct((1, LANES), F32)],
        input_output_aliases={len(args): 1},
        compiler_params=_params(("arbitrary",)),
    )(*args, xn2_all)
    return x1, xn2_all, route, counts, conv_new, h_new, s_new


def _moe(xn2, route_parts, counts, lw, dims, tm=512, n_pass=2):
    t, d = xn2.shape[0], dims['d']
    g, n_exp = dims['g'], dims['g'] * dims['e']
    ff = lw['w_gate'].shape[2]
    counts = counts[0, g:g + n_exp].astype(jnp.int32)
    padded = ((counts + tm - 1) // tm) * tm
    ends = jnp.cumsum(padded)
    offs = ends - padded
    n_tiles = n_pass * ((TOP_K * t + n_exp * tm + n_pass * tm - 1) // (n_pass * tm))
    n_slots = n_tiles * tm
    per_pass = n_tiles // n_pass
    experts = jnp.arange(n_exp, dtype=jnp.int32)[None, :]
    dests = []
    for j in range(TOP_K):
        per_part = []
        for r in route_parts:
            ids = r[:, j].astype(jnp.int32)
            base = jnp.sum(jnp.where(ids[:, None] == experts, offs[None, :], 0), axis=1)
            per_part.append(base + r[:, 4 + j].astype(jnp.int32))
        dests.append(per_part)
    dest = jnp.concatenate([dp for per_part in dests for dp in per_part], axis=0)
    tok = jnp.arange(t, dtype=jnp.int32)
    n_pad = n_slots - TOP_K * t
    pad_cnt = jnp.concatenate([padded - counts, n_slots - ends[-1:]])
    pad_from = jnp.concatenate([offs + counts, ends[-1:]])
    pad_end = jnp.cumsum(pad_cnt)
    pad_i = jnp.arange(n_pad, dtype=jnp.int32)
    which = jnp.sum((pad_end[None, :] <= pad_i[:, None]).astype(jnp.int32), axis=1)
    shift = pad_from - (pad_end - pad_cnt)
    groups = jnp.arange(n_exp + 1, dtype=jnp.int32)[None, :]
    pad_slot = pad_i + jnp.sum(jnp.where(which[:, None] == groups, shift[None, :], 0), axis=1)
    _, src = lax.sort_key_val(jnp.concatenate([dest, pad_slot]),
                              jnp.concatenate([tok] * TOP_K + [pad_i % t]))
    n_used = (ends[-1] // tm).astype(jnp.int32)
    tile_row = jnp.arange(n_tiles, dtype=jnp.int32) * tm
    tile_e = jnp.sum((ends[None, :] <= tile_row[:, None]).astype(jnp.int32), axis=1)
    last_e = jnp.sum((ends <= (n_used - 1) * tm).astype(jnp.int32))
    tile_e = jnp.where(tile_row < ends[-1], tile_e, last_e)
    def expert_pass(first_tile, ys_table):
        xs = xn2.at[src[first_tile * tm:(first_tile + per_pass) * tm]].get(mode='promise_in_bounds')

        te = tile_e[first_tile:first_tile + per_pass]
        nu = jnp.clip(n_used - first_tile, 0, per_pass)
        idx = jnp.arange(per_pass, dtype=jnp.int32)
        first = (idx < nu) & ((idx == 0) | (te != jnp.roll(te, 1)))
        slot = (jnp.cumsum(first.astype(jnp.int32)) - 1) % 2
        later = lax.cummin(jnp.where(first, idx, per_pass), reverse=True)
        nxt_at = jnp.concatenate([later[1:], jnp.full((1,), per_pass, jnp.int32)])
        nxt = jnp.where(nxt_at < per_pass, te[jnp.minimum(nxt_at, per_pass - 1)], -1)
        plan = jnp.stack([te, first.astype(jnp.int32), slot, nxt]).astype(jnp.int32)

        def rows(i, plan, nu):
            return (jnp.maximum(jnp.minimum(i, nu[0] - 1), 0), 0)

        hbm = pl.BlockSpec(memory_space=pl.ANY)
        in_specs = [pl.BlockSpec((tm, d // 2), rows), hbm, hbm, hbm]
        args = [xs, lw['w_gate'], lw['w_up'], lw['w_down']]
        aliases = {}
        if ys_table is not None:
            in_specs.append(hbm)
            args.append(ys_table)
            aliases = {2 + len(args) - 1: 0}
        return pl.pallas_call(
            _expert_kernel,
            grid_spec=pltpu.PrefetchScalarGridSpec(
                num_scalar_prefetch=2,
                grid=(per_pass,),
                in_specs=in_specs,
                out_specs=pl.BlockSpec((tm, d // 2), lambda i, plan, nu: (first_tile + i, 0)),
                scratch_shapes=[pltpu.VMEM((2, d, ff), F32), pltpu.VMEM((2, d, ff), F32),
                                pltpu.VMEM((2, ff, d), F32), pltpu.SemaphoreType.DMA((3, 2))],
            ),
            out_shape=jax.ShapeDtypeStruct((n_slots, d // 2), F32),
            input_output_aliases=aliases,
            compiler_params=_params(("arbitrary",)),
        )(plan, nu.reshape(1), *args)

    ys = None
    for p in range(n_pass):
        ys = expert_pass(p * per_pass, ys)
    return [(ys, tuple(dests[j][p] for j in range(TOP_K))) for p in range(len(route_parts))]


def _final(x1, ys, dest_pair, route, g, n_parts):
    t, d = x1.shape
    rows = t // n_parts
    tl = _tile(rows, 512)
    out = None
    for p in range(n_parts):
        first = p * rows // tl
        y1, y2 = (ys.at[dp[p * rows:(p + 1) * rows]].get(mode='promise_in_bounds') for dp in dest_pair)
        row = pl.BlockSpec((tl, d), lambda i, first=first: (first + i, 0))
        packed = pl.BlockSpec((tl, d // 2), lambda i: (i, 0))
        in_specs = [row, packed, packed, pl.BlockSpec((tl, LANES), lambda i, first=first: (first + i, 0)),
                    _const_spec((1, d))]
        args = [x1, y1, y2, route, g]
        aliases = {}
        if out is not None:
            in_specs.append(pl.BlockSpec(memory_space=pl.ANY))
            args.append(out)
            aliases = {len(args) - 1: 0}
        out = pl.pallas_call(
            _final_kernel,
            grid=(rows // tl,),
            in_specs=in_specs,
            out_specs=row,
            out_shape=jax.ShapeDtypeStruct((t, d), F32),
            input_output_aliases=aliases,
            compiler_params=_params(("parallel",)),
        )(*args)
    return out


def kernel(x_prompt, x_sample, mem_prompt, state_conv, state_lru, state_ret, cache_mem_k, cache_mem_v, norm_mix, norm_ffn, norm_mem, norm_final, w_in, w_mem_kv, conv_w, conv_b, w_r, b_r, w_i, b_i, lru_lambda, w_branch_lru, w_branch_ret, w_branch_mem, w_out, w_grp, b_grp, w_exp_router, b_exp_router, w_gate, w_up, w_down):
    depth = w_in.shape[0]
    assert depth == 1, "the two request groups are chained per layer only for a single-layer trunk"
    assert x_sample.shape[1] == 1
    bp, lp, d = x_prompt.shape
    ns = x_sample.shape[0]
    heads, dk, dv = state_ret.shape[2], state_ret.shape[3], state_ret.shape[4]
    mheads, mdh = cache_mem_k.shape[3], cache_mem_k.shape[4]
    dims = dict(d=d, w=state_lru.shape[2], heads=heads, dk=dk, dv=dv, hk=heads * dk, hv=heads * dv,
                mheads=mheads, hm=mheads * mdh, g=w_exp_router.shape[1], e=w_exp_router.shape[3])
    layer = 0
    p = dict(norm_mix=norm_mix[layer], norm_ffn=norm_ffn[layer], norm_mem=norm_mem[layer], w_in=w_in[layer],
             w_mem_kv=w_mem_kv[layer], conv_w=conv_w[layer], conv_b=conv_b[layer], w_r=w_r[layer], b_r=b_r[layer],
             w_i=w_i[layer], b_i=b_i[layer], lru_lambda=lru_lambda[layer], w_branch_lru=w_branch_lru[layer],
             w_branch_ret=w_branch_ret[layer], w_branch_mem=w_branch_mem[layer], w_out=w_out[layer],
             w_grp=w_grp[layer], b_grp=b_grp[layer], w_exp_router=w_exp_router[layer],
             b_exp_router=b_exp_router[layer], w_gate=w_gate[layer], w_up=w_up[layer], w_down=w_down[layer])
    lw = _prep_layer(p, dims)

    x1p, xn2, route_p, counts, conv_p, lru_p, ret_p, mk, mv = _prompt_mixer(x_prompt, mem_prompt, lw, dims, ns)
    x1s, xn2, route_s, counts, conv_s, lru_s, ret_s = _sample_mixer(
        x_sample[:, 0], state_conv[layer], state_lru[layer], state_ret[layer], cache_mem_k[layer],
        cache_mem_v[layer], counts, xn2, lw, dims)

    tp = bp * lp
    route_p = route_p.reshape(tp, LANES)
    (ys_p, dest_p), (ys_s, dest_s) = _moe(xn2, [route_p, route_s], counts, lw, dims)
    yp = _final(x1p.reshape(tp, d), ys_p, dest_p, route_p, _row(norm_final), 1).reshape(bp, lp, d)
    ys = _final(x1s, ys_s, dest_s, route_s, _row(norm_final), 1).reshape(ns, 1, d)
    return (yp, ys, conv_p[None], lru_p[None], ret_p[None], mk[None], mv[None],
            conv_s[None], lru_s[None], ret_s[None])
```
